```python
import math
import jax, jax.numpy as jnp
from jax import lax
import numpy as np

D_MODEL = 1024
BATCH = 2
SEQ = 16384
DEPTH = 4

CHUNK = 128
A_GROUPS = 4
A_CH = 128
A_WIDTH = A_GROUPS * A_CH
B_GROUPS = 4
B_WIDTH = 512
CONV_W = 3
MIX_IN = 2 * A_WIDTH + 3 * B_WIDTH
MIX_OUT = A_WIDTH + B_WIDTH
N_HEADS = 16
N_KV = 2
HEAD_DIM = 64
GQA_GROUP = N_HEADS // N_KV
WINDOW = 128
QKV_DIM = (N_HEADS + 2 * N_KV) * HEAD_DIM
ATT_OUT = N_HEADS * HEAD_DIM
N_BUCKETS = 32
MAX_DISTANCE = 128
N_GROUPS = 4
EXPERTS_PER_GROUP = 8
N_EXPERTS = N_GROUPS * EXPERTS_PER_GROUP
TOP_K = 2
D_EXPERT = 512
ROW_BLOCK = 128
ALPHA = (2 * DEPTH) ** 0.25
BETA = (8 * DEPTH) ** -0.25
LN_EPS = 1e-5
N_EVEN = (DEPTH + 1) // 2
N_ODD = DEPTH // 2

kernel_name = 'hybrid_gmlp_conv_swa_hmoe_deepnorm'


def layer_norm(x, g, b):
    xf = x.astype(jnp.float32)
    mu = jnp.mean(xf, axis=-1, keepdims=True)
    var = jnp.mean(jnp.square(xf - mu), axis=-1, keepdims=True)
    y = (xf - mu) * lax.rsqrt(var + LN_EPS)
    return (y * g.astype(jnp.float32) + b.astype(jnp.float32)).astype(x.dtype)


def t5_bucket(rel):
    n = jnp.maximum(rel, 0)
    max_exact = N_BUCKETS // 2
    nf = jnp.maximum(n, 1).astype(jnp.float32)
    large = max_exact + (jnp.log(nf / max_exact) / math.log(MAX_DISTANCE / max_exact)
                         * (N_BUCKETS - max_exact)).astype(jnp.int32)
    large = jnp.minimum(large, N_BUCKETS - 1)
    return jnp.where(n < max_exact, n, large)


def gating_conv_mixer(x, w_in, ln_g, ln_b, w_sp, b_sp, conv_w, w_out):
    bsz, s, _ = x.shape
    h = x @ w_in
    u = jax.nn.gelu(h[..., :A_WIDTH], approximate=False)
    v = jax.nn.gelu(h[..., A_WIDTH:2 * A_WIDTH], approximate=False)
    o = 2 * A_WIDTH
    g_b = h[..., o:o + B_WIDTH]
    g_c = h[..., o + B_WIDTH:o + 2 * B_WIDTH]
    hb = h[..., o + 2 * B_WIDTH:o + 3 * B_WIDTH]
    v = layer_norm(v, ln_g, ln_b)
    v = v.reshape(bsz, s // CHUNK, CHUNK, A_GROUPS, A_CH)
    causal = jnp.tril(jnp.ones((CHUNK, CHUNK), dtype=bool))
    ws = jnp.where(causal[None], w_sp, 0)
    sv = jnp.einsum('gij,bnjgc->bnigc', ws, v) + b_sp.T[:, :, None]
    y_a = u * sv.reshape(bsz, s, A_WIDTH)
    z = g_c * hb
    zp = jnp.pad(z, ((0, 0), (CONV_W - 1, 0), (0, 0)))
    conv = conv_w[0] * zp[:, 0:s]
    for k in range(1, CONV_W):
        conv = conv + conv_w[k] * zp[:, k:k + s]
    y_b = g_b * conv
    return jnp.concatenate([y_a, y_b], axis=-1) @ w_out


def sliding_window_attention(x, w_qkv, b_qkv, sinks, w_o, b_o, rel_table):
    bsz, s, _ = x.shape
    nb = s // CHUNK
    qkv = x @ w_qkv + b_qkv
    q = qkv[..., :ATT_OUT] * (HEAD_DIM ** -0.5)
    k = qkv[..., ATT_OUT:ATT_OUT + N_KV * HEAD_DIM].reshape(bsz, s, N_KV, HEAD_DIM)
    v = qkv[..., ATT_OUT + N_KV * HEAD_DIM:].reshape(bsz, s, N_KV, HEAD_DIM)
    q = q.reshape(bsz, nb, CHUNK, N_KV, GQA_GROUP, HEAD_DIM)

    def band(t):
        tp = jnp.pad(t, ((0, 0), (CHUNK, 0), (0, 0), (0, 0)))
        tp = tp.reshape(bsz, nb + 1, CHUNK, N_KV, HEAD_DIM)
        return jnp.concatenate([tp[:, :-1], tp[:, 1:]], axis=2)

    kw, vw = band(k), band(v)
    a = jnp.arange(CHUNK)[:, None]
    c = jnp.arange(2 * CHUNK)[None, :]
    rel = a + CHUNK - c
    bias = rel_table[t5_bucket(rel)].astype(jnp.float32)
    bias = jnp.transpose(bias, (2, 0, 1)).reshape(N_KV, GQA_GROUP, CHUNK, 2 * CHUNK)
    win = (rel >= 0) & (rel < WINDOW)
    key_pos = jnp.arange(nb)[:, None, None] * CHUNK - CHUNK + c[None]
    mask = win[None] & (key_pos >= 0)

    sc = jnp.einsum('bnqkgd,bnckd->bnkgqc', q, kw).astype(jnp.float32) + bias
    sc = jnp.where(mask[None, :, None, None], sc, jnp.finfo(jnp.float32).min)
    sink = sinks.astype(jnp.float32).reshape(1, 1, N_KV, GQA_GROUP, 1)
    m = jnp.maximum(jnp.max(sc, axis=-1), sink)
    p = jnp.exp(sc - m[..., None])
    denom = jnp.sum(p, axis=-1) + jnp.exp(sink - m)
    probs = (p / denom[..., None]).astype(vw.dtype)
    o = jnp.einsum('bnkgqc,bnckd->bnqkgd', probs, vw).reshape(bsz, s, ATT_OUT)
    return o @ w_o + b_o


def hierarchical_moe(x, w_grp, w_exp, wg, wu, wd):
    bsz, s, d = x.shape
    t = bsz * s
    xf = x.reshape(t, d)
    g_logits = (xf @ w_grp).astype(jnp.float32)
    g_prob = jax.nn.softmax(g_logits, axis=-1)
    g_idx = jnp.argmax(g_logits, axis=-1)
    g_p = jnp.take_along_axis(g_prob, g_idx[:, None], axis=1)[:, 0]
    e_logits = (xf @ w_exp).astype(jnp.float32).reshape(t, N_GROUPS, EXPERTS_PER_GROUP)
    e_sel = jnp.take_along_axis(e_logits, g_idx[:, None, None], axis=1)[:, 0]
    e_prob = jax.nn.softmax(e_sel, axis=-1)
    top_p, top_i = lax.top_k(e_prob, TOP_K)
    top_p = top_p / jnp.sum(top_p, axis=-1, keepdims=True)
    gates = (g_p[:, None] * top_p).reshape(-1)
    eid = (g_idx[:, None] * EXPERTS_PER_GROUP + top_i).reshape(-1).astype(jnp.int32)
    tok = jnp.repeat(jnp.arange(t, dtype=jnp.int32), TOP_K)
    n = t * TOP_K
    counts = jnp.zeros((N_EXPERTS,), jnp.int32).at[eid].add(1)
    padded = (counts + ROW_BLOCK - 1) // ROW_BLOCK * ROW_BLOCK
    pad_end = jnp.cumsum(padded)
    pad_start = pad_end - padded
    start = jnp.cumsum(counts) - counts
    order = jnp.argsort(eid, stable=True)
    se = eid[order]
    dest = pad_start[se] + jnp.arange(n, dtype=jnp.int32) - start[se]
    p_rows = n + N_EXPERTS * ROW_BLOCK
    nblk = p_rows // ROW_BLOCK
    row_tok = jnp.full((p_rows,), t, jnp.int32).at[dest].set(tok[order])
    row_gate = jnp.zeros((p_rows,), jnp.float32).at[dest].set(gates[order])
    blk_e = jnp.minimum(jnp.searchsorted(pad_end, jnp.arange(nblk, dtype=jnp.int32) * ROW_BLOCK,
                                         side='right'), N_EXPERTS - 1)
    xs = jnp.concatenate([xf, jnp.zeros((1, d), xf.dtype)], axis=0)[row_tok]
    xs = xs.reshape(nblk, ROW_BLOCK, d)

    def run_block(args):
        xb, e = args
        hid = jax.nn.silu(xb @ wg[e]) * (xb @ wu[e])
        return hid @ wd[e]

    ys = lax.map(run_block, (xs, blk_e)).reshape(p_rows, d)
    out = jax.ops.segment_sum(ys * row_gate[:, None].astype(ys.dtype), row_tok,
                              num_segments=t + 1)[:t]
    return out.reshape(bsz, s, d)


def setup_inputs(seed: int = 0) -> dict:
    key = jax.random.key(seed)
    ks = jax.random.split(key, 24)
    f32 = jnp.float32

    def nrm(k, shape, scale):
        return jax.random.normal(k, shape, f32) * scale

    return {
        'x': nrm(ks[0], (BATCH, SEQ, D_MODEL), 1.0),
        'rel_bias_table': nrm(ks[1], (N_BUCKETS, N_HEADS), 0.2),
        'mix_w_in': nrm(ks[2], (N_EVEN, D_MODEL, MIX_IN), D_MODEL ** -0.5),
        'gmlp_ln_g': 1.0 + nrm(ks[3], (N_EVEN, A_WIDTH), 0.02),
        'gmlp_ln_b': nrm(ks[4], (N_EVEN, A_WIDTH), 0.02),
        'gmlp_w_spatial': nrm(ks[5], (N_EVEN, A_GROUPS, CHUNK, CHUNK), 0.5 * CHUNK ** -0.5),
        'gmlp_b_spatial': 1.0 + nrm(ks[6], (N_EVEN, A_GROUPS, CHUNK), 0.02),
        'conv_w': nrm(ks[7], (N_EVEN, CONV_W, B_WIDTH), CONV_W ** -0.5),
        'mix_w_out': nrm(ks[8], (N_EVEN, MIX_OUT, D_MODEL), BETA * MIX_OUT ** -0.5),
        'attn_w_qkv': nrm(ks[9], (N_ODD, D_MODEL, QKV_DIM), D_MODEL ** -0.5),
        'attn_b_qkv': nrm(ks[10], (N_ODD, QKV_DIM), 0.02),
        'attn_sinks': nrm(ks[11], (N_ODD, N_HEADS), 0.5),
        'attn_w_o': nrm(ks[12], (N_ODD, ATT_OUT, D_MODEL), BETA * ATT_OUT ** -0.5),
        'attn_b_o': nrm(ks[13], (N_ODD, D_MODEL), 0.02),
        'ln1_g': 1.0 + nrm(ks[14], (DEPTH, D_MODEL), 0.02),
        'ln1_b': nrm(ks[15], (DEPTH, D_MODEL), 0.02),
        'ln2_g': 1.0 + nrm(ks[16], (DEPTH, D_MODEL), 0.02),
        'ln2_b': nrm(ks[17], (DEPTH, D_MODEL), 0.02),
        'router_group': nrm(ks[18], (DEPTH, D_MODEL, N_GROUPS), D_MODEL ** -0.5),
        'router_expert': nrm(ks[19], (DEPTH, D_MODEL, N_EXPERTS), D_MODEL ** -0.5),
        'expert_w_gate': nrm(ks[20], (DEPTH, N_EXPERTS, D_MODEL, D_EXPERT), D_MODEL ** -0.5),
        'expert_w_up': nrm(ks[21], (DEPTH, N_EXPERTS, D_MODEL, D_EXPERT), D_MODEL ** -0.5),
        'expert_w_down': nrm(ks[22], (DEPTH, N_EXPERTS, D_EXPERT, D_MODEL), BETA * D_EXPERT ** -0.5),
    }


def reference(x, rel_bias_table, mix_w_in, gmlp_ln_g, gmlp_ln_b, gmlp_w_spatial,
              gmlp_b_spatial, conv_w, mix_w_out, attn_w_qkv, attn_b_qkv, attn_sinks,
              attn_w_o, attn_b_o, ln1_g, ln1_b, ln2_g, ln2_b, router_group,
              router_expert, expert_w_gate, expert_w_up, expert_w_down):
    for l in range(DEPTH):
        i = l // 2
        if l % 2 == 0:
            m = gating_conv_mixer(x, mix_w_in[i], gmlp_ln_g[i], gmlp_ln_b[i],
                                  gmlp_w_spatial[i], gmlp_b_spatial[i], conv_w[i],
                                  mix_w_out[i])
        else:
            m = sliding_window_attention(x, attn_w_qkv[i], attn_b_qkv[i], attn_sinks[i],
                                         attn_w_o[i], attn_b_o[i], rel_bias_table)
        x = layer_norm(ALPHA * x + m, ln1_g[l], ln1_b[l])
        f = hierarchical_moe(x, router_group[l], router_expert[l], expert_w_gate[l],
                             expert_w_up[l], expert_w_down[l])
        x = layer_norm(ALPHA * x + f, ln2_g[l], ln2_b[l])
    return x
```

```python
import functools
import math

import jax
import jax.numpy as jnp
from jax import lax
from jax.experimental import pallas as pl
from jax.experimental.pallas import tpu as pltpu

D_MODEL = 1024
SEQ = 16384
DEPTH = 4
CHUNK = 128
A_GROUPS = 4
A_CH = 128
A_WIDTH = A_GROUPS * A_CH
B_WIDTH = 512
CONV_W = 3
MIX_IN = 2 * A_WIDTH + 3 * B_WIDTH
N_HEADS = 16
N_KV = 2
HEAD_DIM = 64
GQA_GROUP = N_HEADS // N_KV
WINDOW = 128
QKV_DIM = (N_HEADS + 2 * N_KV) * HEAD_DIM
ATT_OUT = N_HEADS * HEAD_DIM
KV_WIDTH = N_KV * HEAD_DIM
N_BUCKETS = 32
MAX_DISTANCE = 128
N_GROUPS = 4
EXPERTS_PER_GROUP = 8
N_EXPERTS = N_GROUPS * EXPERTS_PER_GROUP
TOP_K = 2
D_EXPERT = 512
ALPHA = (2 * DEPTH) ** 0.25
LN_EPS = 1e-5

LANES = 128
ROUTER_COLS = LANES
MIX_ROWS = 256
LN_ROWS = 512
EXPERT_ROWS = 256
VMEM_LIMIT = 56 * 1024 * 1024

F32 = jnp.float32
BF16 = jnp.bfloat16


def _layer_norm(x, g, b):
    mu = jnp.mean(x, axis=-1, keepdims=True)
    xc = x - mu
    var = jnp.mean(xc * xc, axis=-1, keepdims=True)
    return xc * lax.rsqrt(var + LN_EPS) * g + b


def _gelu(x):
    return 0.5 * x * (1.0 + lax.erf(x * (2.0 ** -0.5)))


def _router_logits(x1, wr_ref):
    return jnp.dot(x1, wr_ref[...], preferred_element_type=F32,
                   precision=lax.Precision.HIGHEST)


def _mixer_kernel(x_ref, win_ref, lng_ref, lnb_ref, wsp_ref, bsp_ref, cw_ref, wout_ref,
                  g1_ref, b1_ref, wr_ref, x1_ref, logit_ref, ztail_ref):
    tm = x_ref.shape[0]
    i = pl.program_id(0)

    @pl.when(i % (SEQ // tm) == 0)
    def _():
        ztail_ref[...] = jnp.zeros_like(ztail_ref)

    x = x_ref[...]
    h = jnp.dot(x.astype(BF16), win_ref[...], preferred_element_type=F32)
    u = _gelu(h[:, :A_WIDTH])
    v = _gelu(h[:, A_WIDTH:2 * A_WIDTH])
    o = 2 * A_WIDTH
    g_b = h[:, o:o + B_WIDTH]
    g_c = h[:, o + B_WIDTH:o + 2 * B_WIDTH]
    hb = h[:, o + 2 * B_WIDTH:o + 3 * B_WIDTH]

    v = _layer_norm(v, lng_ref[...], lnb_ref[...]).astype(BF16)
    n_chunks = tm // CHUNK
    ri = lax.broadcasted_iota(jnp.int32, (CHUNK, CHUNK), 0)
    ci = lax.broadcasted_iota(jnp.int32, (CHUNK, CHUNK), 1)
    causal = ci <= ri
    sv_cols = [[None] * A_GROUPS for _ in range(n_chunks)]
    for g in range(A_GROUPS):
        ws = jnp.where(causal, wsp_ref[g], 0.0).astype(BF16)
        vg = jnp.concatenate(
            [v[c * CHUNK:(c + 1) * CHUNK, g * A_CH:(g + 1) * A_CH] for c in range(n_chunks)],
            axis=1)
        sg = jnp.dot(ws, vg, preferred_element_type=F32) + bsp_ref[:, g:g + 1]
        for c in range(n_chunks):
            sv_cols[c][g] = sg[:, c * A_CH:(c + 1) * A_CH]
    sv = jnp.concatenate([jnp.concatenate(row, axis=1) for row in sv_cols], axis=0)
    y_a = u * sv

    z = g_c * hb
    rows = lax.broadcasted_iota(jnp.int32, z.shape, 0)
    tail = ztail_ref[...]
    zm1 = jnp.where(rows == 0, tail[7:8, :], pltpu.roll(z, 1, 0))
    zm2 = jnp.where(rows == 0, tail[6:7, :],
                    jnp.where(rows == 1, tail[7:8, :], pltpu.roll(z, 2, 0)))
    conv = cw_ref[0:1, :] * zm2 + cw_ref[1:2, :] * zm1 + cw_ref[2:3, :] * z
    y_b = g_b * conv
    ztail_ref[...] = z[tm - 8:tm, :]

    y = jnp.concatenate([y_a, y_b], axis=1).astype(BF16)
    m = jnp.dot(y, wout_ref[...], preferred_element_type=F32)
    x1 = _layer_norm(ALPHA * x + m, g1_ref[...], b1_ref[...])
    x1_ref[...] = x1
    logit_ref[...] = _router_logits(x1, wr_ref)


def _mixer_layer(x, w_in, ln_g, ln_b, w_sp, b_sp, conv_w, w_out, g1, b1, w_router):
    t = x.shape[0]
    tm = MIX_ROWS
    const2 = lambda i: (0, 0)
    return pl.pallas_call(
        _mixer_kernel,
        grid=(t // tm,),
        in_specs=[
            pl.BlockSpec((tm, D_MODEL), lambda i: (i, 0)),
            pl.BlockSpec((D_MODEL, MIX_IN), const2),
            pl.BlockSpec((1, A_WIDTH), const2),
            pl.BlockSpec((1, A_WIDTH), const2),
            pl.BlockSpec((A_GROUPS, CHUNK, CHUNK), lambda i: (0, 0, 0)),
            pl.BlockSpec((CHUNK, A_GROUPS), const2),
            pl.BlockSpec((CONV_W, B_WIDTH), const2),
            pl.BlockSpec((A_WIDTH + B_WIDTH, D_MODEL), const2),
            pl.BlockSpec((1, D_MODEL), const2),
            pl.BlockSpec((1, D_MODEL), const2),
            pl.BlockSpec((D_MODEL, ROUTER_COLS), const2),
        ],
        out_specs=[
            pl.BlockSpec((tm, D_MODEL), lambda i: (i, 0)),
            pl.BlockSpec((tm, ROUTER_COLS), lambda i: (i, 0)),
        ],
        out_shape=[
            jax.ShapeDtypeStruct((t, D_MODEL), F32),
            jax.ShapeDtypeStruct((t, ROUTER_COLS), F32),
        ],
        scratch_shapes=[pltpu.VMEM((8, B_WIDTH), F32)],
        compiler_params=pltpu.CompilerParams(
            dimension_semantics=("arbitrary",), vmem_limit_bytes=VMEM_LIMIT),
        name="mixer_layer",
    )(x, w_in.astype(BF16), ln_g.reshape(1, -1), ln_b.reshape(1, -1), w_sp, b_sp.T,
      conv_w, w_out.astype(BF16), g1.reshape(1, -1), b1.reshape(1, -1), w_router)


def _attn_kernel(sink_ref, x_ref, wqkv_ref, bqkv_ref, bias_ref, wo_ref, bo_ref,
                 g1_ref, b1_ref, wr_ref, x1_ref, logit_ref, kprev_ref, vprev_ref):
    i = pl.program_id(0)
    first = i % (SEQ // CHUNK) == 0

    @pl.when(first)
    def _():
        kprev_ref[...] = jnp.zeros_like(kprev_ref)
        vprev_ref[...] = jnp.zeros_like(vprev_ref)

    x = x_ref[...]
    qkv = jnp.dot(x.astype(BF16), wqkv_ref[...], preferred_element_type=F32) + bqkv_ref[...]
    q = (qkv[:, :ATT_OUT] * (HEAD_DIM ** -0.5)).astype(BF16)
    k = qkv[:, ATT_OUT:ATT_OUT + KV_WIDTH].astype(BF16)
    v = qkv[:, ATT_OUT + KV_WIDTH:].astype(BF16)
    kw = jnp.concatenate([kprev_ref[...], k], axis=0)
    vw = jnp.concatenate([vprev_ref[...], v], axis=0)
    kprev_ref[...] = k
    vprev_ref[...] = v

    a = lax.broadcasted_iota(jnp.int32, (CHUNK, 2 * CHUNK), 0)
    c = lax.broadcasted_iota(jnp.int32, (CHUNK, 2 * CHUNK), 1)
    lo = jnp.where(first, CHUNK, 0)
    mask = (c > a) & (c <= a + WINDOW) & (c >= lo)
    neg = jnp.finfo(F32).min

    outs = []
    for h in range(N_HEADS):
        j = h // GQA_GROUP
        qh = q[:, h * HEAD_DIM:(h + 1) * HEAD_DIM]
        kj = kw[:, j * HEAD_DIM:(j + 1) * HEAD_DIM]
        vj = vw[:, j * HEAD_DIM:(j + 1) * HEAD_DIM]
        sc = lax.dot_general(qh, kj, (((1,), (1,)), ((), ())), preferred_element_type=F32)
        sc = jnp.where(mask, sc + bias_ref[h], neg)
        sink = sink_ref[h]
        m = jnp.maximum(jnp.max(sc, axis=-1, keepdims=True), sink)
        p = jnp.exp(sc - m)
        denom = jnp.sum(p, axis=-1, keepdims=True) + jnp.exp(sink - m)
        probs = (p / denom).astype(BF16)
        outs.append(jnp.dot(probs, vj, preferred_element_type=F32))
    o = jnp.concatenate(outs, axis=1).astype(BF16)
    m_out = jnp.dot(o, wo_ref[...], preferred_element_type=F32) + bo_ref[...]
    x1 = _layer_norm(ALPHA * x + m_out, g1_ref[...], b1_ref[...])
    x1_ref[...] = x1
    logit_ref[...] = _router_logits(x1, wr_ref)


def _t5_bucket(rel):
    n = jnp.maximum(rel, 0)
    max_exact = N_BUCKETS // 2
    nf = jnp.maximum(n, 1).astype(F32)
    large = max_exact + (jnp.log(nf / max_exact) / math.log(MAX_DISTANCE / max_exact)
                         * (N_BUCKETS - max_exact)).astype(jnp.int32)
    large = jnp.minimum(large, N_BUCKETS - 1)
    return jnp.where(n < max_exact, n, large)


def _rel_bias(rel_table):
    a = jnp.arange(CHUNK)[:, None]
    c = jnp.arange(2 * CHUNK)[None, :]
    bias = rel_table[_t5_bucket(a + CHUNK - c)].astype(F32)
    return jnp.transpose(bias, (2, 0, 1))


def _attn_layer(x, bias, w_qkv, b_qkv, sinks, w_o, b_o, g1, b1, w_router):
    t = x.shape[0]
    const2 = lambda i: (0, 0)
    return pl.pallas_call(
        _attn_kernel,
        grid=(t // CHUNK,),
        in_specs=[
            pl.BlockSpec(memory_space=pltpu.SMEM),
            pl.BlockSpec((CHUNK, D_MODEL), lambda i: (i, 0)),
            pl.BlockSpec((D_MODEL, QKV_DIM), const2),
            pl.BlockSpec((1, QKV_DIM), const2),
            pl.BlockSpec((N_HEADS, CHUNK, 2 * CHUNK), lambda i: (0, 0, 0)),
            pl.BlockSpec((ATT_OUT, D_MODEL), const2),
            pl.BlockSpec((1, D_MODEL), const2),
            pl.BlockSpec((1, D_MODEL), const2),
            pl.BlockSpec((1, D_MODEL), const2),
            pl.BlockSpec((D_MODEL, ROUTER_COLS), const2),
        ],
        out_specs=[
            pl.BlockSpec((CHUNK, D_MODEL), lambda i: (i, 0)),
            pl.BlockSpec((CHUNK, ROUTER_COLS), lambda i: (i, 0)),
        ],
        out_shape=[
            jax.ShapeDtypeStruct((t, D_MODEL), F32),
            jax.ShapeDtypeStruct((t, ROUTER_COLS), F32),
        ],
        scratch_shapes=[pltpu.VMEM((CHUNK, KV_WIDTH), BF16), pltpu.VMEM((CHUNK, KV_WIDTH), BF16)],
        compiler_params=pltpu.CompilerParams(
            dimension_semantics=("arbitrary",), vmem_limit_bytes=VMEM_LIMIT),
        name="attn_layer",
    )(sinks, x, w_qkv.astype(BF16), b_qkv.reshape(1, -1), bias, w_o.astype(BF16),
      b_o.reshape(1, -1), g1.reshape(1, -1), b1.reshape(1, -1), w_router)


def _expert_kernel(blk_e_ref, n_used_ref, xs_ref, wg_ref, wu_ref, wd_ref, ys_ref,
                   wg_bf, wu_bf, wd_bf):
    i = pl.program_id(0)
    prev = blk_e_ref[jnp.maximum(i - 1, 0)]
    changed = (i == 0) | (blk_e_ref[i] != prev)

    @pl.when(changed)
    def _():
        wg_bf[...] = wg_ref[0].astype(BF16)
        wu_bf[...] = wu_ref[0].astype(BF16)
        wd_bf[...] = wd_ref[0].astype(BF16)

    @pl.when(i < n_used_ref[0])
    def _():
        xb = xs_ref[...]
        gate = jnp.dot(xb, wg_bf[...], preferred_element_type=F32)
        up = jnp.dot(xb, wu_bf[...], preferred_element_type=F32)
        hid = (gate * jax.nn.sigmoid(gate) * up).astype(BF16)
        ys_ref[...] = jnp.dot(hid, wd_bf[...], preferred_element_type=F32)


def _expert_ffn(xs, blk_e, n_used, wg, wu, wd):
    p_rows = xs.shape[0]
    rb = EXPERT_ROWS
    row_map = lambda i, be, nu: (jnp.minimum(i, nu[0] - 1), 0)
    w_map = lambda i, be, nu: (be[i], 0, 0)
    grid_spec = pltpu.PrefetchScalarGridSpec(
        num_scalar_prefetch=2,
        grid=(p_rows // rb,),
        in_specs=[
            pl.BlockSpec((rb, D_MODEL), row_map),
            pl.BlockSpec((1, D_MODEL, D_EXPERT), w_map),
            pl.BlockSpec((1, D_MODEL, D_EXPERT), w_map),
            pl.BlockSpec((1, D_EXPERT, D_MODEL), w_map),
        ],
        out_specs=pl.BlockSpec((rb, D_MODEL), row_map),
        scratch_shapes=[
            pltpu.VMEM((D_MODEL, D_EXPERT), BF16),
            pltpu.VMEM((D_MODEL, D_EXPERT), BF16),
            pltpu.VMEM((D_EXPERT, D_MODEL), BF16),
        ],
    )
    return pl.pallas_call(
        _expert_kernel,
        grid_spec=grid_spec,
        out_shape=jax.ShapeDtypeStruct((p_rows, D_MODEL), F32),
        compiler_params=pltpu.CompilerParams(
            dimension_semantics=("arbitrary",), vmem_limit_bytes=VMEM_LIMIT),
        name="expert_ffn",
    )(blk_e, n_used, xs, wg, wu, wd)


def _combine_ln_kernel(x1_ref, y0_ref, y1_ref, gate_ref, g2_ref, b2_ref, x2_ref):
    gates = gate_ref[...]
    f = gates[:, 0:1] * y0_ref[...] + gates[:, 1:2] * y1_ref[...]
    x2_ref[...] = _layer_norm(ALPHA * x1_ref[...] + f, g2_ref[...], b2_ref[...])


def _combine_ln(x1, y0, y1, gates, g2, b2):
    t = x1.shape[0]
    tm = LN_ROWS
    row = lambda i: (i, 0)
    const2 = lambda i: (0, 0)
    return pl.pallas_call(
        _combine_ln_kernel,
        grid=(t // tm,),
        in_specs=[
            pl.BlockSpec((tm, D_MODEL), row),
            pl.BlockSpec((tm, D_MODEL), row),
            pl.BlockSpec((tm, D_MODEL), row),
            pl.BlockSpec((tm, LANES), row),
            pl.BlockSpec((1, D_MODEL), const2),
            pl.BlockSpec((1, D_MODEL), const2),
        ],
        out_specs=pl.BlockSpec((tm, D_MODEL), row),
        out_shape=jax.ShapeDtypeStruct((t, D_MODEL), F32),
        compiler_params=pltpu.CompilerParams(
            dimension_semantics=("arbitrary",), vmem_limit_bytes=VMEM_LIMIT),
        name="combine_ln",
    )(x1, y0, y1, gates, g2.reshape(1, -1), b2.reshape(1, -1))


def _route(logits):
    t = logits.shape[0]
    g_logits = logits[:, :N_GROUPS]
    g_prob = jax.nn.softmax(g_logits, axis=-1)
    g_idx = jnp.argmax(g_logits, axis=-1)
    g_p = jnp.take_along_axis(g_prob, g_idx[:, None], axis=1)[:, 0]
    e_logits = logits[:, N_GROUPS:N_GROUPS + N_EXPERTS].reshape(t, N_GROUPS, EXPERTS_PER_GROUP)
    e_sel = jnp.take_along_axis(e_logits, g_idx[:, None, None], axis=1)[:, 0]
    e_prob = jax.nn.softmax(e_sel, axis=-1)
    top_p, top_i = lax.top_k(e_prob, TOP_K)
    top_p = top_p / jnp.sum(top_p, axis=-1, keepdims=True)
    gates = g_p[:, None] * top_p
    eid = (g_idx[:, None] * EXPERTS_PER_GROUP + top_i).reshape(-1).astype(jnp.int32)
    n = t * TOP_K
    rb = EXPERT_ROWS
    counts = jnp.zeros((N_EXPERTS,), jnp.int32).at[eid].add(1)
    padded = (counts + rb - 1) // rb * rb
    pad_end = jnp.cumsum(padded)
    pad_start = pad_end - padded
    start = jnp.cumsum(counts) - counts
    order = jnp.argsort(eid, stable=True)
    se = eid[order]
    dest_sorted = pad_start[se] + jnp.arange(n, dtype=jnp.int32) - start[se]
    p_rows = n + N_EXPERTS * rb
    nblk = p_rows // rb
    tok = jnp.repeat(jnp.arange(t, dtype=jnp.int32), TOP_K)
    row_tok = jnp.full((p_rows,), t, jnp.int32).at[dest_sorted].set(tok[order])
    dest = jnp.zeros((n,), jnp.int32).at[order].set(dest_sorted).reshape(t, TOP_K)
    n_used = (pad_end[-1] // rb).astype(jnp.int32)
    blk = jnp.minimum(jnp.arange(nblk, dtype=jnp.int32), n_used - 1)
    blk_e = jnp.minimum(jnp.searchsorted(pad_end, blk * rb, side='right'),
                        N_EXPERTS - 1).astype(jnp.int32)
    return gates, dest, row_tok, blk_e, n_used.reshape(1)


def _moe_layer(x1, logits, wg, wu, wd, g2, b2):
    t = x1.shape[0]
    gates, dest, row_tok, blk_e, n_used = _route(logits)
    xs = jnp.concatenate([x1.astype(BF16), jnp.zeros((1, D_MODEL), BF16)], axis=0)[row_tok]
    ys = _expert_ffn(xs, blk_e, n_used, wg, wu, wd)
    gate_cols = jnp.zeros((t, LANES), F32).at[:, :TOP_K].set(gates)
    return _combine_ln(x1, ys[dest[:, 0]], ys[dest[:, 1]], gate_cols, g2, b2)


def kernel(x, rel_bias_table, mix_w_in, gmlp_ln_g, gmlp_ln_b, gmlp_w_spatial, gmlp_b_spatial, conv_w, mix_w_out, attn_w_qkv, attn_b_qkv, attn_sinks, attn_w_o, attn_b_o, ln1_g, ln1_b, ln2_g, ln2_b, router_group, router_expert, expert_w_gate, expert_w_up, expert_w_down):
    bsz, s, d = x.shape
    assert (s, d) == (SEQ, D_MODEL)
    x = x.reshape(bsz * s, d)
    bias = _rel_bias(rel_bias_table)
    for l in range(DEPTH):
        i = l // 2
        w_router = jnp.zeros((D_MODEL, ROUTER_COLS), F32)
        w_router = w_router.at[:, :N_GROUPS].set(router_group[l])
        w_router = w_router.at[:, N_GROUPS:N_GROUPS + N_EXPERTS].set(router_expert[l])
        if l % 2 == 0:
            x1, logits = _mixer_layer(x, mix_w_in[i], gmlp_ln_g[i], gmlp_ln_b[i],
                                      gmlp_w_spatial[i], gmlp_b_spatial[i], conv_w[i],
                                      mix_w_out[i], ln1_g[l], ln1_b[l], w_router)
        else:
            x1, logits = _attn_layer(x, bias, attn_w_qkv[i], attn_b_qkv[i], attn_sinks[i],
                                     attn_w_o[i], attn_b_o[i], ln1_g[l], ln1_b[l], w_router)
        x = _moe_layer(x1, logits, expert_w_gate[l], expert_w_up[l], expert_w_down[l],
                       ln2_g[l], ln2_b[l])
    return x.reshape(bsz, s, d)
```

```python
import math

import jax
import jax.numpy as jnp
from jax import lax
from jax.experimental import pallas as pl
from jax.experimental.pallas import tpu as pltpu

D_MODEL = 1024
SEQ = 16384
DEPTH = 4
CHUNK = 128
A_GROUPS = 4
A_CH = 128
A_WIDTH = A_GROUPS * A_CH
B_WIDTH = 512
CONV_W = 3
MIX_IN = 2 * A_WIDTH + 3 * B_WIDTH
N_HEADS = 16
N_KV = 2
HEAD_DIM = 64
GQA_GROUP = N_HEADS // N_KV
WINDOW = 128
QKV_DIM = (N_HEADS + 2 * N_KV) * HEAD_DIM
ATT_OUT = N_HEADS * HEAD_DIM
KV_WIDTH = N_KV * HEAD_DIM
N_BUCKETS = 32
MAX_DISTANCE = 128
N_GROUPS = 4
EXPERTS_PER_GROUP = 8
N_EXPERTS = N_GROUPS * EXPERTS_PER_GROUP
TOP_K = 2
D_EXPERT = 512
ALPHA = (2 * DEPTH) ** 0.25
LN_EPS = 1e-5

LANES = 128
ROUTER_COLS = LANES
MIX_ROWS = 256
ROUTE_ROWS = 512
MOVE_ROWS = 256
EXPERT_ROWS = 256
VMEM_LIMIT = 56 * 1024 * 1024

R_E0, R_E1, R_RANK0, R_RANK1, R_GATE0, R_GATE1 = range(6)

F32 = jnp.float32
BF16 = jnp.bfloat16
I32 = jnp.int32


def _layer_norm(x, g, b):
    mu = jnp.mean(x, axis=-1, keepdims=True)
    xc = x - mu
    var = jnp.mean(xc * xc, axis=-1, keepdims=True)
    return xc * lax.rsqrt(var + LN_EPS) * g + b


def _gelu(x):
    return 0.5 * x * (1.0 + lax.erf(x * (2.0 ** -0.5)))


def _router_logits(x1, wr_ref):
    return jnp.dot(x1, wr_ref[...], preferred_element_type=F32,
                   precision=lax.Precision.HIGHEST)


def _mixer_kernel(x_ref, win_ref, lng_ref, lnb_ref, wsp_ref, bsp_ref, cw_ref, wout_ref,
                  g1_ref, b1_ref, wr_ref, x1_ref, logit_ref, ztail_ref):
    tm = x_ref.shape[0]
    i = pl.program_id(0)

    @pl.when(i % (SEQ // tm) == 0)
    def _():
        ztail_ref[...] = jnp.zeros_like(ztail_ref)

    x = x_ref[...]
    h = jnp.dot(x.astype(BF16), win_ref[...], preferred_element_type=F32)
    u = _gelu(h[:, :A_WIDTH])
    v = _gelu(h[:, A_WIDTH:2 * A_WIDTH])
    o = 2 * A_WIDTH
    g_b = h[:, o:o + B_WIDTH]
    g_c = h[:, o + B_WIDTH:o + 2 * B_WIDTH]
    hb = h[:, o + 2 * B_WIDTH:o + 3 * B_WIDTH]

    v = _layer_norm(v, lng_ref[...], lnb_ref[...]).astype(BF16)
    n_chunks = tm // CHUNK
    ri = lax.broadcasted_iota(I32, (CHUNK, CHUNK), 0)
    ci = lax.broadcasted_iota(I32, (CHUNK, CHUNK), 1)
    causal = ci <= ri
    sv_cols = [[None] * A_GROUPS for _ in range(n_chunks)]
    for g in range(A_GROUPS):
        ws = jnp.where(causal, wsp_ref[g], 0.0).astype(BF16)
        vg = jnp.concatenate(
            [v[c * CHUNK:(c + 1) * CHUNK, g * A_CH:(g + 1) * A_CH] for c in range(n_chunks)],
            axis=1)
        sg = jnp.dot(ws, vg, preferred_element_type=F32) + bsp_ref[:, g:g + 1]
        for c in range(n_chunks):
            sv_cols[c][g] = sg[:, c * A_CH:(c + 1) * A_CH]
    sv = jnp.concatenate([jnp.concatenate(row, axis=1) for row in sv_cols], axis=0)
    y_a = u * sv

    z = g_c * hb
    rows = lax.broadcasted_iota(I32, z.shape, 0)
    tail = ztail_ref[...]
    zm1 = jnp.where(rows == 0, tail[7:8, :], pltpu.roll(z, 1, 0))
    zm2 = jnp.where(rows == 0, tail[6:7, :],
                    jnp.where(rows == 1, tail[7:8, :], pltpu.roll(z, 2, 0)))
    conv = cw_ref[0:1, :] * zm2 + cw_ref[1:2, :] * zm1 + cw_ref[2:3, :] * z
    y_b = g_b * conv
    ztail_ref[...] = z[tm - 8:tm, :]

    y = jnp.concatenate([y_a, y_b], axis=1).astype(BF16)
    m = jnp.dot(y, wout_ref[...], preferred_element_type=F32)
    x1 = _layer_norm(ALPHA * x + m, g1_ref[...], b1_ref[...])
    x1_ref[...] = x1
    logit_ref[...] = _router_logits(x1, wr_ref)


def _mixer_layer(x, w_in, ln_g, ln_b, w_sp, b_sp, conv_w, w_out, g1, b1, w_router):
    t = x.shape[0]
    tm = MIX_ROWS
    const2 = lambda i: (0, 0)
    return pl.pallas_call(
        _mixer_kernel,
        grid=(t // tm,),
        in_specs=[
            pl.BlockSpec((tm, D_MODEL), lambda i: (i, 0)),
            pl.BlockSpec((D_MODEL, MIX_IN), const2),
            pl.BlockSpec((1, A_WIDTH), const2),
            pl.BlockSpec((1, A_WIDTH), const2),
            pl.BlockSpec((A_GROUPS, CHUNK, CHUNK), lambda i: (0, 0, 0)),
            pl.BlockSpec((CHUNK, A_GROUPS), const2),
            pl.BlockSpec((CONV_W, B_WIDTH), const2),
            pl.BlockSpec((A_WIDTH + B_WIDTH, D_MODEL), const2),
            pl.BlockSpec((1, D_MODEL), const2),
            pl.BlockSpec((1, D_MODEL), const2),
            pl.BlockSpec((D_MODEL, ROUTER_COLS), const2),
        ],
        out_specs=[
            pl.BlockSpec((tm, D_MODEL), lambda i: (i, 0)),
            pl.BlockSpec((tm, ROUTER_COLS), lambda i: (i, 0)),
        ],
        out_shape=[
            jax.ShapeDtypeStruct((t, D_MODEL), F32),
            jax.ShapeDtypeStruct((t, ROUTER_COLS), F32),
        ],
        scratch_shapes=[pltpu.VMEM((8, B_WIDTH), F32)],
        compiler_params=pltpu.CompilerParams(
            dimension_semantics=("arbitrary",), vmem_limit_bytes=VMEM_LIMIT),
        name="mixer_layer",
    )(x, w_in.astype(BF16), ln_g.reshape(1, -1), ln_b.reshape(1, -1), w_sp, b_sp.T,
      conv_w, w_out.astype(BF16), g1.reshape(1, -1), b1.reshape(1, -1), w_router)


def _attn_kernel(sink_ref, x_ref, wqkv_ref, bqkv_ref, bias_ref, wo_ref, bo_ref,
                 g1_ref, b1_ref, wr_ref, x1_ref, logit_ref, kprev_ref, vprev_ref):
    i = pl.program_id(0)
    first = i % (SEQ // CHUNK) == 0

    @pl.when(first)
    def _():
        kprev_ref[...] = jnp.zeros_like(kprev_ref)
        vprev_ref[...] = jnp.zeros_like(vprev_ref)

    x = x_ref[...]
    qkv = jnp.dot(x.astype(BF16), wqkv_ref[...], preferred_element_type=F32) + bqkv_ref[...]
    q = (qkv[:, :ATT_OUT] * (HEAD_DIM ** -0.5)).astype(BF16)
    k = qkv[:, ATT_OUT:ATT_OUT + KV_WIDTH].astype(BF16)
    v = qkv[:, ATT_OUT + KV_WIDTH:].astype(BF16)
    kw = jnp.concatenate([kprev_ref[...], k], axis=0)
    vw = jnp.concatenate([vprev_ref[...], v], axis=0)
    kprev_ref[...] = k
    vprev_ref[...] = v

    a = lax.broadcasted_iota(I32, (CHUNK, 2 * CHUNK), 0)
    c = lax.broadcasted_iota(I32, (CHUNK, 2 * CHUNK), 1)
    lo = jnp.where(first, CHUNK, 0)
    mask = (c > a) & (c <= a + WINDOW) & (c >= lo)
    neg = jnp.finfo(F32).min

    outs = []
    for h in range(N_HEADS):
        j = h // GQA_GROUP
        qh = q[:, h * HEAD_DIM:(h + 1) * HEAD_DIM]
        kj = kw[:, j * HEAD_DIM:(j + 1) * HEAD_DIM]
        vj = vw[:, j * HEAD_DIM:(j + 1) * HEAD_DIM]
        sc = lax.dot_general(qh, kj, (((1,), (1,)), ((), ())), preferred_element_type=F32)
        sc = jnp.where(mask, sc + bias_ref[h], neg)
        sink = sink_ref[h]
        m = jnp.maximum(jnp.max(sc, axis=-1, keepdims=True), sink)
        p = jnp.exp(sc - m)
        denom = jnp.sum(p, axis=-1, keepdims=True) + jnp.exp(sink - m)
        probs = (p / denom).astype(BF16)
        outs.append(jnp.dot(probs, vj, preferred_element_type=F32))
    o = jnp.concatenate(outs, axis=1).astype(BF16)
    m_out = jnp.dot(o, wo_ref[...], preferred_element_type=F32) + bo_ref[...]
    x1 = _layer_norm(ALPHA * x + m_out, g1_ref[...], b1_ref[...])
    x1_ref[...] = x1
    logit_ref[...] = _router_logits(x1, wr_ref)


def _t5_bucket(rel):
    n = jnp.maximum(rel, 0)
    max_exact = N_BUCKETS // 2
    nf = jnp.maximum(n, 1).astype(F32)
    large = max_exact + (jnp.log(nf / max_exact) / math.log(MAX_DISTANCE / max_exact)
                         * (N_BUCKETS - max_exact)).astype(I32)
    large = jnp.minimum(large, N_BUCKETS - 1)
    return jnp.where(n < max_exact, n, large)


def _rel_bias(rel_table):
    a = jnp.arange(CHUNK)[:, None]
    c = jnp.arange(2 * CHUNK)[None, :]
    bias = rel_table[_t5_bucket(a + CHUNK - c)].astype(F32)
    return jnp.transpose(bias, (2, 0, 1))


def _attn_layer(x, bias, w_qkv, b_qkv, sinks, w_o, b_o, g1, b1, w_router):
    t = x.shape[0]
    const2 = lambda i: (0, 0)
    return pl.pallas_call(
        _attn_kernel,
        grid=(t // CHUNK,),
        in_specs=[
            pl.BlockSpec(memory_space=pltpu.SMEM),
            pl.BlockSpec((CHUNK, D_MODEL), lambda i: (i, 0)),
            pl.BlockSpec((D_MODEL, QKV_DIM), const2),
            pl.BlockSpec((1, QKV_DIM), const2),
            pl.BlockSpec((N_HEADS, CHUNK, 2 * CHUNK), lambda i: (0, 0, 0)),
            pl.BlockSpec((ATT_OUT, D_MODEL), const2),
            pl.BlockSpec((1, D_MODEL), const2),
            pl.BlockSpec((1, D_MODEL), const2),
            pl.BlockSpec((1, D_MODEL), const2),
            pl.BlockSpec((D_MODEL, ROUTER_COLS), const2),
        ],
        out_specs=[
            pl.BlockSpec((CHUNK, D_MODEL), lambda i: (i, 0)),
            pl.BlockSpec((CHUNK, ROUTER_COLS), lambda i: (i, 0)),
        ],
        out_shape=[
            jax.ShapeDtypeStruct((t, D_MODEL), F32),
            jax.ShapeDtypeStruct((t, ROUTER_COLS), F32),
        ],
        scratch_shapes=[pltpu.VMEM((CHUNK, KV_WIDTH), BF16), pltpu.VMEM((CHUNK, KV_WIDTH), BF16)],
        compiler_params=pltpu.CompilerParams(
            dimension_semantics=("arbitrary",), vmem_limit_bytes=VMEM_LIMIT),
        name="attn_layer",
    )(sinks, x, w_qkv.astype(BF16), b_qkv.reshape(1, -1), bias, w_o.astype(BF16),
      b_o.reshape(1, -1), g1.reshape(1, -1), b1.reshape(1, -1), w_router)


def _route_kernel(logit_ref, route_ref, count_ref, base_ref):
    tm = logit_ref.shape[0]
    i = pl.program_id(0)

    @pl.when(i == 0)
    def _():
        base_ref[...] = jnp.zeros_like(base_ref)

    lg = logit_ref[...]
    lane = lax.broadcasted_iota(I32, lg.shape, 1)
    ninf = -jnp.inf

    def first_argmax(vals):
        m = jnp.max(vals, axis=-1, keepdims=True)
        idx = jnp.min(jnp.where(vals == m, lane, LANES), axis=-1, keepdims=True)
        return m, idx

    is_g = lane < N_GROUPS
    gmax, g_idx = first_argmax(jnp.where(is_g, lg, ninf))
    g_p = 1.0 / jnp.sum(jnp.where(is_g, jnp.exp(lg - gmax), 0.0), axis=-1, keepdims=True)

    e_lo = N_GROUPS + g_idx * EXPERTS_PER_GROUP
    in_group = (lane >= e_lo) & (lane < e_lo + EXPERTS_PER_GROUP)
    el = jnp.where(in_group, lg, ninf)
    m1, i1 = first_argmax(el)
    m2, i2 = first_argmax(jnp.where(lane == i1, ninf, el))
    a2 = jnp.exp(m2 - m1)
    gate0 = g_p / (1.0 + a2)
    gate1 = g_p * a2 / (1.0 + a2)

    hit0 = lane == i1
    hit1 = lane == i2
    onehot = jnp.where(hit0 | hit1, 1.0, 0.0)
    r = lax.broadcasted_iota(I32, (tm, tm), 0)
    c = lax.broadcasted_iota(I32, (tm, tm), 1)
    strict_lower = jnp.where(c < r, 1.0, 0.0).astype(BF16)
    before = jnp.dot(strict_lower, onehot.astype(BF16), preferred_element_type=F32)
    before = before + base_ref[...]
    rank0 = jnp.sum(jnp.where(hit0, before, 0.0), axis=-1, keepdims=True)
    rank1 = jnp.sum(jnp.where(hit1, before, 0.0), axis=-1, keepdims=True)
    base_ref[...] += jnp.sum(onehot, axis=0, keepdims=True)

    rec = jnp.zeros(lg.shape, F32)
    for col, val in ((R_E0, (i1 - N_GROUPS).astype(F32)), (R_E1, (i2 - N_GROUPS).astype(F32)),
                     (R_RANK0, rank0), (R_RANK1, rank1), (R_GATE0, gate0), (R_GATE1, gate1)):
        rec = jnp.where(lane == col, val, rec)
    route_ref[...] = rec
    count_ref[...] = base_ref[...]


def _route(logits):
    t = logits.shape[0]
    tm = ROUTE_ROWS
    return pl.pallas_call(
        _route_kernel,
        grid=(t // tm,),
        in_specs=[pl.BlockSpec((tm, ROUTER_COLS), lambda i: (i, 0))],
        out_specs=[
            pl.BlockSpec((tm, LANES), lambda i: (i, 0)),
            pl.BlockSpec((1, LANES), lambda i: (0, 0)),
        ],
        out_shape=[
            jax.ShapeDtypeStruct((t, LANES), F32),
            jax.ShapeDtypeStruct((1, LANES), F32),
        ],
        scratch_shapes=[pltpu.VMEM((1, LANES), F32)],
        compiler_params=pltpu.CompilerParams(
            dimension_semantics=("arbitrary",), vmem_limit_bytes=VMEM_LIMIT),
        name="route",
    )(logits)


def _plan(route, count_lanes):
    t = route.shape[0]
    rb = EXPERT_ROWS
    counts = count_lanes[0, N_GROUPS:N_GROUPS + N_EXPERTS].astype(I32)
    end = jnp.cumsum(counts)
    start = end - counts
    e = route[:, R_E0:R_E1 + 1].astype(I32)
    rank = route[:, R_RANK0:R_RANK1 + 1].astype(I32)
    onehot = e[:, :, None] == jnp.arange(N_EXPERTS, dtype=I32)
    dest = jnp.sum(jnp.where(onehot, start, 0), axis=-1) + rank

    first_blk = start // rb
    last_blk = jnp.maximum(end - 1, 0) // rb
    n_steps_e = jnp.where(counts > 0, last_blk - first_blk + 1, 0)
    step_end = jnp.cumsum(n_steps_e)
    step_start = step_end - n_steps_e
    n_steps = step_end[-1]
    max_steps = t * TOP_K // rb + N_EXPERTS - 1
    s = jnp.minimum(jnp.arange(max_steps, dtype=I32), n_steps - 1)
    step_e = jnp.minimum(jnp.searchsorted(step_end, s, side='right'), N_EXPERTS - 1).astype(I32)
    step_blk = first_blk[step_e] + s - step_start[step_e]
    step_lo = jnp.clip(start[step_e] - step_blk * rb, 0, rb)
    step_hi = jnp.clip(end[step_e] - step_blk * rb, 0, rb)
    return dest, (step_blk, step_e, step_lo, step_hi, n_steps.reshape(1))


def _dest_blocks(dest, tm):
    t = dest.shape[0]
    return dest.reshape(t // tm, tm, TOP_K).transpose(0, 2, 1).reshape(t // tm, 1, TOP_K * tm)


def _row_copy(src_ref, src_row, dst_ref, dst_row, sem):
    return pltpu.make_async_copy(src_ref.at[pl.ds(src_row, 1)], dst_ref.at[pl.ds(dst_row, 1)], sem)


def _dispatch_kernel(dest_ref, x1_ref, xs_hbm, sem):
    tm = x1_ref.shape[0]

    def issue(r, carry):
        for k in range(TOP_K):
            _row_copy(x1_ref, r, xs_hbm, dest_ref[0, 0, k * tm + r], sem).start()
        return carry

    lax.fori_loop(0, tm, issue, 0)
    for k in range(TOP_K):
        pltpu.make_async_copy(x1_ref, xs_hbm.at[pl.ds(0, tm)], sem).wait()


def _dispatch(x1, dest):
    t = x1.shape[0]
    tm = MOVE_ROWS
    return pl.pallas_call(
        _dispatch_kernel,
        grid=(t // tm,),
        in_specs=[
            pl.BlockSpec((1, 1, TOP_K * tm), lambda i: (i, 0, 0), memory_space=pltpu.SMEM),
            pl.BlockSpec((tm, D_MODEL), lambda i: (i, 0)),
        ],
        out_specs=pl.BlockSpec(memory_space=pl.ANY),
        out_shape=jax.ShapeDtypeStruct((t * TOP_K, D_MODEL), F32),
        scratch_shapes=[pltpu.SemaphoreType.DMA],
        compiler_params=pltpu.CompilerParams(
            dimension_semantics=("arbitrary",), vmem_limit_bytes=VMEM_LIMIT),
        name="dispatch",
    )(_dest_blocks(dest, tm), x1)


def _expert_kernel(blk_ref, e_ref, lo_ref, hi_ref, n_steps_ref, xs_ref, wg_ref, wu_ref, wd_ref,
                   ys_ref, wg_bf, wu_bf, wd_bf):
    s = pl.program_id(0)
    prev = jnp.maximum(s - 1, 0)
    new_expert = (s == 0) | (e_ref[s] != e_ref[prev])
    new_block = (s == 0) | (blk_ref[s] != blk_ref[prev])

    @pl.when(new_expert)
    def _():
        wg_bf[...] = wg_ref[0, 0].astype(BF16)
        wu_bf[...] = wu_ref[0, 0].astype(BF16)
        wd_bf[...] = wd_ref[0, 0].astype(BF16)

    @pl.when(s < n_steps_ref[0])
    def _():
        xb = xs_ref[...].astype(BF16)
        gate = jnp.dot(xb, wg_bf[...], preferred_element_type=F32)
        up = jnp.dot(xb, wu_bf[...], preferred_element_type=F32)
        hid = (gate * jax.nn.sigmoid(gate) * up).astype(BF16)
        y = jnp.dot(hid, wd_bf[...], preferred_element_type=F32)
        rows = lax.broadcasted_iota(I32, y.shape, 0)
        y = jnp.where((rows >= lo_ref[s]) & (rows < hi_ref[s]), y, 0.0)

        @pl.when(new_block)
        def _():
            ys_ref[...] = y

        @pl.when(jnp.logical_not(new_block))
        def _():
            ys_ref[...] += y


def _expert_ffn(xs, steps, layer, wg, wu, wd):
    step_blk, step_e, step_lo, step_hi, n_steps = steps
    rb = EXPERT_ROWS
    row_map = lambda s, blk, e, lo, hi, n: (blk[s], 0)
    w_map = lambda s, blk, e, lo, hi, n: (layer, e[s], 0, 0)
    grid_spec = pltpu.PrefetchScalarGridSpec(
        num_scalar_prefetch=5,
        grid=(step_blk.shape[0],),
        in_specs=[
            pl.BlockSpec((rb, D_MODEL), row_map),
            pl.BlockSpec((1, 1, D_MODEL, D_EXPERT), w_map),
            pl.BlockSpec((1, 1, D_MODEL, D_EXPERT), w_map),
            pl.BlockSpec((1, 1, D_EXPERT, D_MODEL), w_map),
        ],
        out_specs=pl.BlockSpec((rb, D_MODEL), row_map),
        scratch_shapes=[
            pltpu.VMEM((D_MODEL, D_EXPERT), BF16),
            pltpu.VMEM((D_MODEL, D_EXPERT), BF16),
            pltpu.VMEM((D_EXPERT, D_MODEL), BF16),
        ],
    )
    return pl.pallas_call(
        _expert_kernel,
        grid_spec=grid_spec,
        out_shape=jax.ShapeDtypeStruct(xs.shape, F32),
        compiler_params=pltpu.CompilerParams(
            dimension_semantics=("arbitrary",), vmem_limit_bytes=VMEM_LIMIT),
        name="expert_ffn",
    )(step_blk, step_e, step_lo, step_hi, n_steps, xs, wg, wu, wd)


def _combine_kernel(dest_ref, dest_next_ref, x1_ref, route_ref, g2_ref, b2_ref, ys_hbm,
                    x2_ref, ybuf, sems):
    tm = x1_ref.shape[0]
    i = pl.program_id(0)
    n = pl.num_programs(0)
    slot = i % 2

    def gather(d_ref, to_slot):
        def issue(r, carry):
            for k in range(TOP_K):
                _row_copy(ys_hbm, d_ref[0, 0, k * tm + r], ybuf.at[to_slot, k], r,
                          sems.at[to_slot]).start()
            return carry
        lax.fori_loop(0, tm, issue, 0)

    @pl.when(i == 0)
    def _():
        gather(dest_ref, slot)

    @pl.when(i + 1 < n)
    def _():
        gather(dest_next_ref, 1 - slot)

    for k in range(TOP_K):
        pltpu.make_async_copy(ys_hbm.at[pl.ds(0, tm)], ybuf.at[slot, k], sems.at[slot]).wait()

    route = route_ref[...]
    f = (route[:, R_GATE0:R_GATE0 + 1] * ybuf[slot, 0]
         + route[:, R_GATE1:R_GATE1 + 1] * ybuf[slot, 1])
    x2_ref[...] = _layer_norm(ALPHA * x1_ref[...] + f, g2_ref[...], b2_ref[...])


def _combine_ln(x1, ys, dest, route, g2, b2):
    t = x1.shape[0]
    tm = MOVE_ROWS
    n = t // tm
    row = lambda i: (i, 0)
    const2 = lambda i: (0, 0)
    dest_blocks = _dest_blocks(dest, tm)
    smem_block = lambda index_map: pl.BlockSpec((1, 1, TOP_K * tm), index_map,
                                                memory_space=pltpu.SMEM)
    return pl.pallas_call(
        _combine_kernel,
        grid=(n,),
        in_specs=[
            smem_block(lambda i: (i, 0, 0)),
            smem_block(lambda i: (jnp.minimum(i + 1, n - 1), 0, 0)),
            pl.BlockSpec((tm, D_MODEL), row),
            pl.BlockSpec((tm, LANES), row),
            pl.BlockSpec((1, D_MODEL), const2),
            pl.BlockSpec((1, D_MODEL), const2),
            pl.BlockSpec(memory_space=pl.ANY),
        ],
        out_specs=pl.BlockSpec((tm, D_MODEL), row),
        out_shape=jax.ShapeDtypeStruct((t, D_MODEL), F32),
        scratch_shapes=[
            pltpu.VMEM((2, TOP_K, tm, D_MODEL), F32),
            pltpu.SemaphoreType.DMA((2,)),
        ],
        compiler_params=pltpu.CompilerParams(
            dimension_semantics=("arbitrary",), vmem_limit_bytes=VMEM_LIMIT),
        name="combine_ln",
    )(dest_blocks, dest_blocks, x1, route, g2.reshape(1, -1), b2.reshape(1, -1), ys)


def _moe_layer(x1, logits, layer, wg, wu, wd, g2, b2):
    route, counts = _route(logits)
    dest, steps = _plan(route, counts)
    xs = _dispatch(x1, dest)
    ys = _expert_ffn(xs, steps, layer, wg, wu, wd)
    return _combine_ln(x1, ys, dest, route, g2, b2)


def kernel(x, rel_bias_table, mix_w_in, gmlp_ln_g, gmlp_ln_b, gmlp_w_spatial, gmlp_b_spatial, conv_w, mix_w_out, attn_w_qkv, attn_b_qkv, attn_sinks, attn_w_o, attn_b_o, ln1_g, ln1_b, ln2_g, ln2_b, router_group, router_expert, expert_w_gate, expert_w_up, expert_w_down):
    bsz, s, d = x.shape
    assert (s, d) == (SEQ, D_MODEL)
    x = x.reshape(bsz * s, d)
    bias = _rel_bias(rel_bias_table)
    pad = jnp.zeros((DEPTH, D_MODEL, ROUTER_COLS - N_GROUPS - N_EXPERTS), F32)
    w_router = jnp.concatenate([router_group, router_expert, pad], axis=-1)
    for l in range(DEPTH):
        i = l // 2
        if l % 2 == 0:
            x1, logits = _mixer_layer(x, mix_w_in[i], gmlp_ln_g[i], gmlp_ln_b[i],
                                      gmlp_w_spatial[i], gmlp_b_spatial[i], conv_w[i],
                                      mix_w_out[i], ln1_g[l], ln1_b[l], w_router[l])
        else:
            x1, logits = _attn_layer(x, bias, attn_w_qkv[i], attn_b_qkv[i], attn_sinks[i],
                                     attn_w_o[i], attn_b_o[i], ln1_g[l], ln1_b[l], w_router[l])
        x = _moe_layer(x1, logits, l, expert_w_gate, expert_w_up, expert_w_down,
                       ln2_g[l], ln2_b[l])
    return x.reshape(bsz, s, d)
```

```python
import math

import jax
import jax.numpy as jnp
from jax import lax
from jax.experimental import pallas as pl
from jax.experimental.pallas import tpu as pltpu

D_MODEL = 1024
SEQ = 16384
DEPTH = 4
CHUNK = 128
A_GROUPS = 4
A_CH = 128
A_WIDTH = A_GROUPS * A_CH
B_WIDTH = 512
CONV_W = 3
MIX_IN = 2 * A_WIDTH + 3 * B_WIDTH
N_HEADS = 16
N_KV = 2
HEAD_DIM = 64
GQA_GROUP = N_HEADS // N_KV
WINDOW = 128
QKV_DIM = (N_HEADS + 2 * N_KV) * HEAD_DIM
ATT_OUT = N_HEADS * HEAD_DIM
KV_WIDTH = N_KV * HEAD_DIM
N_BUCKETS = 32
MAX_DISTANCE = 128
N_GROUPS = 4
EXPERTS_PER_GROUP = 8
N_EXPERTS = N_GROUPS * EXPERTS_PER_GROUP
TOP_K = 2
D_EXPERT = 512
ALPHA = (2 * DEPTH) ** 0.25
LN_EPS = 1e-5

LANES = 128
ROUTER_COLS = LANES
MIX_ROWS = 256
ROUTE_ROWS = 512
MOVE_ROWS = 256
EXPERT_ROWS = 512
ROWS_PER_ISSUE = 8
VMEM_LIMIT = 56 * 1024 * 1024

R_E0, R_E1, R_RANK0, R_RANK1, R_GATE0, R_GATE1 = range(6)

F32 = jnp.float32
BF16 = jnp.bfloat16
I32 = jnp.int32


def _layer_norm(x, g, b):
    mu = jnp.mean(x, axis=-1, keepdims=True)
    xc = x - mu
    var = jnp.mean(xc * xc, axis=-1, keepdims=True)
    return xc * lax.rsqrt(var + LN_EPS) * g + b


def _gelu(x):
    return 0.5 * x * (1.0 + lax.erf(x * (2.0 ** -0.5)))


def _router_logits(x1, wr_ref):
    return jnp.dot(x1, wr_ref[...], preferred_element_type=F32,
                   precision=lax.Precision.HIGHEST)


def _mixer_kernel(x_ref, win_ref, lng_ref, lnb_ref, wsp_ref, bsp_ref, cw_ref, wout_ref,
                  g1_ref, b1_ref, wr_ref, x1_ref, logit_ref, ztail_ref):
    tm = x_ref.shape[0]
    i = pl.program_id(0)

    @pl.when(i % (SEQ // tm) == 0)
    def _():
        ztail_ref[...] = jnp.zeros_like(ztail_ref)

    x = x_ref[...]
    h = jnp.dot(x.astype(BF16), win_ref[...], preferred_element_type=F32)
    u = _gelu(h[:, :A_WIDTH])
    v = _gelu(h[:, A_WIDTH:2 * A_WIDTH])
    o = 2 * A_WIDTH
    g_b = h[:, o:o + B_WIDTH]
    g_c = h[:, o + B_WIDTH:o + 2 * B_WIDTH]
    hb = h[:, o + 2 * B_WIDTH:o + 3 * B_WIDTH]

    v = _layer_norm(v, lng_ref[...], lnb_ref[...]).astype(BF16)
    n_chunks = tm // CHUNK
    ri = lax.broadcasted_iota(I32, (CHUNK, CHUNK), 0)
    ci = lax.broadcasted_iota(I32, (CHUNK, CHUNK), 1)
    causal = ci <= ri
    sv_cols = [[None] * A_GROUPS for _ in range(n_chunks)]
    for g in range(A_GROUPS):
        ws = jnp.where(causal, wsp_ref[g], 0.0).astype(BF16)
        vg = jnp.concatenate(
            [v[c * CHUNK:(c + 1) * CHUNK, g * A_CH:(g + 1) * A_CH] for c in range(n_chunks)],
            axis=1)
        sg = jnp.dot(ws, vg, preferred_element_type=F32) + bsp_ref[:, g:g + 1]
        for c in range(n_chunks):
            sv_cols[c][g] = sg[:, c * A_CH:(c + 1) * A_CH]
    sv = jnp.concatenate([jnp.concatenate(row, axis=1) for row in sv_cols], axis=0)
    y_a = u * sv

    z = g_c * hb
    rows = lax.broadcasted_iota(I32, z.shape, 0)
    tail = ztail_ref[...]
    zm1 = jnp.where(rows == 0, tail[7:8, :], pltpu.roll(z, 1, 0))
    zm2 = jnp.where(rows == 0, tail[6:7, :],
                    jnp.where(rows == 1, tail[7:8, :], pltpu.roll(z, 2, 0)))
    conv = cw_ref[0:1, :] * zm2 + cw_ref[1:2, :] * zm1 + cw_ref[2:3, :] * z
    y_b = g_b * conv
    ztail_ref[...] = z[tm - 8:tm, :]

    y = jnp.concatenate([y_a, y_b], axis=1).astype(BF16)
    m = jnp.dot(y, wout_ref[...], preferred_element_type=F32)
    x1 = _layer_norm(ALPHA * x + m, g1_ref[...], b1_ref[...])
    x1_ref[...] = x1
    logit_ref[...] = _router_logits(x1, wr_ref)


def _mixer_layer(x, w_in, ln_g, ln_b, w_sp, b_sp, conv_w, w_out, g1, b1, w_router):
    t = x.shape[0]
    tm = MIX_ROWS
    const2 = lambda i: (0, 0)
    return pl.pallas_call(
        _mixer_kernel,
        grid=(t // tm,),
        in_specs=[
            pl.BlockSpec((tm, D_MODEL), lambda i: (i, 0)),
            pl.BlockSpec((D_MODEL, MIX_IN), const2),
            pl.BlockSpec((1, A_WIDTH), const2),
            pl.BlockSpec((1, A_WIDTH), const2),
            pl.BlockSpec((A_GROUPS, CHUNK, CHUNK), lambda i: (0, 0, 0)),
            pl.BlockSpec((CHUNK, A_GROUPS), const2),
            pl.BlockSpec((CONV_W, B_WIDTH), const2),
            pl.BlockSpec((A_WIDTH + B_WIDTH, D_MODEL), const2),
            pl.BlockSpec((1, D_MODEL), const2),
            pl.BlockSpec((1, D_MODEL), const2),
            pl.BlockSpec((D_MODEL, ROUTER_COLS), const2),
        ],
        out_specs=[
            pl.BlockSpec((tm, D_MODEL), lambda i: (i, 0)),
            pl.BlockSpec((tm, ROUTER_COLS), lambda i: (i, 0)),
        ],
        out_shape=[
            jax.ShapeDtypeStruct((t, D_MODEL), F32),
            jax.ShapeDtypeStruct((t, ROUTER_COLS), F32),
        ],
        scratch_shapes=[pltpu.VMEM((8, B_WIDTH), F32)],
        compiler_params=pltpu.CompilerParams(
            dimension_semantics=("arbitrary",), vmem_limit_bytes=VMEM_LIMIT),
        name="mixer_layer",
    )(x, w_in.astype(BF16), ln_g.reshape(1, -1), ln_b.reshape(1, -1), w_sp, b_sp.T,
      conv_w, w_out.astype(BF16), g1.reshape(1, -1), b1.reshape(1, -1), w_router)


def _attn_kernel(sink_ref, x_ref, wqkv_ref, bqkv_ref, bias_ref, wo_ref, bo_ref,
                 g1_ref, b1_ref, wr_ref, x1_ref, logit_ref, kprev_ref, vprev_ref):
    i = pl.program_id(0)
    first = i % (SEQ // CHUNK) == 0

    @pl.when(first)
    def _():
        kprev_ref[...] = jnp.zeros_like(kprev_ref)
        vprev_ref[...] = jnp.zeros_like(vprev_ref)

    x = x_ref[...]
    qkv = jnp.dot(x.astype(BF16), wqkv_ref[...], preferred_element_type=F32) + bqkv_ref[...]
    q = (qkv[:, :ATT_OUT] * (HEAD_DIM ** -0.5)).astype(BF16)
    k = qkv[:, ATT_OUT:ATT_OUT + KV_WIDTH].astype(BF16)
    v = qkv[:, ATT_OUT + KV_WIDTH:].astype(BF16)
    kw = jnp.concatenate([kprev_ref[...], k], axis=0)
    vw = jnp.concatenate([vprev_ref[...], v], axis=0)
    kprev_ref[...] = k
    vprev_ref[...] = v

    a = lax.broadcasted_iota(I32, (CHUNK, 2 * CHUNK), 0)
    c = lax.broadcasted_iota(I32, (CHUNK, 2 * CHUNK), 1)
    lo = jnp.where(first, CHUNK, 0)
    mask = (c > a) & (c <= a + WINDOW) & (c >= lo)
    neg = jnp.finfo(F32).min

    outs = []
    for h in range(N_HEADS):
        j = h // GQA_GROUP
        qh = q[:, h * HEAD_DIM:(h + 1) * HEAD_DIM]
        kj = kw[:, j * HEAD_DIM:(j + 1) * HEAD_DIM]
        vj = vw[:, j * HEAD_DIM:(j + 1) * HEAD_DIM]
        sc = lax.dot_general(qh, kj, (((1,), (1,)), ((), ())), preferred_element_type=F32)
        sc = jnp.where(mask, sc + bias_ref[h], neg)
        sink = sink_ref[h]
        m = jnp.maximum(jnp.max(sc, axis=-1, keepdims=True), sink)
        p = jnp.exp(sc - m)
        denom = jnp.sum(p, axis=-1, keepdims=True) + jnp.exp(sink - m)
        probs = (p / denom).astype(BF16)
        outs.append(jnp.dot(probs, vj, preferred_element_type=F32))
    o = jnp.concatenate(outs, axis=1).astype(BF16)
    m_out = jnp.dot(o, wo_ref[...], preferred_element_type=F32) + bo_ref[...]
    x1 = _layer_norm(ALPHA * x + m_out, g1_ref[...], b1_ref[...])
    x1_ref[...] = x1
    logit_ref[...] = _router_logits(x1, wr_ref)


def _t5_bucket(rel):
    n = jnp.maximum(rel, 0)
    max_exact = N_BUCKETS // 2
    nf = jnp.maximum(n, 1).astype(F32)
    large = max_exact + (jnp.log(nf / max_exact) / math.log(MAX_DISTANCE / max_exact)
                         * (N_BUCKETS - max_exact)).astype(I32)
    large = jnp.minimum(large, N_BUCKETS - 1)
    return jnp.where(n < max_exact, n, large)


def _rel_bias(rel_table):
    a = jnp.arange(CHUNK)[:, None]
    c = jnp.arange(2 * CHUNK)[None, :]
    onehot = jax.nn.one_hot(_t5_bucket(a + CHUNK - c), N_BUCKETS, dtype=F32)
    return jnp.einsum('acb,bh->hac', onehot, rel_table.astype(F32),
                      precision=lax.Precision.HIGHEST)


def _attn_layer(x, bias, w_qkv, b_qkv, sinks, w_o, b_o, g1, b1, w_router):
    t = x.shape[0]
    const2 = lambda i: (0, 0)
    return pl.pallas_call(
        _attn_kernel,
        grid=(t // CHUNK,),
        in_specs=[
            pl.BlockSpec(memory_space=pltpu.SMEM),
            pl.BlockSpec((CHUNK, D_MODEL), lambda i: (i, 0)),
            pl.BlockSpec((D_MODEL, QKV_DIM), const2),
            pl.BlockSpec((1, QKV_DIM), const2),
            pl.BlockSpec((N_HEADS, CHUNK, 2 * CHUNK), lambda i: (0, 0, 0)),
            pl.BlockSpec((ATT_OUT, D_MODEL), const2),
            pl.BlockSpec((1, D_MODEL), const2),
            pl.BlockSpec((1, D_MODEL), const2),
            pl.BlockSpec((1, D_MODEL), const2),
            pl.BlockSpec((D_MODEL, ROUTER_COLS), const2),
        ],
        out_specs=[
            pl.BlockSpec((CHUNK, D_MODEL), lambda i: (i, 0)),
            pl.BlockSpec((CHUNK, ROUTER_COLS), lambda i: (i, 0)),
        ],
        out_shape=[
            jax.ShapeDtypeStruct((t, D_MODEL), F32),
            jax.ShapeDtypeStruct((t, ROUTER_COLS), F32),
        ],
        scratch_shapes=[pltpu.VMEM((CHUNK, KV_WIDTH), BF16), pltpu.VMEM((CHUNK, KV_WIDTH), BF16)],
        compiler_params=pltpu.CompilerParams(
            dimension_semantics=("arbitrary",), vmem_limit_bytes=VMEM_LIMIT),
        name="attn_layer",
    )(sinks, x, w_qkv.astype(BF16), b_qkv.reshape(1, -1), bias, w_o.astype(BF16),
      b_o.reshape(1, -1), g1.reshape(1, -1), b1.reshape(1, -1), w_router)


def _route_kernel(logit_ref, route_ref, route_t_ref, count_ref, base_ref):
    tm = logit_ref.shape[0]
    i = pl.program_id(0)

    @pl.when(i == 0)
    def _():
        base_ref[...] = jnp.zeros_like(base_ref)

    lg = logit_ref[...]
    lane = lax.broadcasted_iota(I32, lg.shape, 1)
    ninf = -jnp.inf

    def first_argmax(vals):
        m = jnp.max(vals, axis=-1, keepdims=True)
        idx = jnp.min(jnp.where(vals == m, lane, LANES), axis=-1, keepdims=True)
        return m, idx

    is_g = lane < N_GROUPS
    gmax, g_idx = first_argmax(jnp.where(is_g, lg, ninf))
    g_p = 1.0 / jnp.sum(jnp.where(is_g, jnp.exp(lg - gmax), 0.0), axis=-1, keepdims=True)

    e_lo = N_GROUPS + g_idx * EXPERTS_PER_GROUP
    in_group = (lane >= e_lo) & (lane < e_lo + EXPERTS_PER_GROUP)
    el = jnp.where(in_group, lg, ninf)
    m1, i1 = first_argmax(el)
    m2, i2 = first_argmax(jnp.where(lane == i1, ninf, el))
    a2 = jnp.exp(m2 - m1)
    gate0 = g_p / (1.0 + a2)
    gate1 = g_p * a2 / (1.0 + a2)

    hit0 = lane == i1
    hit1 = lane == i2
    onehot = jnp.where(hit0 | hit1, 1.0, 0.0)
    r = lax.broadcasted_iota(I32, (tm, tm), 0)
    c = lax.broadcasted_iota(I32, (tm, tm), 1)
    strict_lower = jnp.where(c < r, 1.0, 0.0).astype(BF16)
    before = jnp.dot(strict_lower, onehot.astype(BF16), preferred_element_type=F32)
    before = before + base_ref[...]
    rank0 = jnp.sum(jnp.where(hit0, before, 0.0), axis=-1, keepdims=True)
    rank1 = jnp.sum(jnp.where(hit1, before, 0.0), axis=-1, keepdims=True)
    base_ref[...] += jnp.sum(onehot, axis=0, keepdims=True)

    rec = jnp.zeros(lg.shape, F32)
    for col, val in ((R_E0, (i1 - N_GROUPS).astype(F32)), (R_E1, (i2 - N_GROUPS).astype(F32)),
                     (R_RANK0, rank0), (R_RANK1, rank1), (R_GATE0, gate0), (R_GATE1, gate1)):
        rec = jnp.where(lane == col, val, rec)
    route_ref[...] = rec
    route_t_ref[...] = jnp.transpose(rec)[0:8, :]
    count_ref[...] = base_ref[...]


def _route(logits):
    t = logits.shape[0]
    tm = ROUTE_ROWS
    return pl.pallas_call(
        _route_kernel,
        grid=(t // tm,),
        in_specs=[pl.BlockSpec((tm, ROUTER_COLS), lambda i: (i, 0))],
        out_specs=[
            pl.BlockSpec((tm, LANES), lambda i: (i, 0)),
            pl.BlockSpec((8, tm), lambda i: (0, i)),
            pl.BlockSpec((1, LANES), lambda i: (0, 0)),
        ],
        out_shape=[
            jax.ShapeDtypeStruct((t, LANES), F32),
            jax.ShapeDtypeStruct((8, t), F32),
            jax.ShapeDtypeStruct((1, LANES), F32),
        ],
        scratch_shapes=[pltpu.VMEM((1, LANES), F32)],
        compiler_params=pltpu.CompilerParams(
            dimension_semantics=("arbitrary",), vmem_limit_bytes=VMEM_LIMIT),
        name="route",
    )(logits)


def _plan(route_t, count_lanes):
    t = route_t.shape[1]
    rb = EXPERT_ROWS
    experts = jnp.arange(N_EXPERTS, dtype=I32)
    counts = count_lanes[0, N_GROUPS:N_GROUPS + N_EXPERTS].astype(I32)
    end = jnp.cumsum(counts)
    start = end - counts
    e = route_t[R_E0:R_E1 + 1].astype(I32)
    rank = route_t[R_RANK0:R_RANK1 + 1].astype(I32)
    onehot = e[:, None, :] == experts[None, :, None]
    dest = jnp.sum(jnp.where(onehot, start[None, :, None], 0), axis=1) + rank

    first_blk = start // rb
    last_blk = jnp.maximum(end - 1, 0) // rb
    n_steps_e = jnp.where(counts > 0, last_blk - first_blk + 1, 0)
    step_end = jnp.cumsum(n_steps_e)
    step_start = step_end - n_steps_e
    n_steps = step_end[-1]
    max_steps = t * TOP_K // rb + N_EXPERTS - 1
    s = jnp.minimum(jnp.arange(max_steps, dtype=I32), n_steps - 1)
    step_e = jnp.minimum(jnp.sum((step_end[None, :] <= s[:, None]).astype(I32), axis=1),
                         N_EXPERTS - 1)
    pick = step_e[:, None] == experts[None, :]
    take = lambda v: jnp.sum(jnp.where(pick, v[None, :], 0), axis=1)
    step_blk = take(first_blk) + s - take(step_start)
    step_lo = jnp.clip(take(start) - step_blk * rb, 0, rb)
    step_hi = jnp.clip(take(end) - step_blk * rb, 0, rb)
    return dest, (step_blk, step_e, step_lo, step_hi, n_steps.reshape(1))


def _dest_blocks(dest, tm):
    t = dest.shape[1]
    return dest.reshape(TOP_K, t // tm, tm).transpose(1, 0, 2).reshape(t // tm, 1, TOP_K * tm)


def _row_copy(src_ref, src_row, dst_ref, dst_row, sem):
    return pltpu.make_async_copy(src_ref.at[pl.ds(src_row, 1)], dst_ref.at[pl.ds(dst_row, 1)], sem)


def _dispatch_kernel(dest_ref, x1_ref, xs_hbm, sem):
    tm = x1_ref.shape[0]

    def issue(g, carry):
        for j in range(ROWS_PER_ISSUE):
            r = g * ROWS_PER_ISSUE + j
            for k in range(TOP_K):
                _row_copy(x1_ref, r, xs_hbm, dest_ref[0, 0, k * tm + r], sem).start(priority=k)
        return carry

    lax.fori_loop(0, tm // ROWS_PER_ISSUE, issue, 0)
    for k in range(TOP_K):
        pltpu.make_async_copy(x1_ref, xs_hbm.at[pl.ds(0, tm)], sem).wait()


def _dispatch(x1, dest):
    t = x1.shape[0]
    tm = MOVE_ROWS
    return pl.pallas_call(
        _dispatch_kernel,
        grid=(t // tm,),
        in_specs=[
            pl.BlockSpec((1, 1, TOP_K * tm), lambda i: (i, 0, 0), memory_space=pltpu.SMEM),
            pl.BlockSpec((tm, D_MODEL), lambda i: (i, 0)),
        ],
        out_specs=pl.BlockSpec(memory_space=pl.ANY),
        out_shape=jax.ShapeDtypeStruct((t * TOP_K, D_MODEL), F32),
        scratch_shapes=[pltpu.SemaphoreType.DMA],
        compiler_params=pltpu.CompilerParams(
            dimension_semantics=("arbitrary",), vmem_limit_bytes=VMEM_LIMIT),
        name="dispatch",
    )(_dest_blocks(dest, tm), x1)


def _expert_kernel(blk_ref, e_ref, lo_ref, hi_ref, n_steps_ref, xs_ref, wg_ref, wu_ref, wd_ref,
                   ys_ref, wg_bf, wu_bf, wd_bf):
    s = pl.program_id(0)
    prev = jnp.maximum(s - 1, 0)
    new_expert = (s == 0) | (e_ref[s] != e_ref[prev])
    new_block = (s == 0) | (blk_ref[s] != blk_ref[prev])

    @pl.when(new_expert)
    def _():
        wg_bf[...] = wg_ref[0, 0].astype(BF16)
        wu_bf[...] = wu_ref[0, 0].astype(BF16)
        wd_bf[...] = wd_ref[0, 0].astype(BF16)

    @pl.when(s < n_steps_ref[0])
    def _():
        xb = xs_ref[...].astype(BF16)
        gate = jnp.dot(xb, wg_bf[...], preferred_element_type=F32)
        up = jnp.dot(xb, wu_bf[...], preferred_element_type=F32)
        hid = (gate * jax.nn.sigmoid(gate) * up).astype(BF16)
        y = jnp.dot(hid, wd_bf[...], preferred_element_type=F32)
        rows = lax.broadcasted_iota(I32, y.shape, 0)
        y = jnp.where((rows >= lo_ref[s]) & (rows < hi_ref[s]), y, 0.0)

        @pl.when(new_block)
        def _():
            ys_ref[...] = y

        @pl.when(jnp.logical_not(new_block))
        def _():
            ys_ref[...] += y


def _expert_ffn(xs, steps, layer, wg, wu, wd):
    step_blk, step_e, step_lo, step_hi, n_steps = steps
    rb = EXPERT_ROWS
    row_map = lambda s, blk, e, lo, hi, n: (blk[s], 0)
    w_map = lambda s, blk, e, lo, hi, n: (layer, e[s], 0, 0)
    grid_spec = pltpu.PrefetchScalarGridSpec(
        num_scalar_prefetch=5,
        grid=(step_blk.shape[0],),
        in_specs=[
            pl.BlockSpec((rb, D_MODEL), row_map),
            pl.BlockSpec((1, 1, D_MODEL, D_EXPERT), w_map),
            pl.BlockSpec((1, 1, D_MODEL, D_EXPERT), w_map),
            pl.BlockSpec((1, 1, D_EXPERT, D_MODEL), w_map),
        ],
        out_specs=pl.BlockSpec((rb, D_MODEL), row_map),
        scratch_shapes=[
            pltpu.VMEM((D_MODEL, D_EXPERT), BF16),
            pltpu.VMEM((D_MODEL, D_EXPERT), BF16),
            pltpu.VMEM((D_EXPERT, D_MODEL), BF16),
        ],
    )
    return pl.pallas_call(
        _expert_kernel,
        grid_spec=grid_spec,
        out_shape=jax.ShapeDtypeStruct(xs.shape, F32),
        compiler_params=pltpu.CompilerParams(
            dimension_semantics=("arbitrary",), vmem_limit_bytes=VMEM_LIMIT),
        name="expert_ffn",
    )(step_blk, step_e, step_lo, step_hi, n_steps, xs, wg, wu, wd)


def _combine_kernel(dest_ref, dest_next_ref, x1_ref, route_ref, g2_ref, b2_ref, ys_hbm,
                    x2_ref, ybuf, sems):
    tm = x1_ref.shape[0]
    i = pl.program_id(0)
    n = pl.num_programs(0)
    slot = i % 2

    def gather(d_ref, to_slot):
        def issue(g, carry):
            for j in range(ROWS_PER_ISSUE):
                r = g * ROWS_PER_ISSUE + j
                for k in range(TOP_K):
                    _row_copy(ys_hbm, d_ref[0, 0, k * tm + r], ybuf.at[to_slot, k], r,
                              sems.at[to_slot]).start(priority=k)
            return carry
        lax.fori_loop(0, tm // ROWS_PER_ISSUE, issue, 0)

    @pl.when(i == 0)
    def _():
        gather(dest_ref, slot)

    @pl.when(i + 1 < n)
    def _():
        gather(dest_next_ref, 1 - slot)

    for k in range(TOP_K):
        pltpu.make_async_copy(ys_hbm.at[pl.ds(0, tm)], ybuf.at[slot, k], sems.at[slot]).wait()

    route = route_ref[...]
    f = (route[:, R_GATE0:R_GATE0 + 1] * ybuf[slot, 0]
         + route[:, R_GATE1:R_GATE1 + 1] * ybuf[slot, 1])
    x2_ref[...] = _layer_norm(ALPHA * x1_ref[...] + f, g2_ref[...], b2_ref[...])


def _combine_ln(x1, ys, dest, route, g2, b2):
    t = x1.shape[0]
    tm = MOVE_ROWS
    n = t // tm
    row = lambda i: (i, 0)
    const2 = lambda i: (0, 0)
    dest_blocks = _dest_blocks(dest, tm)
    smem_block = lambda index_map: pl.BlockSpec((1, 1, TOP_K * tm), index_map,
                                                memory_space=pltpu.SMEM)
    return pl.pallas_call(
        _combine_kernel,
        grid=(n,),
        in_specs=[
            smem_block(lambda i: (i, 0, 0)),
            smem_block(lambda i: (jnp.minimum(i + 1, n - 1), 0, 0)),
            pl.BlockSpec((tm, D_MODEL), row),
            pl.BlockSpec((tm, LANES), row),
            pl.BlockSpec((1, D_MODEL), const2),
            pl.BlockSpec((1, D_MODEL), const2),
            pl.BlockSpec(memory_space=pl.ANY),
        ],
        out_specs=pl.BlockSpec((tm, D_MODEL), row),
        out_shape=jax.ShapeDtypeStruct((t, D_MODEL), F32),
        scratch_shapes=[
            pltpu.VMEM((2, TOP_K, tm, D_MODEL), F32),
            pltpu.SemaphoreType.DMA((2,)),
        ],
        compiler_params=pltpu.CompilerParams(
            dimension_semantics=("arbitrary",), vmem_limit_bytes=VMEM_LIMIT),
        name="combine_ln",
    )(dest_blocks, dest_blocks, x1, route, g2.reshape(1, -1), b2.reshape(1, -1), ys)


def _moe_layer(x1, logits, layer, wg, wu, wd, g2, b2):
    route, route_t, counts = _route(logits)
    dest, steps = _plan(route_t, counts)
    xs = _dispatch(x1, dest)
    ys = _expert_ffn(xs, steps, layer, wg, wu, wd)
    return _combine_ln(x1, ys, dest, route, g2, b2)


def kernel(x, rel_bias_table, mix_w_in, gmlp_ln_g, gmlp_ln_b, gmlp_w_spatial, gmlp_b_spatial, conv_w, mix_w_out, attn_w_qkv, attn_b_qkv, attn_sinks, attn_w_o, attn_b_o, ln1_g, ln1_b, ln2_g, ln2_b, router_group, router_expert, expert_w_gate, expert_w_up, expert_w_down):
    bsz, s, d = x.shape
    assert (s, d) == (SEQ, D_MODEL)
    x = x.reshape(bsz * s, d)
    bias = _rel_bias(rel_bias_table)
    pad = jnp.zeros((DEPTH, D_MODEL, ROUTER_COLS - N_GROUPS - N_EXPERTS), F32)
    w_router = jnp.concatenate([router_group, router_expert, pad], axis=-1)
    for l in range(DEPTH):
        i = l // 2
        if l % 2 == 0:
            x1, logits = _mixer_layer(x, mix_w_in[i], gmlp_ln_g[i], gmlp_ln_b[i],
                                      gmlp_w_spatial[i], gmlp_b_spatial[i], conv_w[i],
                                      mix_w_out[i], ln1_g[l], ln1_b[l], w_router[l])
        else:
            x1, logits = _attn_layer(x, bias, attn_w_qkv[i], attn_b_qkv[i], attn_sinks[i],
                                     attn_w_o[i], attn_b_o[i], ln1_g[l], ln1_b[l], w_router[l])
        x = _moe_layer(x1, logits, l, expert_w_gate, expert_w_up, expert_w_down,
                       ln2_g[l], ln2_b[l])
    return x.reshape(bsz, s, d)
```

```python
import math

import jax
import jax.numpy as jnp
from jax import lax
from jax.experimental import pallas as pl
from jax.experimental.pallas import tpu as pltpu

D_MODEL = 1024
SEQ = 16384
DEPTH = 4
CHUNK = 128
A_GROUPS = 4
A_CH = 128
A_WIDTH = A_GROUPS * A_CH
B_WIDTH = 512
CONV_W = 3
MIX_IN = 2 * A_WIDTH + 3 * B_WIDTH
N_HEADS = 16
N_KV = 2
HEAD_DIM = 64
GQA_GROUP = N_HEADS // N_KV
WINDOW = 128
QKV_DIM = (N_HEADS + 2 * N_KV) * HEAD_DIM
ATT_OUT = N_HEADS * HEAD_DIM
KV_WIDTH = N_KV * HEAD_DIM
N_BUCKETS = 32
MAX_DISTANCE = 128
N_GROUPS = 4
EXPERTS_PER_GROUP = 8
N_EXPERTS = N_GROUPS * EXPERTS_PER_GROUP
TOP_K = 2
D_EXPERT = 512
ALPHA = (2 * DEPTH) ** 0.25
LN_EPS = 1e-5

LANES = 128
ROUTER_COLS = LANES
MIX_ROWS = 256
ROUTE_ROWS = 512
MOVE_ROWS = 256
EXPERT_ROWS = 512
ROWS_PER_ISSUE = 8
VMEM_LIMIT = 56 * 1024 * 1024

R_E0, R_E1, R_RANK0, R_RANK1, R_GATE0, R_GATE1 = range(6)

F32 = jnp.float32
BF16 = jnp.bfloat16
I32 = jnp.int32


def _layer_norm(x, g, b):
    mu = jnp.mean(x, axis=-1, keepdims=True)
    xc = x - mu
    var = jnp.mean(xc * xc, axis=-1, keepdims=True)
    return xc * lax.rsqrt(var + LN_EPS) * g + b


def _gelu(x):
    return 0.5 * x * (1.0 + lax.erf(x * (2.0 ** -0.5)))


def _router_logits(x1, wr_ref):
    return jnp.dot(x1.astype(BF16), wr_ref[...], preferred_element_type=F32)


def _mixer_kernel(x_ref, win_ref, lng_ref, lnb_ref, wsp_ref, bsp_ref, cw_ref, wout_ref,
                  g1_ref, b1_ref, wr_ref, x1_ref, logit_ref, ztail_ref):
    tm = x_ref.shape[0]
    i = pl.program_id(0)

    @pl.when(i % (SEQ // tm) == 0)
    def _():
        ztail_ref[...] = jnp.zeros_like(ztail_ref)

    x = x_ref[...]
    h = jnp.dot(x.astype(BF16), win_ref[...], preferred_element_type=F32)
    u = _gelu(h[:, :A_WIDTH])
    v = _gelu(h[:, A_WIDTH:2 * A_WIDTH])
    o = 2 * A_WIDTH
    g_b = h[:, o:o + B_WIDTH]
    g_c = h[:, o + B_WIDTH:o + 2 * B_WIDTH]
    hb = h[:, o + 2 * B_WIDTH:o + 3 * B_WIDTH]

    v = _layer_norm(v, lng_ref[...], lnb_ref[...]).astype(BF16)
    n_chunks = tm // CHUNK
    ri = lax.broadcasted_iota(I32, (CHUNK, CHUNK), 0)
    ci = lax.broadcasted_iota(I32, (CHUNK, CHUNK), 1)
    causal = ci <= ri
    sv_cols = [[None] * A_GROUPS for _ in range(n_chunks)]
    for g in range(A_GROUPS):
        ws = jnp.where(causal, wsp_ref[g], 0.0).astype(BF16)
        vg = jnp.concatenate(
            [v[c * CHUNK:(c + 1) * CHUNK, g * A_CH:(g + 1) * A_CH] for c in range(n_chunks)],
            axis=1)
        sg = jnp.dot(ws, vg, preferred_element_type=F32) + bsp_ref[:, g:g + 1]
        for c in range(n_chunks):
            sv_cols[c][g] = sg[:, c * A_CH:(c + 1) * A_CH]
    sv = jnp.concatenate([jnp.concatenate(row, axis=1) for row in sv_cols], axis=0)
    y_a = u * sv

    z = g_c * hb
    rows = lax.broadcasted_iota(I32, z.shape, 0)
    tail = ztail_ref[...]
    zm1 = jnp.where(rows == 0, tail[7:8, :], pltpu.roll(z, 1, 0))
    zm2 = jnp.where(rows == 0, tail[6:7, :],
                    jnp.where(rows == 1, tail[7:8, :], pltpu.roll(z, 2, 0)))
    conv = cw_ref[0:1, :] * zm2 + cw_ref[1:2, :] * zm1 + cw_ref[2:3, :] * z
    y_b = g_b * conv
    ztail_ref[...] = z[tm - 8:tm, :]

    y = jnp.concatenate([y_a, y_b], axis=1).astype(BF16)
    m = jnp.dot(y, wout_ref[...], preferred_element_type=F32)
    x1 = _layer_norm(ALPHA * x + m, g1_ref[...], b1_ref[...])
    x1_ref[...] = x1
    logit_ref[...] = _router_logits(x1, wr_ref)


def _mixer_layer(x, w_in, ln_g, ln_b, w_sp, b_sp, conv_w, w_out, g1, b1, w_router):
    t = x.shape[0]
    tm = MIX_ROWS
    const2 = lambda i: (0, 0)
    return pl.pallas_call(
        _mixer_kernel,
        grid=(t // tm,),
        in_specs=[
            pl.BlockSpec((tm, D_MODEL), lambda i: (i, 0)),
            pl.BlockSpec((D_MODEL, MIX_IN), const2),
            pl.BlockSpec((1, A_WIDTH), const2),
            pl.BlockSpec((1, A_WIDTH), const2),
            pl.BlockSpec((A_GROUPS, CHUNK, CHUNK), lambda i: (0, 0, 0)),
            pl.BlockSpec((CHUNK, A_GROUPS), const2),
            pl.BlockSpec((CONV_W, B_WIDTH), const2),
            pl.BlockSpec((A_WIDTH + B_WIDTH, D_MODEL), const2),
            pl.BlockSpec((1, D_MODEL), const2),
            pl.BlockSpec((1, D_MODEL), const2),
            pl.BlockSpec((D_MODEL, ROUTER_COLS), const2),
        ],
        out_specs=[
            pl.BlockSpec((tm, D_MODEL), lambda i: (i, 0)),
            pl.BlockSpec((tm, ROUTER_COLS), lambda i: (i, 0)),
        ],
        out_shape=[
            jax.ShapeDtypeStruct((t, D_MODEL), F32),
            jax.ShapeDtypeStruct((t, ROUTER_COLS), F32),
        ],
        scratch_shapes=[pltpu.VMEM((8, B_WIDTH), F32)],
        compiler_params=pltpu.CompilerParams(
            dimension_semantics=("arbitrary",), vmem_limit_bytes=VMEM_LIMIT),
        name="mixer_layer",
    )(x, w_in.astype(BF16), ln_g.reshape(1, -1), ln_b.reshape(1, -1), w_sp, b_sp.T,
      conv_w, w_out.astype(BF16), g1.reshape(1, -1), b1.reshape(1, -1), w_router.astype(BF16))


def _attn_kernel(sink_ref, x_ref, wqkv_ref, bqkv_ref, bias_ref, wo_ref, bo_ref,
                 g1_ref, b1_ref, wr_ref, x1_ref, logit_ref, kprev_ref, vprev_ref):
    i = pl.program_id(0)
    first = i % (SEQ // CHUNK) == 0

    @pl.when(first)
    def _():
        kprev_ref[...] = jnp.zeros_like(kprev_ref)
        vprev_ref[...] = jnp.zeros_like(vprev_ref)

    x = x_ref[...]
    qkv = jnp.dot(x.astype(BF16), wqkv_ref[...], preferred_element_type=F32) + bqkv_ref[...]
    q = (qkv[:, :ATT_OUT] * (HEAD_DIM ** -0.5)).astype(BF16)

    lane = lax.broadcasted_iota(I32, (CHUNK, KV_WIDTH), 1)
    low = lane < HEAD_DIM

    def halves(t):
        swapped = pltpu.roll(t, HEAD_DIM, 1)
        zero = jnp.zeros_like(t)
        return [jnp.where(low, t, zero).astype(BF16), jnp.where(low, zero, swapped).astype(BF16),
                jnp.where(low, swapped, zero).astype(BF16), jnp.where(low, zero, t).astype(BF16)]

    k_cur = halves(qkv[:, ATT_OUT:ATT_OUT + KV_WIDTH])
    v_cur = halves(qkv[:, ATT_OUT + KV_WIDTH:])

    def stacked(prev_ref, cur, j):
        return jnp.concatenate([prev_ref[2 * j], cur[2 * j], prev_ref[2 * j + 1], cur[2 * j + 1]],
                               axis=0)

    k_rhs = [stacked(kprev_ref, k_cur, j) for j in range(N_KV)]
    v_rhs = [stacked(vprev_ref, v_cur, j) for j in range(N_KV)]
    for idx in range(2 * N_KV):
        kprev_ref[idx] = k_cur[idx]
        vprev_ref[idx] = v_cur[idx]

    a = lax.broadcasted_iota(I32, (CHUNK, 2 * CHUNK), 0)
    c = lax.broadcasted_iota(I32, (CHUNK, 2 * CHUNK), 1)
    lo = jnp.where(first, CHUNK, 0)
    mask = (c > a) & (c <= a + WINDOW) & (c >= lo)
    neg = jnp.finfo(F32).min

    outs = []
    for pair in range(N_HEADS // 2):
        j = (2 * pair) // GQA_GROUP
        qp = q[:, pair * 2 * HEAD_DIM:(pair + 1) * 2 * HEAD_DIM]
        sc2 = lax.dot_general(qp, k_rhs[j], (((1,), (1,)), ((), ())),
                              preferred_element_type=F32) + bias_ref[pair]
        probs = []
        for side in range(2):
            sc = jnp.where(mask, sc2[:, side * 2 * CHUNK:(side + 1) * 2 * CHUNK], neg)
            sink = sink_ref[2 * pair + side]
            m = jnp.maximum(jnp.max(sc, axis=-1, keepdims=True), sink)
            p = jnp.exp(sc - m)
            denom = jnp.sum(p, axis=-1, keepdims=True) + jnp.exp(sink - m)
            probs.append((p / denom).astype(BF16))
        outs.append(jnp.dot(jnp.concatenate(probs, axis=1), v_rhs[j],
                            preferred_element_type=F32))
    o = jnp.concatenate(outs, axis=1).astype(BF16)
    m_out = jnp.dot(o, wo_ref[...], preferred_element_type=F32) + bo_ref[...]
    x1 = _layer_norm(ALPHA * x + m_out, g1_ref[...], b1_ref[...])
    x1_ref[...] = x1
    logit_ref[...] = _router_logits(x1, wr_ref)


def _t5_bucket(rel):
    n = jnp.maximum(rel, 0)
    max_exact = N_BUCKETS // 2
    nf = jnp.maximum(n, 1).astype(F32)
    large = max_exact + (jnp.log(nf / max_exact) / math.log(MAX_DISTANCE / max_exact)
                         * (N_BUCKETS - max_exact)).astype(I32)
    large = jnp.minimum(large, N_BUCKETS - 1)
    return jnp.where(n < max_exact, n, large)


def _rel_bias(rel_table):
    a = jnp.arange(CHUNK)[:, None]
    c = jnp.arange(2 * CHUNK)[None, :]
    onehot = jax.nn.one_hot(_t5_bucket(a + CHUNK - c), N_BUCKETS, dtype=F32)
    bias = jnp.einsum('acb,bh->hac', onehot, rel_table.astype(F32),
                      precision=lax.Precision.HIGHEST)
    bias = bias.reshape(N_HEADS // 2, 2, CHUNK, 2 * CHUNK).transpose(0, 2, 1, 3)
    return bias.reshape(N_HEADS // 2, CHUNK, 4 * CHUNK)


def _attn_layer(x, bias, w_qkv, b_qkv, sinks, w_o, b_o, g1, b1, w_router):
    t = x.shape[0]
    const2 = lambda i: (0, 0)
    return pl.pallas_call(
        _attn_kernel,
        grid=(t // CHUNK,),
        in_specs=[
            pl.BlockSpec(memory_space=pltpu.SMEM),
            pl.BlockSpec((CHUNK, D_MODEL), lambda i: (i, 0)),
            pl.BlockSpec((D_MODEL, QKV_DIM), const2),
            pl.BlockSpec((1, QKV_DIM), const2),
            pl.BlockSpec((N_HEADS // 2, CHUNK, 4 * CHUNK), lambda i: (0, 0, 0)),
            pl.BlockSpec((ATT_OUT, D_MODEL), const2),
            pl.BlockSpec((1, D_MODEL), const2),
            pl.BlockSpec((1, D_MODEL), const2),
            pl.BlockSpec((1, D_MODEL), const2),
            pl.BlockSpec((D_MODEL, ROUTER_COLS), const2),
        ],
        out_specs=[
            pl.BlockSpec((CHUNK, D_MODEL), lambda i: (i, 0)),
            pl.BlockSpec((CHUNK, ROUTER_COLS), lambda i: (i, 0)),
        ],
        out_shape=[
            jax.ShapeDtypeStruct((t, D_MODEL), F32),
            jax.ShapeDtypeStruct((t, ROUTER_COLS), F32),
        ],
        scratch_shapes=[pltpu.VMEM((2 * N_KV, CHUNK, KV_WIDTH), BF16),
                        pltpu.VMEM((2 * N_KV, CHUNK, KV_WIDTH), BF16)],
        compiler_params=pltpu.CompilerParams(
            dimension_semantics=("arbitrary",), vmem_limit_bytes=VMEM_LIMIT),
        name="attn_layer",
    )(sinks, x, w_qkv.astype(BF16), b_qkv.reshape(1, -1), bias, w_o.astype(BF16),
      b_o.reshape(1, -1), g1.reshape(1, -1), b1.reshape(1, -1), w_router.astype(BF16))


def _route_kernel(logit_ref, route_ref, route_t_ref, count_ref, base_ref):
    tm = logit_ref.shape[0]
    i = pl.program_id(0)

    @pl.when(i == 0)
    def _():
        base_ref[...] = jnp.zeros_like(base_ref)

    lg = logit_ref[...]
    lane = lax.broadcasted_iota(I32, lg.shape, 1)
    ninf = -jnp.inf

    def first_argmax(vals):
        m = jnp.max(vals, axis=-1, keepdims=True)
        idx = jnp.min(jnp.where(vals == m, lane, LANES), axis=-1, keepdims=True)
        return m, idx

    is_g = lane < N_GROUPS
    gmax, g_idx = first_argmax(jnp.where(is_g, lg, ninf))
    g_p = 1.0 / jnp.sum(jnp.where(is_g, jnp.exp(lg - gmax), 0.0), axis=-1, keepdims=True)

    e_lo = N_GROUPS + g_idx * EXPERTS_PER_GROUP
    in_group = (lane >= e_lo) & (lane < e_lo + EXPERTS_PER_GROUP)
    el = jnp.where(in_group, lg, ninf)
    m1, i1 = first_argmax(el)
    m2, i2 = first_argmax(jnp.where(lane == i1, ninf, el))
    a2 = jnp.exp(m2 - m1)
    gate0 = g_p / (1.0 + a2)
    gate1 = g_p * a2 / (1.0 + a2)

    hit0 = lane == i1
    hit1 = lane == i2
    onehot = jnp.where(hit0 | hit1, 1.0, 0.0)
    r = lax.broadcasted_iota(I32, (tm, tm), 0)
    c = lax.broadcasted_iota(I32, (tm, tm), 1)
    strict_lower = jnp.where(c < r, 1.0, 0.0).astype(BF16)
    before = jnp.dot(strict_lower, onehot.astype(BF16), preferred_element_type=F32)
    before = before + base_ref[...]
    rank0 = jnp.sum(jnp.where(hit0, before, 0.0), axis=-1, keepdims=True)
    rank1 = jnp.sum(jnp.where(hit1, before, 0.0), axis=-1, keepdims=True)
    base_ref[...] += jnp.sum(onehot, axis=0, keepdims=True)

    rec = jnp.zeros(lg.shape, F32)
    for col, val in ((R_E0, (i1 - N_GROUPS).astype(F32)), (R_E1, (i2 - N_GROUPS).astype(F32)),
                     (R_RANK0, rank0), (R_RANK1, rank1), (R_GATE0, gate0), (R_GATE1, gate1)):
        rec = jnp.where(lane == col, val, rec)
    route_ref[...] = rec
    route_t_ref[...] = jnp.transpose(rec)[0:8, :]
    count_ref[...] = base_ref[...]


def _route(logits):
    t = logits.shape[0]
    tm = ROUTE_ROWS
    return pl.pallas_call(
        _route_kernel,
        grid=(t // tm,),
        in_specs=[pl.BlockSpec((tm, ROUTER_COLS), lambda i: (i, 0))],
        out_specs=[
            pl.BlockSpec((tm, LANES), lambda i: (i, 0)),
            pl.BlockSpec((8, tm), lambda i: (0, i)),
            pl.BlockSpec((1, LANES), lambda i: (0, 0)),
        ],
        out_shape=[
            jax.ShapeDtypeStruct((t, LANES), F32),
            jax.ShapeDtypeStruct((8, t), F32),
            jax.ShapeDtypeStruct((1, LANES), F32),
        ],
        scratch_shapes=[pltpu.VMEM((1, LANES), F32)],
        compiler_params=pltpu.CompilerParams(
            dimension_semantics=("arbitrary",), vmem_limit_bytes=VMEM_LIMIT),
        name="route",
    )(logits)


def _plan(route_t, count_lanes):
    t = route_t.shape[1]
    rb = EXPERT_ROWS
    experts = jnp.arange(N_EXPERTS, dtype=I32)
    counts = count_lanes[0, N_GROUPS:N_GROUPS + N_EXPERTS].astype(I32)
    end = jnp.cumsum(counts)
    start = end - counts
    e = route_t[R_E0:R_E1 + 1].astype(I32)
    rank = route_t[R_RANK0:R_RANK1 + 1].astype(I32)
    onehot = e[:, None, :] == experts[None, :, None]
    dest = jnp.sum(jnp.where(onehot, start[None, :, None], 0), axis=1) + rank

    first_blk = start // rb
    last_blk = jnp.maximum(end - 1, 0) // rb
    n_steps_e = jnp.where(counts > 0, last_blk - first_blk + 1, 0)
    step_end = jnp.cumsum(n_steps_e)
    step_start = step_end - n_steps_e
    n_steps = step_end[-1]
    max_steps = t * TOP_K // rb + N_EXPERTS - 1
    s = jnp.minimum(jnp.arange(max_steps, dtype=I32), n_steps - 1)
    step_e = jnp.minimum(jnp.sum((step_end[None, :] <= s[:, None]).astype(I32), axis=1),
                         N_EXPERTS - 1)
    pick = step_e[:, None] == experts[None, :]
    take = lambda v: jnp.sum(jnp.where(pick, v[None, :], 0), axis=1)
    step_blk = take(first_blk) + s - take(step_start)
    step_lo = jnp.clip(take(start) - step_blk * rb, 0, rb)
    step_hi = jnp.clip(take(end) - step_blk * rb, 0, rb)
    return dest, (step_blk, step_e, step_lo, step_hi, n_steps.reshape(1))


def _dest_blocks(dest, tm):
    t = dest.shape[1]
    return dest.reshape(TOP_K, t // tm, tm).transpose(1, 0, 2).reshape(t // tm, 1, TOP_K * tm)


def _row_copy(src_ref, src_row, dst_ref, dst_row, sem):
    return pltpu.make_async_copy(src_ref.at[pl.ds(src_row, 1)], dst_ref.at[pl.ds(dst_row, 1)], sem)


def _dispatch_kernel(dest_ref, x1_hbm, xs_hbm, sem):
    tm = dest_ref.shape[2] // TOP_K
    i = pl.program_id(0)
    base = i * tm

    def issue(g, carry):
        for j in range(ROWS_PER_ISSUE):
            r = g * ROWS_PER_ISSUE + j
            for k in range(TOP_K):
                _row_copy(x1_hbm, base + r, xs_hbm, dest_ref[0, 0, k * tm + r],
                          sem).start(priority=k)
        return carry

    lax.fori_loop(0, tm // ROWS_PER_ISSUE, issue, 0)

    def wait_one_step():
        for k in range(TOP_K):
            pltpu.make_async_copy(x1_hbm.at[pl.ds(0, tm)], xs_hbm.at[pl.ds(0, tm)], sem).wait()

    @pl.when(i > 0)
    def _():
        wait_one_step()

    @pl.when(i == pl.num_programs(0) - 1)
    def _():
        wait_one_step()


def _dispatch(x1, dest):
    t = x1.shape[0]
    tm = MOVE_ROWS
    return pl.pallas_call(
        _dispatch_kernel,
        grid=(t // tm,),
        in_specs=[
            pl.BlockSpec((1, 1, TOP_K * tm), lambda i: (i, 0, 0), memory_space=pltpu.SMEM),
            pl.BlockSpec(memory_space=pl.ANY),
        ],
        out_specs=pl.BlockSpec(memory_space=pl.ANY),
        out_shape=jax.ShapeDtypeStruct((t * TOP_K, D_MODEL), F32),
        scratch_shapes=[pltpu.SemaphoreType.DMA],
        compiler_params=pltpu.CompilerParams(
            dimension_semantics=("arbitrary",), vmem_limit_bytes=VMEM_LIMIT),
        name="dispatch",
    )(_dest_blocks(dest, tm), x1)


def _expert_kernel(blk_ref, e_ref, lo_ref, hi_ref, n_steps_ref, xs_ref, wg_ref, wu_ref, wd_ref,
                   ys_ref, wg_bf, wu_bf, wd_bf):
    s = pl.program_id(0)
    prev = jnp.maximum(s - 1, 0)
    new_expert = (s == 0) | (e_ref[s] != e_ref[prev])
    new_block = (s == 0) | (blk_ref[s] != blk_ref[prev])

    @pl.when(new_expert)
    def _():
        wg_bf[...] = wg_ref[0, 0].astype(BF16)
        wu_bf[...] = wu_ref[0, 0].astype(BF16)
        wd_bf[...] = wd_ref[0, 0].astype(BF16)

    @pl.when(s < n_steps_ref[0])
    def _():
        xb = xs_ref[...].astype(BF16)
        gate = jnp.dot(xb, wg_bf[...], preferred_element_type=F32)
        up = jnp.dot(xb, wu_bf[...], preferred_element_type=F32)
        hid = (gate * jax.nn.sigmoid(gate) * up).astype(BF16)
        y = jnp.dot(hid, wd_bf[...], preferred_element_type=F32)
        rows = lax.broadcasted_iota(I32, y.shape, 0)
        y = jnp.where((rows >= lo_ref[s]) & (rows < hi_ref[s]), y, 0.0)

        @pl.when(new_block)
        def _():
            ys_ref[...] = y

        @pl.when(jnp.logical_not(new_block))
        def _():
            ys_ref[...] += y


def _expert_ffn(xs, steps, layer, wg, wu, wd):
    step_blk, step_e, step_lo, step_hi, n_steps = steps
    rb = EXPERT_ROWS
    row_map = lambda s, blk, e, lo, hi, n: (blk[s], 0)
    w_map = lambda s, blk, e, lo, hi, n: (layer, e[s], 0, 0)
    grid_spec = pltpu.PrefetchScalarGridSpec(
        num_scalar_prefetch=5,
        grid=(step_blk.shape[0],),
        in_specs=[
            pl.BlockSpec((rb, D_MODEL), row_map),
            pl.BlockSpec((1, 1, D_MODEL, D_EXPERT), w_map),
            pl.BlockSpec((1, 1, D_MODEL, D_EXPERT), w_map),
            pl.BlockSpec((1, 1, D_EXPERT, D_MODEL), w_map),
        ],
        out_specs=pl.BlockSpec((rb, D_MODEL), row_map),
        scratch_shapes=[
            pltpu.VMEM((D_MODEL, D_EXPERT), BF16),
            pltpu.VMEM((D_MODEL, D_EXPERT), BF16),
            pltpu.VMEM((D_EXPERT, D_MODEL), BF16),
        ],
    )
    return pl.pallas_call(
        _expert_kernel,
        grid_spec=grid_spec,
        out_shape=jax.ShapeDtypeStruct(xs.shape, F32),
        compiler_params=pltpu.CompilerParams(
            dimension_semantics=("arbitrary",), vmem_limit_bytes=VMEM_LIMIT),
        name="expert_ffn",
    )(step_blk, step_e, step_lo, step_hi, n_steps, xs, wg, wu, wd)


def _combine_kernel(dest_ref, dest_next_ref, x1_ref, route_ref, g2_ref, b2_ref, ys_hbm,
                    x2_ref, ybuf, sems):
    tm = x1_ref.shape[0]
    i = pl.program_id(0)
    n = pl.num_programs(0)
    slot = i % 2

    def gather(d_ref, to_slot):
        def issue(g, carry):
            for j in range(ROWS_PER_ISSUE):
                r = g * ROWS_PER_ISSUE + j
                for k in range(TOP_K):
                    _row_copy(ys_hbm, d_ref[0, 0, k * tm + r], ybuf.at[to_slot, k], r,
                              sems.at[to_slot]).start(priority=k)
            return carry
        lax.fori_loop(0, tm // ROWS_PER_ISSUE, issue, 0)

    @pl.when(i == 0)
    def _():
        gather(dest_ref, slot)

    @pl.when(i + 1 < n)
    def _():
        gather(dest_next_ref, 1 - slot)

    for k in range(TOP_K):
        pltpu.make_async_copy(ys_hbm.at[pl.ds(0, tm)], ybuf.at[slot, k], sems.at[slot]).wait()

    route = route_ref[...]
    f = (route[:, R_GATE0:R_GATE0 + 1] * ybuf[slot, 0]
         + route[:, R_GATE1:R_GATE1 + 1] * ybuf[slot, 1])
    x2_ref[...] = _layer_norm(ALPHA * x1_ref[...] + f, g2_ref[...], b2_ref[...])


def _combine_ln(x1, ys, dest, route, g2, b2):
    t = x1.shape[0]
    tm = MOVE_ROWS
    n = t // tm
    row = lambda i: (i, 0)
    const2 = lambda i: (0, 0)
    dest_blocks = _dest_blocks(dest, tm)
    smem_block = lambda index_map: pl.BlockSpec((1, 1, TOP_K * tm), index_map,
                                                memory_space=pltpu.SMEM)
    return pl.pallas_call(
        _combine_kernel,
        grid=(n,),
        in_specs=[
            smem_block(lambda i: (i, 0, 0)),
            smem_block(lambda i: (jnp.minimum(i + 1, n - 1), 0, 0)),
            pl.BlockSpec((tm, D_MODEL), row),
            pl.BlockSpec((tm, LANES), row),
            pl.BlockSpec((1, D_MODEL), const2),
            pl.BlockSpec((1, D_MODEL), const2),
            pl.BlockSpec(memory_space=pl.ANY),
        ],
        out_specs=pl.BlockSpec((tm, D_MODEL), row),
        out_shape=jax.ShapeDtypeStruct((t, D_MODEL), F32),
        scratch_shapes=[
            pltpu.VMEM((2, TOP_K, tm, D_MODEL), F32),
            pltpu.SemaphoreType.DMA((2,)),
        ],
        compiler_params=pltpu.CompilerParams(
            dimension_semantics=("arbitrary",), vmem_limit_bytes=VMEM_LIMIT),
        name="combine_ln",
    )(dest_blocks, dest_blocks, x1, route, g2.reshape(1, -1), b2.reshape(1, -1), ys)


def _moe_layer(x1, logits, layer, wg, wu, wd, g2, b2):
    route, route_t, counts = _route(logits)
    dest, steps = _plan(route_t, counts)
    xs = _dispatch(x1, dest)
    ys = _expert_ffn(xs, steps, layer, wg, wu, wd)
    return _combine_ln(x1, ys, dest, route, g2, b2)


def kernel(x, rel_bias_table, mix_w_in, gmlp_ln_g, gmlp_ln_b, gmlp_w_spatial, gmlp_b_spatial, conv_w, mix_w_out, attn_w_qkv, attn_b_qkv, attn_sinks, attn_w_o, attn_b_o, ln1_g, ln1_b, ln2_g, ln2_b, router_group, router_expert, expert_w_gate, expert_w_up, expert_w_down):
    bsz, s, d = x.shape
    assert (s, d) == (SEQ, D_MODEL)
    x = x.reshape(bsz * s, d)
    bias = _rel_bias(rel_bias_table)
    pad = jnp.zeros((DEPTH, D_MODEL, ROUTER_COLS - N_GROUPS - N_EXPERTS), F32)
    w_router = jnp.concatenate([router_group, router_expert, pad], axis=-1)
    for l in range(DEPTH):
        i = l // 2
        if l % 2 == 0:
            x1, logits = _mixer_layer(x, mix_w_in[i], gmlp_ln_g[i], gmlp_ln_b[i],
                                      gmlp_w_spatial[i], gmlp_b_spatial[i], conv_w[i],
                                      mix_w_out[i], ln1_g[l], ln1_b[l], w_router[l])
        else:
            x1, logits = _attn_layer(x, bias, attn_w_qkv[i], attn_b_qkv[i], attn_sinks[i],
                                     attn_w_o[i], attn_b_o[i], ln1_g[l], ln1_b[l], w_router[l])
        x = _moe_layer(x1, logits, l, expert_w_gate, expert_w_up, expert_w_down,
                       ln2_g[l], ln2_b[l])
    return x.reshape(bsz, s, d)
```

```python
import math

import jax
import jax.numpy as jnp
from jax import lax
from jax.experimental import pallas as pl
from jax.experimental.pallas import tpu as pltpu

D_MODEL = 1024
SEQ = 16384
DEPTH = 4
CHUNK = 128
A_GROUPS = 4
A_CH = 128
A_WIDTH = A_GROUPS * A_CH
B_WIDTH = 512
CONV_W = 3
MIX_IN = 2 * A_WIDTH + 3 * B_WIDTH
N_HEADS = 16
N_KV = 2
HEAD_DIM = 64
GQA_GROUP = N_HEADS // N_KV
WINDOW = 128
QKV_DIM = (N_HEADS + 2 * N_KV) * HEAD_DIM
ATT_OUT = N_HEADS * HEAD_DIM
KV_WIDTH = N_KV * HEAD_DIM
N_BUCKETS = 32
MAX_DISTANCE = 128
N_GROUPS = 4
EXPERTS_PER_GROUP = 8
N_EXPERTS = N_GROUPS * EXPERTS_PER_GROUP
TOP_K = 2
D_EXPERT = 512
ALPHA = (2 * DEPTH) ** 0.25
LN_EPS = 1e-5

LANES = 128
ROUTER_COLS = LANES
MIX_ROWS = 256
ROUTE_ROWS = 512
MOVE_ROWS = 256
EXPERT_ROWS = 512
ROWS_PER_ISSUE = 8
DISPATCH_SLOTS = 3
VMEM_LIMIT = 56 * 1024 * 1024

R_E0, R_E1, R_RANK0, R_RANK1, R_GATE0, R_GATE1 = range(6)

F32 = jnp.float32
BF16 = jnp.bfloat16
I32 = jnp.int32


def _layer_norm(x, g, b):
    mu = jnp.mean(x, axis=-1, keepdims=True)
    xc = x - mu
    var = jnp.mean(xc * xc, axis=-1, keepdims=True)
    return xc * lax.rsqrt(var + LN_EPS) * g + b


def _gelu(x):
    return 0.5 * x * (1.0 + lax.erf(x * (2.0 ** -0.5)))


def _router_logits(x1, wr_ref):
    return jnp.dot(x1.astype(BF16), wr_ref[...], preferred_element_type=F32)


def _mixer_kernel(x_ref, win_ref, lng_ref, lnb_ref, wsp_ref, bsp_ref, cw_ref, wout_ref,
                  g1_ref, b1_ref, wr_ref, x1_ref, logit_ref, ztail_ref):
    tm = x_ref.shape[0]
    i = pl.program_id(0)

    @pl.when(i % (SEQ // tm) == 0)
    def _():
        ztail_ref[...] = jnp.zeros_like(ztail_ref)

    x = x_ref[...]
    h = jnp.dot(x.astype(BF16), win_ref[...], preferred_element_type=F32)
    u = _gelu(h[:, :A_WIDTH])
    v = _gelu(h[:, A_WIDTH:2 * A_WIDTH])
    o = 2 * A_WIDTH
    g_b = h[:, o:o + B_WIDTH]
    g_c = h[:, o + B_WIDTH:o + 2 * B_WIDTH]
    hb = h[:, o + 2 * B_WIDTH:o + 3 * B_WIDTH]

    v = _layer_norm(v, lng_ref[...], lnb_ref[...]).astype(BF16)
    n_chunks = tm // CHUNK
    ri = lax.broadcasted_iota(I32, (CHUNK, CHUNK), 0)
    ci = lax.broadcasted_iota(I32, (CHUNK, CHUNK), 1)
    causal = ci <= ri
    sv_cols = [[None] * A_GROUPS for _ in range(n_chunks)]
    for g in range(A_GROUPS):
        ws = jnp.where(causal, wsp_ref[g], 0.0).astype(BF16)
        vg = jnp.concatenate(
            [v[c * CHUNK:(c + 1) * CHUNK, g * A_CH:(g + 1) * A_CH] for c in range(n_chunks)],
            axis=1)
        sg = jnp.dot(ws, vg, preferred_element_type=F32) + bsp_ref[:, g:g + 1]
        for c in range(n_chunks):
            sv_cols[c][g] = sg[:, c * A_CH:(c + 1) * A_CH]
    sv = jnp.concatenate([jnp.concatenate(row, axis=1) for row in sv_cols], axis=0)
    y_a = u * sv

    z = g_c * hb
    rows = lax.broadcasted_iota(I32, z.shape, 0)
    tail = ztail_ref[...]
    zm1 = jnp.where(rows == 0, tail[7:8, :], pltpu.roll(z, 1, 0))
    zm2 = jnp.where(rows == 0, tail[6:7, :],
                    jnp.where(rows == 1, tail[7:8, :], pltpu.roll(z, 2, 0)))
    conv = cw_ref[0:1, :] * zm2 + cw_ref[1:2, :] * zm1 + cw_ref[2:3, :] * z
    y_b = g_b * conv
    ztail_ref[...] = z[tm - 8:tm, :]

    y = jnp.concatenate([y_a, y_b], axis=1).astype(BF16)
    m = jnp.dot(y, wout_ref[...], preferred_element_type=F32)
    x1 = _layer_norm(ALPHA * x + m, g1_ref[...], b1_ref[...])
    x1_ref[...] = x1
    logit_ref[...] = _router_logits(x1, wr_ref)


def _mixer_layer(x, w_in, ln_g, ln_b, w_sp, b_sp, conv_w, w_out, g1, b1, w_router):
    t = x.shape[0]
    tm = MIX_ROWS
    const2 = lambda i: (0, 0)
    return pl.pallas_call(
        _mixer_kernel,
        grid=(t // tm,),
        in_specs=[
            pl.BlockSpec((tm, D_MODEL), lambda i: (i, 0)),
            pl.BlockSpec((D_MODEL, MIX_IN), const2),
            pl.BlockSpec((1, A_WIDTH), const2),
            pl.BlockSpec((1, A_WIDTH), const2),
            pl.BlockSpec((A_GROUPS, CHUNK, CHUNK), lambda i: (0, 0, 0)),
            pl.BlockSpec((CHUNK, A_GROUPS), const2),
            pl.BlockSpec((CONV_W, B_WIDTH), const2),
            pl.BlockSpec((A_WIDTH + B_WIDTH, D_MODEL), const2),
            pl.BlockSpec((1, D_MODEL), const2),
            pl.BlockSpec((1, D_MODEL), const2),
            pl.BlockSpec((D_MODEL, ROUTER_COLS), const2),
        ],
        out_specs=[
            pl.BlockSpec((tm, D_MODEL), lambda i: (i, 0)),
            pl.BlockSpec((tm, ROUTER_COLS), lambda i: (i, 0)),
        ],
        out_shape=[
            jax.ShapeDtypeStruct((t, D_MODEL), F32),
            jax.ShapeDtypeStruct((t, ROUTER_COLS), F32),
        ],
        scratch_shapes=[pltpu.VMEM((8, B_WIDTH), F32)],
        compiler_params=pltpu.CompilerParams(
            dimension_semantics=("arbitrary",), vmem_limit_bytes=VMEM_LIMIT),
        name="mixer_layer",
    )(x, w_in.astype(BF16), ln_g.reshape(1, -1), ln_b.reshape(1, -1), w_sp, b_sp.T,
      conv_w, w_out.astype(BF16), g1.reshape(1, -1), b1.reshape(1, -1), w_router.astype(BF16))


def _attn_kernel(sink_ref, x_ref, wqkv_ref, bqkv_ref, bias_ref, wo_ref, bo_ref,
                 g1_ref, b1_ref, wr_ref, x1_ref, logit_ref, kprev_ref, vprev_ref):
    i = pl.program_id(0)
    first = i % (SEQ // CHUNK) == 0

    @pl.when(first)
    def _():
        kprev_ref[...] = jnp.zeros_like(kprev_ref)
        vprev_ref[...] = jnp.zeros_like(vprev_ref)

    x = x_ref[...]
    qkv = jnp.dot(x.astype(BF16), wqkv_ref[...], preferred_element_type=F32) + bqkv_ref[...]
    q = (qkv[:, :ATT_OUT] * (HEAD_DIM ** -0.5)).astype(BF16)

    lane = lax.broadcasted_iota(I32, (CHUNK, KV_WIDTH), 1)
    low = lane < HEAD_DIM

    def halves(t):
        swapped = pltpu.roll(t, HEAD_DIM, 1)
        zero = jnp.zeros_like(t)
        return [jnp.where(low, t, zero).astype(BF16), jnp.where(low, zero, swapped).astype(BF16),
                jnp.where(low, swapped, zero).astype(BF16), jnp.where(low, zero, t).astype(BF16)]

    k_cur = halves(qkv[:, ATT_OUT:ATT_OUT + KV_WIDTH])
    v_cur = halves(qkv[:, ATT_OUT + KV_WIDTH:])

    def stacked(prev_ref, cur, j):
        return jnp.concatenate([prev_ref[2 * j], cur[2 * j], prev_ref[2 * j + 1], cur[2 * j + 1]],
                               axis=0)

    k_rhs = [stacked(kprev_ref, k_cur, j) for j in range(N_KV)]
    v_rhs = [stacked(vprev_ref, v_cur, j) for j in range(N_KV)]
    for idx in range(2 * N_KV):
        kprev_ref[idx] = k_cur[idx]
        vprev_ref[idx] = v_cur[idx]

    a = lax.broadcasted_iota(I32, (CHUNK, 2 * CHUNK), 0)
    c = lax.broadcasted_iota(I32, (CHUNK, 2 * CHUNK), 1)
    lo = jnp.where(first, CHUNK, 0)
    mask = (c > a) & (c <= a + WINDOW) & (c >= lo)
    neg = jnp.finfo(F32).min

    n_pairs = N_HEADS // 2
    kv_of = lambda pair: (2 * pair) // GQA_GROUP
    scores = [lax.dot_general(q[:, pair * 2 * HEAD_DIM:(pair + 1) * 2 * HEAD_DIM],
                              k_rhs[kv_of(pair)], (((1,), (1,)), ((), ())),
                              preferred_element_type=F32) for pair in range(n_pairs)]
    probs = []
    for pair in range(n_pairs):
        sc2 = scores[pair] + bias_ref[pair]
        sides = []
        for side in range(2):
            sc = jnp.where(mask, sc2[:, side * 2 * CHUNK:(side + 1) * 2 * CHUNK], neg)
            sink = sink_ref[2 * pair + side]
            m = jnp.maximum(jnp.max(sc, axis=-1, keepdims=True), sink)
            p = jnp.exp(sc - m)
            denom = jnp.sum(p, axis=-1, keepdims=True) + jnp.exp(sink - m)
            sides.append((p / denom).astype(BF16))
        probs.append(jnp.concatenate(sides, axis=1))
    outs = [jnp.dot(probs[pair], v_rhs[kv_of(pair)], preferred_element_type=F32)
            for pair in range(n_pairs)]
    o = jnp.concatenate(outs, axis=1).astype(BF16)
    m_out = jnp.dot(o, wo_ref[...], preferred_element_type=F32) + bo_ref[...]
    x1 = _layer_norm(ALPHA * x + m_out, g1_ref[...], b1_ref[...])
    x1_ref[...] = x1
    logit_ref[...] = _router_logits(x1, wr_ref)


def _t5_bucket(rel):
    n = jnp.maximum(rel, 0)
    max_exact = N_BUCKETS // 2
    nf = jnp.maximum(n, 1).astype(F32)
    large = max_exact + (jnp.log(nf / max_exact) / math.log(MAX_DISTANCE / max_exact)
                         * (N_BUCKETS - max_exact)).astype(I32)
    large = jnp.minimum(large, N_BUCKETS - 1)
    return jnp.where(n < max_exact, n, large)


def _rel_bias(rel_table):
    a = jnp.arange(CHUNK)[:, None]
    c = jnp.arange(2 * CHUNK)[None, :]
    onehot = jax.nn.one_hot(_t5_bucket(a + CHUNK - c), N_BUCKETS, dtype=F32)
    bias = jnp.einsum('acb,bh->hac', onehot, rel_table.astype(F32),
                      precision=lax.Precision.HIGHEST)
    bias = bias.reshape(N_HEADS // 2, 2, CHUNK, 2 * CHUNK).transpose(0, 2, 1, 3)
    return bias.reshape(N_HEADS // 2, CHUNK, 4 * CHUNK)


def _attn_layer(x, bias, w_qkv, b_qkv, sinks, w_o, b_o, g1, b1, w_router):
    t = x.shape[0]
    const2 = lambda i: (0, 0)
    return pl.pallas_call(
        _attn_kernel,
        grid=(t // CHUNK,),
        in_specs=[
            pl.BlockSpec(memory_space=pltpu.SMEM),
            pl.BlockSpec((CHUNK, D_MODEL), lambda i: (i, 0)),
            pl.BlockSpec((D_MODEL, QKV_DIM), const2),
            pl.BlockSpec((1, QKV_DIM), const2),
            pl.BlockSpec((N_HEADS // 2, CHUNK, 4 * CHUNK), lambda i: (0, 0, 0)),
            pl.BlockSpec((ATT_OUT, D_MODEL), const2),
            pl.BlockSpec((1, D_MODEL), const2),
            pl.BlockSpec((1, D_MODEL), const2),
            pl.BlockSpec((1, D_MODEL), const2),
            pl.BlockSpec((D_MODEL, ROUTER_COLS), const2),
        ],
        out_specs=[
            pl.BlockSpec((CHUNK, D_MODEL), lambda i: (i, 0)),
            pl.BlockSpec((CHUNK, ROUTER_COLS), lambda i: (i, 0)),
        ],
        out_shape=[
            jax.ShapeDtypeStruct((t, D_MODEL), F32),
            jax.ShapeDtypeStruct((t, ROUTER_COLS), F32),
        ],
        scratch_shapes=[pltpu.VMEM((2 * N_KV, CHUNK, KV_WIDTH), BF16),
                        pltpu.VMEM((2 * N_KV, CHUNK, KV_WIDTH), BF16)],
        compiler_params=pltpu.CompilerParams(
            dimension_semantics=("arbitrary",), vmem_limit_bytes=VMEM_LIMIT),
        name="attn_layer",
    )(sinks, x, w_qkv.astype(BF16), b_qkv.reshape(1, -1), bias, w_o.astype(BF16),
      b_o.reshape(1, -1), g1.reshape(1, -1), b1.reshape(1, -1), w_router.astype(BF16))


def _route_kernel(logit_ref, route_ref, route_t_ref, count_ref, base_ref):
    tm = logit_ref.shape[0]
    i = pl.program_id(0)

    @pl.when(i == 0)
    def _():
        base_ref[...] = jnp.zeros_like(base_ref)

    lg = logit_ref[...]
    lane = lax.broadcasted_iota(I32, lg.shape, 1)
    ninf = -jnp.inf

    def first_argmax(vals):
        m = jnp.max(vals, axis=-1, keepdims=True)
        idx = jnp.min(jnp.where(vals == m, lane, LANES), axis=-1, keepdims=True)
        return m, idx

    is_g = lane < N_GROUPS
    gmax, g_idx = first_argmax(jnp.where(is_g, lg, ninf))
    g_p = 1.0 / jnp.sum(jnp.where(is_g, jnp.exp(lg - gmax), 0.0), axis=-1, keepdims=True)

    e_lo = N_GROUPS + g_idx * EXPERTS_PER_GROUP
    in_group = (lane >= e_lo) & (lane < e_lo + EXPERTS_PER_GROUP)
    el = jnp.where(in_group, lg, ninf)
    m1, i1 = first_argmax(el)
    m2, i2 = first_argmax(jnp.where(lane == i1, ninf, el))
    a2 = jnp.exp(m2 - m1)
    gate0 = g_p / (1.0 + a2)
    gate1 = g_p * a2 / (1.0 + a2)

    hit0 = lane == i1
    hit1 = lane == i2
    onehot = jnp.where(hit0 | hit1, 1.0, 0.0)
    r = lax.broadcasted_iota(I32, (tm, tm), 0)
    c = lax.broadcasted_iota(I32, (tm, tm), 1)
    strict_lower = jnp.where(c < r, 1.0, 0.0).astype(BF16)
    before = jnp.dot(strict_lower, onehot.astype(BF16), preferred_element_type=F32)
    before = before + base_ref[...]
    rank0 = jnp.sum(jnp.where(hit0, before, 0.0), axis=-1, keepdims=True)
    rank1 = jnp.sum(jnp.where(hit1, before, 0.0), axis=-1, keepdims=True)
    base_ref[...] += jnp.sum(onehot, axis=0, keepdims=True)

    rec = jnp.zeros(lg.shape, F32)
    for col, val in ((R_E0, (i1 - N_GROUPS).astype(F32)), (R_E1, (i2 - N_GROUPS).astype(F32)),
                     (R_RANK0, rank0), (R_RANK1, rank1), (R_GATE0, gate0), (R_GATE1, gate1)):
        rec = jnp.where(lane == col, val, rec)
    route_ref[...] = rec
    route_t_ref[...] = jnp.transpose(rec)[0:8, :]
    count_ref[...] = base_ref[...]


def _route(logits):
    t = logits.shape[0]
    tm = ROUTE_ROWS
    return pl.pallas_call(
        _route_kernel,
        grid=(t // tm,),
        in_specs=[pl.BlockSpec((tm, ROUTER_COLS), lambda i: (i, 0))],
        out_specs=[
            pl.BlockSpec((tm, LANES), lambda i: (i, 0)),
            pl.BlockSpec((8, tm), lambda i: (0, i)),
            pl.BlockSpec((1, LANES), lambda i: (0, 0)),
        ],
        out_shape=[
            jax.ShapeDtypeStruct((t, LANES), F32),
            jax.ShapeDtypeStruct((8, t), F32),
            jax.ShapeDtypeStruct((1, LANES), F32),
        ],
        scratch_shapes=[pltpu.VMEM((1, LANES), F32)],
        compiler_params=pltpu.CompilerParams(
            dimension_semantics=("arbitrary",), vmem_limit_bytes=VMEM_LIMIT),
        name="route",
    )(logits)


def _plan(route_t, count_lanes):
    t = route_t.shape[1]
    rb = EXPERT_ROWS
    experts = jnp.arange(N_EXPERTS, dtype=I32)
    counts = count_lanes[0, N_GROUPS:N_GROUPS + N_EXPERTS].astype(I32)
    end = jnp.cumsum(counts)
    start = end - counts
    e = route_t[R_E0:R_E1 + 1].astype(I32)
    rank = route_t[R_RANK0:R_RANK1 + 1].astype(I32)
    onehot = e[:, None, :] == experts[None, :, None]
    dest = jnp.sum(jnp.where(onehot, start[None, :, None], 0), axis=1) + rank

    first_blk = start // rb
    last_blk = jnp.maximum(end - 1, 0) // rb
    n_steps_e = jnp.where(counts > 0, last_blk - first_blk + 1, 0)
    step_end = jnp.cumsum(n_steps_e)
    step_start = step_end - n_steps_e
    n_steps = step_end[-1]
    max_steps = t * TOP_K // rb + N_EXPERTS - 1
    s = jnp.minimum(jnp.arange(max_steps, dtype=I32), n_steps - 1)
    step_e = jnp.minimum(jnp.sum((step_end[None, :] <= s[:, None]).astype(I32), axis=1),
                         N_EXPERTS - 1)
    pick = step_e[:, None] == experts[None, :]
    take = lambda v: jnp.sum(jnp.where(pick, v[None, :], 0), axis=1)
    step_blk = take(first_blk) + s - take(step_start)
    step_lo = jnp.clip(take(start) - step_blk * rb, 0, rb)
    step_hi = jnp.clip(take(end) - step_blk * rb, 0, rb)
    return dest, (step_blk, step_e, step_lo, step_hi, n_steps.reshape(1))


def _dest_blocks(dest, tm):
    t = dest.shape[1]
    return dest.reshape(TOP_K, t // tm, tm).transpose(1, 0, 2).reshape(t // tm, 1, TOP_K * tm)


def _row_copy(src_ref, src_row, dst_ref, dst_row, sem):
    return pltpu.make_async_copy(src_ref.at[pl.ds(src_row, 1)], dst_ref.at[pl.ds(dst_row, 1)], sem)


def _dispatch_kernel(dest_ref, x1_hbm, xs_hbm, xbuf, load_sems, row_sems):
    tm = xbuf.shape[1]
    i = pl.program_id(0)
    n = pl.num_programs(0)
    slot = i % DISPATCH_SLOTS
    next_slot = (i + 1) % DISPATCH_SLOTS

    def load(tile, s):
        return pltpu.make_async_copy(x1_hbm.at[pl.ds(tile * tm, tm)], xbuf.at[s], load_sems.at[s])

    def drain_rows(s):
        for k in range(TOP_K):
            pltpu.make_async_copy(xbuf.at[s], xs_hbm.at[pl.ds(0, tm)], row_sems.at[s]).wait()

    @pl.when(i == 0)
    def _():
        load(0, 0).start()

    @pl.when(i >= DISPATCH_SLOTS - 1)
    def _():
        drain_rows(next_slot)

    @pl.when(i + 1 < n)
    def _():
        load(i + 1, next_slot).start()

    load(i, slot).wait()

    def issue(g, carry):
        for j in range(ROWS_PER_ISSUE):
            r = g * ROWS_PER_ISSUE + j
            for k in range(TOP_K):
                _row_copy(xbuf.at[slot], r, xs_hbm, dest_ref[0, 0, k * tm + r],
                          row_sems.at[slot]).start(priority=k)
        return carry

    lax.fori_loop(0, tm // ROWS_PER_ISSUE, issue, 0)

    @pl.when(i == n - 1)
    def _():
        for back in range(DISPATCH_SLOTS - 1):
            drain_rows((i - back) % DISPATCH_SLOTS)


def _dispatch(x1, dest):
    t = x1.shape[0]
    tm = MOVE_ROWS
    return pl.pallas_call(
        _dispatch_kernel,
        grid=(t // tm,),
        in_specs=[
            pl.BlockSpec((1, 1, TOP_K * tm), lambda i: (i, 0, 0), memory_space=pltpu.SMEM),
            pl.BlockSpec(memory_space=pl.ANY),
        ],
        out_specs=pl.BlockSpec(memory_space=pl.ANY),
        out_shape=jax.ShapeDtypeStruct((t * TOP_K, D_MODEL), F32),
        scratch_shapes=[
            pltpu.VMEM((DISPATCH_SLOTS, tm, D_MODEL), F32),
            pltpu.SemaphoreType.DMA((DISPATCH_SLOTS,)),
            pltpu.SemaphoreType.DMA((DISPATCH_SLOTS,)),
        ],
        compiler_params=pltpu.CompilerParams(
            dimension_semantics=("arbitrary",), vmem_limit_bytes=VMEM_LIMIT),
        name="dispatch",
    )(_dest_blocks(dest, tm), x1)


def _expert_kernel(blk_ref, e_ref, lo_ref, hi_ref, n_steps_ref, xs_ref, wg_ref, wu_ref, wd_ref,
                   ys_ref, wg_bf, wu_bf, wd_bf):
    s = pl.program_id(0)
    prev = jnp.maximum(s - 1, 0)
    new_expert = (s == 0) | (e_ref[s] != e_ref[prev])
    new_block = (s == 0) | (blk_ref[s] != blk_ref[prev])

    @pl.when(new_expert)
    def _():
        wg_bf[...] = wg_ref[0, 0].astype(BF16)
        wu_bf[...] = wu_ref[0, 0].astype(BF16)
        wd_bf[...] = wd_ref[0, 0].astype(BF16)

    @pl.when(s < n_steps_ref[0])
    def _():
        xb = xs_ref[...].astype(BF16)
        gate = jnp.dot(xb, wg_bf[...], preferred_element_type=F32)
        up = jnp.dot(xb, wu_bf[...], preferred_element_type=F32)
        hid = (gate * jax.nn.sigmoid(gate) * up).astype(BF16)
        y = jnp.dot(hid, wd_bf[...], preferred_element_type=F32)
        rows = lax.broadcasted_iota(I32, y.shape, 0)
        y = jnp.where((rows >= lo_ref[s]) & (rows < hi_ref[s]), y, 0.0)

        @pl.when(new_block)
        def _():
            ys_ref[...] = y

        @pl.when(jnp.logical_not(new_block))
        def _():
            ys_ref[...] += y


def _expert_ffn(xs, steps, layer, wg, wu, wd):
    step_blk, step_e, step_lo, step_hi, n_steps = steps
    rb = EXPERT_ROWS
    row_map = lambda s, blk, e, lo, hi, n: (blk[s], 0)
    w_map = lambda s, blk, e, lo, hi, n: (layer, e[s], 0, 0)
    grid_spec = pltpu.PrefetchScalarGridSpec(
        num_scalar_prefetch=5,
        grid=(step_blk.shape[0],),
        in_specs=[
            pl.BlockSpec((rb, D_MODEL), row_map),
            pl.BlockSpec((1, 1, D_MODEL, D_EXPERT), w_map),
            pl.BlockSpec((1, 1, D_MODEL, D_EXPERT), w_map),
            pl.BlockSpec((1, 1, D_EXPERT, D_MODEL), w_map),
        ],
        out_specs=pl.BlockSpec((rb, D_MODEL), row_map),
        scratch_shapes=[
            pltpu.VMEM((D_MODEL, D_EXPERT), BF16),
            pltpu.VMEM((D_MODEL, D_EXPERT), BF16),
            pltpu.VMEM((D_EXPERT, D_MODEL), BF16),
        ],
    )
    return pl.pallas_call(
        _expert_kernel,
        grid_spec=grid_spec,
        out_shape=jax.ShapeDtypeStruct(xs.shape, F32),
        compiler_params=pltpu.CompilerParams(
            dimension_semantics=("arbitrary",), vmem_limit_bytes=VMEM_LIMIT),
        name="expert_ffn",
    )(step_blk, step_e, step_lo, step_hi, n_steps, xs, wg, wu, wd)


def _combine_kernel(dest_ref, dest_next_ref, x1_ref, route_ref, g2_ref, b2_ref, ys_hbm,
                    x2_ref, ybuf, sems):
    tm = x1_ref.shape[0]
    i = pl.program_id(0)
    n = pl.num_programs(0)
    slot = i % 2

    def gather(d_ref, to_slot):
        def issue(g, carry):
            for j in range(ROWS_PER_ISSUE):
                r = g * ROWS_PER_ISSUE + j
                for k in range(TOP_K):
                    _row_copy(ys_hbm, d_ref[0, 0, k * tm + r], ybuf.at[to_slot, k], r,
                              sems.at[to_slot]).start(priority=k)
            return carry
        lax.fori_loop(0, tm // ROWS_PER_ISSUE, issue, 0)

    @pl.when(i == 0)
    def _():
        gather(dest_ref, slot)

    @pl.when(i + 1 < n)
    def _():
        gather(dest_next_ref, 1 - slot)

    for k in range(TOP_K):
        pltpu.make_async_copy(ys_hbm.at[pl.ds(0, tm)], ybuf.at[slot, k], sems.at[slot]).wait()

    route = route_ref[...]
    f = (route[:, R_GATE0:R_GATE0 + 1] * ybuf[slot, 0]
         + route[:, R_GATE1:R_GATE1 + 1] * ybuf[slot, 1])
    x2_ref[...] = _layer_norm(ALPHA * x1_ref[...] + f, g2_ref[...], b2_ref[...])


def _combine_ln(x1, ys, dest, route, g2, b2):
    t = x1.shape[0]
    tm = MOVE_ROWS
    n = t // tm
    row = lambda i: (i, 0)
    const2 = lambda i: (0, 0)
    dest_blocks = _dest_blocks(dest, tm)
    smem_block = lambda index_map: pl.BlockSpec((1, 1, TOP_K * tm), index_map,
                                                memory_space=pltpu.SMEM)
    return pl.pallas_call(
        _combine_kernel,
        grid=(n,),
        in_specs=[
            smem_block(lambda i: (i, 0, 0)),
            smem_block(lambda i: (jnp.minimum(i + 1, n - 1), 0, 0)),
            pl.BlockSpec((tm, D_MODEL), row),
            pl.BlockSpec((tm, LANES), row),
            pl.BlockSpec((1, D_MODEL), const2),
            pl.BlockSpec((1, D_MODEL), const2),
            pl.BlockSpec(memory_space=pl.ANY),
        ],
        out_specs=pl.BlockSpec((tm, D_MODEL), row),
        out_shape=jax.ShapeDtypeStruct((t, D_MODEL), F32),
        scratch_shapes=[
            pltpu.VMEM((2, TOP_K, tm, D_MODEL), F32),
            pltpu.SemaphoreType.DMA((2,)),
        ],
        compiler_params=pltpu.CompilerParams(
            dimension_semantics=("arbitrary",), vmem_limit_bytes=VMEM_LIMIT),
        name="combine_ln",
    )(dest_blocks, dest_blocks, x1, route, g2.reshape(1, -1), b2.reshape(1, -1), ys)


def _moe_layer(x1, logits, layer, wg, wu, wd, g2, b2):
    route, route_t, counts = _route(logits)
    dest, steps = _plan(route_t, counts)
    xs = _dispatch(x1, dest)
    ys = _expert_ffn(xs, steps, layer, wg, wu, wd)
    return _combine_ln(x1, ys, dest, route, g2, b2)


def kernel(x, rel_bias_table, mix_w_in, gmlp_ln_g, gmlp_ln_b, gmlp_w_spatial, gmlp_b_spatial, conv_w, mix_w_out, attn_w_qkv, attn_b_qkv, attn_sinks, attn_w_o, attn_b_o, ln1_g, ln1_b, ln2_g, ln2_b, router_group, router_expert, expert_w_gate, expert_w_up, expert_w_down):
    bsz, s, d = x.shape
    assert (s, d) == (SEQ, D_MODEL)
    x = x.reshape(bsz * s, d)
    bias = _rel_bias(rel_bias_table)
    pad = jnp.zeros((DEPTH, D_MODEL, ROUTER_COLS - N_GROUPS - N_EXPERTS), F32)
    w_router = jnp.concatenate([router_group, router_expert, pad], axis=-1)
    for l in range(DEPTH):
        i = l // 2
        if l % 2 == 0:
            x1, logits = _mixer_layer(x, mix_w_in[i], gmlp_ln_g[i], gmlp_ln_b[i],
                                      gmlp_w_spatial[i], gmlp_b_spatial[i], conv_w[i],
                                      mix_w_out[i], ln1_g[l], ln1_b[l], w_router[l])
        else:
            x1, logits = _attn_layer(x, bias, attn_w_qkv[i], attn_b_qkv[i], attn_sinks[i],
                                     attn_w_o[i], attn_b_o[i], ln1_g[l], ln1_b[l], w_router[l])
        x = _moe_layer(x1, logits, l, expert_w_gate, expert_w_up, expert_w_down,
                       ln2_g[l], ln2_b[l])
    return x.reshape(bsz, s, d)
```

```python
import math

import jax
import jax.numpy as jnp
from jax import lax
from jax.experimental import pallas as pl
from jax.experimental.pallas import tpu as pltpu

D_MODEL = 1024
SEQ = 16384
DEPTH = 4
CHUNK = 128
A_GROUPS = 4
A_CH = 128
A_WIDTH = A_GROUPS * A_CH
B_WIDTH = 512
CONV_W = 3
MIX_IN = 2 * A_WIDTH + 3 * B_WIDTH
N_HEADS = 16
N_KV = 2
HEAD_DIM = 64
GQA_GROUP = N_HEADS // N_KV
WINDOW = 128
QKV_DIM = (N_HEADS + 2 * N_KV) * HEAD_DIM
ATT_OUT = N_HEADS * HEAD_DIM
KV_WIDTH = N_KV * HEAD_DIM
N_BUCKETS = 32
MAX_DISTANCE = 128
N_GROUPS = 4
EXPERTS_PER_GROUP = 8
N_EXPERTS = N_GROUPS * EXPERTS_PER_GROUP
TOP_K = 2
D_EXPERT = 512
ALPHA = (2 * DEPTH) ** 0.25
LN_EPS = 1e-5

LANES = 128
SUBLANES = 8
ROW_TILE = (SUBLANES, LANES)
assert D_MODEL == SUBLANES * LANES
ROUTER_COLS = LANES
MIX_ROWS = 256
ROUTE_ROWS = 512
MOVE_ROWS = 256
EXPERT_ROWS = 512
ROWS_PER_ISSUE = 8
DISPATCH_SLOTS = 3
VMEM_LIMIT = 56 * 1024 * 1024

R_E0, R_E1, R_RANK0, R_RANK1, R_GATE0, R_GATE1 = range(6)

F32 = jnp.float32
BF16 = jnp.bfloat16
I32 = jnp.int32


def _layer_norm(x, g, b):
    mu = jnp.mean(x, axis=-1, keepdims=True)
    xc = x - mu
    var = jnp.mean(xc * xc, axis=-1, keepdims=True)
    return xc * lax.rsqrt(var + LN_EPS) * g + b


def _gelu(x):
    return 0.5 * x * (1.0 + lax.erf(x * (2.0 ** -0.5)))


def _router_logits(x1, wr_ref):
    return jnp.dot(x1.astype(BF16), wr_ref[...], preferred_element_type=F32)


def _mixer_kernel(x_ref, win_ref, lng_ref, lnb_ref, wsp_ref, bsp_ref, cw_ref, wout_ref,
                  g1_ref, b1_ref, wr_ref, x1_ref, logit_ref, ztail_ref):
    tm = x_ref.shape[0]
    i = pl.program_id(0)

    @pl.when(i % (SEQ // tm) == 0)
    def _():
        ztail_ref[...] = jnp.zeros_like(ztail_ref)

    x = x_ref[...]
    h = jnp.dot(x.astype(BF16), win_ref[...], preferred_element_type=F32)
    u = _gelu(h[:, :A_WIDTH])
    v = _gelu(h[:, A_WIDTH:2 * A_WIDTH])
    o = 2 * A_WIDTH
    g_b = h[:, o:o + B_WIDTH]
    g_c = h[:, o + B_WIDTH:o + 2 * B_WIDTH]
    hb = h[:, o + 2 * B_WIDTH:o + 3 * B_WIDTH]

    v = _layer_norm(v, lng_ref[...], lnb_ref[...]).astype(BF16)
    n_chunks = tm // CHUNK
    ri = lax.broadcasted_iota(I32, (CHUNK, CHUNK), 0)
    ci = lax.broadcasted_iota(I32, (CHUNK, CHUNK), 1)
    causal = ci <= ri
    sv_cols = [[None] * A_GROUPS for _ in range(n_chunks)]
    for g in range(A_GROUPS):
        ws = jnp.where(causal, wsp_ref[g], 0.0).astype(BF16)
        vg = jnp.concatenate(
            [v[c * CHUNK:(c + 1) * CHUNK, g * A_CH:(g + 1) * A_CH] for c in range(n_chunks)],
            axis=1)
        sg = jnp.dot(ws, vg, preferred_element_type=F32) + bsp_ref[:, g:g + 1]
        for c in range(n_chunks):
            sv_cols[c][g] = sg[:, c * A_CH:(c + 1) * A_CH]
    sv = jnp.concatenate([jnp.concatenate(row, axis=1) for row in sv_cols], axis=0)
    y_a = u * sv

    z = g_c * hb
    rows = lax.broadcasted_iota(I32, z.shape, 0)
    tail = ztail_ref[...]
    zm1 = jnp.where(rows == 0, tail[7:8, :], pltpu.roll(z, 1, 0))
    zm2 = jnp.where(rows == 0, tail[6:7, :],
                    jnp.where(rows == 1, tail[7:8, :], pltpu.roll(z, 2, 0)))
    conv = cw_ref[0:1, :] * zm2 + cw_ref[1:2, :] * zm1 + cw_ref[2:3, :] * z
    y_b = g_b * conv
    ztail_ref[...] = z[tm - 8:tm, :]

    y = jnp.concatenate([y_a, y_b], axis=1).astype(BF16)
    m = jnp.dot(y, wout_ref[...], preferred_element_type=F32)
    x1 = _layer_norm(ALPHA * x + m, g1_ref[...], b1_ref[...])
    x1_ref[...] = x1
    logit_ref[...] = _router_logits(x1, wr_ref)


def _mixer_layer(x, w_in, ln_g, ln_b, w_sp, b_sp, conv_w, w_out, g1, b1, w_router):
    t = x.shape[0]
    tm = MIX_ROWS
    const2 = lambda i: (0, 0)
    return pl.pallas_call(
        _mixer_kernel,
        grid=(t // tm,),
        in_specs=[
            pl.BlockSpec((tm, D_MODEL), lambda i: (i, 0)),
            pl.BlockSpec((D_MODEL, MIX_IN), const2),
            pl.BlockSpec((1, A_WIDTH), const2),
            pl.BlockSpec((1, A_WIDTH), const2),
            pl.BlockSpec((A_GROUPS, CHUNK, CHUNK), lambda i: (0, 0, 0)),
            pl.BlockSpec((CHUNK, A_GROUPS), const2),
            pl.BlockSpec((CONV_W, B_WIDTH), const2),
            pl.BlockSpec((A_WIDTH + B_WIDTH, D_MODEL), const2),
            pl.BlockSpec((1, D_MODEL), const2),
            pl.BlockSpec((1, D_MODEL), const2),
            pl.BlockSpec((D_MODEL, ROUTER_COLS), const2),
        ],
        out_specs=[
            pl.BlockSpec((tm, D_MODEL), lambda i: (i, 0)),
            pl.BlockSpec((tm, ROUTER_COLS), lambda i: (i, 0)),
        ],
        out_shape=[
            jax.ShapeDtypeStruct((t, D_MODEL), F32),
            jax.ShapeDtypeStruct((t, ROUTER_COLS), F32),
        ],
        scratch_shapes=[pltpu.VMEM((8, B_WIDTH), F32)],
        compiler_params=pltpu.CompilerParams(
            dimension_semantics=("arbitrary",), vmem_limit_bytes=VMEM_LIMIT),
        name="mixer_layer",
    )(x, w_in.astype(BF16), ln_g.reshape(1, -1), ln_b.reshape(1, -1), w_sp, b_sp.T,
      conv_w, w_out.astype(BF16), g1.reshape(1, -1), b1.reshape(1, -1), w_router.astype(BF16))


def _attn_kernel(sink_ref, x_ref, wqkv_ref, bqkv_ref, bias_ref, wo_ref, bo_ref,
                 g1_ref, b1_ref, wr_ref, x1_ref, logit_ref, kprev_ref, vprev_ref):
    i = pl.program_id(0)
    first = i % (SEQ // CHUNK) == 0

    @pl.when(first)
    def _():
        kprev_ref[...] = jnp.zeros_like(kprev_ref)
        vprev_ref[...] = jnp.zeros_like(vprev_ref)

    x = x_ref[...]
    qkv = jnp.dot(x.astype(BF16), wqkv_ref[...], preferred_element_type=F32) + bqkv_ref[...]
    q = (qkv[:, :ATT_OUT] * (HEAD_DIM ** -0.5)).astype(BF16)

    lane = lax.broadcasted_iota(I32, (CHUNK, KV_WIDTH), 1)
    low = lane < HEAD_DIM

    def halves(t):
        swapped = pltpu.roll(t, HEAD_DIM, 1)
        zero = jnp.zeros_like(t)
        return [jnp.where(low, t, zero).astype(BF16), jnp.where(low, zero, swapped).astype(BF16),
                jnp.where(low, swapped, zero).astype(BF16), jnp.where(low, zero, t).astype(BF16)]

    k_cur = halves(qkv[:, ATT_OUT:ATT_OUT + KV_WIDTH])
    v_cur = halves(qkv[:, ATT_OUT + KV_WIDTH:])

    def stacked(prev_ref, cur, j):
        return jnp.concatenate([prev_ref[2 * j], cur[2 * j], prev_ref[2 * j + 1], cur[2 * j + 1]],
                               axis=0)

    k_rhs = [stacked(kprev_ref, k_cur, j) for j in range(N_KV)]
    v_rhs = [stacked(vprev_ref, v_cur, j) for j in range(N_KV)]
    for idx in range(2 * N_KV):
        kprev_ref[idx] = k_cur[idx]
        vprev_ref[idx] = v_cur[idx]

    a = lax.broadcasted_iota(I32, (CHUNK, 2 * CHUNK), 0)
    c = lax.broadcasted_iota(I32, (CHUNK, 2 * CHUNK), 1)
    lo = jnp.where(first, CHUNK, 0)
    mask = (c > a) & (c <= a + WINDOW) & (c >= lo)
    neg = jnp.finfo(F32).min

    n_pairs = N_HEADS // 2
    kv_of = lambda pair: (2 * pair) // GQA_GROUP
    scores = [lax.dot_general(q[:, pair * 2 * HEAD_DIM:(pair + 1) * 2 * HEAD_DIM],
                              k_rhs[kv_of(pair)], (((1,), (1,)), ((), ())),
                              preferred_element_type=F32) for pair in range(n_pairs)]
    probs = []
    for pair in range(n_pairs):
        sc2 = scores[pair] + bias_ref[pair]
        sides = []
        for side in range(2):
            sc = jnp.where(mask, sc2[:, side * 2 * CHUNK:(side + 1) * 2 * CHUNK], neg)
            sink = sink_ref[2 * pair + side]
            m = jnp.maximum(jnp.max(sc, axis=-1, keepdims=True), sink)
            p = jnp.exp(sc - m)
            denom = jnp.sum(p, axis=-1, keepdims=True) + jnp.exp(sink - m)
            sides.append((p / denom).astype(BF16))
        probs.append(jnp.concatenate(sides, axis=1))
    outs = [jnp.dot(probs[pair], v_rhs[kv_of(pair)], preferred_element_type=F32)
            for pair in range(n_pairs)]
    o = jnp.concatenate(outs, axis=1).astype(BF16)
    m_out = jnp.dot(o, wo_ref[...], preferred_element_type=F32) + bo_ref[...]
    x1 = _layer_norm(ALPHA * x + m_out, g1_ref[...], b1_ref[...])
    x1_ref[...] = x1
    logit_ref[...] = _router_logits(x1, wr_ref)


def _t5_bucket(rel):
    n = jnp.maximum(rel, 0)
    max_exact = N_BUCKETS // 2
    nf = jnp.maximum(n, 1).astype(F32)
    large = max_exact + (jnp.log(nf / max_exact) / math.log(MAX_DISTANCE / max_exact)
                         * (N_BUCKETS - max_exact)).astype(I32)
    large = jnp.minimum(large, N_BUCKETS - 1)
    return jnp.where(n < max_exact, n, large)


def _rel_bias(rel_table):
    a = jnp.arange(CHUNK)[:, None]
    c = jnp.arange(2 * CHUNK)[None, :]
    onehot = jax.nn.one_hot(_t5_bucket(a + CHUNK - c), N_BUCKETS, dtype=F32)
    bias = jnp.einsum('acb,bh->hac', onehot, rel_table.astype(F32),
                      precision=lax.Precision.HIGHEST)
    bias = bias.reshape(N_HEADS // 2, 2, CHUNK, 2 * CHUNK).transpose(0, 2, 1, 3)
    return bias.reshape(N_HEADS // 2, CHUNK, 4 * CHUNK)


def _attn_layer(x, bias, w_qkv, b_qkv, sinks, w_o, b_o, g1, b1, w_router):
    t = x.shape[0]
    const2 = lambda i: (0, 0)
    return pl.pallas_call(
        _attn_kernel,
        grid=(t // CHUNK,),
        in_specs=[
            pl.BlockSpec(memory_space=pltpu.SMEM),
            pl.BlockSpec((CHUNK, D_MODEL), lambda i: (i, 0)),
            pl.BlockSpec((D_MODEL, QKV_DIM), const2),
            pl.BlockSpec((1, QKV_DIM), const2),
            pl.BlockSpec((N_HEADS // 2, CHUNK, 4 * CHUNK), lambda i: (0, 0, 0)),
            pl.BlockSpec((ATT_OUT, D_MODEL), const2),
            pl.BlockSpec((1, D_MODEL), const2),
            pl.BlockSpec((1, D_MODEL), const2),
            pl.BlockSpec((1, D_MODEL), const2),
            pl.BlockSpec((D_MODEL, ROUTER_COLS), const2),
        ],
        out_specs=[
            pl.BlockSpec((CHUNK, D_MODEL), lambda i: (i, 0)),
            pl.BlockSpec((CHUNK, ROUTER_COLS), lambda i: (i, 0)),
        ],
        out_shape=[
            jax.ShapeDtypeStruct((t, D_MODEL), F32),
            jax.ShapeDtypeStruct((t, ROUTER_COLS), F32),
        ],
        scratch_shapes=[pltpu.VMEM((2 * N_KV, CHUNK, KV_WIDTH), BF16),
                        pltpu.VMEM((2 * N_KV, CHUNK, KV_WIDTH), BF16)],
        compiler_params=pltpu.CompilerParams(
            dimension_semantics=("arbitrary",), vmem_limit_bytes=VMEM_LIMIT),
        name="attn_layer",
    )(sinks, x, w_qkv.astype(BF16), b_qkv.reshape(1, -1), bias, w_o.astype(BF16),
      b_o.reshape(1, -1), g1.reshape(1, -1), b1.reshape(1, -1), w_router.astype(BF16))


def _route_kernel(logit_ref, route_ref, route_t_ref, count_ref, base_ref):
    tm = logit_ref.shape[0]
    i = pl.program_id(0)

    @pl.when(i == 0)
    def _():
        base_ref[...] = jnp.zeros_like(base_ref)

    lg = logit_ref[...]
    lane = lax.broadcasted_iota(I32, lg.shape, 1)
    ninf = -jnp.inf

    def first_argmax(vals):
        m = jnp.max(vals, axis=-1, keepdims=True)
        idx = jnp.min(jnp.where(vals == m, lane, LANES), axis=-1, keepdims=True)
        return m, idx

    is_g = lane < N_GROUPS
    gmax, g_idx = first_argmax(jnp.where(is_g, lg, ninf))
    g_p = 1.0 / jnp.sum(jnp.where(is_g, jnp.exp(lg - gmax), 0.0), axis=-1, keepdims=True)

    e_lo = N_GROUPS + g_idx * EXPERTS_PER_GROUP
    in_group = (lane >= e_lo) & (lane < e_lo + EXPERTS_PER_GROUP)
    el = jnp.where(in_group, lg, ninf)
    m1, i1 = first_argmax(el)
    m2, i2 = first_argmax(jnp.where(lane == i1, ninf, el))
    a2 = jnp.exp(m2 - m1)
    gate0 = g_p / (1.0 + a2)
    gate1 = g_p * a2 / (1.0 + a2)

    hit0 = lane == i1
    hit1 = lane == i2
    onehot = jnp.where(hit0 | hit1, 1.0, 0.0)
    r = lax.broadcasted_iota(I32, (tm, tm), 0)
    c = lax.broadcasted_iota(I32, (tm, tm), 1)
    strict_lower = jnp.where(c < r, 1.0, 0.0).astype(BF16)
    before = jnp.dot(strict_lower, onehot.astype(BF16), preferred_element_type=F32)
    before = before + base_ref[...]
    rank0 = jnp.sum(jnp.where(hit0, before, 0.0), axis=-1, keepdims=True)
    rank1 = jnp.sum(jnp.where(hit1, before, 0.0), axis=-1, keepdims=True)
    base_ref[...] += jnp.sum(onehot, axis=0, keepdims=True)

    rec = jnp.zeros(lg.shape, F32)
    for col, val in ((R_E0, (i1 - N_GROUPS).astype(F32)), (R_E1, (i2 - N_GROUPS).astype(F32)),
                     (R_RANK0, rank0), (R_RANK1, rank1), (R_GATE0, gate0), (R_GATE1, gate1)):
        rec = jnp.where(lane == col, val, rec)
    route_ref[...] = rec
    route_t_ref[...] = jnp.transpose(rec)[0:8, :]
    count_ref[...] = base_ref[...]


def _route(logits):
    t = logits.shape[0]
    tm = ROUTE_ROWS
    return pl.pallas_call(
        _route_kernel,
        grid=(t // tm,),
        in_specs=[pl.BlockSpec((tm, ROUTER_COLS), lambda i: (i, 0))],
        out_specs=[
            pl.BlockSpec((tm, LANES), lambda i: (i, 0)),
            pl.BlockSpec((8, tm), lambda i: (0, i)),
            pl.BlockSpec((1, LANES), lambda i: (0, 0)),
        ],
        out_shape=[
            jax.ShapeDtypeStruct((t, LANES), F32),
            jax.ShapeDtypeStruct((8, t), F32),
            jax.ShapeDtypeStruct((1, LANES), F32),
        ],
        scratch_shapes=[pltpu.VMEM((1, LANES), F32)],
        compiler_params=pltpu.CompilerParams(
            dimension_semantics=("arbitrary",), vmem_limit_bytes=VMEM_LIMIT),
        name="route",
    )(logits)


def _plan(route_t, count_lanes):
    t = route_t.shape[1]
    rb = EXPERT_ROWS
    experts = jnp.arange(N_EXPERTS, dtype=I32)
    counts = count_lanes[0, N_GROUPS:N_GROUPS + N_EXPERTS].astype(I32)
    end = jnp.cumsum(counts)
    start = end - counts
    e = route_t[R_E0:R_E1 + 1].astype(I32)
    rank = route_t[R_RANK0:R_RANK1 + 1].astype(I32)
    onehot = e[:, None, :] == experts[None, :, None]
    dest = jnp.sum(jnp.where(onehot, start[None, :, None], 0), axis=1) + rank

    first_blk = start // rb
    last_blk = jnp.maximum(end - 1, 0) // rb
    n_steps_e = jnp.where(counts > 0, last_blk - first_blk + 1, 0)
    step_end = jnp.cumsum(n_steps_e)
    step_start = step_end - n_steps_e
    n_steps = step_end[-1]
    max_steps = t * TOP_K // rb + N_EXPERTS - 1
    s = jnp.minimum(jnp.arange(max_steps, dtype=I32), n_steps - 1)
    step_e = jnp.minimum(jnp.sum((step_end[None, :] <= s[:, None]).astype(I32), axis=1),
                         N_EXPERTS - 1)
    pick = step_e[:, None] == experts[None, :]
    take = lambda v: jnp.sum(jnp.where(pick, v[None, :], 0), axis=1)
    step_blk = take(first_blk) + s - take(step_start)
    step_lo = jnp.clip(take(start) - step_blk * rb, 0, rb)
    step_hi = jnp.clip(take(end) - step_blk * rb, 0, rb)
    return dest, (step_blk, step_e, step_lo, step_hi, n_steps.reshape(1))


def _dest_blocks(dest, tm):
    t = dest.shape[1]
    return dest.reshape(TOP_K, t // tm, tm).transpose(1, 0, 2).reshape(t // tm, 1, TOP_K * tm)


def _to_row_tiles(x):
    return x.reshape(x.shape[0], ROW_TILE[0], ROW_TILE[1])


def _from_row_tiles(x):
    return x.reshape(x.shape[0], D_MODEL)


def _dispatch_kernel(dest_ref, x1_ref, xs_hbm, xbuf, row_sems):
    tm = x1_ref.shape[0]
    i = pl.program_id(0)
    n = pl.num_programs(0)
    slot = i % DISPATCH_SLOTS

    def drain_rows(s):
        for k in range(TOP_K):
            pltpu.make_async_copy(xbuf.at[s], xs_hbm.at[pl.ds(0, tm)], row_sems.at[s]).wait()

    @pl.when(i >= DISPATCH_SLOTS)
    def _():
        drain_rows(slot)

    xbuf[slot] = _to_row_tiles(x1_ref[...])

    def issue(g, carry):
        for j in range(ROWS_PER_ISSUE):
            r = g * ROWS_PER_ISSUE + j
            for k in range(TOP_K):
                pltpu.make_async_copy(xbuf.at[slot, r], xs_hbm.at[dest_ref[0, 0, k * tm + r]],
                                      row_sems.at[slot]).start(priority=k)
        return carry

    lax.fori_loop(0, tm // ROWS_PER_ISSUE, issue, 0)

    @pl.when(i == n - 1)
    def _():
        for back in range(DISPATCH_SLOTS):
            drain_rows((i - back) % DISPATCH_SLOTS)


def _dispatch(x1, dest):
    t = x1.shape[0]
    tm = MOVE_ROWS
    return pl.pallas_call(
        _dispatch_kernel,
        grid=(t // tm,),
        in_specs=[
            pl.BlockSpec((1, 1, TOP_K * tm), lambda i: (i, 0, 0), memory_space=pltpu.SMEM),
            pl.BlockSpec((tm, D_MODEL), lambda i: (i, 0)),
        ],
        out_specs=pl.BlockSpec(memory_space=pl.ANY),
        out_shape=jax.ShapeDtypeStruct((t * TOP_K,) + ROW_TILE, F32),
        scratch_shapes=[
            pltpu.VMEM((DISPATCH_SLOTS, tm) + ROW_TILE, F32),
            pltpu.SemaphoreType.DMA((DISPATCH_SLOTS,)),
        ],
        compiler_params=pltpu.CompilerParams(
            dimension_semantics=("arbitrary",), vmem_limit_bytes=VMEM_LIMIT),
        name="dispatch",
    )(_dest_blocks(dest, tm), x1)


def _expert_kernel(blk_ref, e_ref, lo_ref, hi_ref, n_steps_ref, xs_ref, wg_ref, wu_ref, wd_ref,
                   ys_ref, wg_bf, wu_bf, wd_bf):
    s = pl.program_id(0)
    prev = jnp.maximum(s - 1, 0)
    new_expert = (s == 0) | (e_ref[s] != e_ref[prev])
    new_block = (s == 0) | (blk_ref[s] != blk_ref[prev])

    @pl.when(new_expert)
    def _():
        wg_bf[...] = wg_ref[0, 0].astype(BF16)
        wu_bf[...] = wu_ref[0, 0].astype(BF16)
        wd_bf[...] = wd_ref[0, 0].astype(BF16)

    @pl.when(s < n_steps_ref[0])
    def _():
        xb = _from_row_tiles(xs_ref[...]).astype(BF16)
        gate = jnp.dot(xb, wg_bf[...], preferred_element_type=F32)
        up = jnp.dot(xb, wu_bf[...], preferred_element_type=F32)
        hid = (gate * jax.nn.sigmoid(gate) * up).astype(BF16)
        y = jnp.dot(hid, wd_bf[...], preferred_element_type=F32)
        rows = lax.broadcasted_iota(I32, y.shape, 0)
        y = _to_row_tiles(jnp.where((rows >= lo_ref[s]) & (rows < hi_ref[s]), y, 0.0))

        @pl.when(new_block)
        def _():
            ys_ref[...] = y

        @pl.when(jnp.logical_not(new_block))
        def _():
            ys_ref[...] += y


def _expert_ffn(xs, steps, layer, wg, wu, wd):
    step_blk, step_e, step_lo, step_hi, n_steps = steps
    rb = EXPERT_ROWS
    row_map = lambda s, blk, e, lo, hi, n: (blk[s], 0, 0)
    w_map = lambda s, blk, e, lo, hi, n: (layer, e[s], 0, 0)
    grid_spec = pltpu.PrefetchScalarGridSpec(
        num_scalar_prefetch=5,
        grid=(step_blk.shape[0],),
        in_specs=[
            pl.BlockSpec((rb,) + ROW_TILE, row_map),
            pl.BlockSpec((1, 1, D_MODEL, D_EXPERT), w_map),
            pl.BlockSpec((1, 1, D_MODEL, D_EXPERT), w_map),
            pl.BlockSpec((1, 1, D_EXPERT, D_MODEL), w_map),
        ],
        out_specs=pl.BlockSpec((rb,) + ROW_TILE, row_map),
        scratch_shapes=[
            pltpu.VMEM((D_MODEL, D_EXPERT), BF16),
            pltpu.VMEM((D_MODEL, D_EXPERT), BF16),
            pltpu.VMEM((D_EXPERT, D_MODEL), BF16),
        ],
    )
    return pl.pallas_call(
        _expert_kernel,
        grid_spec=grid_spec,
        out_shape=jax.ShapeDtypeStruct(xs.shape, F32),
        compiler_params=pltpu.CompilerParams(
            dimension_semantics=("arbitrary",), vmem_limit_bytes=VMEM_LIMIT),
        name="expert_ffn",
    )(step_blk, step_e, step_lo, step_hi, n_steps, xs, wg, wu, wd)


def _combine_kernel(dest_ref, dest_next_ref, x1_ref, route_ref, g2_ref, b2_ref, ys_hbm,
                    x2_ref, ybuf, sems):
    tm = x1_ref.shape[0]
    i = pl.program_id(0)
    n = pl.num_programs(0)
    slot = i % 2

    def gather(d_ref, to_slot):
        def issue(g, carry):
            for j in range(ROWS_PER_ISSUE):
                r = g * ROWS_PER_ISSUE + j
                for k in range(TOP_K):
                    pltpu.make_async_copy(ys_hbm.at[d_ref[0, 0, k * tm + r]],
                                          ybuf.at[to_slot, k, r],
                                          sems.at[to_slot]).start(priority=k)
            return carry
        lax.fori_loop(0, tm // ROWS_PER_ISSUE, issue, 0)

    @pl.when(i == 0)
    def _():
        gather(dest_ref, slot)

    @pl.when(i + 1 < n)
    def _():
        gather(dest_next_ref, 1 - slot)

    for k in range(TOP_K):
        pltpu.make_async_copy(ys_hbm.at[pl.ds(0, tm)], ybuf.at[slot, k], sems.at[slot]).wait()

    route = route_ref[...]
    f = (route[:, R_GATE0:R_GATE0 + 1] * _from_row_tiles(ybuf[slot, 0])
         + route[:, R_GATE1:R_GATE1 + 1] * _from_row_tiles(ybuf[slot, 1]))
    x2_ref[...] = _layer_norm(ALPHA * x1_ref[...] + f, g2_ref[...], b2_ref[...])


def _combine_ln(x1, ys, dest, route, g2, b2):
    t = x1.shape[0]
    tm = MOVE_ROWS
    n = t // tm
    row = lambda i: (i, 0)
    const2 = lambda i: (0, 0)
    dest_blocks = _dest_blocks(dest, tm)
    smem_block = lambda index_map: pl.BlockSpec((1, 1, TOP_K * tm), index_map,
                                                memory_space=pltpu.SMEM)
    return pl.pallas_call(
        _combine_kernel,
        grid=(n,),
        in_specs=[
            smem_block(lambda i: (i, 0, 0)),
            smem_block(lambda i: (jnp.minimum(i + 1, n - 1), 0, 0)),
            pl.BlockSpec((tm, D_MODEL), row),
            pl.BlockSpec((tm, LANES), row),
            pl.BlockSpec((1, D_MODEL), const2),
            pl.BlockSpec((1, D_MODEL), const2),
            pl.BlockSpec(memory_space=pl.ANY),
        ],
        out_specs=pl.BlockSpec((tm, D_MODEL), row),
        out_shape=jax.ShapeDtypeStruct((t, D_MODEL), F32),
        scratch_shapes=[
            pltpu.VMEM((2, TOP_K, tm) + ROW_TILE, F32),
            pltpu.SemaphoreType.DMA((2,)),
        ],
        compiler_params=pltpu.CompilerParams(
            dimension_semantics=("arbitrary",), vmem_limit_bytes=VMEM_LIMIT),
        name="combine_ln",
    )(dest_blocks, dest_blocks, x1, route, g2.reshape(1, -1), b2.reshape(1, -1), ys)


def _moe_layer(x1, logits, layer, wg, wu, wd, g2, b2):
    route, route_t, counts = _route(logits)
    dest, steps = _plan(route_t, counts)
    xs = _dispatch(x1, dest)
    ys = _expert_ffn(xs, steps, layer, wg, wu, wd)
    return _combine_ln(x1, ys, dest, route, g2, b2)


def kernel(x, rel_bias_table, mix_w_in, gmlp_ln_g, gmlp_ln_b, gmlp_w_spatial, gmlp_b_spatial, conv_w, mix_w_out, attn_w_qkv, attn_b_qkv, attn_sinks, attn_w_o, attn_b_o, ln1_g, ln1_b, ln2_g, ln2_b, router_group, router_expert, expert_w_gate, expert_w_up, expert_w_down):
    bsz, s, d = x.shape
    assert (s, d) == (SEQ, D_MODEL)
    x = x.reshape(bsz * s, d)
    bias = _rel_bias(rel_bias_table)
    pad = jnp.zeros((DEPTH, D_MODEL, ROUTER_COLS - N_GROUPS - N_EXPERTS), F32)
    w_router = jnp.concatenate([router_group, router_expert, pad], axis=-1)
    for l in range(DEPTH):
        i = l // 2
        if l % 2 == 0:
            x1, logits = _mixer_layer(x, mix_w_in[i], gmlp_ln_g[i], gmlp_ln_b[i],
                                      gmlp_w_spatial[i], gmlp_b_spatial[i], conv_w[i],
                                      mix_w_out[i], ln1_g[l], ln1_b[l], w_router[l])
        else:
            x1, logits = _attn_layer(x, bias, attn_w_qkv[i], attn_b_qkv[i], attn_sinks[i],
                                     attn_w_o[i], attn_b_o[i], ln1_g[l], ln1_b[l], w_router[l])
        x = _moe_layer(x1, logits, l, expert_w_gate, expert_w_up, expert_w_down,
                       ln2_g[l], ln2_b[l])
    return x.reshape(bsz, s, d)
```

```python
import math

import jax
import jax.numpy as jnp
from jax import lax
from jax.experimental import pallas as pl
from jax.experimental.pallas import tpu as pltpu

D_MODEL = 1024
SEQ = 16384
DEPTH = 4
CHUNK = 128
A_GROUPS = 4
A_CH = 128
A_WIDTH = A_GROUPS * A_CH
B_WIDTH = 512
CONV_W = 3
MIX_IN = 2 * A_WIDTH + 3 * B_WIDTH
N_HEADS = 16
N_KV = 2
HEAD_DIM = 64
GQA_GROUP = N_HEADS // N_KV
WINDOW = 128
QKV_DIM = (N_HEADS + 2 * N_KV) * HEAD_DIM
ATT_OUT = N_HEADS * HEAD_DIM
KV_WIDTH = N_KV * HEAD_DIM
N_BUCKETS = 32
MAX_DISTANCE = 128
N_GROUPS = 4
EXPERTS_PER_GROUP = 8
N_EXPERTS = N_GROUPS * EXPERTS_PER_GROUP
TOP_K = 2
D_EXPERT = 512
ALPHA = (2 * DEPTH) ** 0.25
LN_EPS = 1e-5

LANES = 128
SUBLANES = 8
ROW_TILE = (SUBLANES, LANES)
assert D_MODEL == SUBLANES * LANES
ROUTER_COLS = LANES
MIX_ROWS = 512
ATTN_ROWS = 512
ROUTE_ROWS = 512
MOVE_ROWS = 256
EXPERT_ROWS = 512
ROWS_PER_ISSUE = 8
DISPATCH_SLOTS = 3
VMEM_LIMIT = 56 * 1024 * 1024

R_E0, R_E1, R_RANK0, R_RANK1, R_GATE0, R_GATE1 = range(6)

F32 = jnp.float32
BF16 = jnp.bfloat16
I32 = jnp.int32


def _layer_norm(x, g, b):
    mu = jnp.mean(x, axis=-1, keepdims=True)
    xc = x - mu
    var = jnp.mean(xc * xc, axis=-1, keepdims=True)
    return xc * lax.rsqrt(var + LN_EPS) * g + b


def _gelu(x):
    return 0.5 * x * (1.0 + lax.erf(x * (2.0 ** -0.5)))


def _router_logits(x1, wr_ref):
    return jnp.dot(x1.astype(BF16), wr_ref[...], preferred_element_type=F32)


def _mixer_kernel(x_ref, win_ref, lng_ref, lnb_ref, wsp_ref, bsp_ref, cw_ref, wout_ref,
                  g1_ref, b1_ref, wr_ref, x1_ref, logit_ref, ztail_ref):
    tm = x_ref.shape[0]
    i = pl.program_id(0)

    @pl.when(i % (SEQ // tm) == 0)
    def _():
        ztail_ref[...] = jnp.zeros_like(ztail_ref)

    x = x_ref[...]
    h = jnp.dot(x.astype(BF16), win_ref[...], preferred_element_type=F32)
    u = _gelu(h[:, :A_WIDTH])
    v = _gelu(h[:, A_WIDTH:2 * A_WIDTH])
    o = 2 * A_WIDTH
    g_b = h[:, o:o + B_WIDTH]
    g_c = h[:, o + B_WIDTH:o + 2 * B_WIDTH]
    hb = h[:, o + 2 * B_WIDTH:o + 3 * B_WIDTH]

    v = _layer_norm(v, lng_ref[...], lnb_ref[...]).astype(BF16)
    n_chunks = tm // CHUNK
    ri = lax.broadcasted_iota(I32, (CHUNK, CHUNK), 0)
    ci = lax.broadcasted_iota(I32, (CHUNK, CHUNK), 1)
    causal = ci <= ri
    sv_cols = [[None] * A_GROUPS for _ in range(n_chunks)]
    for g in range(A_GROUPS):
        ws = jnp.where(causal, wsp_ref[g], 0.0).astype(BF16)
        vg = jnp.concatenate(
            [v[c * CHUNK:(c + 1) * CHUNK, g * A_CH:(g + 1) * A_CH] for c in range(n_chunks)],
            axis=1)
        sg = jnp.dot(ws, vg, preferred_element_type=F32) + bsp_ref[:, g:g + 1]
        for c in range(n_chunks):
            sv_cols[c][g] = sg[:, c * A_CH:(c + 1) * A_CH]
    sv = jnp.concatenate([jnp.concatenate(row, axis=1) for row in sv_cols], axis=0)
    y_a = u * sv

    z = g_c * hb
    rows = lax.broadcasted_iota(I32, z.shape, 0)
    tail = ztail_ref[...]
    zm1 = jnp.where(rows == 0, tail[7:8, :], pltpu.roll(z, 1, 0))
    zm2 = jnp.where(rows == 0, tail[6:7, :],
                    jnp.where(rows == 1, tail[7:8, :], pltpu.roll(z, 2, 0)))
    conv = cw_ref[0:1, :] * zm2 + cw_ref[1:2, :] * zm1 + cw_ref[2:3, :] * z
    y_b = g_b * conv
    ztail_ref[...] = z[tm - 8:tm, :]

    y = jnp.concatenate([y_a, y_b], axis=1).astype(BF16)
    m = jnp.dot(y, wout_ref[...], preferred_element_type=F32)
    x1 = _layer_norm(ALPHA * x + m, g1_ref[...], b1_ref[...])
    x1_ref[...] = x1
    logit_ref[...] = _router_logits(x1, wr_ref)


def _mixer_layer(x, w_in, ln_g, ln_b, w_sp, b_sp, conv_w, w_out, g1, b1, w_router):
    t = x.shape[0]
    tm = MIX_ROWS
    const2 = lambda i: (0, 0)
    return pl.pallas_call(
        _mixer_kernel,
        grid=(t // tm,),
        in_specs=[
            pl.BlockSpec((tm, D_MODEL), lambda i: (i, 0)),
            pl.BlockSpec((D_MODEL, MIX_IN), const2),
            pl.BlockSpec((1, A_WIDTH), const2),
            pl.BlockSpec((1, A_WIDTH), const2),
            pl.BlockSpec((A_GROUPS, CHUNK, CHUNK), lambda i: (0, 0, 0)),
            pl.BlockSpec((CHUNK, A_GROUPS), const2),
            pl.BlockSpec((CONV_W, B_WIDTH), const2),
            pl.BlockSpec((A_WIDTH + B_WIDTH, D_MODEL), const2),
            pl.BlockSpec((1, D_MODEL), const2),
            pl.BlockSpec((1, D_MODEL), const2),
            pl.BlockSpec((D_MODEL, ROUTER_COLS), const2),
        ],
        out_specs=[
            pl.BlockSpec((tm, D_MODEL), lambda i: (i, 0)),
            pl.BlockSpec((tm, ROUTER_COLS), lambda i: (i, 0)),
        ],
        out_shape=[
            jax.ShapeDtypeStruct((t, D_MODEL), F32),
            jax.ShapeDtypeStruct((t, ROUTER_COLS), F32),
        ],
        scratch_shapes=[pltpu.VMEM((8, B_WIDTH), F32)],
        compiler_params=pltpu.CompilerParams(
            dimension_semantics=("arbitrary",), vmem_limit_bytes=VMEM_LIMIT),
        name="mixer_layer",
    )(x, w_in.astype(BF16), ln_g.reshape(1, -1), ln_b.reshape(1, -1), w_sp, b_sp.T,
      conv_w, w_out.astype(BF16), g1.reshape(1, -1), b1.reshape(1, -1), w_router.astype(BF16))


def _attn_kernel(sink_ref, x_ref, wqkv_ref, bqkv_ref, bias_ref, wo_ref, bo_ref,
                 g1_ref, b1_ref, wr_ref, x1_ref, logit_ref, kprev_ref, vprev_ref):
    tm = x_ref.shape[0]
    i = pl.program_id(0)
    first = i % (SEQ // tm) == 0

    @pl.when(first)
    def _():
        kprev_ref[...] = jnp.zeros_like(kprev_ref)
        vprev_ref[...] = jnp.zeros_like(vprev_ref)

    x = x_ref[...]
    qkv = jnp.dot(x.astype(BF16), wqkv_ref[...], preferred_element_type=F32) + bqkv_ref[...]
    q = (qkv[:, :ATT_OUT] * (HEAD_DIM ** -0.5)).astype(BF16)

    lane = lax.broadcasted_iota(I32, (CHUNK, KV_WIDTH), 1)
    low = lane < HEAD_DIM

    def halves(t):
        swapped = pltpu.roll(t, HEAD_DIM, 1)
        zero = jnp.zeros_like(t)
        return [jnp.where(low, t, zero).astype(BF16), jnp.where(low, zero, swapped).astype(BF16),
                jnp.where(low, swapped, zero).astype(BF16), jnp.where(low, zero, t).astype(BF16)]

    def stacked(prev, cur, j):
        return jnp.concatenate([prev[2 * j], cur[2 * j], prev[2 * j + 1], cur[2 * j + 1]], axis=0)

    a = lax.broadcasted_iota(I32, (CHUNK, 2 * CHUNK), 0)
    c = lax.broadcasted_iota(I32, (CHUNK, 2 * CHUNK), 1)
    window = (c > a) & (c <= a + WINDOW)
    neg = jnp.finfo(F32).min
    n_pairs = N_HEADS // 2
    kv_of = lambda pair: (2 * pair) // GQA_GROUP

    k_prev = [kprev_ref[idx] for idx in range(2 * N_KV)]
    v_prev = [vprev_ref[idx] for idx in range(2 * N_KV)]
    o_blocks = []
    for blk in range(tm // CHUNK):
        rows = slice(blk * CHUNK, (blk + 1) * CHUNK)
        k_cur = halves(qkv[rows, ATT_OUT:ATT_OUT + KV_WIDTH])
        v_cur = halves(qkv[rows, ATT_OUT + KV_WIDTH:])
        k_rhs = [stacked(k_prev, k_cur, j) for j in range(N_KV)]
        v_rhs = [stacked(v_prev, v_cur, j) for j in range(N_KV)]
        k_prev, v_prev = k_cur, v_cur
        mask = window & (c >= jnp.where(first, CHUNK, 0)) if blk == 0 else window

        scores = [lax.dot_general(q[rows, pair * 2 * HEAD_DIM:(pair + 1) * 2 * HEAD_DIM],
                                  k_rhs[kv_of(pair)], (((1,), (1,)), ((), ())),
                                  preferred_element_type=F32) for pair in range(n_pairs)]
        probs = []
        for pair in range(n_pairs):
            sc2 = scores[pair] + bias_ref[pair]
            sides = []
            for side in range(2):
                sc = jnp.where(mask, sc2[:, side * 2 * CHUNK:(side + 1) * 2 * CHUNK], neg)
                sink = sink_ref[2 * pair + side]
                m = jnp.maximum(jnp.max(sc, axis=-1, keepdims=True), sink)
                p = jnp.exp(sc - m)
                denom = jnp.sum(p, axis=-1, keepdims=True) + jnp.exp(sink - m)
                sides.append((p / denom).astype(BF16))
            probs.append(jnp.concatenate(sides, axis=1))
        outs = [jnp.dot(probs[pair], v_rhs[kv_of(pair)], preferred_element_type=F32)
                for pair in range(n_pairs)]
        o_blocks.append(jnp.concatenate(outs, axis=1).astype(BF16))
    for idx in range(2 * N_KV):
        kprev_ref[idx] = k_prev[idx]
        vprev_ref[idx] = v_prev[idx]
    o = jnp.concatenate(o_blocks, axis=0)
    m_out = jnp.dot(o, wo_ref[...], preferred_element_type=F32) + bo_ref[...]
    x1 = _layer_norm(ALPHA * x + m_out, g1_ref[...], b1_ref[...])
    x1_ref[...] = x1
    logit_ref[...] = _router_logits(x1, wr_ref)


def _t5_bucket(rel):
    n = jnp.maximum(rel, 0)
    max_exact = N_BUCKETS // 2
    nf = jnp.maximum(n, 1).astype(F32)
    large = max_exact + (jnp.log(nf / max_exact) / math.log(MAX_DISTANCE / max_exact)
                         * (N_BUCKETS - max_exact)).astype(I32)
    large = jnp.minimum(large, N_BUCKETS - 1)
    return jnp.where(n < max_exact, n, large)


def _rel_bias(rel_table):
    a = jnp.arange(CHUNK)[:, None]
    c = jnp.arange(2 * CHUNK)[None, :]
    onehot = jax.nn.one_hot(_t5_bucket(a + CHUNK - c), N_BUCKETS, dtype=F32)
    bias = jnp.einsum('acb,bh->hac', onehot, rel_table.astype(F32),
                      precision=lax.Precision.HIGHEST)
    bias = bias.reshape(N_HEADS // 2, 2, CHUNK, 2 * CHUNK).transpose(0, 2, 1, 3)
    return bias.reshape(N_HEADS // 2, CHUNK, 4 * CHUNK)


def _attn_layer(x, bias, w_qkv, b_qkv, sinks, w_o, b_o, g1, b1, w_router):
    t = x.shape[0]
    tm = ATTN_ROWS
    const2 = lambda i: (0, 0)
    return pl.pallas_call(
        _attn_kernel,
        grid=(t // tm,),
        in_specs=[
            pl.BlockSpec(memory_space=pltpu.SMEM),
            pl.BlockSpec((tm, D_MODEL), lambda i: (i, 0)),
            pl.BlockSpec((D_MODEL, QKV_DIM), const2),
            pl.BlockSpec((1, QKV_DIM), const2),
            pl.BlockSpec((N_HEADS // 2, CHUNK, 4 * CHUNK), lambda i: (0, 0, 0)),
            pl.BlockSpec((ATT_OUT, D_MODEL), const2),
            pl.BlockSpec((1, D_MODEL), const2),
            pl.BlockSpec((1, D_MODEL), const2),
            pl.BlockSpec((1, D_MODEL), const2),
            pl.BlockSpec((D_MODEL, ROUTER_COLS), const2),
        ],
        out_specs=[
            pl.BlockSpec((tm, D_MODEL), lambda i: (i, 0)),
            pl.BlockSpec((tm, ROUTER_COLS), lambda i: (i, 0)),
        ],
        out_shape=[
            jax.ShapeDtypeStruct((t, D_MODEL), F32),
            jax.ShapeDtypeStruct((t, ROUTER_COLS), F32),
        ],
        scratch_shapes=[pltpu.VMEM((2 * N_KV, CHUNK, KV_WIDTH), BF16),
                        pltpu.VMEM((2 * N_KV, CHUNK, KV_WIDTH), BF16)],
        compiler_params=pltpu.CompilerParams(
            dimension_semantics=("arbitrary",), vmem_limit_bytes=VMEM_LIMIT),
        name="attn_layer",
    )(sinks, x, w_qkv.astype(BF16), b_qkv.reshape(1, -1), bias, w_o.astype(BF16),
      b_o.reshape(1, -1), g1.reshape(1, -1), b1.reshape(1, -1), w_router.astype(BF16))


def _route_kernel(logit_ref, route_ref, route_t_ref, count_ref, base_ref):
    tm = logit_ref.shape[0]
    i = pl.program_id(0)

    @pl.when(i == 0)
    def _():
        base_ref[...] = jnp.zeros_like(base_ref)

    lg = logit_ref[...]
    lane = lax.broadcasted_iota(I32, lg.shape, 1)
    ninf = -jnp.inf

    def first_argmax(vals):
        m = jnp.max(vals, axis=-1, keepdims=True)
        idx = jnp.min(jnp.where(vals == m, lane, LANES), axis=-1, keepdims=True)
        return m, idx

    is_g = lane < N_GROUPS
    gmax, g_idx = first_argmax(jnp.where(is_g, lg, ninf))
    g_p = 1.0 / jnp.sum(jnp.where(is_g, jnp.exp(lg - gmax), 0.0), axis=-1, keepdims=True)

    e_lo = N_GROUPS + g_idx * EXPERTS_PER_GROUP
    in_group = (lane >= e_lo) & (lane < e_lo + EXPERTS_PER_GROUP)
    el = jnp.where(in_group, lg, ninf)
    m1, i1 = first_argmax(el)
    m2, i2 = first_argmax(jnp.where(lane == i1, ninf, el))
    a2 = jnp.exp(m2 - m1)
    gate0 = g_p / (1.0 + a2)
    gate1 = g_p * a2 / (1.0 + a2)

    hit0 = lane == i1
    hit1 = lane == i2
    onehot = jnp.where(hit0 | hit1, 1.0, 0.0)
    r = lax.broadcasted_iota(I32, (tm, tm), 0)
    c = lax.broadcasted_iota(I32, (tm, tm), 1)
    strict_lower = jnp.where(c < r, 1.0, 0.0).astype(BF16)
    before = jnp.dot(strict_lower, onehot.astype(BF16), preferred_element_type=F32)
    before = before + base_ref[...]
    rank0 = jnp.sum(jnp.where(hit0, before, 0.0), axis=-1, keepdims=True)
    rank1 = jnp.sum(jnp.where(hit1, before, 0.0), axis=-1, keepdims=True)
    base_ref[...] += jnp.sum(onehot, axis=0, keepdims=True)

    rec = jnp.zeros(lg.shape, F32)
    for col, val in ((R_E0, (i1 - N_GROUPS).astype(F32)), (R_E1, (i2 - N_GROUPS).astype(F32)),
                     (R_RANK0, rank0), (R_RANK1, rank1), (R_GATE0, gate0), (R_GATE1, gate1)):
        rec = jnp.where(lane == col, val, rec)
    route_ref[...] = rec
    route_t_ref[...] = jnp.transpose(rec)[0:8, :]
    count_ref[...] = base_ref[...]


def _route(logits):
    t = logits.shape[0]
    tm = ROUTE_ROWS
    return pl.pallas_call(
        _route_kernel,
        grid=(t // tm,),
        in_specs=[pl.BlockSpec((tm, ROUTER_COLS), lambda i: (i, 0))],
        out_specs=[
            pl.BlockSpec((tm, LANES), lambda i: (i, 0)),
            pl.BlockSpec((8, tm), lambda i: (0, i)),
            pl.BlockSpec((1, LANES), lambda i: (0, 0)),
        ],
        out_shape=[
            jax.ShapeDtypeStruct((t, LANES), F32),
            jax.ShapeDtypeStruct((8, t), F32),
            jax.ShapeDtypeStruct((1, LANES), F32),
        ],
        scratch_shapes=[pltpu.VMEM((1, LANES), F32)],
        compiler_params=pltpu.CompilerParams(
            dimension_semantics=("arbitrary",), vmem_limit_bytes=VMEM_LIMIT),
        name="route",
    )(logits)


def _plan(route_t, count_lanes):
    t = route_t.shape[1]
    rb = EXPERT_ROWS
    experts = jnp.arange(N_EXPERTS, dtype=I32)
    counts = count_lanes[0, N_GROUPS:N_GROUPS + N_EXPERTS].astype(I32)
    end = jnp.cumsum(counts)
    start = end - counts
    e = route_t[R_E0:R_E1 + 1].astype(I32)
    rank = route_t[R_RANK0:R_RANK1 + 1].astype(I32)
    onehot = e[:, None, :] == experts[None, :, None]
    dest = jnp.sum(jnp.where(onehot, start[None, :, None], 0), axis=1) + rank

    first_blk = start // rb
    last_blk = jnp.maximum(end - 1, 0) // rb
    n_steps_e = jnp.where(counts > 0, last_blk - first_blk + 1, 0)
    step_end = jnp.cumsum(n_steps_e)
    step_start = step_end - n_steps_e
    n_steps = step_end[-1]
    max_steps = t * TOP_K // rb + N_EXPERTS - 1
    s = jnp.minimum(jnp.arange(max_steps, dtype=I32), n_steps - 1)
    step_e = jnp.minimum(jnp.sum((step_end[None, :] <= s[:, None]).astype(I32), axis=1),
                         N_EXPERTS - 1)
    pick = step_e[:, None] == experts[None, :]
    take = lambda v: jnp.sum(jnp.where(pick, v[None, :], 0), axis=1)
    step_blk = take(first_blk) + s - take(step_start)
    step_lo = jnp.clip(take(start) - step_blk * rb, 0, rb)
    step_hi = jnp.clip(take(end) - step_blk * rb, 0, rb)
    return dest, (step_blk, step_e, step_lo, step_hi, n_steps.reshape(1))


def _dest_blocks(dest, tm):
    t = dest.shape[1]
    return dest.reshape(TOP_K, t // tm, tm).transpose(1, 0, 2).reshape(t // tm, 1, TOP_K * tm)


def _to_row_tiles(x):
    return x.reshape(x.shape[0], ROW_TILE[0], ROW_TILE[1])


def _from_row_tiles(x):
    return x.reshape(x.shape[0], D_MODEL)


def _dispatch_kernel(dest_ref, x1_ref, xs_hbm, xbuf, row_sems):
    tm = x1_ref.shape[0]
    i = pl.program_id(0)
    n = pl.num_programs(0)
    slot = i % DISPATCH_SLOTS

    def drain_rows(s):
        for k in range(TOP_K):
            pltpu.make_async_copy(xbuf.at[s], xs_hbm.at[pl.ds(0, tm)], row_sems.at[s]).wait()

    @pl.when(i >= DISPATCH_SLOTS)
    def _():
        drain_rows(slot)

    xbuf[slot] = _to_row_tiles(x1_ref[...])

    def issue(g, carry):
        for j in range(ROWS_PER_ISSUE):
            r = g * ROWS_PER_ISSUE + j
            for k in range(TOP_K):
                pltpu.make_async_copy(xbuf.at[slot, r], xs_hbm.at[dest_ref[0, 0, k * tm + r]],
                                      row_sems.at[slot]).start(priority=k)
        return carry

    lax.fori_loop(0, tm // ROWS_PER_ISSUE, issue, 0)

    @pl.when(i == n - 1)
    def _():
        for back in range(DISPATCH_SLOTS):
            drain_rows((i - back) % DISPATCH_SLOTS)


def _dispatch(x1, dest):
    t = x1.shape[0]
    tm = MOVE_ROWS
    return pl.pallas_call(
        _dispatch_kernel,
        grid=(t // tm,),
        in_specs=[
            pl.BlockSpec((1, 1, TOP_K * tm), lambda i: (i, 0, 0), memory_space=pltpu.SMEM),
            pl.BlockSpec((tm, D_MODEL), lambda i: (i, 0)),
        ],
        out_specs=pl.BlockSpec(memory_space=pl.ANY),
        out_shape=jax.ShapeDtypeStruct((t * TOP_K,) + ROW_TILE, F32),
        scratch_shapes=[
            pltpu.VMEM((DISPATCH_SLOTS, tm) + ROW_TILE, F32),
            pltpu.SemaphoreType.DMA((DISPATCH_SLOTS,)),
        ],
        compiler_params=pltpu.CompilerParams(
            dimension_semantics=("arbitrary",), vmem_limit_bytes=VMEM_LIMIT),
        name="dispatch",
    )(_dest_blocks(dest, tm), x1)


def _expert_kernel(blk_ref, e_ref, lo_ref, hi_ref, n_steps_ref, xs_ref, wg_ref, wu_ref, wd_ref,
                   ys_ref, wg_bf, wu_bf, wd_bf):
    s = pl.program_id(0)
    prev = jnp.maximum(s - 1, 0)
    new_expert = (s == 0) | (e_ref[s] != e_ref[prev])
    new_block = (s == 0) | (blk_ref[s] != blk_ref[prev])

    @pl.when(new_expert)
    def _():
        wg_bf[...] = wg_ref[0, 0].astype(BF16)
        wu_bf[...] = wu_ref[0, 0].astype(BF16)
        wd_bf[...] = wd_ref[0, 0].astype(BF16)

    @pl.when(s < n_steps_ref[0])
    def _():
        xb = _from_row_tiles(xs_ref[...]).astype(BF16)
        gate = jnp.dot(xb, wg_bf[...], preferred_element_type=F32)
        up = jnp.dot(xb, wu_bf[...], preferred_element_type=F32)
        hid = (gate * jax.nn.sigmoid(gate) * up).astype(BF16)
        y = jnp.dot(hid, wd_bf[...], preferred_element_type=F32)
        rows = lax.broadcasted_iota(I32, y.shape, 0)
        y = _to_row_tiles(jnp.where((rows >= lo_ref[s]) & (rows < hi_ref[s]), y, 0.0))

        @pl.when(new_block)
        def _():
            ys_ref[...] = y

        @pl.when(jnp.logical_not(new_block))
        def _():
            ys_ref[...] += y


def _expert_ffn(xs, steps, layer, wg, wu, wd):
    step_blk, step_e, step_lo, step_hi, n_steps = steps
    rb = EXPERT_ROWS
    row_map = lambda s, blk, e, lo, hi, n: (blk[s], 0, 0)
    w_map = lambda s, blk, e, lo, hi, n: (layer, e[s], 0, 0)
    grid_spec = pltpu.PrefetchScalarGridSpec(
        num_scalar_prefetch=5,
        grid=(step_blk.shape[0],),
        in_specs=[
            pl.BlockSpec((rb,) + ROW_TILE, row_map),
            pl.BlockSpec((1, 1, D_MODEL, D_EXPERT), w_map),
            pl.BlockSpec((1, 1, D_MODEL, D_EXPERT), w_map),
            pl.BlockSpec((1, 1, D_EXPERT, D_MODEL), w_map),
        ],
        out_specs=pl.BlockSpec((rb,) + ROW_TILE, row_map),
        scratch_shapes=[
            pltpu.VMEM((D_MODEL, D_EXPERT), BF16),
            pltpu.VMEM((D_MODEL, D_EXPERT), BF16),
            pltpu.VMEM((D_EXPERT, D_MODEL), BF16),
        ],
    )
    return pl.pallas_call(
        _expert_kernel,
        grid_spec=grid_spec,
        out_shape=jax.ShapeDtypeStruct(xs.shape, F32),
        compiler_params=pltpu.CompilerParams(
            dimension_semantics=("arbitrary",), vmem_limit_bytes=VMEM_LIMIT),
        name="expert_ffn",
    )(step_blk, step_e, step_lo, step_hi, n_steps, xs, wg, wu, wd)


def _combine_kernel(dest_ref, dest_next_ref, x1_ref, route_ref, g2_ref, b2_ref, ys_hbm,
                    x2_ref, ybuf, sems):
    tm = x1_ref.shape[0]
    i = pl.program_id(0)
    n = pl.num_programs(0)
    slot = i % 2

    def gather(d_ref, to_slot):
        def issue(g, carry):
            for j in range(ROWS_PER_ISSUE):
                r = g * ROWS_PER_ISSUE + j
                for k in range(TOP_K):
                    pltpu.make_async_copy(ys_hbm.at[d_ref[0, 0, k * tm + r]],
                                          ybuf.at[to_slot, k, r],
                                          sems.at[to_slot]).start(priority=k)
            return carry
        lax.fori_loop(0, tm // ROWS_PER_ISSUE, issue, 0)

    @pl.when(i == 0)
    def _():
        gather(dest_ref, slot)

    @pl.when(i + 1 < n)
    def _():
        gather(dest_next_ref, 1 - slot)

    for k in range(TOP_K):
        pltpu.make_async_copy(ys_hbm.at[pl.ds(0, tm)], ybuf.at[slot, k], sems.at[slot]).wait()

    route = route_ref[...]
    f = (route[:, R_GATE0:R_GATE0 + 1] * _from_row_tiles(ybuf[slot, 0])
         + route[:, R_GATE1:R_GATE1 + 1] * _from_row_tiles(ybuf[slot, 1]))
    x2_ref[...] = _layer_norm(ALPHA * x1_ref[...] + f, g2_ref[...], b2_ref[...])


def _combine_ln(x1, ys, dest, route, g2, b2):
    t = x1.shape[0]
    tm = MOVE_ROWS
    n = t // tm
    row = lambda i: (i, 0)
    const2 = lambda i: (0, 0)
    dest_blocks = _dest_blocks(dest, tm)
    smem_block = lambda index_map: pl.BlockSpec((1, 1, TOP_K * tm), index_map,
                                                memory_space=pltpu.SMEM)
    return pl.pallas_call(
        _combine_kernel,
        grid=(n,),
        in_specs=[
            smem_block(lambda i: (i, 0, 0)),
            smem_block(lambda i: (jnp.minimum(i + 1, n - 1), 0, 0)),
            pl.BlockSpec((tm, D_MODEL), row),
            pl.BlockSpec((tm, LANES), row),
            pl.BlockSpec((1, D_MODEL), const2),
            pl.BlockSpec((1, D_MODEL), const2),
            pl.BlockSpec(memory_space=pl.ANY),
        ],
        out_specs=pl.BlockSpec((tm, D_MODEL), row),
        out_shape=jax.ShapeDtypeStruct((t, D_MODEL), F32),
        scratch_shapes=[
            pltpu.VMEM((2, TOP_K, tm) + ROW_TILE, F32),
            pltpu.SemaphoreType.DMA((2,)),
        ],
        compiler_params=pltpu.CompilerParams(
            dimension_semantics=("arbitrary",), vmem_limit_bytes=VMEM_LIMIT),
        name="combine_ln",
    )(dest_blocks, dest_blocks, x1, route, g2.reshape(1, -1), b2.reshape(1, -1), ys)


def _moe_layer(x1, logits, layer, wg, wu, wd, g2, b2):
    route, route_t, counts = _route(logits)
    dest, steps = _plan(route_t, counts)
    xs = _dispatch(x1, dest)
    ys = _expert_ffn(xs, steps, layer, wg, wu, wd)
    return _combine_ln(x1, ys, dest, route, g2, b2)


def kernel(x, rel_bias_table, mix_w_in, gmlp_ln_g, gmlp_ln_b, gmlp_w_spatial, gmlp_b_spatial, conv_w, mix_w_out, attn_w_qkv, attn_b_qkv, attn_sinks, attn_w_o, attn_b_o, ln1_g, ln1_b, ln2_g, ln2_b, router_group, router_expert, expert_w_gate, expert_w_up, expert_w_down):
    bsz, s, d = x.shape
    assert (s, d) == (SEQ, D_MODEL)
    x = x.reshape(bsz * s, d)
    bias = _rel_bias(rel_bias_table)
    pad = jnp.zeros((DEPTH, D_MODEL, ROUTER_COLS - N_GROUPS - N_EXPERTS), F32)
    w_router = jnp.concatenate([router_group, router_expert, pad], axis=-1)
    for l in range(DEPTH):
        i = l // 2
        if l % 2 == 0:
            x1, logits = _mixer_layer(x, mix_w_in[i], gmlp_ln_g[i], gmlp_ln_b[i],
                                      gmlp_w_spatial[i], gmlp_b_spatial[i], conv_w[i],
                                      mix_w_out[i], ln1_g[l], ln1_b[l], w_router[l])
        else:
            x1, logits = _attn_layer(x, bias, attn_w_qkv[i], attn_b_qkv[i], attn_sinks[i],
                                     attn_w_o[i], attn_b_o[i], ln1_g[l], ln1_b[l], w_router[l])
        x = _moe_layer(x1, logits, l, expert_w_gate, expert_w_up, expert_w_down,
                       ln2_g[l], ln2_b[l])
    return x.reshape(bsz, s, d)
```

```python
import math

import jax
import jax.numpy as jnp
from jax import lax
from jax.experimental import pallas as pl
from jax.experimental.pallas import tpu as pltpu

D_MODEL = 1024
SEQ = 16384
DEPTH = 4
CHUNK = 128
A_GROUPS = 4
A_CH = 128
A_WIDTH = A_GROUPS * A_CH
B_WIDTH = 512
CONV_W = 3
MIX_IN = 2 * A_WIDTH + 3 * B_WIDTH
N_HEADS = 16
N_KV = 2
HEAD_DIM = 64
GQA_GROUP = N_HEADS // N_KV
WINDOW = 128
QKV_DIM = (N_HEADS + 2 * N_KV) * HEAD_DIM
ATT_OUT = N_HEADS * HEAD_DIM
KV_WIDTH = N_KV * HEAD_DIM
N_BUCKETS = 32
MAX_DISTANCE = 128
N_GROUPS = 4
EXPERTS_PER_GROUP = 8
N_EXPERTS = N_GROUPS * EXPERTS_PER_GROUP
TOP_K = 2
D_EXPERT = 512
ALPHA = (2 * DEPTH) ** 0.25
LN_EPS = 1e-5

LANES = 128
SUBLANES = 8
ROW_TILE = (SUBLANES, LANES)
assert D_MODEL == SUBLANES * LANES
ROUTER_ROWS = SUBLANES + N_EXPERTS
MIX_ROWS = 512
ATTN_ROWS = 512
MOVE_ROWS = 512
EXPERT_ROWS = 512
ROWS_PER_ISSUE = 8
DISPATCH_SLOTS = 3
VMEM_LIMIT = 56 * 1024 * 1024

R_E0, R_E1, R_RANK0, R_RANK1, R_GATE0, R_GATE1 = range(6)

F32 = jnp.float32
BF16 = jnp.bfloat16
I32 = jnp.int32


def _layer_norm(x, g, b):
    mu = jnp.mean(x, axis=-1, keepdims=True)
    xc = x - mu
    var = jnp.mean(xc * xc, axis=-1, keepdims=True)
    return xc * lax.rsqrt(var + LN_EPS) * g + b


def _gelu(x):
    return 0.5 * x * (1.0 + lax.erf(x * (2.0 ** -0.5)))


def _mixer_kernel(x_ref, win_ref, lng_ref, lnb_ref, wsp_ref, bsp_ref, cw_ref, wout_ref,
                  g1_ref, b1_ref, wr_ref, tri_ref, x1_ref, route_ref, route_t_ref, count_ref,
                  ztail_ref, base_ref):
    tm = x_ref.shape[0]
    i = pl.program_id(0)

    @pl.when(i % (SEQ // tm) == 0)
    def _():
        ztail_ref[...] = jnp.zeros_like(ztail_ref)

    x = x_ref[...]
    h = jnp.dot(x.astype(BF16), win_ref[...], preferred_element_type=F32)
    u = _gelu(h[:, :A_WIDTH])
    v = _gelu(h[:, A_WIDTH:2 * A_WIDTH])
    o = 2 * A_WIDTH
    g_b = h[:, o:o + B_WIDTH]
    g_c = h[:, o + B_WIDTH:o + 2 * B_WIDTH]
    hb = h[:, o + 2 * B_WIDTH:o + 3 * B_WIDTH]

    v = _layer_norm(v, lng_ref[...], lnb_ref[...]).astype(BF16)
    n_chunks = tm // CHUNK
    ri = lax.broadcasted_iota(I32, (CHUNK, CHUNK), 0)
    ci = lax.broadcasted_iota(I32, (CHUNK, CHUNK), 1)
    causal = ci <= ri
    sv_cols = [[None] * A_GROUPS for _ in range(n_chunks)]
    for g in range(A_GROUPS):
        ws = jnp.where(causal, wsp_ref[g], 0.0).astype(BF16)
        vg = jnp.concatenate(
            [v[c * CHUNK:(c + 1) * CHUNK, g * A_CH:(g + 1) * A_CH] for c in range(n_chunks)],
            axis=1)
        sg = jnp.dot(ws, vg, preferred_element_type=F32) + bsp_ref[:, g:g + 1]
        for c in range(n_chunks):
            sv_cols[c][g] = sg[:, c * A_CH:(c + 1) * A_CH]
    sv = jnp.concatenate([jnp.concatenate(row, axis=1) for row in sv_cols], axis=0)
    y_a = u * sv

    z = g_c * hb
    rows = lax.broadcasted_iota(I32, z.shape, 0)
    tail = ztail_ref[...]
    zm1 = jnp.where(rows == 0, tail[7:8, :], pltpu.roll(z, 1, 0))
    zm2 = jnp.where(rows == 0, tail[6:7, :],
                    jnp.where(rows == 1, tail[7:8, :], pltpu.roll(z, 2, 0)))
    conv = cw_ref[0:1, :] * zm2 + cw_ref[1:2, :] * zm1 + cw_ref[2:3, :] * z
    y_b = g_b * conv
    ztail_ref[...] = z[tm - 8:tm, :]

    y = jnp.concatenate([y_a, y_b], axis=1).astype(BF16)
    m = jnp.dot(y, wout_ref[...], preferred_element_type=F32)
    x1 = _layer_norm(ALPHA * x + m, g1_ref[...], b1_ref[...])
    x1_ref[...] = x1
    _route_tile(x1, wr_ref, tri_ref, route_ref, route_t_ref, count_ref, base_ref)


def _mixer_layer(x, w_in, ln_g, ln_b, w_sp, b_sp, conv_w, w_out, g1, b1, w_router):
    t = x.shape[0]
    tm = MIX_ROWS
    const2 = lambda i: (0, 0)
    r_in, r_out, r_shape, r_scratch = _route_specs(t, tm)
    return pl.pallas_call(
        _mixer_kernel,
        grid=(t // tm,),
        in_specs=[
            pl.BlockSpec((tm, D_MODEL), lambda i: (i, 0)),
            pl.BlockSpec((D_MODEL, MIX_IN), const2),
            pl.BlockSpec((1, A_WIDTH), const2),
            pl.BlockSpec((1, A_WIDTH), const2),
            pl.BlockSpec((A_GROUPS, CHUNK, CHUNK), lambda i: (0, 0, 0)),
            pl.BlockSpec((CHUNK, A_GROUPS), const2),
            pl.BlockSpec((CONV_W, B_WIDTH), const2),
            pl.BlockSpec((A_WIDTH + B_WIDTH, D_MODEL), const2),
            pl.BlockSpec((1, D_MODEL), const2),
            pl.BlockSpec((1, D_MODEL), const2),
        ] + r_in,
        out_specs=[pl.BlockSpec((tm, D_MODEL), lambda i: (i, 0))] + r_out,
        out_shape=[jax.ShapeDtypeStruct((t, D_MODEL), F32)] + r_shape,
        scratch_shapes=[pltpu.VMEM((8, B_WIDTH), F32), r_scratch],
        compiler_params=pltpu.CompilerParams(
            dimension_semantics=("arbitrary",), vmem_limit_bytes=VMEM_LIMIT),
        name="mixer_layer",
    )(x, w_in.astype(BF16), ln_g.reshape(1, -1), ln_b.reshape(1, -1), w_sp, b_sp.T,
      conv_w, w_out.astype(BF16), g1.reshape(1, -1), b1.reshape(1, -1),
      *_route_operands(*w_router, tm))


def _attn_kernel(sink_ref, x_ref, wqkv_ref, bqkv_ref, bias_ref, wo_ref, bo_ref,
                 g1_ref, b1_ref, wr_ref, tri_ref, x1_ref, route_ref, route_t_ref, count_ref,
                 kprev_ref, vprev_ref, base_ref):
    tm = x_ref.shape[0]
    i = pl.program_id(0)
    first = i % (SEQ // tm) == 0

    @pl.when(first)
    def _():
        kprev_ref[...] = jnp.zeros_like(kprev_ref)
        vprev_ref[...] = jnp.zeros_like(vprev_ref)

    x = x_ref[...]
    qkv = jnp.dot(x.astype(BF16), wqkv_ref[...], preferred_element_type=F32) + bqkv_ref[...]
    q = (qkv[:, :ATT_OUT] * (HEAD_DIM ** -0.5)).astype(BF16)

    lane = lax.broadcasted_iota(I32, (CHUNK, KV_WIDTH), 1)
    low = lane < HEAD_DIM

    def halves(t):
        swapped = pltpu.roll(t, HEAD_DIM, 1)
        zero = jnp.zeros_like(t)
        return [jnp.where(low, t, zero).astype(BF16), jnp.where(low, zero, swapped).astype(BF16),
                jnp.where(low, swapped, zero).astype(BF16), jnp.where(low, zero, t).astype(BF16)]

    def stacked(prev, cur, j):
        return jnp.concatenate([prev[2 * j], cur[2 * j], prev[2 * j + 1], cur[2 * j + 1]], axis=0)

    a = lax.broadcasted_iota(I32, (CHUNK, 2 * CHUNK), 0)
    c = lax.broadcasted_iota(I32, (CHUNK, 2 * CHUNK), 1)
    window = (c > a) & (c <= a + WINDOW)
    neg = jnp.finfo(F32).min
    n_pairs = N_HEADS // 2
    kv_of = lambda pair: (2 * pair) // GQA_GROUP

    k_prev = [kprev_ref[idx] for idx in range(2 * N_KV)]
    v_prev = [vprev_ref[idx] for idx in range(2 * N_KV)]
    o_blocks = []
    for blk in range(tm // CHUNK):
        rows = slice(blk * CHUNK, (blk + 1) * CHUNK)
        k_cur = halves(qkv[rows, ATT_OUT:ATT_OUT + KV_WIDTH])
        v_cur = halves(qkv[rows, ATT_OUT + KV_WIDTH:])
        k_rhs = [stacked(k_prev, k_cur, j) for j in range(N_KV)]
        v_rhs = [stacked(v_prev, v_cur, j) for j in range(N_KV)]
        k_prev, v_prev = k_cur, v_cur
        mask = window & (c >= jnp.where(first, CHUNK, 0)) if blk == 0 else window

        scores = [lax.dot_general(q[rows, pair * 2 * HEAD_DIM:(pair + 1) * 2 * HEAD_DIM],
                                  k_rhs[kv_of(pair)], (((1,), (1,)), ((), ())),
                                  preferred_element_type=F32) for pair in range(n_pairs)]
        probs = []
        for pair in range(n_pairs):
            sc2 = scores[pair] + bias_ref[pair]
            sides = []
            for side in range(2):
                sc = jnp.where(mask, sc2[:, side * 2 * CHUNK:(side + 1) * 2 * CHUNK], neg)
                sink = sink_ref[2 * pair + side]
                m = jnp.maximum(jnp.max(sc, axis=-1, keepdims=True), sink)
                p = jnp.exp(sc - m)
                denom = jnp.sum(p, axis=-1, keepdims=True) + jnp.exp(sink - m)
                sides.append((p / denom).astype(BF16))
            probs.append(jnp.concatenate(sides, axis=1))
        outs = [jnp.dot(probs[pair], v_rhs[kv_of(pair)], preferred_element_type=F32)
                for pair in range(n_pairs)]
        o_blocks.append(jnp.concatenate(outs, axis=1).astype(BF16))
    for idx in range(2 * N_KV):
        kprev_ref[idx] = k_prev[idx]
        vprev_ref[idx] = v_prev[idx]
    o = jnp.concatenate(o_blocks, axis=0)
    m_out = jnp.dot(o, wo_ref[...], preferred_element_type=F32) + bo_ref[...]
    x1 = _layer_norm(ALPHA * x + m_out, g1_ref[...], b1_ref[...])
    x1_ref[...] = x1
    _route_tile(x1, wr_ref, tri_ref, route_ref, route_t_ref, count_ref, base_ref)


def _t5_bucket(rel):
    n = jnp.maximum(rel, 0)
    max_exact = N_BUCKETS // 2
    nf = jnp.maximum(n, 1).astype(F32)
    large = max_exact + (jnp.log(nf / max_exact) / math.log(MAX_DISTANCE / max_exact)
                         * (N_BUCKETS - max_exact)).astype(I32)
    large = jnp.minimum(large, N_BUCKETS - 1)
    return jnp.where(n < max_exact, n, large)


def _rel_bias(rel_table):
    a = jnp.arange(CHUNK)[:, None]
    c = jnp.arange(2 * CHUNK)[None, :]
    onehot = jax.nn.one_hot(_t5_bucket(a + CHUNK - c), N_BUCKETS, dtype=F32)
    bias = jnp.einsum('acb,bh->hac', onehot, rel_table.astype(F32),
                      precision=lax.Precision.HIGHEST)
    bias = bias.reshape(N_HEADS // 2, 2, CHUNK, 2 * CHUNK).transpose(0, 2, 1, 3)
    return bias.reshape(N_HEADS // 2, CHUNK, 4 * CHUNK)


def _attn_layer(x, bias, w_qkv, b_qkv, sinks, w_o, b_o, g1, b1, w_router):
    t = x.shape[0]
    tm = ATTN_ROWS
    const2 = lambda i: (0, 0)
    r_in, r_out, r_shape, r_scratch = _route_specs(t, tm)
    return pl.pallas_call(
        _attn_kernel,
        grid=(t // tm,),
        in_specs=[
            pl.BlockSpec(memory_space=pltpu.SMEM),
            pl.BlockSpec((tm, D_MODEL), lambda i: (i, 0)),
            pl.BlockSpec((D_MODEL, QKV_DIM), const2),
            pl.BlockSpec((1, QKV_DIM), const2),
            pl.BlockSpec((N_HEADS // 2, CHUNK, 4 * CHUNK), lambda i: (0, 0, 0)),
            pl.BlockSpec((ATT_OUT, D_MODEL), const2),
            pl.BlockSpec((1, D_MODEL), const2),
            pl.BlockSpec((1, D_MODEL), const2),
            pl.BlockSpec((1, D_MODEL), const2),
        ] + r_in,
        out_specs=[pl.BlockSpec((tm, D_MODEL), lambda i: (i, 0))] + r_out,
        out_shape=[jax.ShapeDtypeStruct((t, D_MODEL), F32)] + r_shape,
        scratch_shapes=[pltpu.VMEM((2 * N_KV, CHUNK, KV_WIDTH), BF16),
                        pltpu.VMEM((2 * N_KV, CHUNK, KV_WIDTH), BF16), r_scratch],
        compiler_params=pltpu.CompilerParams(
            dimension_semantics=("arbitrary",), vmem_limit_bytes=VMEM_LIMIT),
        name="attn_layer",
    )(sinks, x, w_qkv.astype(BF16), b_qkv.reshape(1, -1), bias, w_o.astype(BF16),
      b_o.reshape(1, -1), g1.reshape(1, -1), b1.reshape(1, -1), *_route_operands(*w_router, tm))


def _route_tile(x1, wr_ref, tri_ref, route_ref, route_t_ref, count_ref, base_ref):
    @pl.when(pl.program_id(0) == 0)
    def _():
        base_ref[...] = jnp.zeros_like(base_ref)

    tm = x1.shape[0]
    lt = lax.dot_general(wr_ref[...], x1.astype(BF16), (((1,), (1,)), ((), ())),
                         preferred_element_type=F32)
    sub = lax.broadcasted_iota(I32, (SUBLANES, tm), 0).astype(F32)
    ninf = -jnp.inf

    def first_argmax(vals):
        m = jnp.max(vals, axis=0, keepdims=True)
        idx = jnp.min(jnp.where(vals == m, sub, float(SUBLANES)), axis=0, keepdims=True)
        return m, idx

    is_g = sub < N_GROUPS
    g_rows = lt[0:SUBLANES]
    gmax, g_idx = first_argmax(jnp.where(is_g, g_rows, ninf))
    g_p = 1.0 / jnp.sum(jnp.where(is_g, jnp.exp(g_rows - gmax), 0.0), axis=0, keepdims=True)

    group_rows = lambda a, g: a[SUBLANES * (g + 1):SUBLANES * (g + 2)]
    el = group_rows(lt, 0)
    for g in range(1, N_GROUPS):
        el = jnp.where(g_idx == g, group_rows(lt, g), el)
    m1, j1 = first_argmax(el)
    m2, j2 = first_argmax(jnp.where(sub == j1, ninf, el))
    a2 = jnp.exp(m2 - m1)
    gate0 = g_p / (1.0 + a2)
    gate1 = g_p * a2 / (1.0 + a2)

    hit0 = [(g_idx == g) & (sub == j1) for g in range(N_GROUPS)]
    hit1 = [(g_idx == g) & (sub == j2) for g in range(N_GROUPS)]
    onehot = jnp.concatenate([jnp.where(h0 | h1, 1.0, 0.0) for h0, h1 in zip(hit0, hit1)],
                             axis=0)
    before = jnp.dot(onehot.astype(BF16), tri_ref[...], preferred_element_type=F32)
    before = before + base_ref[:, 0:1]

    def picked(hits):
        total = jnp.zeros((SUBLANES, tm), F32)
        for g in range(N_GROUPS):
            total = total + jnp.where(hits[g], before[SUBLANES * g:SUBLANES * (g + 1)], 0.0)
        return jnp.sum(total, axis=0, keepdims=True)

    rank0 = picked(hit0)
    rank1 = picked(hit1)
    base_ref[...] += jnp.sum(onehot, axis=1, keepdims=True)

    e_base = g_idx * EXPERTS_PER_GROUP
    rec_t = jnp.concatenate([e_base + j1, e_base + j2, rank0, rank1, gate0, gate1,
                             jnp.zeros((SUBLANES - 6, tm), F32)], axis=0)
    route_t_ref[...] = rec_t
    padded = jnp.concatenate([rec_t, jnp.zeros((LANES - SUBLANES, tm), F32)], axis=0)
    route_ref[...] = jnp.transpose(padded)
    count_ref[...] = base_ref[...]


def _route_specs(t, tm):
    const2 = lambda i: (0, 0)
    in_specs = [pl.BlockSpec((ROUTER_ROWS, D_MODEL), const2), pl.BlockSpec((tm, tm), const2)]
    out_specs = [
        pl.BlockSpec((tm, LANES), lambda i: (i, 0)),
        pl.BlockSpec((SUBLANES, tm), lambda i: (0, i)),
        pl.BlockSpec((N_EXPERTS, LANES), const2),
    ]
    out_shape = [
        jax.ShapeDtypeStruct((t, LANES), F32),
        jax.ShapeDtypeStruct((SUBLANES, t), F32),
        jax.ShapeDtypeStruct((N_EXPERTS, LANES), F32),
    ]
    return in_specs, out_specs, out_shape, pltpu.VMEM((N_EXPERTS, LANES), F32)


def _route_operands(w_group, w_expert, tm):
    pad = jnp.zeros((D_MODEL, SUBLANES - N_GROUPS), F32)
    w_rows = jnp.concatenate([w_group, pad, w_expert], axis=1).T
    return w_rows.astype(BF16), jnp.triu(jnp.ones((tm, tm), BF16), 1)


def _plan(route_t, count_lanes):
    t = route_t.shape[1]
    rb = EXPERT_ROWS
    experts = jnp.arange(N_EXPERTS, dtype=I32)
    counts = count_lanes[:, 0].astype(I32)
    end = jnp.cumsum(counts)
    start = end - counts
    e = route_t[R_E0:R_E1 + 1].astype(I32)
    rank = route_t[R_RANK0:R_RANK1 + 1].astype(I32)
    onehot = e[:, None, :] == experts[None, :, None]
    dest = jnp.sum(jnp.where(onehot, start[None, :, None], 0), axis=1) + rank

    first_blk = start // rb
    last_blk = jnp.maximum(end - 1, 0) // rb
    n_steps_e = jnp.where(counts > 0, last_blk - first_blk + 1, 0)
    step_end = jnp.cumsum(n_steps_e)
    step_start = step_end - n_steps_e
    n_steps = step_end[-1]
    max_steps = t * TOP_K // rb + N_EXPERTS - 1
    s = jnp.minimum(jnp.arange(max_steps, dtype=I32), n_steps - 1)
    step_e = jnp.minimum(jnp.sum((step_end[None, :] <= s[:, None]).astype(I32), axis=1),
                         N_EXPERTS - 1)
    pick = step_e[:, None] == experts[None, :]
    take = lambda v: jnp.sum(jnp.where(pick, v[None, :], 0), axis=1)
    step_blk = take(first_blk) + s - take(step_start)
    step_lo = jnp.clip(take(start) - step_blk * rb, 0, rb)
    step_hi = jnp.clip(take(end) - step_blk * rb, 0, rb)
    return dest, (step_blk, step_e, step_lo, step_hi, n_steps.reshape(1))


def _dest_blocks(dest, tm):
    t = dest.shape[1]
    return dest.reshape(TOP_K, t // tm, tm).transpose(1, 0, 2).reshape(t // tm, 1, TOP_K * tm)


def _to_row_tiles(x):
    return x.reshape(x.shape[0], ROW_TILE[0], ROW_TILE[1])


def _from_row_tiles(x):
    return x.reshape(x.shape[0], D_MODEL)


def _dispatch_kernel(dest_ref, x1_ref, xs_hbm, xbuf, row_sems):
    tm = x1_ref.shape[0]
    i = pl.program_id(0)
    n = pl.num_programs(0)
    slot = i % DISPATCH_SLOTS

    def drain_rows(s):
        for k in range(TOP_K):
            pltpu.make_async_copy(xbuf.at[s], xs_hbm.at[pl.ds(0, tm)], row_sems.at[s]).wait()

    @pl.when(i >= DISPATCH_SLOTS)
    def _():
        drain_rows(slot)

    xbuf[slot] = _to_row_tiles(x1_ref[...])

    def issue(g, carry):
        for j in range(ROWS_PER_ISSUE):
            r = g * ROWS_PER_ISSUE + j
            for k in range(TOP_K):
                pltpu.make_async_copy(xbuf.at[slot, r], xs_hbm.at[dest_ref[0, 0, k * tm + r]],
                                      row_sems.at[slot]).start(priority=k)
        return carry

    lax.fori_loop(0, tm // ROWS_PER_ISSUE, issue, 0)

    @pl.when(i == n - 1)
    def _():
        for back in range(DISPATCH_SLOTS):
            drain_rows((i - back) % DISPATCH_SLOTS)


def _dispatch(x1, dest):
    t = x1.shape[0]
    tm = MOVE_ROWS
    return pl.pallas_call(
        _dispatch_kernel,
        grid=(t // tm,),
        in_specs=[
            pl.BlockSpec((1, 1, TOP_K * tm), lambda i: (i, 0, 0), memory_space=pltpu.SMEM),
            pl.BlockSpec((tm, D_MODEL), lambda i: (i, 0)),
        ],
        out_specs=pl.BlockSpec(memory_space=pl.ANY),
        out_shape=jax.ShapeDtypeStruct((t * TOP_K,) + ROW_TILE, F32),
        scratch_shapes=[
            pltpu.VMEM((DISPATCH_SLOTS, tm) + ROW_TILE, F32),
            pltpu.SemaphoreType.DMA((DISPATCH_SLOTS,)),
        ],
        compiler_params=pltpu.CompilerParams(
            dimension_semantics=("arbitrary",), vmem_limit_bytes=VMEM_LIMIT),
        name="dispatch",
    )(_dest_blocks(dest, tm), x1)


def _expert_kernel(blk_ref, e_ref, lo_ref, hi_ref, n_steps_ref, xs_ref, wg_ref, wu_ref, wd_ref,
                   ys_ref, wg_bf, wu_bf, wd_bf):
    s = pl.program_id(0)
    prev = jnp.maximum(s - 1, 0)
    new_expert = (s == 0) | (e_ref[s] != e_ref[prev])
    new_block = (s == 0) | (blk_ref[s] != blk_ref[prev])

    @pl.when(new_expert)
    def _():
        wg_bf[...] = wg_ref[0, 0].astype(BF16)
        wu_bf[...] = wu_ref[0, 0].astype(BF16)
        wd_bf[...] = wd_ref[0, 0].astype(BF16)

    @pl.when(s < n_steps_ref[0])
    def _():
        xb = _from_row_tiles(xs_ref[...]).astype(BF16)
        gate = jnp.dot(xb, wg_bf[...], preferred_element_type=F32)
        up = jnp.dot(xb, wu_bf[...], preferred_element_type=F32)
        hid = (gate * jax.nn.sigmoid(gate) * up).astype(BF16)
        y = jnp.dot(hid, wd_bf[...], preferred_element_type=F32)
        rows = lax.broadcasted_iota(I32, y.shape, 0)
        y = _to_row_tiles(jnp.where((rows >= lo_ref[s]) & (rows < hi_ref[s]), y, 0.0))

        @pl.when(new_block)
        def _():
            ys_ref[...] = y

        @pl.when(jnp.logical_not(new_block))
        def _():
            ys_ref[...] += y


def _expert_ffn(xs, steps, layer, wg, wu, wd):
    step_blk, step_e, step_lo, step_hi, n_steps = steps
    rb = EXPERT_ROWS
    row_map = lambda s, blk, e, lo, hi, n: (blk[s], 0, 0)
    w_map = lambda s, blk, e, lo, hi, n: (layer, e[s], 0, 0)
    grid_spec = pltpu.PrefetchScalarGridSpec(
        num_scalar_prefetch=5,
        grid=(step_blk.shape[0],),
        in_specs=[
            pl.BlockSpec((rb,) + ROW_TILE, row_map),
            pl.BlockSpec((1, 1, D_MODEL, D_EXPERT), w_map),
            pl.BlockSpec((1, 1, D_MODEL, D_EXPERT), w_map),
            pl.BlockSpec((1, 1, D_EXPERT, D_MODEL), w_map),
        ],
        out_specs=pl.BlockSpec((rb,) + ROW_TILE, row_map),
        scratch_shapes=[
            pltpu.VMEM((D_MODEL, D_EXPERT), BF16),
            pltpu.VMEM((D_MODEL, D_EXPERT), BF16),
            pltpu.VMEM((D_EXPERT, D_MODEL), BF16),
        ],
    )
    return pl.pallas_call(
        _expert_kernel,
        grid_spec=grid_spec,
        out_shape=jax.ShapeDtypeStruct(xs.shape, F32),
        compiler_params=pltpu.CompilerParams(
            dimension_semantics=("arbitrary",), vmem_limit_bytes=VMEM_LIMIT),
        name="expert_ffn",
    )(step_blk, step_e, step_lo, step_hi, n_steps, xs, wg, wu, wd)


def _combine_kernel(dest_ref, dest_next_ref, x1_ref, route_ref, g2_ref, b2_ref, ys_hbm,
                    x2_ref, ybuf, sems):
    tm = x1_ref.shape[0]
    i = pl.program_id(0)
    n = pl.num_programs(0)
    slot = i % 2

    def gather(d_ref, to_slot):
        def issue(g, carry):
            for j in range(ROWS_PER_ISSUE):
                r = g * ROWS_PER_ISSUE + j
                for k in range(TOP_K):
                    pltpu.make_async_copy(ys_hbm.at[d_ref[0, 0, k * tm + r]],
                                          ybuf.at[to_slot, k, r],
                                          sems.at[to_slot]).start(priority=k)
            return carry
        lax.fori_loop(0, tm // ROWS_PER_ISSUE, issue, 0)

    @pl.when(i == 0)
    def _():
        gather(dest_ref, slot)

    @pl.when(i + 1 < n)
    def _():
        gather(dest_next_ref, 1 - slot)

    for k in range(TOP_K):
        pltpu.make_async_copy(ys_hbm.at[pl.ds(0, tm)], ybuf.at[slot, k], sems.at[slot]).wait()

    route = route_ref[...]
    f = (route[:, R_GATE0:R_GATE0 + 1] * _from_row_tiles(ybuf[slot, 0])
         + route[:, R_GATE1:R_GATE1 + 1] * _from_row_tiles(ybuf[slot, 1]))
    x2_ref[...] = _layer_norm(ALPHA * x1_ref[...] + f, g2_ref[...], b2_ref[...])


def _combine_ln(x1, ys, dest, route, g2, b2):
    t = x1.shape[0]
    tm = MOVE_ROWS
    n = t // tm
    row = lambda i: (i, 0)
    const2 = lambda i: (0, 0)
    dest_blocks = _dest_blocks(dest, tm)
    smem_block = lambda index_map: pl.BlockSpec((1, 1, TOP_K * tm), index_map,
                                                memory_space=pltpu.SMEM)
    return pl.pallas_call(
        _combine_kernel,
        grid=(n,),
        in_specs=[
            smem_block(lambda i: (i, 0, 0)),
            smem_block(lambda i: (jnp.minimum(i + 1, n - 1), 0, 0)),
            pl.BlockSpec((tm, D_MODEL), row),
            pl.BlockSpec((tm, LANES), row),
            pl.BlockSpec((1, D_MODEL), const2),
            pl.BlockSpec((1, D_MODEL), const2),
            pl.BlockSpec(memory_space=pl.ANY),
        ],
        out_specs=pl.BlockSpec((tm, D_MODEL), row),
        out_shape=jax.ShapeDtypeStruct((t, D_MODEL), F32),
        scratch_shapes=[
            pltpu.VMEM((2, TOP_K, tm) + ROW_TILE, F32),
            pltpu.SemaphoreType.DMA((2,)),
        ],
        compiler_params=pltpu.CompilerParams(
            dimension_semantics=("arbitrary",), vmem_limit_bytes=VMEM_LIMIT),
        name="combine_ln",
    )(dest_blocks, dest_blocks, x1, route, g2.reshape(1, -1), b2.reshape(1, -1), ys)


def _moe_layer(x1, route, route_t, counts, layer, wg, wu, wd, g2, b2):
    dest, steps = _plan(route_t, counts)
    xs = _dispatch(x1, dest)
    ys = _expert_ffn(xs, steps, layer, wg, wu, wd)
    return _combine_ln(x1, ys, dest, route, g2, b2)


def kernel(x, rel_bias_table, mix_w_in, gmlp_ln_g, gmlp_ln_b, gmlp_w_spatial, gmlp_b_spatial, conv_w, mix_w_out, attn_w_qkv, attn_b_qkv, attn_sinks, attn_w_o, attn_b_o, ln1_g, ln1_b, ln2_g, ln2_b, router_group, router_expert, expert_w_gate, expert_w_up, expert_w_down):
    bsz, s, d = x.shape
    assert (s, d) == (SEQ, D_MODEL)
    x = x.reshape(bsz * s, d)
    bias = _rel_bias(rel_bias_table)
    for l in range(DEPTH):
        i = l // 2
        w_router = (router_group[l], router_expert[l])
        if l % 2 == 0:
            x1, *routing = _mixer_layer(x, mix_w_in[i], gmlp_ln_g[i], gmlp_ln_b[i],
                                        gmlp_w_spatial[i], gmlp_b_spatial[i], conv_w[i],
                                        mix_w_out[i], ln1_g[l], ln1_b[l], w_router)
        else:
            x1, *routing = _attn_layer(x, bias, attn_w_qkv[i], attn_b_qkv[i], attn_sinks[i],
                                       attn_w_o[i], attn_b_o[i], ln1_g[l], ln1_b[l], w_router)
        x = _moe_layer(x1, *routing, l, expert_w_gate, expert_w_up, expert_w_down,
                       ln2_g[l], ln2_b[l])
    return x.reshape(bsz, s, d)
```

```python
import math

import jax
import jax.numpy as jnp
from jax import lax
from jax.experimental import pallas as pl
from jax.experimental.pallas import tpu as pltpu

D_MODEL = 1024
SEQ = 16384
DEPTH = 4
CHUNK = 128
A_GROUPS = 4
A_CH = 128
A_WIDTH = A_GROUPS * A_CH
B_WIDTH = 512
CONV_W = 3
MIX_IN = 2 * A_WIDTH + 3 * B_WIDTH
N_HEADS = 16
N_KV = 2
HEAD_DIM = 64
GQA_GROUP = N_HEADS // N_KV
WINDOW = 128
QKV_DIM = (N_HEADS + 2 * N_KV) * HEAD_DIM
ATT_OUT = N_HEADS * HEAD_DIM
KV_WIDTH = N_KV * HEAD_DIM
N_BUCKETS = 32
MAX_DISTANCE = 128
N_GROUPS = 4
EXPERTS_PER_GROUP = 8
N_EXPERTS = N_GROUPS * EXPERTS_PER_GROUP
TOP_K = 2
D_EXPERT = 512
ALPHA = (2 * DEPTH) ** 0.25
LN_EPS = 1e-5

LANES = 128
SUBLANES = 8
ROW_TILE = (SUBLANES, LANES)
assert D_MODEL == SUBLANES * LANES
ROUTER_ROWS = SUBLANES + N_EXPERTS
MIX_ROWS = 1024
ATTN_ROWS = 1024
MOVE_ROWS = 512
EXPERT_ROWS = 512
ROWS_PER_ISSUE = 16
DISPATCH_SLOTS = 3
VMEM_LIMIT = 56 * 1024 * 1024

R_E0, R_E1, R_RANK0, R_RANK1, R_GATE0, R_GATE1 = range(6)

F32 = jnp.float32
BF16 = jnp.bfloat16
I32 = jnp.int32


def _layer_norm(x, g, b):
    mu = jnp.mean(x, axis=-1, keepdims=True)
    xc = x - mu
    var = jnp.mean(xc * xc, axis=-1, keepdims=True)
    return xc * lax.rsqrt(var + LN_EPS) * g + b


def _gelu(x):
    return 0.5 * x * (1.0 + lax.erf(x * (2.0 ** -0.5)))


def _mixer_kernel(x_ref, win_ref, lng_ref, lnb_ref, wsp_ref, bsp_ref, cw_ref, wout_ref,
                  g1_ref, b1_ref, wr_ref, tri_ref, x1_ref, route_ref, route_t_ref, count_ref,
                  ztail_ref, base_ref):
    tm = x_ref.shape[0]
    i = pl.program_id(0)

    @pl.when(i % (SEQ // tm) == 0)
    def _():
        ztail_ref[...] = jnp.zeros_like(ztail_ref)

    x = x_ref[...]
    h = jnp.dot(x.astype(BF16), win_ref[...], preferred_element_type=F32)
    u = _gelu(h[:, :A_WIDTH])
    v = _gelu(h[:, A_WIDTH:2 * A_WIDTH])
    o = 2 * A_WIDTH
    g_b = h[:, o:o + B_WIDTH]
    g_c = h[:, o + B_WIDTH:o + 2 * B_WIDTH]
    hb = h[:, o + 2 * B_WIDTH:o + 3 * B_WIDTH]

    v = _layer_norm(v, lng_ref[...], lnb_ref[...]).astype(BF16)
    n_chunks = tm // CHUNK
    ri = lax.broadcasted_iota(I32, (CHUNK, CHUNK), 0)
    ci = lax.broadcasted_iota(I32, (CHUNK, CHUNK), 1)
    causal = ci <= ri
    sv_cols = [[None] * A_GROUPS for _ in range(n_chunks)]
    for g in range(A_GROUPS):
        ws = jnp.where(causal, wsp_ref[g], 0.0).astype(BF16)
        vg = jnp.concatenate(
            [v[c * CHUNK:(c + 1) * CHUNK, g * A_CH:(g + 1) * A_CH] for c in range(n_chunks)],
            axis=1)
        sg = jnp.dot(ws, vg, preferred_element_type=F32) + bsp_ref[:, g:g + 1]
        for c in range(n_chunks):
            sv_cols[c][g] = sg[:, c * A_CH:(c + 1) * A_CH]
    sv = jnp.concatenate([jnp.concatenate(row, axis=1) for row in sv_cols], axis=0)
    y_a = u * sv

    z = g_c * hb
    rows = lax.broadcasted_iota(I32, z.shape, 0)
    tail = ztail_ref[...]
    zm1 = jnp.where(rows == 0, tail[7:8, :], pltpu.roll(z, 1, 0))
    zm2 = jnp.where(rows == 0, tail[6:7, :],
                    jnp.where(rows == 1, tail[7:8, :], pltpu.roll(z, 2, 0)))
    conv = cw_ref[0:1, :] * zm2 + cw_ref[1:2, :] * zm1 + cw_ref[2:3, :] * z
    y_b = g_b * conv
    ztail_ref[...] = z[tm - 8:tm, :]

    y = jnp.concatenate([y_a, y_b], axis=1).astype(BF16)
    m = jnp.dot(y, wout_ref[...], preferred_element_type=F32)
    x1 = _layer_norm(ALPHA * x + m, g1_ref[...], b1_ref[...])
    x1_ref[...] = x1
    _route_tile(x1, wr_ref, tri_ref, route_ref, route_t_ref, count_ref, base_ref)


def _mixer_layer(x, w_in, ln_g, ln_b, w_sp, b_sp, conv_w, w_out, g1, b1, w_router):
    t = x.shape[0]
    tm = MIX_ROWS
    const2 = lambda i: (0, 0)
    r_in, r_out, r_shape, r_scratch = _route_specs(t, tm)
    return pl.pallas_call(
        _mixer_kernel,
        grid=(t // tm,),
        in_specs=[
            pl.BlockSpec((tm, D_MODEL), lambda i: (i, 0)),
            pl.BlockSpec((D_MODEL, MIX_IN), const2),
            pl.BlockSpec((1, A_WIDTH), const2),
            pl.BlockSpec((1, A_WIDTH), const2),
            pl.BlockSpec((A_GROUPS, CHUNK, CHUNK), lambda i: (0, 0, 0)),
            pl.BlockSpec((CHUNK, A_GROUPS), const2),
            pl.BlockSpec((CONV_W, B_WIDTH), const2),
            pl.BlockSpec((A_WIDTH + B_WIDTH, D_MODEL), const2),
            pl.BlockSpec((1, D_MODEL), const2),
            pl.BlockSpec((1, D_MODEL), const2),
        ] + r_in,
        out_specs=[pl.BlockSpec((tm, D_MODEL), lambda i: (i, 0))] + r_out,
        out_shape=[jax.ShapeDtypeStruct((t, D_MODEL), F32)] + r_shape,
        scratch_shapes=[pltpu.VMEM((8, B_WIDTH), F32), r_scratch],
        compiler_params=pltpu.CompilerParams(
            dimension_semantics=("arbitrary",), vmem_limit_bytes=VMEM_LIMIT),
        name="mixer_layer",
    )(x, w_in.astype(BF16), ln_g.reshape(1, -1), ln_b.reshape(1, -1), w_sp, b_sp.T,
      conv_w, w_out.astype(BF16), g1.reshape(1, -1), b1.reshape(1, -1),
      *_route_operands(*w_router, tm))


def _attn_kernel(sink_ref, x_ref, wqkv_ref, bqkv_ref, bias_ref, wo_ref, bo_ref,
                 g1_ref, b1_ref, wr_ref, tri_ref, x1_ref, route_ref, route_t_ref, count_ref,
                 kprev_ref, vprev_ref, base_ref):
    tm = x_ref.shape[0]
    i = pl.program_id(0)
    first = i % (SEQ // tm) == 0

    @pl.when(first)
    def _():
        kprev_ref[...] = jnp.zeros_like(kprev_ref)
        vprev_ref[...] = jnp.zeros_like(vprev_ref)

    x = x_ref[...]
    qkv = jnp.dot(x.astype(BF16), wqkv_ref[...], preferred_element_type=F32) + bqkv_ref[...]
    q = (qkv[:, :ATT_OUT] * (HEAD_DIM ** -0.5)).astype(BF16)

    lane = lax.broadcasted_iota(I32, (CHUNK, KV_WIDTH), 1)
    low = lane < HEAD_DIM

    def halves(t):
        swapped = pltpu.roll(t, HEAD_DIM, 1)
        zero = jnp.zeros_like(t)
        return [jnp.where(low, t, zero).astype(BF16), jnp.where(low, zero, swapped).astype(BF16),
                jnp.where(low, swapped, zero).astype(BF16), jnp.where(low, zero, t).astype(BF16)]

    def stacked(prev, cur, j):
        return jnp.concatenate([prev[2 * j], cur[2 * j], prev[2 * j + 1], cur[2 * j + 1]], axis=0)

    a = lax.broadcasted_iota(I32, (CHUNK, 2 * CHUNK), 0)
    c = lax.broadcasted_iota(I32, (CHUNK, 2 * CHUNK), 1)
    window = (c > a) & (c <= a + WINDOW)
    neg = jnp.finfo(F32).min
    n_pairs = N_HEADS // 2
    kv_of = lambda pair: (2 * pair) // GQA_GROUP

    k_prev = [kprev_ref[idx] for idx in range(2 * N_KV)]
    v_prev = [vprev_ref[idx] for idx in range(2 * N_KV)]
    o_blocks = []
    for blk in range(tm // CHUNK):
        rows = slice(blk * CHUNK, (blk + 1) * CHUNK)
        k_cur = halves(qkv[rows, ATT_OUT:ATT_OUT + KV_WIDTH])
        v_cur = halves(qkv[rows, ATT_OUT + KV_WIDTH:])
        k_rhs = [stacked(k_prev, k_cur, j) for j in range(N_KV)]
        v_rhs = [stacked(v_prev, v_cur, j) for j in range(N_KV)]
        k_prev, v_prev = k_cur, v_cur
        mask = window & (c >= jnp.where(first, CHUNK, 0)) if blk == 0 else window

        scores = [lax.dot_general(q[rows, pair * 2 * HEAD_DIM:(pair + 1) * 2 * HEAD_DIM],
                                  k_rhs[kv_of(pair)], (((1,), (1,)), ((), ())),
                                  preferred_element_type=F32) for pair in range(n_pairs)]
        probs, inv_denoms = [], []
        for pair in range(n_pairs):
            sc2 = scores[pair] + bias_ref[pair]
            sides = []
            for side in range(2):
                sc = jnp.where(mask, sc2[:, side * 2 * CHUNK:(side + 1) * 2 * CHUNK], neg)
                sink = sink_ref[2 * pair + side]
                m = jnp.maximum(jnp.max(sc, axis=-1, keepdims=True), sink)
                p = jnp.exp(sc - m)
                denom = jnp.sum(p, axis=-1, keepdims=True) + jnp.exp(sink - m)
                sides.append(p.astype(BF16))
                inv_denoms.append(jnp.broadcast_to(1.0 / denom, (CHUNK, HEAD_DIM)))
            probs.append(jnp.concatenate(sides, axis=1))
        outs = [jnp.dot(probs[pair], v_rhs[kv_of(pair)], preferred_element_type=F32)
                for pair in range(n_pairs)]
        o = jnp.concatenate(outs, axis=1) * jnp.concatenate(inv_denoms, axis=1)
        o_blocks.append(o.astype(BF16))
    for idx in range(2 * N_KV):
        kprev_ref[idx] = k_prev[idx]
        vprev_ref[idx] = v_prev[idx]
    o = jnp.concatenate(o_blocks, axis=0)
    m_out = jnp.dot(o, wo_ref[...], preferred_element_type=F32) + bo_ref[...]
    x1 = _layer_norm(ALPHA * x + m_out, g1_ref[...], b1_ref[...])
    x1_ref[...] = x1
    _route_tile(x1, wr_ref, tri_ref, route_ref, route_t_ref, count_ref, base_ref)


def _t5_bucket(rel):
    n = jnp.maximum(rel, 0)
    max_exact = N_BUCKETS // 2
    nf = jnp.maximum(n, 1).astype(F32)
    large = max_exact + (jnp.log(nf / max_exact) / math.log(MAX_DISTANCE / max_exact)
                         * (N_BUCKETS - max_exact)).astype(I32)
    large = jnp.minimum(large, N_BUCKETS - 1)
    return jnp.where(n < max_exact, n, large)


def _rel_bias(rel_table):
    a = jnp.arange(CHUNK)[:, None]
    c = jnp.arange(2 * CHUNK)[None, :]
    onehot = jax.nn.one_hot(_t5_bucket(a + CHUNK - c), N_BUCKETS, dtype=F32)
    bias = jnp.einsum('acb,bh->hac', onehot, rel_table.astype(F32),
                      precision=lax.Precision.HIGHEST)
    bias = bias.reshape(N_HEADS // 2, 2, CHUNK, 2 * CHUNK).transpose(0, 2, 1, 3)
    return bias.reshape(N_HEADS // 2, CHUNK, 4 * CHUNK)


def _attn_layer(x, bias, w_qkv, b_qkv, sinks, w_o, b_o, g1, b1, w_router):
    t = x.shape[0]
    tm = ATTN_ROWS
    const2 = lambda i: (0, 0)
    r_in, r_out, r_shape, r_scratch = _route_specs(t, tm)
    return pl.pallas_call(
        _attn_kernel,
        grid=(t // tm,),
        in_specs=[
            pl.BlockSpec(memory_space=pltpu.SMEM),
            pl.BlockSpec((tm, D_MODEL), lambda i: (i, 0)),
            pl.BlockSpec((D_MODEL, QKV_DIM), const2),
            pl.BlockSpec((1, QKV_DIM), const2),
            pl.BlockSpec((N_HEADS // 2, CHUNK, 4 * CHUNK), lambda i: (0, 0, 0)),
            pl.BlockSpec((ATT_OUT, D_MODEL), const2),
            pl.BlockSpec((1, D_MODEL), const2),
            pl.BlockSpec((1, D_MODEL), const2),
            pl.BlockSpec((1, D_MODEL), const2),
        ] + r_in,
        out_specs=[pl.BlockSpec((tm, D_MODEL), lambda i: (i, 0))] + r_out,
        out_shape=[jax.ShapeDtypeStruct((t, D_MODEL), F32)] + r_shape,
        scratch_shapes=[pltpu.VMEM((2 * N_KV, CHUNK, KV_WIDTH), BF16),
                        pltpu.VMEM((2 * N_KV, CHUNK, KV_WIDTH), BF16), r_scratch],
        compiler_params=pltpu.CompilerParams(
            dimension_semantics=("arbitrary",), vmem_limit_bytes=VMEM_LIMIT),
        name="attn_layer",
    )(sinks, x, w_qkv.astype(BF16), b_qkv.reshape(1, -1), bias, w_o.astype(BF16),
      b_o.reshape(1, -1), g1.reshape(1, -1), b1.reshape(1, -1), *_route_operands(*w_router, tm))


def _route_tile(x1, wr_ref, tri_ref, route_ref, route_t_ref, count_ref, base_ref):
    @pl.when(pl.program_id(0) == 0)
    def _():
        base_ref[...] = jnp.zeros_like(base_ref)

    tm = x1.shape[0]
    lt = lax.dot_general(wr_ref[...], x1.astype(BF16), (((1,), (1,)), ((), ())),
                         preferred_element_type=F32)
    sub = lax.broadcasted_iota(I32, (SUBLANES, tm), 0).astype(F32)
    ninf = -jnp.inf

    def first_argmax(vals):
        m = jnp.max(vals, axis=0, keepdims=True)
        idx = jnp.min(jnp.where(vals == m, sub, float(SUBLANES)), axis=0, keepdims=True)
        return m, idx

    is_g = sub < N_GROUPS
    g_rows = lt[0:SUBLANES]
    gmax, g_idx = first_argmax(jnp.where(is_g, g_rows, ninf))
    g_p = 1.0 / jnp.sum(jnp.where(is_g, jnp.exp(g_rows - gmax), 0.0), axis=0, keepdims=True)

    group_rows = lambda a, g: a[SUBLANES * (g + 1):SUBLANES * (g + 2)]
    el = group_rows(lt, 0)
    for g in range(1, N_GROUPS):
        el = jnp.where(g_idx == g, group_rows(lt, g), el)
    m1, j1 = first_argmax(el)
    m2, j2 = first_argmax(jnp.where(sub == j1, ninf, el))
    a2 = jnp.exp(m2 - m1)
    gate0 = g_p / (1.0 + a2)
    gate1 = g_p * a2 / (1.0 + a2)

    hit0 = [(g_idx == g) & (sub == j1) for g in range(N_GROUPS)]
    hit1 = [(g_idx == g) & (sub == j2) for g in range(N_GROUPS)]
    onehot = jnp.concatenate([jnp.where(h0 | h1, 1.0, 0.0) for h0, h1 in zip(hit0, hit1)],
                             axis=0)
    before = jnp.dot(onehot.astype(BF16), tri_ref[...], preferred_element_type=F32)
    before = before + base_ref[:, 0:1]

    def picked(hits):
        total = jnp.zeros((SUBLANES, tm), F32)
        for g in range(N_GROUPS):
            total = total + jnp.where(hits[g], before[SUBLANES * g:SUBLANES * (g + 1)], 0.0)
        return jnp.sum(total, axis=0, keepdims=True)

    rank0 = picked(hit0)
    rank1 = picked(hit1)
    base_ref[...] += jnp.sum(onehot, axis=1, keepdims=True)

    e_base = g_idx * EXPERTS_PER_GROUP
    rec_t = jnp.concatenate([e_base + j1, e_base + j2, rank0, rank1, gate0, gate1,
                             jnp.zeros((SUBLANES - 6, tm), F32)], axis=0)
    route_t_ref[...] = rec_t
    padded = jnp.concatenate([rec_t, jnp.zeros((LANES - SUBLANES, tm), F32)], axis=0)
    route_ref[...] = jnp.transpose(padded)
    count_ref[...] = base_ref[...]


def _route_specs(t, tm):
    const2 = lambda i: (0, 0)
    in_specs = [pl.BlockSpec((ROUTER_ROWS, D_MODEL), const2), pl.BlockSpec((tm, tm), const2)]
    out_specs = [
        pl.BlockSpec((tm, LANES), lambda i: (i, 0)),
        pl.BlockSpec((SUBLANES, tm), lambda i: (0, i)),
        pl.BlockSpec((N_EXPERTS, LANES), const2),
    ]
    out_shape = [
        jax.ShapeDtypeStruct((t, LANES), F32),
        jax.ShapeDtypeStruct((SUBLANES, t), F32),
        jax.ShapeDtypeStruct((N_EXPERTS, LANES), F32),
    ]
    return in_specs, out_specs, out_shape, pltpu.VMEM((N_EXPERTS, LANES), F32)


def _route_operands(w_group, w_expert, tm):
    pad = jnp.zeros((D_MODEL, SUBLANES - N_GROUPS), F32)
    w_rows = jnp.concatenate([w_group, pad, w_expert], axis=1).T
    return w_rows.astype(BF16), jnp.triu(jnp.ones((tm, tm), BF16), 1)


def _plan(route_t, count_lanes):
    t = route_t.shape[1]
    rb = EXPERT_ROWS
    experts = jnp.arange(N_EXPERTS, dtype=I32)
    counts = count_lanes[:, 0].astype(I32)
    end = jnp.cumsum(counts)
    start = end - counts
    e = route_t[R_E0:R_E1 + 1].astype(I32)
    rank = route_t[R_RANK0:R_RANK1 + 1].astype(I32)
    onehot = e[:, None, :] == experts[None, :, None]
    dest = jnp.sum(jnp.where(onehot, start[None, :, None], 0), axis=1) + rank

    first_blk = start // rb
    last_blk = jnp.maximum(end - 1, 0) // rb
    n_steps_e = jnp.where(counts > 0, last_blk - first_blk + 1, 0)
    step_end = jnp.cumsum(n_steps_e)
    step_start = step_end - n_steps_e
    n_steps = step_end[-1]
    max_steps = t * TOP_K // rb + N_EXPERTS - 1
    s = jnp.minimum(jnp.arange(max_steps, dtype=I32), n_steps - 1)
    step_e = jnp.minimum(jnp.sum((step_end[None, :] <= s[:, None]).astype(I32), axis=1),
                         N_EXPERTS - 1)
    pick = step_e[:, None] == experts[None, :]
    take = lambda v: jnp.sum(jnp.where(pick, v[None, :], 0), axis=1)
    step_blk = take(first_blk) + s - take(step_start)
    step_lo = jnp.clip(take(start) - step_blk * rb, 0, rb)
    step_hi = jnp.clip(take(end) - step_blk * rb, 0, rb)
    return dest, (step_blk, step_e, step_lo, step_hi, n_steps.reshape(1))


def _dest_blocks(dest, tm):
    t = dest.shape[1]
    return dest.reshape(TOP_K, t // tm, tm).transpose(1, 0, 2).reshape(t // tm, 1, TOP_K * tm)


def _to_row_tiles(x):
    return x.reshape(x.shape[0], ROW_TILE[0], ROW_TILE[1])


def _from_row_tiles(x):
    return x.reshape(x.shape[0], D_MODEL)


def _dispatch_kernel(dest_ref, x1_ref, xs_hbm, xbuf, row_sems):
    tm = x1_ref.shape[0]
    i = pl.program_id(0)
    n = pl.num_programs(0)
    slot = i % DISPATCH_SLOTS

    def drain_rows(s):
        for k in range(TOP_K):
            pltpu.make_async_copy(xbuf.at[s], xs_hbm.at[pl.ds(0, tm)], row_sems.at[s]).wait()

    @pl.when(i >= DISPATCH_SLOTS)
    def _():
        drain_rows(slot)

    xbuf[slot] = _to_row_tiles(x1_ref[...])

    def issue(g, carry):
        for j in range(ROWS_PER_ISSUE):
            r = g * ROWS_PER_ISSUE + j
            for k in range(TOP_K):
                pltpu.make_async_copy(xbuf.at[slot, r], xs_hbm.at[dest_ref[0, 0, k * tm + r]],
                                      row_sems.at[slot]).start(priority=k)
        return carry

    lax.fori_loop(0, tm // ROWS_PER_ISSUE, issue, 0)

    @pl.when(i == n - 1)
    def _():
        for back in range(DISPATCH_SLOTS):
            drain_rows((i - back) % DISPATCH_SLOTS)


def _dispatch(x1, dest):
    t = x1.shape[0]
    tm = MOVE_ROWS
    return pl.pallas_call(
        _dispatch_kernel,
        grid=(t // tm,),
        in_specs=[
            pl.BlockSpec((1, 1, TOP_K * tm), lambda i: (i, 0, 0), memory_space=pltpu.SMEM),
            pl.BlockSpec((tm, D_MODEL), lambda i: (i, 0)),
        ],
        out_specs=pl.BlockSpec(memory_space=pl.ANY),
        out_shape=jax.ShapeDtypeStruct((t * TOP_K,) + ROW_TILE, F32),
        scratch_shapes=[
            pltpu.VMEM((DISPATCH_SLOTS, tm) + ROW_TILE, F32),
            pltpu.SemaphoreType.DMA((DISPATCH_SLOTS,)),
        ],
        compiler_params=pltpu.CompilerParams(
            dimension_semantics=("arbitrary",), vmem_limit_bytes=VMEM_LIMIT),
        name="dispatch",
    )(_dest_blocks(dest, tm), x1)


def _expert_kernel(blk_ref, e_ref, lo_ref, hi_ref, n_steps_ref, xs_ref, wg_ref, wu_ref, wd_ref,
                   ys_ref, wg_bf, wu_bf, wd_bf):
    s = pl.program_id(0)
    prev = jnp.maximum(s - 1, 0)
    new_expert = (s == 0) | (e_ref[s] != e_ref[prev])
    new_block = (s == 0) | (blk_ref[s] != blk_ref[prev])

    @pl.when(new_expert)
    def _():
        wg_bf[...] = wg_ref[0, 0].astype(BF16)
        wu_bf[...] = wu_ref[0, 0].astype(BF16)
        wd_bf[...] = wd_ref[0, 0].astype(BF16)

    @pl.when(s < n_steps_ref[0])
    def _():
        xb = _from_row_tiles(xs_ref[...]).astype(BF16)
        gate = jnp.dot(xb, wg_bf[...], preferred_element_type=F32)
        up = jnp.dot(xb, wu_bf[...], preferred_element_type=F32)
        hid = (gate * jax.nn.sigmoid(gate) * up).astype(BF16)
        y = jnp.dot(hid, wd_bf[...], preferred_element_type=F32)
        rows = lax.broadcasted_iota(I32, y.shape, 0)
        y = _to_row_tiles(jnp.where((rows >= lo_ref[s]) & (rows < hi_ref[s]), y, 0.0))

        @pl.when(new_block)
        def _():
            ys_ref[...] = y

        @pl.when(jnp.logical_not(new_block))
        def _():
            ys_ref[...] += y


def _expert_ffn(xs, steps, layer, wg, wu, wd):
    step_blk, step_e, step_lo, step_hi, n_steps = steps
    rb = EXPERT_ROWS
    row_map = lambda s, blk, e, lo, hi, n: (blk[s], 0, 0)
    w_map = lambda s, blk, e, lo, hi, n: (layer, e[s], 0, 0)
    grid_spec = pltpu.PrefetchScalarGridSpec(
        num_scalar_prefetch=5,
        grid=(step_blk.shape[0],),
        in_specs=[
            pl.BlockSpec((rb,) + ROW_TILE, row_map),
            pl.BlockSpec((1, 1, D_MODEL, D_EXPERT), w_map),
            pl.BlockSpec((1, 1, D_MODEL, D_EXPERT), w_map),
            pl.BlockSpec((1, 1, D_EXPERT, D_MODEL), w_map),
        ],
        out_specs=pl.BlockSpec((rb,) + ROW_TILE, row_map),
        scratch_shapes=[
            pltpu.VMEM((D_MODEL, D_EXPERT), BF16),
            pltpu.VMEM((D_MODEL, D_EXPERT), BF16),
            pltpu.VMEM((D_EXPERT, D_MODEL), BF16),
        ],
    )
    return pl.pallas_call(
        _expert_kernel,
        grid_spec=grid_spec,
        out_shape=jax.ShapeDtypeStruct(xs.shape, F32),
        compiler_params=pltpu.CompilerParams(
            dimension_semantics=("arbitrary",), vmem_limit_bytes=VMEM_LIMIT),
        name="expert_ffn",
    )(step_blk, step_e, step_lo, step_hi, n_steps, xs, wg, wu, wd)


def _combine_kernel(dest_ref, dest_next_ref, x1_ref, route_ref, g2_ref, b2_ref, ys_hbm,
                    x2_ref, ybuf, sems):
    tm = x1_ref.shape[0]
    i = pl.program_id(0)
    n = pl.num_programs(0)
    slot = i % 2

    def gather(d_ref, to_slot):
        def issue(g, carry):
            for j in range(ROWS_PER_ISSUE):
                r = g * ROWS_PER_ISSUE + j
                for k in range(TOP_K):
                    pltpu.make_async_copy(ys_hbm.at[d_ref[0, 0, k * tm + r]],
                                          ybuf.at[to_slot, k, r],
                                          sems.at[to_slot]).start(priority=k)
            return carry
        lax.fori_loop(0, tm // ROWS_PER_ISSUE, issue, 0)

    @pl.when(i == 0)
    def _():
        gather(dest_ref, slot)

    @pl.when(i + 1 < n)
    def _():
        gather(dest_next_ref, 1 - slot)

    for k in range(TOP_K):
        pltpu.make_async_copy(ys_hbm.at[pl.ds(0, tm)], ybuf.at[slot, k], sems.at[slot]).wait()

    route = route_ref[...]
    f = (route[:, R_GATE0:R_GATE0 + 1] * _from_row_tiles(ybuf[slot, 0])
         + route[:, R_GATE1:R_GATE1 + 1] * _from_row_tiles(ybuf[slot, 1]))
    x2_ref[...] = _layer_norm(ALPHA * x1_ref[...] + f, g2_ref[...], b2_ref[...])


def _combine_ln(x1, ys, dest, route, g2, b2):
    t = x1.shape[0]
    tm = MOVE_ROWS
    n = t // tm
    row = lambda i: (i, 0)
    const2 = lambda i: (0, 0)
    dest_blocks = _dest_blocks(dest, tm)
    smem_block = lambda index_map: pl.BlockSpec((1, 1, TOP_K * tm), index_map,
                                                memory_space=pltpu.SMEM)
    return pl.pallas_call(
        _combine_kernel,
        grid=(n,),
        in_specs=[
            smem_block(lambda i: (i, 0, 0)),
            smem_block(lambda i: (jnp.minimum(i + 1, n - 1), 0, 0)),
            pl.BlockSpec((tm, D_MODEL), row),
            pl.BlockSpec((tm, LANES), row),
            pl.BlockSpec((1, D_MODEL), const2),
            pl.BlockSpec((1, D_MODEL), const2),
            pl.BlockSpec(memory_space=pl.ANY),
        ],
        out_specs=pl.BlockSpec((tm, D_MODEL), row),
        out_shape=jax.ShapeDtypeStruct((t, D_MODEL), F32),
        scratch_shapes=[
            pltpu.VMEM((2, TOP_K, tm) + ROW_TILE, F32),
            pltpu.SemaphoreType.DMA((2,)),
        ],
        compiler_params=pltpu.CompilerParams(
            dimension_semantics=("arbitrary",), vmem_limit_bytes=VMEM_LIMIT),
        name="combine_ln",
    )(dest_blocks, dest_blocks, x1, route, g2.reshape(1, -1), b2.reshape(1, -1), ys)


def _moe_layer(x1, route, route_t, counts, layer, wg, wu, wd, g2, b2):
    dest, steps = _plan(route_t, counts)
    xs = _dispatch(x1, dest)
    ys = _expert_ffn(xs, steps, layer, wg, wu, wd)
    return _combine_ln(x1, ys, dest, route, g2, b2)


def kernel(x, rel_bias_table, mix_w_in, gmlp_ln_g, gmlp_ln_b, gmlp_w_spatial, gmlp_b_spatial, conv_w, mix_w_out, attn_w_qkv, attn_b_qkv, attn_sinks, attn_w_o, attn_b_o, ln1_g, ln1_b, ln2_g, ln2_b, router_group, router_expert, expert_w_gate, expert_w_up, expert_w_down):
    bsz, s, d = x.shape
    assert (s, d) == (SEQ, D_MODEL)
    x = x.reshape(bsz * s, d)
    bias = _rel_bias(rel_bias_table)
    for l in range(DEPTH):
        i = l // 2
        w_router = (router_group[l], router_expert[l])
        if l % 2 == 0:
            x1, *routing = _mixer_layer(x, mix_w_in[i], gmlp_ln_g[i], gmlp_ln_b[i],
                                        gmlp_w_spatial[i], gmlp_b_spatial[i], conv_w[i],
                                        mix_w_out[i], ln1_g[l], ln1_b[l], w_router)
        else:
            x1, *routing = _attn_layer(x, bias, attn_w_qkv[i], attn_b_qkv[i], attn_sinks[i],
                                       attn_w_o[i], attn_b_o[i], ln1_g[l], ln1_b[l], w_router)
        x = _moe_layer(x1, *routing, l, expert_w_gate, expert_w_up, expert_w_down,
                       ln2_g[l], ln2_b[l])
    return x.reshape(bsz, s, d)
```

```python
import functools
import math

import jax
import jax.numpy as jnp
from jax import lax
from jax.experimental import pallas as pl
from jax.experimental.pallas import tpu as pltpu

D_MODEL = 1024
SEQ = 16384
DEPTH = 4
CHUNK = 128
A_GROUPS = 4
A_CH = 128
A_WIDTH = A_GROUPS * A_CH
B_WIDTH = 512
CONV_W = 3
MIX_IN = 2 * A_WIDTH + 3 * B_WIDTH
N_HEADS = 16
N_KV = 2
HEAD_DIM = 64
GQA_GROUP = N_HEADS // N_KV
WINDOW = 128
QKV_DIM = (N_HEADS + 2 * N_KV) * HEAD_DIM
ATT_OUT = N_HEADS * HEAD_DIM
KV_WIDTH = N_KV * HEAD_DIM
N_BUCKETS = 32
MAX_DISTANCE = 128
N_GROUPS = 4
EXPERTS_PER_GROUP = 8
N_EXPERTS = N_GROUPS * EXPERTS_PER_GROUP
TOP_K = 2
D_EXPERT = 512
ALPHA = (2 * DEPTH) ** 0.25
LN_EPS = 1e-5

LANES = 128
SUBLANES = 8
ROW_TILE = (SUBLANES, LANES)
assert D_MODEL == SUBLANES * LANES
ROUTER_ROWS = SUBLANES + N_EXPERTS
MIX_ROWS = 1024
ATTN_ROWS = 1024
MOVE_ROWS = 512
EXPERT_ROWS = 512
ROWS_PER_ISSUE = 16
DISPATCH_SLOTS = 3
VMEM_LIMIT = 56 * 1024 * 1024

R_E0, R_E1, R_RANK0, R_RANK1, R_GATE0, R_GATE1 = range(6)

F32 = jnp.float32
BF16 = jnp.bfloat16
I32 = jnp.int32


def _layer_norm(x, g, b):
    mu = jnp.mean(x, axis=-1, keepdims=True)
    xc = x - mu
    var = jnp.mean(xc * xc, axis=-1, keepdims=True)
    return xc * lax.rsqrt(var + LN_EPS) * g + b


def _gelu(x):
    return 0.5 * x * (1.0 + lax.erf(x * (2.0 ** -0.5)))


def _mixer_kernel(*refs, fused):
    x, refs = _tile_input(refs, fused)
    (win_ref, lng_ref, lnb_ref, wsp_ref, bsp_ref, cw_ref, wout_ref, g1_ref, b1_ref, wr_ref,
     tri_ref, x1_ref, route_ref, route_t_ref, count_ref, ztail_ref, base_ref) = refs
    tm = x1_ref.shape[0]
    i = pl.program_id(0)

    @pl.when(i % (SEQ // tm) == 0)
    def _():
        ztail_ref[...] = jnp.zeros_like(ztail_ref)

    h = jnp.dot(x.astype(BF16), win_ref[...], preferred_element_type=F32)
    u = _gelu(h[:, :A_WIDTH])
    v = _gelu(h[:, A_WIDTH:2 * A_WIDTH])
    o = 2 * A_WIDTH
    g_b = h[:, o:o + B_WIDTH]
    g_c = h[:, o + B_WIDTH:o + 2 * B_WIDTH]
    hb = h[:, o + 2 * B_WIDTH:o + 3 * B_WIDTH]

    v = _layer_norm(v, lng_ref[...], lnb_ref[...]).astype(BF16)
    n_chunks = tm // CHUNK
    ri = lax.broadcasted_iota(I32, (CHUNK, CHUNK), 0)
    ci = lax.broadcasted_iota(I32, (CHUNK, CHUNK), 1)
    causal = ci <= ri
    sv_cols = [[None] * A_GROUPS for _ in range(n_chunks)]
    for g in range(A_GROUPS):
        ws = jnp.where(causal, wsp_ref[g], 0.0).astype(BF16)
        vg = jnp.concatenate(
            [v[c * CHUNK:(c + 1) * CHUNK, g * A_CH:(g + 1) * A_CH] for c in range(n_chunks)],
            axis=1)
        sg = jnp.dot(ws, vg, preferred_element_type=F32) + bsp_ref[:, g:g + 1]
        for c in range(n_chunks):
            sv_cols[c][g] = sg[:, c * A_CH:(c + 1) * A_CH]
    sv = jnp.concatenate([jnp.concatenate(row, axis=1) for row in sv_cols], axis=0)
    y_a = u * sv

    z = g_c * hb
    rows = lax.broadcasted_iota(I32, z.shape, 0)
    tail = ztail_ref[...]
    zm1 = jnp.where(rows == 0, tail[7:8, :], pltpu.roll(z, 1, 0))
    zm2 = jnp.where(rows == 0, tail[6:7, :],
                    jnp.where(rows == 1, tail[7:8, :], pltpu.roll(z, 2, 0)))
    conv = cw_ref[0:1, :] * zm2 + cw_ref[1:2, :] * zm1 + cw_ref[2:3, :] * z
    y_b = g_b * conv
    ztail_ref[...] = z[tm - 8:tm, :]

    y = jnp.concatenate([y_a, y_b], axis=1).astype(BF16)
    m = jnp.dot(y, wout_ref[...], preferred_element_type=F32)
    x1 = _layer_norm(ALPHA * x + m, g1_ref[...], b1_ref[...])
    x1_ref[...] = x1
    _route_tile(x1, wr_ref, tri_ref, route_ref, route_t_ref, count_ref, base_ref)


def _input_operands(x, pending, plain_rows):
    if pending is None:
        return False, plain_rows, [pl.BlockSpec((plain_rows, D_MODEL), lambda i: (i, 0))], [], (x,)
    return (True, MOVE_ROWS) + _combine_operands(pending, MOVE_ROWS)


def _mixer_layer(x, pending, w_in, ln_g, ln_b, w_sp, b_sp, conv_w, w_out, g1, b1, w_router):
    fused, tm, x_specs, x_scratch, x_operands = _input_operands(x, pending, MIX_ROWS)
    t = x_operands[2].shape[0] if fused else x.shape[0]
    const2 = lambda i: (0, 0)
    r_in, r_out, r_shape, r_scratch = _route_specs(t, tm)
    return pl.pallas_call(
        functools.partial(_mixer_kernel, fused=fused),
        grid=(t // tm,),
        in_specs=x_specs + [
            pl.BlockSpec((D_MODEL, MIX_IN), const2),
            pl.BlockSpec((1, A_WIDTH), const2),
            pl.BlockSpec((1, A_WIDTH), const2),
            pl.BlockSpec((A_GROUPS, CHUNK, CHUNK), lambda i: (0, 0, 0)),
            pl.BlockSpec((CHUNK, A_GROUPS), const2),
            pl.BlockSpec((CONV_W, B_WIDTH), const2),
            pl.BlockSpec((A_WIDTH + B_WIDTH, D_MODEL), const2),
            pl.BlockSpec((1, D_MODEL), const2),
            pl.BlockSpec((1, D_MODEL), const2),
        ] + r_in,
        out_specs=[pl.BlockSpec((tm, D_MODEL), lambda i: (i, 0))] + r_out,
        out_shape=[jax.ShapeDtypeStruct((t, D_MODEL), F32)] + r_shape,
        scratch_shapes=[pltpu.VMEM((8, B_WIDTH), F32), r_scratch] + x_scratch,
        compiler_params=pltpu.CompilerParams(
            dimension_semantics=("arbitrary",), vmem_limit_bytes=VMEM_LIMIT),
        name="mixer_layer",
    )(*x_operands, w_in.astype(BF16), ln_g.reshape(1, -1), ln_b.reshape(1, -1), w_sp, b_sp.T,
      conv_w, w_out.astype(BF16), g1.reshape(1, -1), b1.reshape(1, -1),
      *_route_operands(*w_router, tm))


def _attn_kernel(*refs, fused):
    x, refs = _tile_input(refs, fused)
    (sink_ref, wqkv_ref, bqkv_ref, bias_ref, wo_ref, bo_ref, g1_ref, b1_ref, wr_ref, tri_ref,
     x1_ref, route_ref, route_t_ref, count_ref, kprev_ref, vprev_ref, base_ref) = refs
    tm = x1_ref.shape[0]
    i = pl.program_id(0)
    first = i % (SEQ // tm) == 0

    @pl.when(first)
    def _():
        kprev_ref[...] = jnp.zeros_like(kprev_ref)
        vprev_ref[...] = jnp.zeros_like(vprev_ref)

    qkv = jnp.dot(x.astype(BF16), wqkv_ref[...], preferred_element_type=F32) + bqkv_ref[...]
    q = (qkv[:, :ATT_OUT] * (HEAD_DIM ** -0.5)).astype(BF16)

    lane = lax.broadcasted_iota(I32, (CHUNK, KV_WIDTH), 1)
    low = lane < HEAD_DIM

    def halves(t):
        swapped = pltpu.roll(t, HEAD_DIM, 1)
        zero = jnp.zeros_like(t)
        return [jnp.where(low, t, zero).astype(BF16), jnp.where(low, zero, swapped).astype(BF16),
                jnp.where(low, swapped, zero).astype(BF16), jnp.where(low, zero, t).astype(BF16)]

    def stacked(prev, cur, j):
        return jnp.concatenate([prev[2 * j], cur[2 * j], prev[2 * j + 1], cur[2 * j + 1]], axis=0)

    a = lax.broadcasted_iota(I32, (CHUNK, 2 * CHUNK), 0)
    c = lax.broadcasted_iota(I32, (CHUNK, 2 * CHUNK), 1)
    window = (c > a) & (c <= a + WINDOW)
    neg = jnp.finfo(F32).min
    n_pairs = N_HEADS // 2
    kv_of = lambda pair: (2 * pair) // GQA_GROUP

    k_prev = [kprev_ref[idx] for idx in range(2 * N_KV)]
    v_prev = [vprev_ref[idx] for idx in range(2 * N_KV)]
    o_blocks = []
    for blk in range(tm // CHUNK):
        rows = slice(blk * CHUNK, (blk + 1) * CHUNK)
        k_cur = halves(qkv[rows, ATT_OUT:ATT_OUT + KV_WIDTH])
        v_cur = halves(qkv[rows, ATT_OUT + KV_WIDTH:])
        k_rhs = [stacked(k_prev, k_cur, j) for j in range(N_KV)]
        v_rhs = [stacked(v_prev, v_cur, j) for j in range(N_KV)]
        k_prev, v_prev = k_cur, v_cur
        mask = window & (c >= jnp.where(first, CHUNK, 0)) if blk == 0 else window

        scores = [lax.dot_general(q[rows, pair * 2 * HEAD_DIM:(pair + 1) * 2 * HEAD_DIM],
                                  k_rhs[kv_of(pair)], (((1,), (1,)), ((), ())),
                                  preferred_element_type=F32) for pair in range(n_pairs)]
        probs, inv_denoms = [], []
        for pair in range(n_pairs):
            sc2 = scores[pair] + bias_ref[pair]
            sides = []
            for side in range(2):
                sc = jnp.where(mask, sc2[:, side * 2 * CHUNK:(side + 1) * 2 * CHUNK], neg)
                sink = sink_ref[2 * pair + side]
                m = jnp.maximum(jnp.max(sc, axis=-1, keepdims=True), sink)
                p = jnp.exp(sc - m)
                denom = jnp.sum(p, axis=-1, keepdims=True) + jnp.exp(sink - m)
                sides.append(p.astype(BF16))
                inv_denoms.append(jnp.broadcast_to(1.0 / denom, (CHUNK, HEAD_DIM)))
            probs.append(jnp.concatenate(sides, axis=1))
        outs = [jnp.dot(probs[pair], v_rhs[kv_of(pair)], preferred_element_type=F32)
                for pair in range(n_pairs)]
        o = jnp.concatenate(outs, axis=1) * jnp.concatenate(inv_denoms, axis=1)
        o_blocks.append(o.astype(BF16))
    for idx in range(2 * N_KV):
        kprev_ref[idx] = k_prev[idx]
        vprev_ref[idx] = v_prev[idx]
    o = jnp.concatenate(o_blocks, axis=0)
    m_out = jnp.dot(o, wo_ref[...], preferred_element_type=F32) + bo_ref[...]
    x1 = _layer_norm(ALPHA * x + m_out, g1_ref[...], b1_ref[...])
    x1_ref[...] = x1
    _route_tile(x1, wr_ref, tri_ref, route_ref, route_t_ref, count_ref, base_ref)


def _t5_bucket(rel):
    n = jnp.maximum(rel, 0)
    max_exact = N_BUCKETS // 2
    nf = jnp.maximum(n, 1).astype(F32)
    large = max_exact + (jnp.log(nf / max_exact) / math.log(MAX_DISTANCE / max_exact)
                         * (N_BUCKETS - max_exact)).astype(I32)
    large = jnp.minimum(large, N_BUCKETS - 1)
    return jnp.where(n < max_exact, n, large)


def _rel_bias(rel_table):
    a = jnp.arange(CHUNK)[:, None]
    c = jnp.arange(2 * CHUNK)[None, :]
    onehot = jax.nn.one_hot(_t5_bucket(a + CHUNK - c), N_BUCKETS, dtype=F32)
    bias = jnp.einsum('acb,bh->hac', onehot, rel_table.astype(F32),
                      precision=lax.Precision.HIGHEST)
    bias = bias.reshape(N_HEADS // 2, 2, CHUNK, 2 * CHUNK).transpose(0, 2, 1, 3)
    return bias.reshape(N_HEADS // 2, CHUNK, 4 * CHUNK)


def _attn_layer(x, pending, bias, w_qkv, b_qkv, sinks, w_o, b_o, g1, b1, w_router):
    fused, tm, x_specs, x_scratch, x_operands = _input_operands(x, pending, ATTN_ROWS)
    t = x_operands[2].shape[0] if fused else x.shape[0]
    const2 = lambda i: (0, 0)
    r_in, r_out, r_shape, r_scratch = _route_specs(t, tm)
    return pl.pallas_call(
        functools.partial(_attn_kernel, fused=fused),
        grid=(t // tm,),
        in_specs=x_specs + [
            pl.BlockSpec(memory_space=pltpu.SMEM),
            pl.BlockSpec((D_MODEL, QKV_DIM), const2),
            pl.BlockSpec((1, QKV_DIM), const2),
            pl.BlockSpec((N_HEADS // 2, CHUNK, 4 * CHUNK), lambda i: (0, 0, 0)),
            pl.BlockSpec((ATT_OUT, D_MODEL), const2),
            pl.BlockSpec((1, D_MODEL), const2),
            pl.BlockSpec((1, D_MODEL), const2),
            pl.BlockSpec((1, D_MODEL), const2),
        ] + r_in,
        out_specs=[pl.BlockSpec((tm, D_MODEL), lambda i: (i, 0))] + r_out,
        out_shape=[jax.ShapeDtypeStruct((t, D_MODEL), F32)] + r_shape,
        scratch_shapes=[pltpu.VMEM((2 * N_KV, CHUNK, KV_WIDTH), BF16),
                        pltpu.VMEM((2 * N_KV, CHUNK, KV_WIDTH), BF16), r_scratch] + x_scratch,
        compiler_params=pltpu.CompilerParams(
            dimension_semantics=("arbitrary",), vmem_limit_bytes=VMEM_LIMIT),
        name="attn_layer",
    )(*x_operands, sinks, w_qkv.astype(BF16), b_qkv.reshape(1, -1), bias, w_o.astype(BF16),
      b_o.reshape(1, -1), g1.reshape(1, -1), b1.reshape(1, -1), *_route_operands(*w_router, tm))


def _route_tile(x1, wr_ref, tri_ref, route_ref, route_t_ref, count_ref, base_ref):
    @pl.when(pl.program_id(0) == 0)
    def _():
        base_ref[...] = jnp.zeros_like(base_ref)

    tm = x1.shape[0]
    lt = lax.dot_general(wr_ref[...], x1.astype(BF16), (((1,), (1,)), ((), ())),
                         preferred_element_type=F32)
    sub = lax.broadcasted_iota(I32, (SUBLANES, tm), 0).astype(F32)
    ninf = -jnp.inf

    def first_argmax(vals):
        m = jnp.max(vals, axis=0, keepdims=True)
        idx = jnp.min(jnp.where(vals == m, sub, float(SUBLANES)), axis=0, keepdims=True)
        return m, idx

    is_g = sub < N_GROUPS
    g_rows = lt[0:SUBLANES]
    gmax, g_idx = first_argmax(jnp.where(is_g, g_rows, ninf))
    g_p = 1.0 / jnp.sum(jnp.where(is_g, jnp.exp(g_rows - gmax), 0.0), axis=0, keepdims=True)

    group_rows = lambda a, g: a[SUBLANES * (g + 1):SUBLANES * (g + 2)]
    el = group_rows(lt, 0)
    for g in range(1, N_GROUPS):
        el = jnp.where(g_idx == g, group_rows(lt, g), el)
    m1, j1 = first_argmax(el)
    m2, j2 = first_argmax(jnp.where(sub == j1, ninf, el))
    a2 = jnp.exp(m2 - m1)
    gate0 = g_p / (1.0 + a2)
    gate1 = g_p * a2 / (1.0 + a2)

    hit0 = [(g_idx == g) & (sub == j1) for g in range(N_GROUPS)]
    hit1 = [(g_idx == g) & (sub == j2) for g in range(N_GROUPS)]
    onehot = jnp.concatenate([jnp.where(h0 | h1, 1.0, 0.0) for h0, h1 in zip(hit0, hit1)],
                             axis=0)
    before = jnp.dot(onehot.astype(BF16), tri_ref[...], preferred_element_type=F32)
    before = before + base_ref[:, 0:1]

    def picked(hits):
        total = jnp.zeros((SUBLANES, tm), F32)
        for g in range(N_GROUPS):
            total = total + jnp.where(hits[g], before[SUBLANES * g:SUBLANES * (g + 1)], 0.0)
        return jnp.sum(total, axis=0, keepdims=True)

    rank0 = picked(hit0)
    rank1 = picked(hit1)
    base_ref[...] += jnp.sum(onehot, axis=1, keepdims=True)

    e_base = g_idx * EXPERTS_PER_GROUP
    rec_t = jnp.concatenate([e_base + j1, e_base + j2, rank0, rank1, gate0, gate1,
                             jnp.zeros((SUBLANES - 6, tm), F32)], axis=0)
    route_t_ref[...] = rec_t
    padded = jnp.concatenate([rec_t, jnp.zeros((LANES - SUBLANES, tm), F32)], axis=0)
    route_ref[...] = jnp.transpose(padded)
    count_ref[...] = base_ref[...]


def _route_specs(t, tm):
    const2 = lambda i: (0, 0)
    in_specs = [pl.BlockSpec((ROUTER_ROWS, D_MODEL), const2), pl.BlockSpec((tm, tm), const2)]
    out_specs = [
        pl.BlockSpec((tm, LANES), lambda i: (i, 0)),
        pl.BlockSpec((SUBLANES, tm), lambda i: (0, i)),
        pl.BlockSpec((N_EXPERTS, LANES), const2),
    ]
    out_shape = [
        jax.ShapeDtypeStruct((t, LANES), F32),
        jax.ShapeDtypeStruct((SUBLANES, t), F32),
        jax.ShapeDtypeStruct((N_EXPERTS, LANES), F32),
    ]
    return in_specs, out_specs, out_shape, pltpu.VMEM((N_EXPERTS, LANES), F32)


def _route_operands(w_group, w_expert, tm):
    pad = jnp.zeros((D_MODEL, SUBLANES - N_GROUPS), F32)
    w_rows = jnp.concatenate([w_group, pad, w_expert], axis=1).T
    return w_rows.astype(BF16), jnp.triu(jnp.ones((tm, tm), BF16), 1)


def _plan(route_t, count_lanes):
    t = route_t.shape[1]
    rb = EXPERT_ROWS
    experts = jnp.arange(N_EXPERTS, dtype=I32)
    counts = count_lanes[:, 0].astype(I32)
    end = jnp.cumsum(counts)
    start = end - counts
    e = route_t[R_E0:R_E1 + 1].astype(I32)
    rank = route_t[R_RANK0:R_RANK1 + 1].astype(I32)
    onehot = e[:, None, :] == experts[None, :, None]
    dest = jnp.sum(jnp.where(onehot, start[None, :, None], 0), axis=1) + rank

    first_blk = start // rb
    last_blk = jnp.maximum(end - 1, 0) // rb
    n_steps_e = jnp.where(counts > 0, last_blk - first_blk + 1, 0)
    step_end = jnp.cumsum(n_steps_e)
    step_start = step_end - n_steps_e
    n_steps = step_end[-1]
    max_steps = t * TOP_K // rb + N_EXPERTS - 1
    s = jnp.minimum(jnp.arange(max_steps, dtype=I32), n_steps - 1)
    step_e = jnp.minimum(jnp.sum((step_end[None, :] <= s[:, None]).astype(I32), axis=1),
                         N_EXPERTS - 1)
    pick = step_e[:, None] == experts[None, :]
    take = lambda v: jnp.sum(jnp.where(pick, v[None, :], 0), axis=1)
    step_blk = take(first_blk) + s - take(step_start)
    step_lo = jnp.clip(take(start) - step_blk * rb, 0, rb)
    step_hi = jnp.clip(take(end) - step_blk * rb, 0, rb)
    return dest, (step_blk, step_e, step_lo, step_hi, n_steps.reshape(1))


def _dest_blocks(dest, tm):
    t = dest.shape[1]
    return dest.reshape(TOP_K, t // tm, tm).transpose(1, 0, 2).reshape(t // tm, 1, TOP_K * tm)


def _to_row_tiles(x):
    return x.reshape(x.shape[0], ROW_TILE[0], ROW_TILE[1])


def _from_row_tiles(x):
    return x.reshape(x.shape[0], D_MODEL)


def _dispatch_kernel(dest_ref, x1_ref, xs_hbm, xbuf, row_sems):
    tm = x1_ref.shape[0]
    i = pl.program_id(0)
    n = pl.num_programs(0)
    slot = i % DISPATCH_SLOTS

    def drain_rows(s):
        for k in range(TOP_K):
            pltpu.make_async_copy(xbuf.at[s], xs_hbm.at[pl.ds(0, tm)], row_sems.at[s]).wait()

    @pl.when(i >= DISPATCH_SLOTS)
    def _():
        drain_rows(slot)

    xbuf[slot] = _to_row_tiles(x1_ref[...])

    def issue(g, carry):
        for j in range(ROWS_PER_ISSUE):
            r = g * ROWS_PER_ISSUE + j
            for k in range(TOP_K):
                pltpu.make_async_copy(xbuf.at[slot, r], xs_hbm.at[dest_ref[0, 0, k * tm + r]],
                                      row_sems.at[slot]).start(priority=k)
        return carry

    lax.fori_loop(0, tm // ROWS_PER_ISSUE, issue, 0)

    @pl.when(i == n - 1)
    def _():
        for back in range(DISPATCH_SLOTS):
            drain_rows((i - back) % DISPATCH_SLOTS)


def _dispatch(x1, dest):
    t = x1.shape[0]
    tm = MOVE_ROWS
    return pl.pallas_call(
        _dispatch_kernel,
        grid=(t // tm,),
        in_specs=[
            pl.BlockSpec((1, 1, TOP_K * tm), lambda i: (i, 0, 0), memory_space=pltpu.SMEM),
            pl.BlockSpec((tm, D_MODEL), lambda i: (i, 0)),
        ],
        out_specs=pl.BlockSpec(memory_space=pl.ANY),
        out_shape=jax.ShapeDtypeStruct((t * TOP_K,) + ROW_TILE, F32),
        scratch_shapes=[
            pltpu.VMEM((DISPATCH_SLOTS, tm) + ROW_TILE, F32),
            pltpu.SemaphoreType.DMA((DISPATCH_SLOTS,)),
        ],
        compiler_params=pltpu.CompilerParams(
            dimension_semantics=("arbitrary",), vmem_limit_bytes=VMEM_LIMIT),
        name="dispatch",
    )(_dest_blocks(dest, tm), x1)


def _expert_kernel(blk_ref, e_ref, lo_ref, hi_ref, n_steps_ref, xs_ref, wg_ref, wu_ref, wd_ref,
                   ys_ref, wg_bf, wu_bf, wd_bf):
    s = pl.program_id(0)
    prev = jnp.maximum(s - 1, 0)
    new_expert = (s == 0) | (e_ref[s] != e_ref[prev])
    new_block = (s == 0) | (blk_ref[s] != blk_ref[prev])

    @pl.when(new_expert)
    def _():
        wg_bf[...] = wg_ref[0, 0].astype(BF16)
        wu_bf[...] = wu_ref[0, 0].astype(BF16)
        wd_bf[...] = wd_ref[0, 0].astype(BF16)

    @pl.when(s < n_steps_ref[0])
    def _():
        xb = _from_row_tiles(xs_ref[...]).astype(BF16)
        gate = jnp.dot(xb, wg_bf[...], preferred_element_type=F32)
        up = jnp.dot(xb, wu_bf[...], preferred_element_type=F32)
        hid = (gate * jax.nn.sigmoid(gate) * up).astype(BF16)
        y = jnp.dot(hid, wd_bf[...], preferred_element_type=F32)
        rows = lax.broadcasted_iota(I32, y.shape, 0)
        y = _to_row_tiles(jnp.where((rows >= lo_ref[s]) & (rows < hi_ref[s]), y, 0.0))

        @pl.when(new_block)
        def _():
            ys_ref[...] = y

        @pl.when(jnp.logical_not(new_block))
        def _():
            ys_ref[...] += y


def _expert_ffn(xs, steps, layer, wg, wu, wd):
    step_blk, step_e, step_lo, step_hi, n_steps = steps
    rb = EXPERT_ROWS
    row_map = lambda s, blk, e, lo, hi, n: (blk[s], 0, 0)
    w_map = lambda s, blk, e, lo, hi, n: (layer, e[s], 0, 0)
    grid_spec = pltpu.PrefetchScalarGridSpec(
        num_scalar_prefetch=5,
        grid=(step_blk.shape[0],),
        in_specs=[
            pl.BlockSpec((rb,) + ROW_TILE, row_map),
            pl.BlockSpec((1, 1, D_MODEL, D_EXPERT), w_map),
            pl.BlockSpec((1, 1, D_MODEL, D_EXPERT), w_map),
            pl.BlockSpec((1, 1, D_EXPERT, D_MODEL), w_map),
        ],
        out_specs=pl.BlockSpec((rb,) + ROW_TILE, row_map),
        scratch_shapes=[
            pltpu.VMEM((D_MODEL, D_EXPERT), BF16),
            pltpu.VMEM((D_MODEL, D_EXPERT), BF16),
            pltpu.VMEM((D_EXPERT, D_MODEL), BF16),
        ],
    )
    return pl.pallas_call(
        _expert_kernel,
        grid_spec=grid_spec,
        out_shape=jax.ShapeDtypeStruct(xs.shape, F32),
        compiler_params=pltpu.CompilerParams(
            dimension_semantics=("arbitrary",), vmem_limit_bytes=VMEM_LIMIT),
        name="expert_ffn",
    )(step_blk, step_e, step_lo, step_hi, n_steps, xs, wg, wu, wd)


N_COMBINE_INPUTS = 7


def _combined_rows(dest_ref, dest_next_ref, x1_ref, route_ref, g2_ref, b2_ref, ys_hbm, ybuf, sems):
    tm = x1_ref.shape[0]
    i = pl.program_id(0)
    n = pl.num_programs(0)
    slot = i % 2

    def gather(d_ref, to_slot):
        def issue(g, carry):
            for j in range(ROWS_PER_ISSUE):
                r = g * ROWS_PER_ISSUE + j
                for k in range(TOP_K):
                    pltpu.make_async_copy(ys_hbm.at[d_ref[0, 0, k * tm + r]],
                                          ybuf.at[to_slot, k, r],
                                          sems.at[to_slot]).start(priority=k)
            return carry
        lax.fori_loop(0, tm // ROWS_PER_ISSUE, issue, 0)

    @pl.when(i == 0)
    def _():
        gather(dest_ref, slot)

    @pl.when(i + 1 < n)
    def _():
        gather(dest_next_ref, 1 - slot)

    for k in range(TOP_K):
        pltpu.make_async_copy(ys_hbm.at[pl.ds(0, tm)], ybuf.at[slot, k], sems.at[slot]).wait()

    route = route_ref[...]
    f = (route[:, R_GATE0:R_GATE0 + 1] * _from_row_tiles(ybuf[slot, 0])
         + route[:, R_GATE1:R_GATE1 + 1] * _from_row_tiles(ybuf[slot, 1]))
    return _layer_norm(ALPHA * x1_ref[...] + f, g2_ref[...], b2_ref[...])


def _combine_operands(pending, tm):
    x1, ys, dest, route, g2, b2 = pending
    n = x1.shape[0] // tm
    row = lambda i: (i, 0)
    const2 = lambda i: (0, 0)
    dest_blocks = _dest_blocks(dest, tm)
    smem_block = lambda index_map: pl.BlockSpec((1, 1, TOP_K * tm), index_map,
                                                memory_space=pltpu.SMEM)
    in_specs = [
        smem_block(lambda i: (i, 0, 0)),
        smem_block(lambda i: (jnp.minimum(i + 1, n - 1), 0, 0)),
        pl.BlockSpec((tm, D_MODEL), row),
        pl.BlockSpec((tm, LANES), row),
        pl.BlockSpec((1, D_MODEL), const2),
        pl.BlockSpec((1, D_MODEL), const2),
        pl.BlockSpec(memory_space=pl.ANY),
    ]
    assert len(in_specs) == N_COMBINE_INPUTS
    scratch = [pltpu.VMEM((2, TOP_K, tm) + ROW_TILE, F32), pltpu.SemaphoreType.DMA((2,))]
    operands = (dest_blocks, dest_blocks, x1, route, g2.reshape(1, -1), b2.reshape(1, -1), ys)
    return in_specs, scratch, operands


def _tile_input(refs, fused):
    if not fused:
        return refs[0][...], refs[1:]
    x = _combined_rows(*refs[:N_COMBINE_INPUTS], *refs[-2:])
    return x, refs[N_COMBINE_INPUTS:-2]


def _combine_kernel(*refs):
    x2, (x2_ref,) = _tile_input(refs, True)
    x2_ref[...] = x2


def _combine_ln(pending):
    t = pending[0].shape[0]
    tm = MOVE_ROWS
    in_specs, scratch, operands = _combine_operands(pending, tm)
    return pl.pallas_call(
        _combine_kernel,
        grid=(t // tm,),
        in_specs=in_specs,
        out_specs=pl.BlockSpec((tm, D_MODEL), lambda i: (i, 0)),
        out_shape=jax.ShapeDtypeStruct((t, D_MODEL), F32),
        scratch_shapes=scratch,
        compiler_params=pltpu.CompilerParams(
            dimension_semantics=("arbitrary",), vmem_limit_bytes=VMEM_LIMIT),
        name="combine_ln",
    )(*operands)


def _moe_layer(x1, route, route_t, counts, layer, wg, wu, wd, g2, b2):
    dest, steps = _plan(route_t, counts)
    xs = _dispatch(x1, dest)
    ys = _expert_ffn(xs, steps, layer, wg, wu, wd)
    return (x1, ys, dest, route, g2, b2)


def kernel(x, rel_bias_table, mix_w_in, gmlp_ln_g, gmlp_ln_b, gmlp_w_spatial, gmlp_b_spatial, conv_w, mix_w_out, attn_w_qkv, attn_b_qkv, attn_sinks, attn_w_o, attn_b_o, ln1_g, ln1_b, ln2_g, ln2_b, router_group, router_expert, expert_w_gate, expert_w_up, expert_w_down):
    bsz, s, d = x.shape
    assert (s, d) == (SEQ, D_MODEL)
    x = x.reshape(bsz * s, d)
    bias = _rel_bias(rel_bias_table)
    pending = None
    for l in range(DEPTH):
        i = l // 2
        w_router = (router_group[l], router_expert[l])
        if l % 2 == 0:
            x1, *routing = _mixer_layer(x, pending, mix_w_in[i], gmlp_ln_g[i], gmlp_ln_b[i],
                                        gmlp_w_spatial[i], gmlp_b_spatial[i], conv_w[i],
                                        mix_w_out[i], ln1_g[l], ln1_b[l], w_router)
        else:
            x1, *routing = _attn_layer(x, pending, bias, attn_w_qkv[i], attn_b_qkv[i],
                                       attn_sinks[i], attn_w_o[i], attn_b_o[i], ln1_g[l],
                                       ln1_b[l], w_router)
        pending = _moe_layer(x1, *routing, l, expert_w_gate, expert_w_up, expert_w_down,
                             ln2_g[l], ln2_b[l])
    return _combine_ln(pending).reshape(bsz, s, d)
```

```python
import functools
import math

import jax
import jax.numpy as jnp
from jax import lax
from jax.experimental import pallas as pl
from jax.experimental.pallas import tpu as pltpu

D_MODEL = 1024
SEQ = 16384
DEPTH = 4
CHUNK = 128
A_GROUPS = 4
A_CH = 128
A_WIDTH = A_GROUPS * A_CH
B_WIDTH = 512
CONV_W = 3
MIX_IN = 2 * A_WIDTH + 3 * B_WIDTH
N_HEADS = 16
N_KV = 2
HEAD_DIM = 64
GQA_GROUP = N_HEADS // N_KV
WINDOW = 128
QKV_DIM = (N_HEADS + 2 * N_KV) * HEAD_DIM
ATT_OUT = N_HEADS * HEAD_DIM
KV_WIDTH = N_KV * HEAD_DIM
N_BUCKETS = 32
MAX_DISTANCE = 128
N_GROUPS = 4
EXPERTS_PER_GROUP = 8
N_EXPERTS = N_GROUPS * EXPERTS_PER_GROUP
TOP_K = 2
D_EXPERT = 512
ALPHA = (2 * DEPTH) ** 0.25
LN_EPS = 1e-5

LANES = 128
SUBLANES = 8
ROW_TILE = (SUBLANES, LANES)
assert D_MODEL == SUBLANES * LANES
ROUTER_ROWS = SUBLANES + N_EXPERTS
MIX_ROWS = 1024
ATTN_ROWS = 1024
MOVE_ROWS = 512
EXPERT_ROWS = 512
ROWS_PER_ISSUE = 16
DISPATCH_SLOTS = 3
GATHER_PARTS = 8
assert MOVE_ROWS % (ROWS_PER_ISSUE * GATHER_PARTS) == 0
VMEM_LIMIT = 56 * 1024 * 1024

R_E0, R_E1, R_RANK0, R_RANK1, R_GATE0, R_GATE1 = range(6)

F32 = jnp.float32
BF16 = jnp.bfloat16
I32 = jnp.int32


def _layer_norm(x, g, b):
    mu = jnp.mean(x, axis=-1, keepdims=True)
    xc = x - mu
    var = jnp.mean(xc * xc, axis=-1, keepdims=True)
    return xc * lax.rsqrt(var + LN_EPS) * g + b


def _gelu(x):
    return 0.5 * x * (1.0 + lax.erf(x * (2.0 ** -0.5)))


def _mixer_kernel(*refs, fused):
    x, refs, prefetch = _tile_input(refs, fused)
    (win_ref, lng_ref, lnb_ref, wsp_ref, bsp_ref, cw_ref, wout_ref, g1_ref, b1_ref, wr_ref,
     tri_ref, x1_ref, route_ref, route_t_ref, count_ref, ztail_ref, base_ref) = refs
    tm = x1_ref.shape[0]
    i = pl.program_id(0)

    @pl.when(i % (SEQ // tm) == 0)
    def _():
        ztail_ref[...] = jnp.zeros_like(ztail_ref)

    xb = x.astype(BF16)
    sections = []
    for sec in range(MIX_IN // A_WIDTH):
        cols = pl.ds(sec * A_WIDTH, A_WIDTH)
        sections.append(jnp.dot(xb, win_ref[:, cols], preferred_element_type=F32))
        prefetch(sec)
    u = _gelu(sections[0])
    v = _gelu(sections[1])
    g_b, g_c, hb = sections[2:]

    v = _layer_norm(v, lng_ref[...], lnb_ref[...]).astype(BF16)
    n_chunks = tm // CHUNK
    ri = lax.broadcasted_iota(I32, (CHUNK, CHUNK), 0)
    ci = lax.broadcasted_iota(I32, (CHUNK, CHUNK), 1)
    causal = ci <= ri
    sv_cols = [[None] * A_GROUPS for _ in range(n_chunks)]
    for g in range(A_GROUPS):
        ws = jnp.where(causal, wsp_ref[g], 0.0).astype(BF16)
        vg = jnp.concatenate(
            [v[c * CHUNK:(c + 1) * CHUNK, g * A_CH:(g + 1) * A_CH] for c in range(n_chunks)],
            axis=1)
        sg = jnp.dot(ws, vg, preferred_element_type=F32) + bsp_ref[:, g:g + 1]
        for c in range(n_chunks):
            sv_cols[c][g] = sg[:, c * A_CH:(c + 1) * A_CH]
    sv = jnp.concatenate([jnp.concatenate(row, axis=1) for row in sv_cols], axis=0)
    y_a = u * sv

    z = g_c * hb
    rows = lax.broadcasted_iota(I32, z.shape, 0)
    tail = ztail_ref[...]
    zm1 = jnp.where(rows == 0, tail[7:8, :], pltpu.roll(z, 1, 0))
    zm2 = jnp.where(rows == 0, tail[6:7, :],
                    jnp.where(rows == 1, tail[7:8, :], pltpu.roll(z, 2, 0)))
    conv = cw_ref[0:1, :] * zm2 + cw_ref[1:2, :] * zm1 + cw_ref[2:3, :] * z
    y_b = g_b * conv
    ztail_ref[...] = z[tm - 8:tm, :]
    prefetch(5)

    y = jnp.concatenate([y_a, y_b], axis=1).astype(BF16)
    m = jnp.dot(y, wout_ref[...], preferred_element_type=F32)
    prefetch(6)
    x1 = _layer_norm(ALPHA * x + m, g1_ref[...], b1_ref[...])
    x1_ref[...] = x1
    _route_tile(x1, wr_ref, tri_ref, route_ref, route_t_ref, count_ref, base_ref)
    prefetch(7)


def _input_operands(x, pending, plain_rows):
    if pending is None:
        return False, plain_rows, [pl.BlockSpec((plain_rows, D_MODEL), lambda i: (i, 0))], [], (x,)
    return (True, MOVE_ROWS) + _combine_operands(pending, MOVE_ROWS)


def _mixer_layer(x, pending, w_in, ln_g, ln_b, w_sp, b_sp, conv_w, w_out, g1, b1, w_router):
    fused, tm, x_specs, x_scratch, x_operands = _input_operands(x, pending, MIX_ROWS)
    t = x_operands[2].shape[0] if fused else x.shape[0]
    const2 = lambda i: (0, 0)
    r_in, r_out, r_shape, r_scratch = _route_specs(t, tm)
    return pl.pallas_call(
        functools.partial(_mixer_kernel, fused=fused),
        grid=(t // tm,),
        in_specs=x_specs + [
            pl.BlockSpec((D_MODEL, MIX_IN), const2),
            pl.BlockSpec((1, A_WIDTH), const2),
            pl.BlockSpec((1, A_WIDTH), const2),
            pl.BlockSpec((A_GROUPS, CHUNK, CHUNK), lambda i: (0, 0, 0)),
            pl.BlockSpec((CHUNK, A_GROUPS), const2),
            pl.BlockSpec((CONV_W, B_WIDTH), const2),
            pl.BlockSpec((A_WIDTH + B_WIDTH, D_MODEL), const2),
            pl.BlockSpec((1, D_MODEL), const2),
            pl.BlockSpec((1, D_MODEL), const2),
        ] + r_in,
        out_specs=[pl.BlockSpec((tm, D_MODEL), lambda i: (i, 0))] + r_out,
        out_shape=[jax.ShapeDtypeStruct((t, D_MODEL), F32)] + r_shape,
        scratch_shapes=[pltpu.VMEM((8, B_WIDTH), F32), r_scratch] + x_scratch,
        compiler_params=pltpu.CompilerParams(
            dimension_semantics=("arbitrary",), vmem_limit_bytes=VMEM_LIMIT),
        name="mixer_layer",
    )(*x_operands, w_in.astype(BF16), ln_g.reshape(1, -1), ln_b.reshape(1, -1), w_sp, b_sp.T,
      conv_w, w_out.astype(BF16), g1.reshape(1, -1), b1.reshape(1, -1),
      *_route_operands(*w_router, tm))


def _attn_kernel(*refs, fused):
    x, refs, prefetch = _tile_input(refs, fused)
    (sink_ref, wqkv_ref, bqkv_ref, bias_ref, wo_ref, bo_ref, g1_ref, b1_ref, wr_ref, tri_ref,
     x1_ref, route_ref, route_t_ref, count_ref, kprev_ref, vprev_ref, base_ref) = refs
    tm = x1_ref.shape[0]
    i = pl.program_id(0)
    first = i % (SEQ // tm) == 0

    @pl.when(first)
    def _():
        kprev_ref[...] = jnp.zeros_like(kprev_ref)
        vprev_ref[...] = jnp.zeros_like(vprev_ref)

    qkv = jnp.dot(x.astype(BF16), wqkv_ref[...], preferred_element_type=F32) + bqkv_ref[...]
    q = (qkv[:, :ATT_OUT] * (HEAD_DIM ** -0.5)).astype(BF16)
    n_blocks = tm // CHUNK
    assert not fused or n_blocks + 4 == GATHER_PARTS
    prefetch(0)

    lane = lax.broadcasted_iota(I32, (CHUNK, KV_WIDTH), 1)
    low = lane < HEAD_DIM

    def halves(t):
        swapped = pltpu.roll(t, HEAD_DIM, 1)
        zero = jnp.zeros_like(t)
        return [jnp.where(low, t, zero).astype(BF16), jnp.where(low, zero, swapped).astype(BF16),
                jnp.where(low, swapped, zero).astype(BF16), jnp.where(low, zero, t).astype(BF16)]

    def stacked(prev, cur, j):
        return jnp.concatenate([prev[2 * j], cur[2 * j], prev[2 * j + 1], cur[2 * j + 1]], axis=0)

    a = lax.broadcasted_iota(I32, (CHUNK, 2 * CHUNK), 0)
    c = lax.broadcasted_iota(I32, (CHUNK, 2 * CHUNK), 1)
    window = (c > a) & (c <= a + WINDOW)
    neg = jnp.finfo(F32).min
    n_pairs = N_HEADS // 2
    kv_of = lambda pair: (2 * pair) // GQA_GROUP

    k_prev = [kprev_ref[idx] for idx in range(2 * N_KV)]
    v_prev = [vprev_ref[idx] for idx in range(2 * N_KV)]
    o_blocks = []
    for blk in range(tm // CHUNK):
        rows = slice(blk * CHUNK, (blk + 1) * CHUNK)
        k_cur = halves(qkv[rows, ATT_OUT:ATT_OUT + KV_WIDTH])
        v_cur = halves(qkv[rows, ATT_OUT + KV_WIDTH:])
        k_rhs = [stacked(k_prev, k_cur, j) for j in range(N_KV)]
        v_rhs = [stacked(v_prev, v_cur, j) for j in range(N_KV)]
        k_prev, v_prev = k_cur, v_cur
        mask = window & (c >= jnp.where(first, CHUNK, 0)) if blk == 0 else window

        scores = [lax.dot_general(q[rows, pair * 2 * HEAD_DIM:(pair + 1) * 2 * HEAD_DIM],
                                  k_rhs[kv_of(pair)], (((1,), (1,)), ((), ())),
                                  preferred_element_type=F32) for pair in range(n_pairs)]
        probs, inv_denoms = [], []
        for pair in range(n_pairs):
            sc2 = scores[pair] + bias_ref[pair]
            sides = []
            for side in range(2):
                sc = jnp.where(mask, sc2[:, side * 2 * CHUNK:(side + 1) * 2 * CHUNK], neg)
                sink = sink_ref[2 * pair + side]
                m = jnp.maximum(jnp.max(sc, axis=-1, keepdims=True), sink)
                p = jnp.exp(sc - m)
                denom = jnp.sum(p, axis=-1, keepdims=True) + jnp.exp(sink - m)
                sides.append(p.astype(BF16))
                inv_denoms.append(jnp.broadcast_to(1.0 / denom, (CHUNK, HEAD_DIM)))
            probs.append(jnp.concatenate(sides, axis=1))
        outs = [jnp.dot(probs[pair], v_rhs[kv_of(pair)], preferred_element_type=F32)
                for pair in range(n_pairs)]
        o = jnp.concatenate(outs, axis=1) * jnp.concatenate(inv_denoms, axis=1)
        o_blocks.append(o.astype(BF16))
        prefetch(1 + blk)
    for idx in range(2 * N_KV):
        kprev_ref[idx] = k_prev[idx]
        vprev_ref[idx] = v_prev[idx]
    o = jnp.concatenate(o_blocks, axis=0)
    m_out = jnp.dot(o, wo_ref[...], preferred_element_type=F32) + bo_ref[...]
    prefetch(n_blocks + 1)
    x1 = _layer_norm(ALPHA * x + m_out, g1_ref[...], b1_ref[...])
    x1_ref[...] = x1
    prefetch(n_blocks + 2)
    _route_tile(x1, wr_ref, tri_ref, route_ref, route_t_ref, count_ref, base_ref)
    prefetch(n_blocks + 3)


def _t5_bucket(rel):
    n = jnp.maximum(rel, 0)
    max_exact = N_BUCKETS // 2
    nf = jnp.maximum(n, 1).astype(F32)
    large = max_exact + (jnp.log(nf / max_exact) / math.log(MAX_DISTANCE / max_exact)
                         * (N_BUCKETS - max_exact)).astype(I32)
    large = jnp.minimum(large, N_BUCKETS - 1)
    return jnp.where(n < max_exact, n, large)


def _rel_bias(rel_table):
    a = jnp.arange(CHUNK)[:, None]
    c = jnp.arange(2 * CHUNK)[None, :]
    onehot = jax.nn.one_hot(_t5_bucket(a + CHUNK - c), N_BUCKETS, dtype=F32)
    bias = jnp.einsum('acb,bh->hac', onehot, rel_table.astype(F32),
                      precision=lax.Precision.HIGHEST)
    bias = bias.reshape(N_HEADS // 2, 2, CHUNK, 2 * CHUNK).transpose(0, 2, 1, 3)
    return bias.reshape(N_HEADS // 2, CHUNK, 4 * CHUNK)


def _attn_layer(x, pending, bias, w_qkv, b_qkv, sinks, w_o, b_o, g1, b1, w_router):
    fused, tm, x_specs, x_scratch, x_operands = _input_operands(x, pending, ATTN_ROWS)
    t = x_operands[2].shape[0] if fused else x.shape[0]
    const2 = lambda i: (0, 0)
    r_in, r_out, r_shape, r_scratch = _route_specs(t, tm)
    return pl.pallas_call(
        functools.partial(_attn_kernel, fused=fused),
        grid=(t // tm,),
        in_specs=x_specs + [
            pl.BlockSpec(memory_space=pltpu.SMEM),
            pl.BlockSpec((D_MODEL, QKV_DIM), const2),
            pl.BlockSpec((1, QKV_DIM), const2),
            pl.BlockSpec((N_HEADS // 2, CHUNK, 4 * CHUNK), lambda i: (0, 0, 0)),
            pl.BlockSpec((ATT_OUT, D_MODEL), const2),
            pl.BlockSpec((1, D_MODEL), const2),
            pl.BlockSpec((1, D_MODEL), const2),
            pl.BlockSpec((1, D_MODEL), const2),
        ] + r_in,
        out_specs=[pl.BlockSpec((tm, D_MODEL), lambda i: (i, 0))] + r_out,
        out_shape=[jax.ShapeDtypeStruct((t, D_MODEL), F32)] + r_shape,
        scratch_shapes=[pltpu.VMEM((2 * N_KV, CHUNK, KV_WIDTH), BF16),
                        pltpu.VMEM((2 * N_KV, CHUNK, KV_WIDTH), BF16), r_scratch] + x_scratch,
        compiler_params=pltpu.CompilerParams(
            dimension_semantics=("arbitrary",), vmem_limit_bytes=VMEM_LIMIT),
        name="attn_layer",
    )(*x_operands, sinks, w_qkv.astype(BF16), b_qkv.reshape(1, -1), bias, w_o.astype(BF16),
      b_o.reshape(1, -1), g1.reshape(1, -1), b1.reshape(1, -1), *_route_operands(*w_router, tm))


def _route_tile(x1, wr_ref, tri_ref, route_ref, route_t_ref, count_ref, base_ref):
    @pl.when(pl.program_id(0) == 0)
    def _():
        base_ref[...] = jnp.zeros_like(base_ref)

    tm = x1.shape[0]
    lt = lax.dot_general(wr_ref[...], x1.astype(BF16), (((1,), (1,)), ((), ())),
                         preferred_element_type=F32)
    sub = lax.broadcasted_iota(I32, (SUBLANES, tm), 0).astype(F32)
    ninf = -jnp.inf

    def first_argmax(vals):
        m = jnp.max(vals, axis=0, keepdims=True)
        idx = jnp.min(jnp.where(vals == m, sub, float(SUBLANES)), axis=0, keepdims=True)
        return m, idx

    is_g = sub < N_GROUPS
    g_rows = lt[0:SUBLANES]
    gmax, g_idx = first_argmax(jnp.where(is_g, g_rows, ninf))
    g_p = 1.0 / jnp.sum(jnp.where(is_g, jnp.exp(g_rows - gmax), 0.0), axis=0, keepdims=True)

    group_rows = lambda a, g: a[SUBLANES * (g + 1):SUBLANES * (g + 2)]
    el = group_rows(lt, 0)
    for g in range(1, N_GROUPS):
        el = jnp.where(g_idx == g, group_rows(lt, g), el)
    m1, j1 = first_argmax(el)
    m2, j2 = first_argmax(jnp.where(sub == j1, ninf, el))
    a2 = jnp.exp(m2 - m1)
    gate0 = g_p / (1.0 + a2)
    gate1 = g_p * a2 / (1.0 + a2)

    hit0 = [(g_idx == g) & (sub == j1) for g in range(N_GROUPS)]
    hit1 = [(g_idx == g) & (sub == j2) for g in range(N_GROUPS)]
    onehot = jnp.concatenate([jnp.where(h0 | h1, 1.0, 0.0) for h0, h1 in zip(hit0, hit1)],
                             axis=0)
    before = jnp.dot(onehot.astype(BF16), tri_ref[...], preferred_element_type=F32)
    before = before + base_ref[:, 0:1]

    def picked(hits):
        total = jnp.zeros((SUBLANES, tm), F32)
        for g in range(N_GROUPS):
            total = total + jnp.where(hits[g], before[SUBLANES * g:SUBLANES * (g + 1)], 0.0)
        return jnp.sum(total, axis=0, keepdims=True)

    rank0 = picked(hit0)
    rank1 = picked(hit1)
    base_ref[...] += jnp.sum(onehot, axis=1, keepdims=True)

    e_base = g_idx * EXPERTS_PER_GROUP
    rec_t = jnp.concatenate([e_base + j1, e_base + j2, rank0, rank1, gate0, gate1,
                             jnp.zeros((SUBLANES - 6, tm), F32)], axis=0)
    route_t_ref[...] = rec_t
    padded = jnp.concatenate([rec_t, jnp.zeros((LANES - SUBLANES, tm), F32)], axis=0)
    route_ref[...] = jnp.transpose(padded)
    count_ref[...] = base_ref[...]


def _route_specs(t, tm):
    const2 = lambda i: (0, 0)
    in_specs = [pl.BlockSpec((ROUTER_ROWS, D_MODEL), const2), pl.BlockSpec((tm, tm), const2)]
    out_specs = [
        pl.BlockSpec((tm, LANES), lambda i: (i, 0)),
        pl.BlockSpec((SUBLANES, tm), lambda i: (0, i)),
        pl.BlockSpec((N_EXPERTS, LANES), const2),
    ]
    out_shape = [
        jax.ShapeDtypeStruct((t, LANES), F32),
        jax.ShapeDtypeStruct((SUBLANES, t), F32),
        jax.ShapeDtypeStruct((N_EXPERTS, LANES), F32),
    ]
    return in_specs, out_specs, out_shape, pltpu.VMEM((N_EXPERTS, LANES), F32)


def _route_operands(w_group, w_expert, tm):
    pad = jnp.zeros((D_MODEL, SUBLANES - N_GROUPS), F32)
    w_rows = jnp.concatenate([w_group, pad, w_expert], axis=1).T
    return w_rows.astype(BF16), jnp.triu(jnp.ones((tm, tm), BF16), 1)


def _plan(route_t, count_lanes):
    t = route_t.shape[1]
    rb = EXPERT_ROWS
    experts = jnp.arange(N_EXPERTS, dtype=I32)
    counts = count_lanes[:, 0].astype(I32)
    end = jnp.cumsum(counts)
    start = end - counts
    e = route_t[R_E0:R_E1 + 1].astype(I32)
    rank = route_t[R_RANK0:R_RANK1 + 1].astype(I32)
    onehot = e[:, None, :] == experts[None, :, None]
    dest = jnp.sum(jnp.where(onehot, start[None, :, None], 0), axis=1) + rank

    first_blk = start // rb
    last_blk = jnp.maximum(end - 1, 0) // rb
    n_steps_e = jnp.where(counts > 0, last_blk - first_blk + 1, 0)
    step_end = jnp.cumsum(n_steps_e)
    step_start = step_end - n_steps_e
    n_steps = step_end[-1]
    max_steps = t * TOP_K // rb + N_EXPERTS - 1
    s = jnp.minimum(jnp.arange(max_steps, dtype=I32), n_steps - 1)
    step_e = jnp.minimum(jnp.sum((step_end[None, :] <= s[:, None]).astype(I32), axis=1),
                         N_EXPERTS - 1)
    pick = step_e[:, None] == experts[None, :]
    take = lambda v: jnp.sum(jnp.where(pick, v[None, :], 0), axis=1)
    step_blk = take(first_blk) + s - take(step_start)
    step_lo = jnp.clip(take(start) - step_blk * rb, 0, rb)
    step_hi = jnp.clip(take(end) - step_blk * rb, 0, rb)
    return dest, (step_blk, step_e, step_lo, step_hi, n_steps.reshape(1))


def _dest_blocks(dest, tm):
    t = dest.shape[1]
    return dest.reshape(TOP_K, t // tm, tm).transpose(1, 0, 2).reshape(t // tm, 1, TOP_K * tm)


def _to_row_tiles(x):
    return x.reshape(x.shape[0], ROW_TILE[0], ROW_TILE[1])


def _from_row_tiles(x):
    return x.reshape(x.shape[0], D_MODEL)


def _dispatch_kernel(dest_ref, x1_ref, xs_hbm, xbuf, row_sems):
    tm = x1_ref.shape[0]
    i = pl.program_id(0)
    n = pl.num_programs(0)
    slot = i % DISPATCH_SLOTS

    def drain_rows(s):
        for k in range(TOP_K):
            pltpu.make_async_copy(xbuf.at[s], xs_hbm.at[pl.ds(0, tm)], row_sems.at[s]).wait()

    @pl.when(i >= DISPATCH_SLOTS)
    def _():
        drain_rows(slot)

    xbuf[slot] = _to_row_tiles(x1_ref[...])

    def issue(g, carry):
        for j in range(ROWS_PER_ISSUE):
            r = g * ROWS_PER_ISSUE + j
            for k in range(TOP_K):
                pltpu.make_async_copy(xbuf.at[slot, r], xs_hbm.at[dest_ref[0, 0, k * tm + r]],
                                      row_sems.at[slot]).start(priority=k)
        return carry

    lax.fori_loop(0, tm // ROWS_PER_ISSUE, issue, 0)

    @pl.when(i == n - 1)
    def _():
        for back in range(DISPATCH_SLOTS):
            drain_rows((i - back) % DISPATCH_SLOTS)


def _dispatch(x1, dest):
    t = x1.shape[0]
    tm = MOVE_ROWS
    return pl.pallas_call(
        _dispatch_kernel,
        grid=(t // tm,),
        in_specs=[
            pl.BlockSpec((1, 1, TOP_K * tm), lambda i: (i, 0, 0), memory_space=pltpu.SMEM),
            pl.BlockSpec((tm, D_MODEL), lambda i: (i, 0)),
        ],
        out_specs=pl.BlockSpec(memory_space=pl.ANY),
        out_shape=jax.ShapeDtypeStruct((t * TOP_K,) + ROW_TILE, F32),
        scratch_shapes=[
            pltpu.VMEM((DISPATCH_SLOTS, tm) + ROW_TILE, F32),
            pltpu.SemaphoreType.DMA((DISPATCH_SLOTS,)),
        ],
        compiler_params=pltpu.CompilerParams(
            dimension_semantics=("arbitrary",), vmem_limit_bytes=VMEM_LIMIT),
        name="dispatch",
    )(_dest_blocks(dest, tm), x1)


def _expert_kernel(blk_ref, e_ref, lo_ref, hi_ref, n_steps_ref, xs_ref, wg_ref, wu_ref, wd_ref,
                   ys_ref, wg_bf, wu_bf, wd_bf):
    s = pl.program_id(0)
    prev = jnp.maximum(s - 1, 0)
    new_expert = (s == 0) | (e_ref[s] != e_ref[prev])
    new_block = (s == 0) | (blk_ref[s] != blk_ref[prev])

    @pl.when(new_expert)
    def _():
        wg_bf[...] = wg_ref[0, 0].astype(BF16)
        wu_bf[...] = wu_ref[0, 0].astype(BF16)
        wd_bf[...] = wd_ref[0, 0].astype(BF16)

    @pl.when(s < n_steps_ref[0])
    def _():
        xb = _from_row_tiles(xs_ref[...]).astype(BF16)
        gate = jnp.dot(xb, wg_bf[...], preferred_element_type=F32)
        up = jnp.dot(xb, wu_bf[...], preferred_element_type=F32)
        hid = (gate * jax.nn.sigmoid(gate) * up).astype(BF16)
        y = jnp.dot(hid, wd_bf[...], preferred_element_type=F32)
        rows = lax.broadcasted_iota(I32, y.shape, 0)
        y = _to_row_tiles(jnp.where((rows >= lo_ref[s]) & (rows < hi_ref[s]), y, 0.0))

        @pl.when(new_block)
        def _():
            ys_ref[...] = y

        @pl.when(jnp.logical_not(new_block))
        def _():
            ys_ref[...] += y


def _expert_ffn(xs, steps, layer, wg, wu, wd):
    step_blk, step_e, step_lo, step_hi, n_steps = steps
    rb = EXPERT_ROWS
    row_map = lambda s, blk, e, lo, hi, n: (blk[s], 0, 0)
    w_map = lambda s, blk, e, lo, hi, n: (layer, e[s], 0, 0)
    grid_spec = pltpu.PrefetchScalarGridSpec(
        num_scalar_prefetch=5,
        grid=(step_blk.shape[0],),
        in_specs=[
            pl.BlockSpec((rb,) + ROW_TILE, row_map),
            pl.BlockSpec((1, 1, D_MODEL, D_EXPERT), w_map),
            pl.BlockSpec((1, 1, D_MODEL, D_EXPERT), w_map),
            pl.BlockSpec((1, 1, D_EXPERT, D_MODEL), w_map),
        ],
        out_specs=pl.BlockSpec((rb,) + ROW_TILE, row_map),
        scratch_shapes=[
            pltpu.VMEM((D_MODEL, D_EXPERT), BF16),
            pltpu.VMEM((D_MODEL, D_EXPERT), BF16),
            pltpu.VMEM((D_EXPERT, D_MODEL), BF16),
        ],
    )
    return pl.pallas_call(
        _expert_kernel,
        grid_spec=grid_spec,
        out_shape=jax.ShapeDtypeStruct(xs.shape, F32),
        compiler_params=pltpu.CompilerParams(
            dimension_semantics=("arbitrary",), vmem_limit_bytes=VMEM_LIMIT),
        name="expert_ffn",
    )(step_blk, step_e, step_lo, step_hi, n_steps, xs, wg, wu, wd)


N_COMBINE_INPUTS = 7


def _combined_rows(dest_ref, dest_next_ref, x1_ref, route_ref, g2_ref, b2_ref, ys_hbm, ybuf, sems):
    tm = x1_ref.shape[0]
    i = pl.program_id(0)
    n = pl.num_programs(0)
    slot = i % 2

    def gather(d_ref, to_slot, first_group, n_groups):
        def issue(g, carry):
            for j in range(ROWS_PER_ISSUE):
                r = g * ROWS_PER_ISSUE + j
                for k in range(TOP_K):
                    pltpu.make_async_copy(ys_hbm.at[d_ref[0, 0, k * tm + r]],
                                          ybuf.at[to_slot, k, r],
                                          sems.at[to_slot]).start(priority=k)
            return carry
        lax.fori_loop(first_group, first_group + n_groups, issue, 0)

    groups = tm // ROWS_PER_ISSUE
    part_groups = groups // GATHER_PARTS

    def prefetch(part):
        @pl.when(i + 1 < n)
        def _():
            gather(dest_next_ref, 1 - slot, part * part_groups, part_groups)

    @pl.when(i == 0)
    def _():
        gather(dest_ref, slot, 0, groups)

    for k in range(TOP_K):
        pltpu.make_async_copy(ys_hbm.at[pl.ds(0, tm)], ybuf.at[slot, k], sems.at[slot]).wait()

    route = route_ref[...]
    f = (route[:, R_GATE0:R_GATE0 + 1] * _from_row_tiles(ybuf[slot, 0])
         + route[:, R_GATE1:R_GATE1 + 1] * _from_row_tiles(ybuf[slot, 1]))
    return _layer_norm(ALPHA * x1_ref[...] + f, g2_ref[...], b2_ref[...]), prefetch


def _combine_operands(pending, tm):
    x1, ys, dest, route, g2, b2 = pending
    n = x1.shape[0] // tm
    row = lambda i: (i, 0)
    const2 = lambda i: (0, 0)
    dest_blocks = _dest_blocks(dest, tm)
    smem_block = lambda index_map: pl.BlockSpec((1, 1, TOP_K * tm), index_map,
                                                memory_space=pltpu.SMEM)
    in_specs = [
        smem_block(lambda i: (i, 0, 0)),
        smem_block(lambda i: (jnp.minimum(i + 1, n - 1), 0, 0)),
        pl.BlockSpec((tm, D_MODEL), row),
        pl.BlockSpec((tm, LANES), row),
        pl.BlockSpec((1, D_MODEL), const2),
        pl.BlockSpec((1, D_MODEL), const2),
        pl.BlockSpec(memory_space=pl.ANY),
    ]
    assert len(in_specs) == N_COMBINE_INPUTS
    scratch = [pltpu.VMEM((2, TOP_K, tm) + ROW_TILE, F32), pltpu.SemaphoreType.DMA((2,))]
    operands = (dest_blocks, dest_blocks, x1, route, g2.reshape(1, -1), b2.reshape(1, -1), ys)
    return in_specs, scratch, operands


def _tile_input(refs, fused):
    if not fused:
        return refs[0][...], refs[1:], lambda part: None
    x, prefetch = _combined_rows(*refs[:N_COMBINE_INPUTS], *refs[-2:])
    return x, refs[N_COMBINE_INPUTS:-2], prefetch


def _combine_kernel(*refs):
    x2, (x2_ref,), prefetch = _tile_input(refs, True)
    x2_ref[...] = x2
    for part in range(GATHER_PARTS):
        prefetch(part)


def _combine_ln(pending):
    t = pending[0].shape[0]
    tm = MOVE_ROWS
    in_specs, scratch, operands = _combine_operands(pending, tm)
    return pl.pallas_call(
        _combine_kernel,
        grid=(t // tm,),
        in_specs=in_specs,
        out_specs=pl.BlockSpec((tm, D_MODEL), lambda i: (i, 0)),
        out_shape=jax.ShapeDtypeStruct((t, D_MODEL), F32),
        scratch_shapes=scratch,
        compiler_params=pltpu.CompilerParams(
            dimension_semantics=("arbitrary",), vmem_limit_bytes=VMEM_LIMIT),
        name="combine_ln",
    )(*operands)


def _moe_layer(x1, route, route_t, counts, layer, wg, wu, wd, g2, b2):
    dest, steps = _plan(route_t, counts)
    xs = _dispatch(x1, dest)
    ys = _expert_ffn(xs, steps, layer, wg, wu, wd)
    return (x1, ys, dest, route, g2, b2)


def kernel(x, rel_bias_table, mix_w_in, gmlp_ln_g, gmlp_ln_b, gmlp_w_spatial, gmlp_b_spatial, conv_w, mix_w_out, attn_w_qkv, attn_b_qkv, attn_sinks, attn_w_o, attn_b_o, ln1_g, ln1_b, ln2_g, ln2_b, router_group, router_expert, expert_w_gate, expert_w_up, expert_w_down):
    bsz, s, d = x.shape
    assert (s, d) == (SEQ, D_MODEL)
    x = x.reshape(bsz * s, d)
    bias = _rel_bias(rel_bias_table)
    pending = None
    for l in range(DEPTH):
        i = l // 2
        w_router = (router_group[l], router_expert[l])
        if l % 2 == 0:
            x1, *routing = _mixer_layer(x, pending, mix_w_in[i], gmlp_ln_g[i], gmlp_ln_b[i],
                                        gmlp_w_spatial[i], gmlp_b_spatial[i], conv_w[i],
                                        mix_w_out[i], ln1_g[l], ln1_b[l], w_router)
        else:
            x1, *routing = _attn_layer(x, pending, bias, attn_w_qkv[i], attn_b_qkv[i],
                                       attn_sinks[i], attn_w_o[i], attn_b_o[i], ln1_g[l],
                                       ln1_b[l], w_router)
        pending = _moe_layer(x1, *routing, l, expert_w_gate, expert_w_up, expert_w_down,
                             ln2_g[l], ln2_b[l])
    return _combine_ln(pending).reshape(bsz, s, d)
```

```python
import math

import jax
import jax.numpy as jnp
from jax import lax
from jax.experimental import pallas as pl
from jax.experimental.pallas import tpu as pltpu

D_MODEL = 1024
SEQ = 16384
DEPTH = 4
CHUNK = 128
A_GROUPS = 4
A_CH = 128
A_WIDTH = A_GROUPS * A_CH
B_WIDTH = 512
CONV_W = 3
MIX_IN = 2 * A_WIDTH + 3 * B_WIDTH
N_HEADS = 16
N_KV = 2
HEAD_DIM = 64
GQA_GROUP = N_HEADS // N_KV
WINDOW = 128
QKV_DIM = (N_HEADS + 2 * N_KV) * HEAD_DIM
ATT_OUT = N_HEADS * HEAD_DIM
KV_WIDTH = N_KV * HEAD_DIM
N_BUCKETS = 32
MAX_DISTANCE = 128
N_GROUPS = 4
EXPERTS_PER_GROUP = 8
N_EXPERTS = N_GROUPS * EXPERTS_PER_GROUP
TOP_K = 2
D_EXPERT = 512
ALPHA = (2 * DEPTH) ** 0.25
LN_EPS = 1e-5

LANES = 128
SUBLANES = 8
ROW_TILE = (SUBLANES, LANES)
assert D_MODEL == SUBLANES * LANES
ROUTER_ROWS = SUBLANES + N_EXPERTS
MIX_ROWS = 1024
ATTN_ROWS = 1024
DISPATCH_ROWS = 1024
COMBINE_ROWS = 512
EXPERT_ROWS = 512
ROWS_PER_ISSUE = 16
DISPATCH_SLOTS = 3
VMEM_LIMIT = 56 * 1024 * 1024

R_E0, R_E1, R_RANK0, R_RANK1, R_GATE0, R_GATE1 = range(6)

F32 = jnp.float32
BF16 = jnp.bfloat16
I32 = jnp.int32


def _layer_norm(x, g, b):
    mu = jnp.mean(x, axis=-1, keepdims=True)
    xc = x - mu
    var = jnp.mean(xc * xc, axis=-1, keepdims=True)
    return xc * lax.rsqrt(var + LN_EPS) * g + b


def _gelu(x):
    return 0.5 * x * (1.0 + lax.erf(x * (2.0 ** -0.5)))


def _mixer_kernel(x_ref, win_ref, lng_ref, lnb_ref, wsp_ref, bsp_ref, cw_ref, wout_ref,
                  g1_ref, b1_ref, wr_ref, tri_ref, x1_ref, route_ref, route_t_ref, count_ref,
                  ztail_ref, base_ref):
    tm = x_ref.shape[0]
    i = pl.program_id(0)

    @pl.when(i % (SEQ // tm) == 0)
    def _():
        ztail_ref[...] = jnp.zeros_like(ztail_ref)

    x = x_ref[...]
    h = jnp.dot(x.astype(BF16), win_ref[...], preferred_element_type=F32)
    u = _gelu(h[:, :A_WIDTH])
    v = _gelu(h[:, A_WIDTH:2 * A_WIDTH])
    o = 2 * A_WIDTH
    g_b = h[:, o:o + B_WIDTH]
    g_c = h[:, o + B_WIDTH:o + 2 * B_WIDTH]
    hb = h[:, o + 2 * B_WIDTH:o + 3 * B_WIDTH]

    v = _layer_norm(v, lng_ref[...], lnb_ref[...]).astype(BF16)
    n_chunks = tm // CHUNK
    ri = lax.broadcasted_iota(I32, (CHUNK, CHUNK), 0)
    ci = lax.broadcasted_iota(I32, (CHUNK, CHUNK), 1)
    causal = ci <= ri
    sv_cols = [[None] * A_GROUPS for _ in range(n_chunks)]
    for g in range(A_GROUPS):
        ws = jnp.where(causal, wsp_ref[g], 0.0).astype(BF16)
        vg = jnp.concatenate(
            [v[c * CHUNK:(c + 1) * CHUNK, g * A_CH:(g + 1) * A_CH] for c in range(n_chunks)],
            axis=1)
        sg = jnp.dot(ws, vg, preferred_element_type=F32) + bsp_ref[:, g:g + 1]
        for c in range(n_chunks):
            sv_cols[c][g] = sg[:, c * A_CH:(c + 1) * A_CH]
    sv = jnp.concatenate([jnp.concatenate(row, axis=1) for row in sv_cols], axis=0)
    y_a = u * sv

    z = g_c * hb
    rows = lax.broadcasted_iota(I32, z.shape, 0)
    tail = ztail_ref[...]
    zm1 = jnp.where(rows == 0, tail[7:8, :], pltpu.roll(z, 1, 0))
    zm2 = jnp.where(rows == 0, tail[6:7, :],
                    jnp.where(rows == 1, tail[7:8, :], pltpu.roll(z, 2, 0)))
    conv = cw_ref[0:1, :] * zm2 + cw_ref[1:2, :] * zm1 + cw_ref[2:3, :] * z
    y_b = g_b * conv
    ztail_ref[...] = z[tm - 8:tm, :]

    y = jnp.concatenate([y_a, y_b], axis=1).astype(BF16)
    m = jnp.dot(y, wout_ref[...], preferred_element_type=F32)
    x1 = _layer_norm(ALPHA * x + m, g1_ref[...], b1_ref[...])
    x1_ref[...] = x1
    _route_tile(x1, wr_ref, tri_ref, route_ref, route_t_ref, count_ref, base_ref)


def _mixer_layer(x, w_in, ln_g, ln_b, w_sp, b_sp, conv_w, w_out, g1, b1, w_router):
    t = x.shape[0]
    tm = MIX_ROWS
    const2 = lambda i: (0, 0)
    r_in, r_out, r_shape, r_scratch = _route_specs(t, tm)
    return pl.pallas_call(
        _mixer_kernel,
        grid=(t // tm,),
        in_specs=[
            pl.BlockSpec((tm, D_MODEL), lambda i: (i, 0)),
            pl.BlockSpec((D_MODEL, MIX_IN), const2),
            pl.BlockSpec((1, A_WIDTH), const2),
            pl.BlockSpec((1, A_WIDTH), const2),
            pl.BlockSpec((A_GROUPS, CHUNK, CHUNK), lambda i: (0, 0, 0)),
            pl.BlockSpec((CHUNK, A_GROUPS), const2),
            pl.BlockSpec((CONV_W, B_WIDTH), const2),
            pl.BlockSpec((A_WIDTH + B_WIDTH, D_MODEL), const2),
            pl.BlockSpec((1, D_MODEL), const2),
            pl.BlockSpec((1, D_MODEL), const2),
        ] + r_in,
        out_specs=[pl.BlockSpec((tm, D_MODEL), lambda i: (i, 0))] + r_out,
        out_shape=[jax.ShapeDtypeStruct((t, D_MODEL), F32)] + r_shape,
        scratch_shapes=[pltpu.VMEM((8, B_WIDTH), F32), r_scratch],
        compiler_params=pltpu.CompilerParams(
            dimension_semantics=("arbitrary",), vmem_limit_bytes=VMEM_LIMIT),
        name="mixer_layer",
    )(x, w_in.astype(BF16), ln_g.reshape(1, -1), ln_b.reshape(1, -1), w_sp, b_sp.T,
      conv_w, w_out.astype(BF16), g1.reshape(1, -1), b1.reshape(1, -1),
      *_route_operands(*w_router, tm))


def _attn_kernel(sink_ref, x_ref, wqkv_ref, bqkv_ref, bias_ref, wo_ref, bo_ref,
                 g1_ref, b1_ref, wr_ref, tri_ref, x1_ref, route_ref, route_t_ref, count_ref,
                 kprev_ref, vprev_ref, base_ref):
    tm = x_ref.shape[0]
    i = pl.program_id(0)
    first = i % (SEQ // tm) == 0

    @pl.when(first)
    def _():
        kprev_ref[...] = jnp.zeros_like(kprev_ref)
        vprev_ref[...] = jnp.zeros_like(vprev_ref)

    x = x_ref[...]
    qkv = jnp.dot(x.astype(BF16), wqkv_ref[...], preferred_element_type=F32) + bqkv_ref[...]
    q = (qkv[:, :ATT_OUT] * (HEAD_DIM ** -0.5)).astype(BF16)

    lane = lax.broadcasted_iota(I32, (CHUNK, KV_WIDTH), 1)
    low = lane < HEAD_DIM

    def halves(t):
        swapped = pltpu.roll(t, HEAD_DIM, 1)
        zero = jnp.zeros_like(t)
        return [jnp.where(low, t, zero).astype(BF16), jnp.where(low, zero, swapped).astype(BF16),
                jnp.where(low, swapped, zero).astype(BF16), jnp.where(low, zero, t).astype(BF16)]

    def stacked(prev, cur, j):
        return jnp.concatenate([prev[2 * j], cur[2 * j], prev[2 * j + 1], cur[2 * j + 1]], axis=0)

    a = lax.broadcasted_iota(I32, (CHUNK, 2 * CHUNK), 0)
    c = lax.broadcasted_iota(I32, (CHUNK, 2 * CHUNK), 1)
    window = (c > a) & (c <= a + WINDOW)
    neg = jnp.finfo(F32).min
    n_pairs = N_HEADS // 2
    kv_of = lambda pair: (2 * pair) // GQA_GROUP

    k_prev = [kprev_ref[idx] for idx in range(2 * N_KV)]
    v_prev = [vprev_ref[idx] for idx in range(2 * N_KV)]
    o_blocks = []
    for blk in range(tm // CHUNK):
        rows = slice(blk * CHUNK, (blk + 1) * CHUNK)
        k_cur = halves(qkv[rows, ATT_OUT:ATT_OUT + KV_WIDTH])
        v_cur = halves(qkv[rows, ATT_OUT + KV_WIDTH:])
        k_rhs = [stacked(k_prev, k_cur, j) for j in range(N_KV)]
        v_rhs = [stacked(v_prev, v_cur, j) for j in range(N_KV)]
        k_prev, v_prev = k_cur, v_cur
        mask = window & (c >= jnp.where(first, CHUNK, 0)) if blk == 0 else window

        scores = [lax.dot_general(q[rows, pair * 2 * HEAD_DIM:(pair + 1) * 2 * HEAD_DIM],
                                  k_rhs[kv_of(pair)], (((1,), (1,)), ((), ())),
                                  preferred_element_type=F32) for pair in range(n_pairs)]
        probs, inv_denoms = [], []
        for pair in range(n_pairs):
            sc2 = scores[pair] + bias_ref[pair]
            sides = []
            for side in range(2):
                sc = jnp.where(mask, sc2[:, side * 2 * CHUNK:(side + 1) * 2 * CHUNK], neg)
                sink = sink_ref[2 * pair + side]
                m = jnp.maximum(jnp.max(sc, axis=-1, keepdims=True), sink)
                p = jnp.exp(sc - m)
                denom = jnp.sum(p, axis=-1, keepdims=True) + jnp.exp(sink - m)
                sides.append(p.astype(BF16))
                inv_denoms.append(jnp.broadcast_to(1.0 / denom, (CHUNK, HEAD_DIM)))
            probs.append(jnp.concatenate(sides, axis=1))
        outs = [jnp.dot(probs[pair], v_rhs[kv_of(pair)], preferred_element_type=F32)
                for pair in range(n_pairs)]
        o = jnp.concatenate(outs, axis=1) * jnp.concatenate(inv_denoms, axis=1)
        o_blocks.append(o.astype(BF16))
    for idx in range(2 * N_KV):
        kprev_ref[idx] = k_prev[idx]
        vprev_ref[idx] = v_prev[idx]
    o = jnp.concatenate(o_blocks, axis=0)
    m_out = jnp.dot(o, wo_ref[...], preferred_element_type=F32) + bo_ref[...]
    x1 = _layer_norm(ALPHA * x + m_out, g1_ref[...], b1_ref[...])
    x1_ref[...] = x1
    _route_tile(x1, wr_ref, tri_ref, route_ref, route_t_ref, count_ref, base_ref)


def _t5_bucket(rel):
    n = jnp.maximum(rel, 0)
    max_exact = N_BUCKETS // 2
    nf = jnp.maximum(n, 1).astype(F32)
    large = max_exact + (jnp.log(nf / max_exact) / math.log(MAX_DISTANCE / max_exact)
                         * (N_BUCKETS - max_exact)).astype(I32)
    large = jnp.minimum(large, N_BUCKETS - 1)
    return jnp.where(n < max_exact, n, large)


def _rel_bias(rel_table):
    a = jnp.arange(CHUNK)[:, None]
    c = jnp.arange(2 * CHUNK)[None, :]
    onehot = jax.nn.one_hot(_t5_bucket(a + CHUNK - c), N_BUCKETS, dtype=F32)
    bias = jnp.einsum('acb,bh->hac', onehot, rel_table.astype(F32),
                      precision=lax.Precision.HIGHEST)
    bias = bias.reshape(N_HEADS // 2, 2, CHUNK, 2 * CHUNK).transpose(0, 2, 1, 3)
    return bias.reshape(N_HEADS // 2, CHUNK, 4 * CHUNK)


def _attn_layer(x, bias, w_qkv, b_qkv, sinks, w_o, b_o, g1, b1, w_router):
    t = x.shape[0]
    tm = ATTN_ROWS
    const2 = lambda i: (0, 0)
    r_in, r_out, r_shape, r_scratch = _route_specs(t, tm)
    return pl.pallas_call(
        _attn_kernel,
        grid=(t // tm,),
        in_specs=[
            pl.BlockSpec(memory_space=pltpu.SMEM),
            pl.BlockSpec((tm, D_MODEL), lambda i: (i, 0)),
            pl.BlockSpec((D_MODEL, QKV_DIM), const2),
            pl.BlockSpec((1, QKV_DIM), const2),
            pl.BlockSpec((N_HEADS // 2, CHUNK, 4 * CHUNK), lambda i: (0, 0, 0)),
            pl.BlockSpec((ATT_OUT, D_MODEL), const2),
            pl.BlockSpec((1, D_MODEL), const2),
            pl.BlockSpec((1, D_MODEL), const2),
            pl.BlockSpec((1, D_MODEL), const2),
        ] + r_in,
        out_specs=[pl.BlockSpec((tm, D_MODEL), lambda i: (i, 0))] + r_out,
        out_shape=[jax.ShapeDtypeStruct((t, D_MODEL), F32)] + r_shape,
        scratch_shapes=[pltpu.VMEM((2 * N_KV, CHUNK, KV_WIDTH), BF16),
                        pltpu.VMEM((2 * N_KV, CHUNK, KV_WIDTH), BF16), r_scratch],
        compiler_params=pltpu.CompilerParams(
            dimension_semantics=("arbitrary",), vmem_limit_bytes=VMEM_LIMIT),
        name="attn_layer",
    )(sinks, x, w_qkv.astype(BF16), b_qkv.reshape(1, -1), bias, w_o.astype(BF16),
      b_o.reshape(1, -1), g1.reshape(1, -1), b1.reshape(1, -1), *_route_operands(*w_router, tm))


def _route_tile(x1, wr_ref, tri_ref, route_ref, route_t_ref, count_ref, base_ref):
    @pl.when(pl.program_id(0) == 0)
    def _():
        base_ref[...] = jnp.zeros_like(base_ref)

    tm = x1.shape[0]
    lt = lax.dot_general(wr_ref[...], x1.astype(BF16), (((1,), (1,)), ((), ())),
                         preferred_element_type=F32)
    sub = lax.broadcasted_iota(I32, (SUBLANES, tm), 0).astype(F32)
    ninf = -jnp.inf

    def first_argmax(vals):
        m = jnp.max(vals, axis=0, keepdims=True)
        idx = jnp.min(jnp.where(vals == m, sub, float(SUBLANES)), axis=0, keepdims=True)
        return m, idx

    is_g = sub < N_GROUPS
    g_rows = lt[0:SUBLANES]
    gmax, g_idx = first_argmax(jnp.where(is_g, g_rows, ninf))
    g_p = 1.0 / jnp.sum(jnp.where(is_g, jnp.exp(g_rows - gmax), 0.0), axis=0, keepdims=True)

    group_rows = lambda a, g: a[SUBLANES * (g + 1):SUBLANES * (g + 2)]
    el = group_rows(lt, 0)
    for g in range(1, N_GROUPS):
        el = jnp.where(g_idx == g, group_rows(lt, g), el)
    m1, j1 = first_argmax(el)
    m2, j2 = first_argmax(jnp.where(sub == j1, ninf, el))
    a2 = jnp.exp(m2 - m1)
    gate0 = g_p / (1.0 + a2)
    gate1 = g_p * a2 / (1.0 + a2)

    hit0 = [(g_idx == g) & (sub == j1) for g in range(N_GROUPS)]
    hit1 = [(g_idx == g) & (sub == j2) for g in range(N_GROUPS)]
    onehot = jnp.concatenate([jnp.where(h0 | h1, 1.0, 0.0) for h0, h1 in zip(hit0, hit1)],
                             axis=0)
    before = jnp.dot(onehot.astype(BF16), tri_ref[...], preferred_element_type=F32)
    before = before + base_ref[:, 0:1]

    def picked(hits):
        total = jnp.zeros((SUBLANES, tm), F32)
        for g in range(N_GROUPS):
            total = total + jnp.where(hits[g], before[SUBLANES * g:SUBLANES * (g + 1)], 0.0)
        return jnp.sum(total, axis=0, keepdims=True)

    rank0 = picked(hit0)
    rank1 = picked(hit1)
    base_ref[...] += jnp.sum(onehot, axis=1, keepdims=True)

    e_base = g_idx * EXPERTS_PER_GROUP
    rec_t = jnp.concatenate([e_base + j1, e_base + j2, rank0, rank1, gate0, gate1,
                             jnp.zeros((SUBLANES - 6, tm), F32)], axis=0)
    route_t_ref[...] = rec_t
    padded = jnp.concatenate([rec_t, jnp.zeros((LANES - SUBLANES, tm), F32)], axis=0)
    route_ref[...] = jnp.transpose(padded)
    count_ref[...] = base_ref[...]


def _route_specs(t, tm):
    const2 = lambda i: (0, 0)
    in_specs = [pl.BlockSpec((ROUTER_ROWS, D_MODEL), const2), pl.BlockSpec((tm, tm), const2)]
    out_specs = [
        pl.BlockSpec((tm, LANES), lambda i: (i, 0)),
        pl.BlockSpec((SUBLANES, tm), lambda i: (0, i)),
        pl.BlockSpec((N_EXPERTS, LANES), const2),
    ]
    out_shape = [
        jax.ShapeDtypeStruct((t, LANES), F32),
        jax.ShapeDtypeStruct((SUBLANES, t), F32),
        jax.ShapeDtypeStruct((N_EXPERTS, LANES), F32),
    ]
    return in_specs, out_specs, out_shape, pltpu.VMEM((N_EXPERTS, LANES), F32)


def _route_operands(w_group, w_expert, tm):
    pad = jnp.zeros((D_MODEL, SUBLANES - N_GROUPS), F32)
    w_rows = jnp.concatenate([w_group, pad, w_expert], axis=1).T
    return w_rows.astype(BF16), jnp.triu(jnp.ones((tm, tm), BF16), 1)


def _plan(route_t, count_lanes):
    t = route_t.shape[1]
    rb = EXPERT_ROWS
    experts = jnp.arange(N_EXPERTS, dtype=I32)
    counts = count_lanes[:, 0].astype(I32)
    end = jnp.cumsum(counts)
    start = end - counts
    e = route_t[R_E0:R_E1 + 1].astype(I32)
    rank = route_t[R_RANK0:R_RANK1 + 1].astype(I32)
    onehot = e[:, None, :] == experts[None, :, None]
    dest = jnp.sum(jnp.where(onehot, start[None, :, None], 0), axis=1) + rank

    first_blk = start // rb
    last_blk = jnp.maximum(end - 1, 0) // rb
    n_steps_e = jnp.where(counts > 0, last_blk - first_blk + 1, 0)
    step_end = jnp.cumsum(n_steps_e)
    step_start = step_end - n_steps_e
    n_steps = step_end[-1]
    max_steps = t * TOP_K // rb + N_EXPERTS - 1
    s = jnp.minimum(jnp.arange(max_steps, dtype=I32), n_steps - 1)
    step_e = jnp.minimum(jnp.sum((step_end[None, :] <= s[:, None]).astype(I32), axis=1),
                         N_EXPERTS - 1)
    pick = step_e[:, None] == experts[None, :]
    take = lambda v: jnp.sum(jnp.where(pick, v[None, :], 0), axis=1)
    step_blk = take(first_blk) + s - take(step_start)
    step_lo = jnp.clip(take(start) - step_blk * rb, 0, rb)
    step_hi = jnp.clip(take(end) - step_blk * rb, 0, rb)
    return dest, (step_blk, step_e, step_lo, step_hi, n_steps.reshape(1))


def _dest_blocks(dest, tm):
    t = dest.shape[1]
    return dest.reshape(TOP_K, t // tm, tm).transpose(1, 0, 2).reshape(t // tm, 1, TOP_K * tm)


def _to_row_tiles(x):
    return x.reshape(x.shape[0], ROW_TILE[0], ROW_TILE[1])


def _from_row_tiles(x):
    return x.reshape(x.shape[0], D_MODEL)


def _dispatch_kernel(dest_ref, x1_ref, xs_hbm, xbuf, row_sems):
    tm = x1_ref.shape[0]
    i = pl.program_id(0)
    n = pl.num_programs(0)
    slot = i % DISPATCH_SLOTS

    def drain_rows(s):
        for k in range(TOP_K):
            pltpu.make_async_copy(xbuf.at[s], xs_hbm.at[pl.ds(0, tm)], row_sems.at[s]).wait()

    @pl.when(i >= DISPATCH_SLOTS)
    def _():
        drain_rows(slot)

    xbuf[slot] = _to_row_tiles(x1_ref[...])

    def issue(g, carry):
        for j in range(ROWS_PER_ISSUE):
            r = g * ROWS_PER_ISSUE + j
            for k in range(TOP_K):
                pltpu.make_async_copy(xbuf.at[slot, r], xs_hbm.at[dest_ref[0, 0, k * tm + r]],
                                      row_sems.at[slot]).start(priority=k)
        return carry

    lax.fori_loop(0, tm // ROWS_PER_ISSUE, issue, 0)

    @pl.when(i == n - 1)
    def _():
        for back in range(DISPATCH_SLOTS):
            drain_rows((i - back) % DISPATCH_SLOTS)


def _dispatch(x1, dest):
    t = x1.shape[0]
    tm = DISPATCH_ROWS
    return pl.pallas_call(
        _dispatch_kernel,
        grid=(t // tm,),
        in_specs=[
            pl.BlockSpec((1, 1, TOP_K * tm), lambda i: (i, 0, 0), memory_space=pltpu.SMEM),
            pl.BlockSpec((tm, D_MODEL), lambda i: (i, 0)),
        ],
        out_specs=pl.BlockSpec(memory_space=pl.ANY),
        out_shape=jax.ShapeDtypeStruct((t * TOP_K,) + ROW_TILE, F32),
        scratch_shapes=[
            pltpu.VMEM((DISPATCH_SLOTS, tm) + ROW_TILE, F32),
            pltpu.SemaphoreType.DMA((DISPATCH_SLOTS,)),
        ],
        compiler_params=pltpu.CompilerParams(
            dimension_semantics=("arbitrary",), vmem_limit_bytes=VMEM_LIMIT),
        name="dispatch",
    )(_dest_blocks(dest, tm), x1)


def _expert_kernel(blk_ref, e_ref, lo_ref, hi_ref, n_steps_ref, xs_ref, wg_ref, wu_ref, wd_ref,
                   ys_ref, wg_bf, wu_bf, wd_bf):
    s = pl.program_id(0)
    prev = jnp.maximum(s - 1, 0)
    new_expert = (s == 0) | (e_ref[s] != e_ref[prev])
    new_block = (s == 0) | (blk_ref[s] != blk_ref[prev])

    @pl.when(new_expert)
    def _():
        wg_bf[...] = wg_ref[0, 0].astype(BF16)
        wu_bf[...] = wu_ref[0, 0].astype(BF16)
        wd_bf[...] = wd_ref[0, 0].astype(BF16)

    @pl.when(s < n_steps_ref[0])
    def _():
        xb = _from_row_tiles(xs_ref[...]).astype(BF16)
        gate = jnp.dot(xb, wg_bf[...], preferred_element_type=F32)
        up = jnp.dot(xb, wu_bf[...], preferred_element_type=F32)
        hid = (gate * jax.nn.sigmoid(gate) * up).astype(BF16)
        y = jnp.dot(hid, wd_bf[...], preferred_element_type=F32)
        rows = lax.broadcasted_iota(I32, y.shape, 0)
        y = _to_row_tiles(jnp.where((rows >= lo_ref[s]) & (rows < hi_ref[s]), y, 0.0))

        @pl.when(new_block)
        def _():
            ys_ref[...] = y

        @pl.when(jnp.logical_not(new_block))
        def _():
            ys_ref[...] += y


def _expert_ffn(xs, steps, layer, wg, wu, wd):
    step_blk, step_e, step_lo, step_hi, n_steps = steps
    rb = EXPERT_ROWS
    row_map = lambda s, blk, e, lo, hi, n: (blk[s], 0, 0)
    w_map = lambda s, blk, e, lo, hi, n: (layer, e[s], 0, 0)
    grid_spec = pltpu.PrefetchScalarGridSpec(
        num_scalar_prefetch=5,
        grid=(step_blk.shape[0],),
        in_specs=[
            pl.BlockSpec((rb,) + ROW_TILE, row_map),
            pl.BlockSpec((1, 1, D_MODEL, D_EXPERT), w_map),
            pl.BlockSpec((1, 1, D_MODEL, D_EXPERT), w_map),
            pl.BlockSpec((1, 1, D_EXPERT, D_MODEL), w_map),
        ],
        out_specs=pl.BlockSpec((rb,) + ROW_TILE, row_map),
        scratch_shapes=[
            pltpu.VMEM((D_MODEL, D_EXPERT), BF16),
            pltpu.VMEM((D_MODEL, D_EXPERT), BF16),
            pltpu.VMEM((D_EXPERT, D_MODEL), BF16),
        ],
    )
    return pl.pallas_call(
        _expert_kernel,
        grid_spec=grid_spec,
        out_shape=jax.ShapeDtypeStruct(xs.shape, F32),
        compiler_params=pltpu.CompilerParams(
            dimension_semantics=("arbitrary",), vmem_limit_bytes=VMEM_LIMIT),
        name="expert_ffn",
    )(step_blk, step_e, step_lo, step_hi, n_steps, xs, wg, wu, wd)


def _combine_kernel(dest_ref, dest_next_ref, x1_ref, route_ref, g2_ref, b2_ref, ys_hbm,
                    x2_ref, ybuf, sems):
    tm = x1_ref.shape[0]
    i = pl.program_id(0)
    n = pl.num_programs(0)
    slot = i % 2

    def gather(d_ref, to_slot):
        def issue(g, carry):
            for j in range(ROWS_PER_ISSUE):
                r = g * ROWS_PER_ISSUE + j
                for k in range(TOP_K):
                    pltpu.make_async_copy(ys_hbm.at[d_ref[0, 0, k * tm + r]],
                                          ybuf.at[to_slot, k, r],
                                          sems.at[to_slot]).start(priority=k)
            return carry
        lax.fori_loop(0, tm // ROWS_PER_ISSUE, issue, 0)

    @pl.when(i == 0)
    def _():
        gather(dest_ref, slot)

    @pl.when(i + 1 < n)
    def _():
        gather(dest_next_ref, 1 - slot)

    for k in range(TOP_K):
        pltpu.make_async_copy(ys_hbm.at[pl.ds(0, tm)], ybuf.at[slot, k], sems.at[slot]).wait()

    route = route_ref[...]
    f = (route[:, R_GATE0:R_GATE0 + 1] * _from_row_tiles(ybuf[slot, 0])
         + route[:, R_GATE1:R_GATE1 + 1] * _from_row_tiles(ybuf[slot, 1]))
    x2_ref[...] = _layer_norm(ALPHA * x1_ref[...] + f, g2_ref[...], b2_ref[...])


def _combine_ln(x1, ys, dest, route, g2, b2):
    t = x1.shape[0]
    tm = COMBINE_ROWS
    n = t // tm
    row = lambda i: (i, 0)
    const2 = lambda i: (0, 0)
    dest_blocks = _dest_blocks(dest, tm)
    smem_block = lambda index_map: pl.BlockSpec((1, 1, TOP_K * tm), index_map,
                                                memory_space=pltpu.SMEM)
    return pl.pallas_call(
        _combine_kernel,
        grid=(n,),
        in_specs=[
            smem_block(lambda i: (i, 0, 0)),
            smem_block(lambda i: (jnp.minimum(i + 1, n - 1), 0, 0)),
            pl.BlockSpec((tm, D_MODEL), row),
            pl.BlockSpec((tm, LANES), row),
            pl.BlockSpec((1, D_MODEL), const2),
            pl.BlockSpec((1, D_MODEL), const2),
            pl.BlockSpec(memory_space=pl.ANY),
        ],
        out_specs=pl.BlockSpec((tm, D_MODEL), row),
        out_shape=jax.ShapeDtypeStruct((t, D_MODEL), F32),
        scratch_shapes=[
            pltpu.VMEM((2, TOP_K, tm) + ROW_TILE, F32),
            pltpu.SemaphoreType.DMA((2,)),
        ],
        compiler_params=pltpu.CompilerParams(
            dimension_semantics=("arbitrary",), vmem_limit_bytes=VMEM_LIMIT),
        name="combine_ln",
    )(dest_blocks, dest_blocks, x1, route, g2.reshape(1, -1), b2.reshape(1, -1), ys)


def _moe_layer(x1, route, route_t, counts, layer, wg, wu, wd, g2, b2):
    dest, steps = _plan(route_t, counts)
    xs = _dispatch(x1, dest)
    ys = _expert_ffn(xs, steps, layer, wg, wu, wd)
    return _combine_ln(x1, ys, dest, route, g2, b2)


def kernel(x, rel_bias_table, mix_w_in, gmlp_ln_g, gmlp_ln_b, gmlp_w_spatial, gmlp_b_spatial, conv_w, mix_w_out, attn_w_qkv, attn_b_qkv, attn_sinks, attn_w_o, attn_b_o, ln1_g, ln1_b, ln2_g, ln2_b, router_group, router_expert, expert_w_gate, expert_w_up, expert_w_down):
    bsz, s, d = x.shape
    assert (s, d) == (SEQ, D_MODEL)
    x = x.reshape(bsz * s, d)
    bias = _rel_bias(rel_bias_table)
    for l in range(DEPTH):
        i = l // 2
        w_router = (router_group[l], router_expert[l])
        if l % 2 == 0:
            x1, *routing = _mixer_layer(x, mix_w_in[i], gmlp_ln_g[i], gmlp_ln_b[i],
                                        gmlp_w_spatial[i], gmlp_b_spatial[i], conv_w[i],
                                        mix_w_out[i], ln1_g[l], ln1_b[l], w_router)
        else:
            x1, *routing = _attn_layer(x, bias, attn_w_qkv[i], attn_b_qkv[i], attn_sinks[i],
                                       attn_w_o[i], attn_b_o[i], ln1_g[l], ln1_b[l], w_router)
        x = _moe_layer(x1, *routing, l, expert_w_gate, expert_w_up, expert_w_down,
                       ln2_g[l], ln2_b[l])
    return x.reshape(bsz, s, d)
```

```python
import math

import jax
import jax.numpy as jnp
from jax import lax
from jax.experimental import pallas as pl
from jax.experimental.pallas import tpu as pltpu

D_MODEL = 1024
SEQ = 16384
DEPTH = 4
CHUNK = 128
A_GROUPS = 4
A_CH = 128
A_WIDTH = A_GROUPS * A_CH
B_WIDTH = 512
CONV_W = 3
MIX_IN = 2 * A_WIDTH + 3 * B_WIDTH
N_HEADS = 16
N_KV = 2
HEAD_DIM = 64
GQA_GROUP = N_HEADS // N_KV
WINDOW = 128
QKV_DIM = (N_HEADS + 2 * N_KV) * HEAD_DIM
ATT_OUT = N_HEADS * HEAD_DIM
KV_WIDTH = N_KV * HEAD_DIM
N_BUCKETS = 32
MAX_DISTANCE = 128
N_GROUPS = 4
EXPERTS_PER_GROUP = 8
N_EXPERTS = N_GROUPS * EXPERTS_PER_GROUP
TOP_K = 2
D_EXPERT = 512
ALPHA = (2 * DEPTH) ** 0.25
LN_EPS = 1e-5

LANES = 128
SUBLANES = 8
ROW_TILE = (SUBLANES, LANES)
assert D_MODEL == SUBLANES * LANES
ROUTER_ROWS = SUBLANES + N_EXPERTS
MIX_ROWS = 1024
ATTN_ROWS = 1024
DISPATCH_ROWS = 1024
COMBINE_ROWS = 512
EXPERT_ROWS = 512
ROWS_PER_ISSUE = 16
DISPATCH_SLOTS = 3
VMEM_LIMIT = 56 * 1024 * 1024

R_E0, R_E1, R_RANK0, R_RANK1, R_GATE0, R_GATE1 = range(6)

F32 = jnp.float32
BF16 = jnp.bfloat16
I32 = jnp.int32


def _layer_norm(x, g, b):
    mu = jnp.mean(x, axis=-1, keepdims=True)
    xc = x - mu
    var = jnp.mean(xc * xc, axis=-1, keepdims=True)
    return xc * lax.rsqrt(var + LN_EPS) * g + b


def _gelu(x):
    return 0.5 * x * (1.0 + lax.erf(x * (2.0 ** -0.5)))


def _mixer_kernel(x_ref, win_ref, lng_ref, lnb_ref, wsp_ref, bsp_ref, cw_ref, wout_ref,
                  g1_ref, b1_ref, wr_ref, tri_ref, x1_ref, route_ref, route_t_ref, count_ref,
                  ztail_ref, base_ref):
    tm = x_ref.shape[0]
    i = pl.program_id(0)

    @pl.when(i % (SEQ // tm) == 0)
    def _():
        ztail_ref[...] = jnp.zeros_like(ztail_ref)

    x = x_ref[...]
    h = jnp.dot(x.astype(BF16), win_ref[...], preferred_element_type=F32)
    u = _gelu(h[:, :A_WIDTH])
    v = _gelu(h[:, A_WIDTH:2 * A_WIDTH])
    o = 2 * A_WIDTH
    g_b = h[:, o:o + B_WIDTH]
    g_c = h[:, o + B_WIDTH:o + 2 * B_WIDTH]
    hb = h[:, o + 2 * B_WIDTH:o + 3 * B_WIDTH]

    v = _layer_norm(v, lng_ref[...], lnb_ref[...]).astype(BF16)
    n_chunks = tm // CHUNK
    ri = lax.broadcasted_iota(I32, (CHUNK, CHUNK), 0)
    ci = lax.broadcasted_iota(I32, (CHUNK, CHUNK), 1)
    causal = ci <= ri
    sv_cols = [[None] * A_GROUPS for _ in range(n_chunks)]
    for g in range(A_GROUPS):
        ws = jnp.where(causal, wsp_ref[g], 0.0).astype(BF16)
        vg = jnp.concatenate(
            [v[c * CHUNK:(c + 1) * CHUNK, g * A_CH:(g + 1) * A_CH] for c in range(n_chunks)],
            axis=1)
        sg = jnp.dot(ws, vg, preferred_element_type=F32) + bsp_ref[:, g:g + 1]
        for c in range(n_chunks):
            sv_cols[c][g] = sg[:, c * A_CH:(c + 1) * A_CH]
    sv = jnp.concatenate([jnp.concatenate(row, axis=1) for row in sv_cols], axis=0)
    y_a = u * sv

    z = g_c * hb
    rows = lax.broadcasted_iota(I32, z.shape, 0)
    tail = ztail_ref[...]
    zm1 = jnp.where(rows == 0, tail[7:8, :], pltpu.roll(z, 1, 0))
    zm2 = jnp.where(rows == 0, tail[6:7, :],
                    jnp.where(rows == 1, tail[7:8, :], pltpu.roll(z, 2, 0)))
    conv = cw_ref[0:1, :] * zm2 + cw_ref[1:2, :] * zm1 + cw_ref[2:3, :] * z
    y_b = g_b * conv
    ztail_ref[...] = z[tm - 8:tm, :]

    y = jnp.concatenate([y_a, y_b], axis=1).astype(BF16)
    m = jnp.dot(y, wout_ref[...], preferred_element_type=F32)
    x1 = _layer_norm(ALPHA * x + m, g1_ref[...], b1_ref[...])
    x1_ref[...] = x1
    _route_tile(x1, wr_ref, tri_ref, route_ref, route_t_ref, count_ref, base_ref)


def _mixer_layer(x, w_in, ln_g, ln_b, w_sp, b_sp, conv_w, w_out, g1, b1, w_router):
    t = x.shape[0]
    tm = MIX_ROWS
    const2 = lambda i: (0, 0)
    r_in, r_out, r_shape, r_scratch = _route_specs(t, tm)
    return pl.pallas_call(
        _mixer_kernel,
        grid=(t // tm,),
        in_specs=[
            pl.BlockSpec((tm, D_MODEL), lambda i: (i, 0)),
            pl.BlockSpec((D_MODEL, MIX_IN), const2),
            pl.BlockSpec((1, A_WIDTH), const2),
            pl.BlockSpec((1, A_WIDTH), const2),
            pl.BlockSpec((A_GROUPS, CHUNK, CHUNK), lambda i: (0, 0, 0)),
            pl.BlockSpec((CHUNK, A_GROUPS), const2),
            pl.BlockSpec((CONV_W, B_WIDTH), const2),
            pl.BlockSpec((A_WIDTH + B_WIDTH, D_MODEL), const2),
            pl.BlockSpec((1, D_MODEL), const2),
            pl.BlockSpec((1, D_MODEL), const2),
        ] + r_in,
        out_specs=[pl.BlockSpec((tm, D_MODEL), lambda i: (i, 0))] + r_out,
        out_shape=[jax.ShapeDtypeStruct((t, D_MODEL), F32)] + r_shape,
        scratch_shapes=[pltpu.VMEM((8, B_WIDTH), F32), r_scratch],
        compiler_params=pltpu.CompilerParams(
            dimension_semantics=("arbitrary",), vmem_limit_bytes=VMEM_LIMIT),
        name="mixer_layer",
    )(x, w_in.astype(BF16), ln_g.reshape(1, -1), ln_b.reshape(1, -1), w_sp, b_sp.T,
      conv_w, w_out.astype(BF16), g1.reshape(1, -1), b1.reshape(1, -1),
      *_route_operands(*w_router, tm))


def _attn_kernel(sink_ref, x_ref, wqkv_ref, bqkv_ref, bias_ref, wo_ref, bo_ref,
                 g1_ref, b1_ref, wr_ref, tri_ref, x1_ref, route_ref, route_t_ref, count_ref,
                 kprev_ref, vprev_ref, base_ref):
    tm = x_ref.shape[0]
    i = pl.program_id(0)
    first = i % (SEQ // tm) == 0

    @pl.when(first)
    def _():
        kprev_ref[...] = jnp.zeros_like(kprev_ref)
        vprev_ref[...] = jnp.zeros_like(vprev_ref)

    x = x_ref[...]
    qkv = jnp.dot(x.astype(BF16), wqkv_ref[...], preferred_element_type=F32) + bqkv_ref[...]
    q = (qkv[:, :ATT_OUT] * (HEAD_DIM ** -0.5)).astype(BF16)

    lane = lax.broadcasted_iota(I32, (CHUNK, KV_WIDTH), 1)
    low = lane < HEAD_DIM

    def halves(t):
        swapped = pltpu.roll(t, HEAD_DIM, 1)
        zero = jnp.zeros_like(t)
        return [jnp.where(low, t, zero).astype(BF16), jnp.where(low, zero, swapped).astype(BF16),
                jnp.where(low, swapped, zero).astype(BF16), jnp.where(low, zero, t).astype(BF16)]

    def stacked(prev, cur, j):
        return jnp.concatenate([prev[2 * j], cur[2 * j], prev[2 * j + 1], cur[2 * j + 1]], axis=0)

    a = lax.broadcasted_iota(I32, (CHUNK, 2 * CHUNK), 0)
    c = lax.broadcasted_iota(I32, (CHUNK, 2 * CHUNK), 1)
    window = (c > a) & (c <= a + WINDOW)
    neg = jnp.finfo(F32).min
    n_pairs = N_HEADS // 2
    kv_of = lambda pair: (2 * pair) // GQA_GROUP

    k_prev = [kprev_ref[idx] for idx in range(2 * N_KV)]
    v_prev = [vprev_ref[idx] for idx in range(2 * N_KV)]
    o_blocks = []
    for blk in range(tm // CHUNK):
        rows = slice(blk * CHUNK, (blk + 1) * CHUNK)
        k_cur = halves(qkv[rows, ATT_OUT:ATT_OUT + KV_WIDTH])
        v_cur = halves(qkv[rows, ATT_OUT + KV_WIDTH:])
        k_rhs = [stacked(k_prev, k_cur, j) for j in range(N_KV)]
        v_rhs = [stacked(v_prev, v_cur, j) for j in range(N_KV)]
        k_prev, v_prev = k_cur, v_cur
        mask = window & (c >= jnp.where(first, CHUNK, 0)) if blk == 0 else window

        scores = [lax.dot_general(q[rows, pair * 2 * HEAD_DIM:(pair + 1) * 2 * HEAD_DIM],
                                  k_rhs[kv_of(pair)], (((1,), (1,)), ((), ())),
                                  preferred_element_type=F32) for pair in range(n_pairs)]
        probs, inv_denoms = [], []
        for pair in range(n_pairs):
            sc2 = scores[pair] + bias_ref[pair]
            sides = []
            for side in range(2):
                sc = jnp.where(mask, sc2[:, side * 2 * CHUNK:(side + 1) * 2 * CHUNK], neg)
                sink = sink_ref[2 * pair + side]
                m = jnp.maximum(jnp.max(sc, axis=-1, keepdims=True), sink)
                p = jnp.exp(sc - m)
                denom = jnp.sum(p, axis=-1, keepdims=True) + jnp.exp(sink - m)
                sides.append(p.astype(BF16))
                inv_denoms.append(jnp.broadcast_to(1.0 / denom, (CHUNK, HEAD_DIM)))
            probs.append(jnp.concatenate(sides, axis=1))
        outs = [jnp.dot(probs[pair], v_rhs[kv_of(pair)], preferred_element_type=F32)
                for pair in range(n_pairs)]
        o = jnp.concatenate(outs, axis=1) * jnp.concatenate(inv_denoms, axis=1)
        o_blocks.append(o.astype(BF16))
    for idx in range(2 * N_KV):
        kprev_ref[idx] = k_prev[idx]
        vprev_ref[idx] = v_prev[idx]
    o = jnp.concatenate(o_blocks, axis=0)
    m_out = jnp.dot(o, wo_ref[...], preferred_element_type=F32) + bo_ref[...]
    x1 = _layer_norm(ALPHA * x + m_out, g1_ref[...], b1_ref[...])
    x1_ref[...] = x1
    _route_tile(x1, wr_ref, tri_ref, route_ref, route_t_ref, count_ref, base_ref)


def _t5_bucket(rel):
    n = jnp.maximum(rel, 0)
    max_exact = N_BUCKETS // 2
    nf = jnp.maximum(n, 1).astype(F32)
    large = max_exact + (jnp.log(nf / max_exact) / math.log(MAX_DISTANCE / max_exact)
                         * (N_BUCKETS - max_exact)).astype(I32)
    large = jnp.minimum(large, N_BUCKETS - 1)
    return jnp.where(n < max_exact, n, large)


def _rel_bias(rel_table):
    a = jnp.arange(CHUNK)[:, None]
    c = jnp.arange(2 * CHUNK)[None, :]
    onehot = jax.nn.one_hot(_t5_bucket(a + CHUNK - c), N_BUCKETS, dtype=F32)
    bias = jnp.einsum('acb,bh->hac', onehot, rel_table.astype(F32),
                      precision=lax.Precision.HIGHEST)
    bias = bias.reshape(N_HEADS // 2, 2, CHUNK, 2 * CHUNK).transpose(0, 2, 1, 3)
    return bias.reshape(N_HEADS // 2, CHUNK, 4 * CHUNK)


def _attn_layer(x, bias, w_qkv, b_qkv, sinks, w_o, b_o, g1, b1, w_router):
    t = x.shape[0]
    tm = ATTN_ROWS
    const2 = lambda i: (0, 0)
    r_in, r_out, r_shape, r_scratch = _route_specs(t, tm)
    return pl.pallas_call(
        _attn_kernel,
        grid=(t // tm,),
        in_specs=[
            pl.BlockSpec(memory_space=pltpu.SMEM),
            pl.BlockSpec((tm, D_MODEL), lambda i: (i, 0)),
            pl.BlockSpec((D_MODEL, QKV_DIM), const2),
            pl.BlockSpec((1, QKV_DIM), const2),
            pl.BlockSpec((N_HEADS // 2, CHUNK, 4 * CHUNK), lambda i: (0, 0, 0)),
            pl.BlockSpec((ATT_OUT, D_MODEL), const2),
            pl.BlockSpec((1, D_MODEL), const2),
            pl.BlockSpec((1, D_MODEL), const2),
            pl.BlockSpec((1, D_MODEL), const2),
        ] + r_in,
        out_specs=[pl.BlockSpec((tm, D_MODEL), lambda i: (i, 0))] + r_out,
        out_shape=[jax.ShapeDtypeStruct((t, D_MODEL), F32)] + r_shape,
        scratch_shapes=[pltpu.VMEM((2 * N_KV, CHUNK, KV_WIDTH), BF16),
                        pltpu.VMEM((2 * N_KV, CHUNK, KV_WIDTH), BF16), r_scratch],
        compiler_params=pltpu.CompilerParams(
            dimension_semantics=("arbitrary",), vmem_limit_bytes=VMEM_LIMIT),
        name="attn_layer",
    )(sinks, x, w_qkv.astype(BF16), b_qkv.reshape(1, -1), bias, w_o.astype(BF16),
      b_o.reshape(1, -1), g1.reshape(1, -1), b1.reshape(1, -1), *_route_operands(*w_router, tm))


def _route_tile(x1, wr_ref, tri_ref, route_ref, route_t_ref, count_ref, base_ref):
    @pl.when(pl.program_id(0) == 0)
    def _():
        base_ref[...] = jnp.zeros_like(base_ref)

    tm = x1.shape[0]
    lt = lax.dot_general(wr_ref[...], x1.astype(BF16), (((1,), (1,)), ((), ())),
                         preferred_element_type=F32)
    sub = lax.broadcasted_iota(I32, (SUBLANES, tm), 0).astype(F32)
    ninf = -jnp.inf

    def first_argmax(vals):
        m = jnp.max(vals, axis=0, keepdims=True)
        idx = jnp.min(jnp.where(vals == m, sub, float(SUBLANES)), axis=0, keepdims=True)
        return m, idx

    is_g = sub < N_GROUPS
    g_rows = lt[0:SUBLANES]
    gmax, g_idx = first_argmax(jnp.where(is_g, g_rows, ninf))
    g_p = 1.0 / jnp.sum(jnp.where(is_g, jnp.exp(g_rows - gmax), 0.0), axis=0, keepdims=True)

    group_rows = lambda a, g: a[SUBLANES * (g + 1):SUBLANES * (g + 2)]
    el = group_rows(lt, 0)
    for g in range(1, N_GROUPS):
        el = jnp.where(g_idx == g, group_rows(lt, g), el)
    m1, j1 = first_argmax(el)
    m2, j2 = first_argmax(jnp.where(sub == j1, ninf, el))
    a2 = jnp.exp(m2 - m1)
    gate0 = g_p / (1.0 + a2)
    gate1 = g_p * a2 / (1.0 + a2)

    hit0 = [(g_idx == g) & (sub == j1) for g in range(N_GROUPS)]
    hit1 = [(g_idx == g) & (sub == j2) for g in range(N_GROUPS)]
    onehot = jnp.concatenate([jnp.where(h0 | h1, 1.0, 0.0) for h0, h1 in zip(hit0, hit1)],
                             axis=0)
    before = jnp.dot(onehot.astype(BF16), tri_ref[...], preferred_element_type=F32)
    before = before + base_ref[:, 0:1]

    def picked(hits):
        total = jnp.zeros((SUBLANES, tm), F32)
        for g in range(N_GROUPS):
            total = total + jnp.where(hits[g], before[SUBLANES * g:SUBLANES * (g + 1)], 0.0)
        return jnp.sum(total, axis=0, keepdims=True)

    rank0 = picked(hit0)
    rank1 = picked(hit1)
    base_ref[...] += jnp.sum(onehot, axis=1, keepdims=True)

    e_base = g_idx * EXPERTS_PER_GROUP
    rec_t = jnp.concatenate([e_base + j1, e_base + j2, rank0, rank1, gate0, gate1,
                             jnp.zeros((SUBLANES - 6, tm), F32)], axis=0)
    route_t_ref[...] = rec_t
    padded = jnp.concatenate([rec_t, jnp.zeros((LANES - SUBLANES, tm), F32)], axis=0)
    route_ref[...] = jnp.transpose(padded)
    count_ref[...] = base_ref[...]


def _route_specs(t, tm):
    const2 = lambda i: (0, 0)
    in_specs = [pl.BlockSpec((ROUTER_ROWS, D_MODEL), const2), pl.BlockSpec((tm, tm), const2)]
    out_specs = [
        pl.BlockSpec((tm, LANES), lambda i: (i, 0)),
        pl.BlockSpec((SUBLANES, tm), lambda i: (0, i)),
        pl.BlockSpec((N_EXPERTS, LANES), const2),
    ]
    out_shape = [
        jax.ShapeDtypeStruct((t, LANES), F32),
        jax.ShapeDtypeStruct((SUBLANES, t), F32),
        jax.ShapeDtypeStruct((N_EXPERTS, LANES), F32),
    ]
    return in_specs, out_specs, out_shape, pltpu.VMEM((N_EXPERTS, LANES), F32)


def _route_operands(w_group, w_expert, tm):
    pad = jnp.zeros((D_MODEL, SUBLANES - N_GROUPS), F32)
    w_rows = jnp.concatenate([w_group, pad, w_expert], axis=1).T
    return w_rows.astype(BF16), jnp.triu(jnp.ones((tm, tm), BF16), 1)


def _plan(route_t, count_lanes):
    t = route_t.shape[1]
    rb = EXPERT_ROWS
    experts = jnp.arange(N_EXPERTS, dtype=I32)
    counts = count_lanes[:, 0].astype(I32)
    end = jnp.cumsum(counts)
    start = end - counts
    e = route_t[R_E0:R_E1 + 1].astype(I32)
    rank = route_t[R_RANK0:R_RANK1 + 1].astype(I32)
    onehot = e[:, None, :] == experts[None, :, None]
    dest = jnp.sum(jnp.where(onehot, start[None, :, None], 0), axis=1) + rank

    first_blk = start // rb
    last_blk = jnp.maximum(end - 1, 0) // rb
    n_steps_e = jnp.where(counts > 0, last_blk - first_blk + 1, 0)
    step_end = jnp.cumsum(n_steps_e)
    step_start = step_end - n_steps_e
    n_steps = step_end[-1]
    max_steps = t * TOP_K // rb + N_EXPERTS - 1
    s = jnp.minimum(jnp.arange(max_steps, dtype=I32), n_steps - 1)
    step_e = jnp.minimum(jnp.sum((step_end[None, :] <= s[:, None]).astype(I32), axis=1),
                         N_EXPERTS - 1)
    pick = step_e[:, None] == experts[None, :]
    take = lambda v: jnp.sum(jnp.where(pick, v[None, :], 0), axis=1)
    step_blk = take(first_blk) + s - take(step_start)
    step_lo = jnp.clip(take(start) - step_blk * rb, 0, rb)
    step_hi = jnp.clip(take(end) - step_blk * rb, 0, rb)
    return dest, (step_blk, step_e, step_lo, step_hi, n_steps.reshape(1))


def _dest_blocks(dest, tm):
    t = dest.shape[1]
    return dest.reshape(TOP_K, t // tm, tm).transpose(1, 0, 2).reshape(t // tm, 1, TOP_K * tm)


def _to_row_tiles(x):
    return x.reshape(x.shape[0], ROW_TILE[0], ROW_TILE[1])


def _from_row_tiles(x):
    return x.reshape(x.shape[0], D_MODEL)


def _dispatch_kernel(dest_ref, x1_ref, xs_hbm, xbuf, row_sems):
    tm = x1_ref.shape[0]
    i = pl.program_id(0)
    n = pl.num_programs(0)
    slot = i % DISPATCH_SLOTS

    def drain_rows(s):
        for k in range(TOP_K):
            pltpu.make_async_copy(xbuf.at[s], xs_hbm.at[pl.ds(0, tm)], row_sems.at[s]).wait()

    @pl.when(i >= DISPATCH_SLOTS)
    def _():
        drain_rows(slot)

    xbuf[slot] = _to_row_tiles(x1_ref[...])

    def issue(g, carry):
        for j in range(ROWS_PER_ISSUE):
            r = g * ROWS_PER_ISSUE + j
            for k in range(TOP_K):
                pltpu.make_async_copy(xbuf.at[slot, r], xs_hbm.at[dest_ref[0, 0, k * tm + r]],
                                      row_sems.at[slot]).start(priority=k)
        return carry

    lax.fori_loop(0, tm // ROWS_PER_ISSUE, issue, 0)

    @pl.when(i == n - 1)
    def _():
        for back in range(DISPATCH_SLOTS):
            drain_rows((i - back) % DISPATCH_SLOTS)


def _dispatch(x1, dest):
    t = x1.shape[0]
    tm = DISPATCH_ROWS
    return pl.pallas_call(
        _dispatch_kernel,
        grid=(t // tm,),
        in_specs=[
            pl.BlockSpec((1, 1, TOP_K * tm), lambda i: (i, 0, 0), memory_space=pltpu.SMEM),
            pl.BlockSpec((tm, D_MODEL), lambda i: (i, 0)),
        ],
        out_specs=pl.BlockSpec(memory_space=pl.ANY),
        out_shape=jax.ShapeDtypeStruct((t * TOP_K,) + ROW_TILE, F32),
        scratch_shapes=[
            pltpu.VMEM((DISPATCH_SLOTS, tm) + ROW_TILE, F32),
            pltpu.SemaphoreType.DMA((DISPATCH_SLOTS,)),
        ],
        compiler_params=pltpu.CompilerParams(
            dimension_semantics=("arbitrary",), vmem_limit_bytes=VMEM_LIMIT),
        name="dispatch",
    )(_dest_blocks(dest, tm), x1)


def _expert_kernel(blk_ref, e_ref, lo_ref, hi_ref, n_steps_ref, xs_ref, wg_ref, wu_ref, wd_ref,
                   ys_ref, wg_bf, wu_bf, wd_bf):
    s = pl.program_id(0)
    prev = jnp.maximum(s - 1, 0)
    new_expert = (s == 0) | (e_ref[s] != e_ref[prev])
    new_block = (s == 0) | (blk_ref[s] != blk_ref[prev])

    @pl.when(new_expert)
    def _():
        wg_bf[...] = wg_ref[0, 0].astype(BF16)
        wu_bf[...] = wu_ref[0, 0].astype(BF16)
        wd_bf[...] = wd_ref[0, 0].astype(BF16)

    rb = xs_ref.shape[0]
    half = rb // 2
    lo, hi = lo_ref[s], hi_ref[s]
    active = s < n_steps_ref[0]
    both_halves = (lo < half) & (hi > half)

    def masked_ffn(x_tiles, row0):
        xb = _from_row_tiles(x_tiles).astype(BF16)
        gate = jnp.dot(xb, wg_bf[...], preferred_element_type=F32)
        up = jnp.dot(xb, wu_bf[...], preferred_element_type=F32)
        hid = (gate * jax.nn.sigmoid(gate) * up).astype(BF16)
        y = jnp.dot(hid, wd_bf[...], preferred_element_type=F32)
        rows = lax.broadcasted_iota(I32, y.shape, 0) + row0
        return _to_row_tiles(jnp.where((rows >= lo) & (rows < hi), y, 0.0))

    @pl.when(active & both_halves)
    def _():
        y = masked_ffn(xs_ref[...], 0)

        @pl.when(new_block)
        def _():
            ys_ref[...] = y

        @pl.when(jnp.logical_not(new_block))
        def _():
            ys_ref[...] += y

    @pl.when(active & jnp.logical_not(both_halves))
    def _():
        off = pl.multiple_of(jnp.where(lo < half, 0, half), half)
        y = masked_ffn(xs_ref[pl.ds(off, half)], off)

        @pl.when(new_block)
        def _():
            ys_ref[pl.ds(off, half)] = y
            ys_ref[pl.ds(half - off, half)] = jnp.zeros_like(y)

        @pl.when(jnp.logical_not(new_block))
        def _():
            ys_ref[pl.ds(off, half)] += y


def _expert_ffn(xs, steps, layer, wg, wu, wd):
    step_blk, step_e, step_lo, step_hi, n_steps = steps
    rb = EXPERT_ROWS
    row_map = lambda s, blk, e, lo, hi, n: (blk[s], 0, 0)
    w_map = lambda s, blk, e, lo, hi, n: (layer, e[s], 0, 0)
    grid_spec = pltpu.PrefetchScalarGridSpec(
        num_scalar_prefetch=5,
        grid=(step_blk.shape[0],),
        in_specs=[
            pl.BlockSpec((rb,) + ROW_TILE, row_map),
            pl.BlockSpec((1, 1, D_MODEL, D_EXPERT), w_map),
            pl.BlockSpec((1, 1, D_MODEL, D_EXPERT), w_map),
            pl.BlockSpec((1, 1, D_EXPERT, D_MODEL), w_map),
        ],
        out_specs=pl.BlockSpec((rb,) + ROW_TILE, row_map),
        scratch_shapes=[
            pltpu.VMEM((D_MODEL, D_EXPERT), BF16),
            pltpu.VMEM((D_MODEL, D_EXPERT), BF16),
            pltpu.VMEM((D_EXPERT, D_MODEL), BF16),
        ],
    )
    return pl.pallas_call(
        _expert_kernel,
        grid_spec=grid_spec,
        out_shape=jax.ShapeDtypeStruct(xs.shape, F32),
        compiler_params=pltpu.CompilerParams(
            dimension_semantics=("arbitrary",), vmem_limit_bytes=VMEM_LIMIT),
        name="expert_ffn",
    )(step_blk, step_e, step_lo, step_hi, n_steps, xs, wg, wu, wd)


def _combine_kernel(dest_ref, dest_next_ref, x1_ref, route_ref, g2_ref, b2_ref, ys_hbm,
                    x2_ref, ybuf, sems):
    tm = x1_ref.shape[0]
    i = pl.program_id(0)
    n = pl.num_programs(0)
    slot = i % 2

    def gather(d_ref, to_slot):
        def issue(g, carry):
            for j in range(ROWS_PER_ISSUE):
                r = g * ROWS_PER_ISSUE + j
                for k in range(TOP_K):
                    pltpu.make_async_copy(ys_hbm.at[d_ref[0, 0, k * tm + r]],
                                          ybuf.at[to_slot, k, r],
                                          sems.at[to_slot]).start(priority=k)
            return carry
        lax.fori_loop(0, tm // ROWS_PER_ISSUE, issue, 0)

    @pl.when(i == 0)
    def _():
        gather(dest_ref, slot)

    @pl.when(i + 1 < n)
    def _():
        gather(dest_next_ref, 1 - slot)

    for k in range(TOP_K):
        pltpu.make_async_copy(ys_hbm.at[pl.ds(0, tm)], ybuf.at[slot, k], sems.at[slot]).wait()

    route = route_ref[...]
    f = (route[:, R_GATE0:R_GATE0 + 1] * _from_row_tiles(ybuf[slot, 0])
         + route[:, R_GATE1:R_GATE1 + 1] * _from_row_tiles(ybuf[slot, 1]))
    x2_ref[...] = _layer_norm(ALPHA * x1_ref[...] + f, g2_ref[...], b2_ref[...])


def _combine_ln(x1, ys, dest, route, g2, b2):
    t = x1.shape[0]
    tm = COMBINE_ROWS
    n = t // tm
    row = lambda i: (i, 0)
    const2 = lambda i: (0, 0)
    dest_blocks = _dest_blocks(dest, tm)
    smem_block = lambda index_map: pl.BlockSpec((1, 1, TOP_K * tm), index_map,
                                                memory_space=pltpu.SMEM)
    return pl.pallas_call(
        _combine_kernel,
        grid=(n,),
        in_specs=[
            smem_block(lambda i: (i, 0, 0)),
            smem_block(lambda i: (jnp.minimum(i + 1, n - 1), 0, 0)),
            pl.BlockSpec((tm, D_MODEL), row),
            pl.BlockSpec((tm, LANES), row),
            pl.BlockSpec((1, D_MODEL), const2),
            pl.BlockSpec((1, D_MODEL), const2),
            pl.BlockSpec(memory_space=pl.ANY),
        ],
        out_specs=pl.BlockSpec((tm, D_MODEL), row),
        out_shape=jax.ShapeDtypeStruct((t, D_MODEL), F32),
        scratch_shapes=[
            pltpu.VMEM((2, TOP_K, tm) + ROW_TILE, F32),
            pltpu.SemaphoreType.DMA((2,)),
        ],
        compiler_params=pltpu.CompilerParams(
            dimension_semantics=("arbitrary",), vmem_limit_bytes=VMEM_LIMIT),
        name="combine_ln",
    )(dest_blocks, dest_blocks, x1, route, g2.reshape(1, -1), b2.reshape(1, -1), ys)


def _moe_layer(x1, route, route_t, counts, layer, wg, wu, wd, g2, b2):
    dest, steps = _plan(route_t, counts)
    xs = _dispatch(x1, dest)
    ys = _expert_ffn(xs, steps, layer, wg, wu, wd)
    return _combine_ln(x1, ys, dest, route, g2, b2)


def kernel(x, rel_bias_table, mix_w_in, gmlp_ln_g, gmlp_ln_b, gmlp_w_spatial, gmlp_b_spatial, conv_w, mix_w_out, attn_w_qkv, attn_b_qkv, attn_sinks, attn_w_o, attn_b_o, ln1_g, ln1_b, ln2_g, ln2_b, router_group, router_expert, expert_w_gate, expert_w_up, expert_w_down):
    bsz, s, d = x.shape
    assert (s, d) == (SEQ, D_MODEL)
    x = x.reshape(bsz * s, d)
    bias = _rel_bias(rel_bias_table)
    for l in range(DEPTH):
        i = l // 2
        w_router = (router_group[l], router_expert[l])
        if l % 2 == 0:
            x1, *routing = _mixer_layer(x, mix_w_in[i], gmlp_ln_g[i], gmlp_ln_b[i],
                                        gmlp_w_spatial[i], gmlp_b_spatial[i], conv_w[i],
                                        mix_w_out[i], ln1_g[l], ln1_b[l], w_router)
        else:
            x1, *routing = _attn_layer(x, bias, attn_w_qkv[i], attn_b_qkv[i], attn_sinks[i],
                                       attn_w_o[i], attn_b_o[i], ln1_g[l], ln1_b[l], w_router)
        x = _moe_layer(x1, *routing, l, expert_w_gate, expert_w_up, expert_w_down,
                       ln2_g[l], ln2_b[l])
    return x.reshape(bsz, s, d)
```

```python
import math

import jax
import jax.numpy as jnp
from jax import lax
from jax.experimental import pallas as pl
from jax.experimental.pallas import tpu as pltpu

D_MODEL = 1024
SEQ = 16384
DEPTH = 4
CHUNK = 128
A_GROUPS = 4
A_CH = 128
A_WIDTH = A_GROUPS * A_CH
B_WIDTH = 512
CONV_W = 3
MIX_IN = 2 * A_WIDTH + 3 * B_WIDTH
N_HEADS = 16
N_KV = 2
HEAD_DIM = 64
GQA_GROUP = N_HEADS // N_KV
WINDOW = 128
QKV_DIM = (N_HEADS + 2 * N_KV) * HEAD_DIM
ATT_OUT = N_HEADS * HEAD_DIM
KV_WIDTH = N_KV * HEAD_DIM
N_BUCKETS = 32
MAX_DISTANCE = 128
N_GROUPS = 4
EXPERTS_PER_GROUP = 8
N_EXPERTS = N_GROUPS * EXPERTS_PER_GROUP
TOP_K = 2
D_EXPERT = 512
ALPHA = (2 * DEPTH) ** 0.25
LN_EPS = 1e-5

LANES = 128
SUBLANES = 8
ROW_TILE = (SUBLANES, LANES)
assert D_MODEL == SUBLANES * LANES
ROUTER_ROWS = SUBLANES + N_EXPERTS
MIX_ROWS = 1024
ATTN_ROWS = 1024
DISPATCH_ROWS = 1024
COMBINE_ROWS = 512
EXPERT_ROWS = 512
ROWS_PER_ISSUE = 16
DISPATCH_SLOTS = 3
VMEM_LIMIT = 56 * 1024 * 1024

R_E0, R_E1, R_RANK0, R_RANK1, R_GATE0, R_GATE1 = range(6)

F32 = jnp.float32
BF16 = jnp.bfloat16
I32 = jnp.int32
U32 = jnp.uint32


def _layer_norm(x, g, b):
    mu = jnp.mean(x, axis=-1, keepdims=True)
    xc = x - mu
    var = jnp.mean(xc * xc, axis=-1, keepdims=True)
    return xc * lax.rsqrt(var + LN_EPS) * g + b


def _gelu(x):
    return 0.5 * x * (1.0 + lax.erf(x * (2.0 ** -0.5)))


def _mixer_kernel(x_ref, win_ref, lng_ref, lnb_ref, wsp_ref, bsp_ref, cw_ref, wout_ref,
                  g1_ref, b1_ref, wr_ref, tri_ref, x1_ref, route_ref, route_t_ref, count_ref,
                  ztail_ref, base_ref):
    tm = x_ref.shape[0]
    i = pl.program_id(0)

    @pl.when(i % (SEQ // tm) == 0)
    def _():
        ztail_ref[...] = jnp.zeros_like(ztail_ref)

    x = x_ref[...]
    h = jnp.dot(x.astype(BF16), win_ref[...], preferred_element_type=F32)
    u = _gelu(h[:, :A_WIDTH])
    v = _gelu(h[:, A_WIDTH:2 * A_WIDTH])
    o = 2 * A_WIDTH
    g_b = h[:, o:o + B_WIDTH]
    g_c = h[:, o + B_WIDTH:o + 2 * B_WIDTH]
    hb = h[:, o + 2 * B_WIDTH:o + 3 * B_WIDTH]

    v = _layer_norm(v, lng_ref[...], lnb_ref[...]).astype(BF16)
    n_chunks = tm // CHUNK
    ri = lax.broadcasted_iota(I32, (CHUNK, CHUNK), 0)
    ci = lax.broadcasted_iota(I32, (CHUNK, CHUNK), 1)
    causal = ci <= ri
    sv_cols = [[None] * A_GROUPS for _ in range(n_chunks)]
    for g in range(A_GROUPS):
        ws = jnp.where(causal, wsp_ref[g], 0.0).astype(BF16)
        vg = jnp.concatenate(
            [v[c * CHUNK:(c + 1) * CHUNK, g * A_CH:(g + 1) * A_CH] for c in range(n_chunks)],
            axis=1)
        sg = jnp.dot(ws, vg, preferred_element_type=F32) + bsp_ref[:, g:g + 1]
        for c in range(n_chunks):
            sv_cols[c][g] = sg[:, c * A_CH:(c + 1) * A_CH]
    sv = jnp.concatenate([jnp.concatenate(row, axis=1) for row in sv_cols], axis=0)
    y_a = u * sv

    z = g_c * hb
    rows = lax.broadcasted_iota(I32, z.shape, 0)
    tail = ztail_ref[...]
    zm1 = jnp.where(rows == 0, tail[7:8, :], pltpu.roll(z, 1, 0))
    zm2 = jnp.where(rows == 0, tail[6:7, :],
                    jnp.where(rows == 1, tail[7:8, :], pltpu.roll(z, 2, 0)))
    conv = cw_ref[0:1, :] * zm2 + cw_ref[1:2, :] * zm1 + cw_ref[2:3, :] * z
    y_b = g_b * conv
    ztail_ref[...] = z[tm - 8:tm, :]

    y = jnp.concatenate([y_a, y_b], axis=1).astype(BF16)
    m = jnp.dot(y, wout_ref[...], preferred_element_type=F32)
    x1 = _layer_norm(ALPHA * x + m, g1_ref[...], b1_ref[...])
    x1_ref[...] = x1
    _route_tile(x1, wr_ref, tri_ref, route_ref, route_t_ref, count_ref, base_ref)


def _mixer_layer(x, w_in, ln_g, ln_b, w_sp, b_sp, conv_w, w_out, g1, b1, w_router):
    t = x.shape[0]
    tm = MIX_ROWS
    const2 = lambda i: (0, 0)
    r_in, r_out, r_shape, r_scratch = _route_specs(t, tm)
    return pl.pallas_call(
        _mixer_kernel,
        grid=(t // tm,),
        in_specs=[
            pl.BlockSpec((tm, D_MODEL), lambda i: (i, 0)),
            pl.BlockSpec((D_MODEL, MIX_IN), const2),
            pl.BlockSpec((1, A_WIDTH), const2),
            pl.BlockSpec((1, A_WIDTH), const2),
            pl.BlockSpec((A_GROUPS, CHUNK, CHUNK), lambda i: (0, 0, 0)),
            pl.BlockSpec((CHUNK, A_GROUPS), const2),
            pl.BlockSpec((CONV_W, B_WIDTH), const2),
            pl.BlockSpec((A_WIDTH + B_WIDTH, D_MODEL), const2),
            pl.BlockSpec((1, D_MODEL), const2),
            pl.BlockSpec((1, D_MODEL), const2),
        ] + r_in,
        out_specs=[pl.BlockSpec((tm, D_MODEL), lambda i: (i, 0))] + r_out,
        out_shape=[jax.ShapeDtypeStruct((t, D_MODEL), F32)] + r_shape,
        scratch_shapes=[pltpu.VMEM((8, B_WIDTH), F32), r_scratch],
        compiler_params=pltpu.CompilerParams(
            dimension_semantics=("arbitrary",), vmem_limit_bytes=VMEM_LIMIT),
        name="mixer_layer",
    )(x, w_in.astype(BF16), ln_g.reshape(1, -1), ln_b.reshape(1, -1), w_sp, b_sp.T,
      conv_w, w_out.astype(BF16), g1.reshape(1, -1), b1.reshape(1, -1),
      *_route_operands(*w_router, tm))


def _attn_kernel(sink_ref, x_ref, wqkv_ref, bqkv_ref, bias_ref, wo_ref, bo_ref,
                 g1_ref, b1_ref, wr_ref, tri_ref, x1_ref, route_ref, route_t_ref, count_ref,
                 kprev_ref, vprev_ref, base_ref):
    tm = x_ref.shape[0]
    i = pl.program_id(0)
    first = i % (SEQ // tm) == 0

    @pl.when(first)
    def _():
        kprev_ref[...] = jnp.zeros_like(kprev_ref)
        vprev_ref[...] = jnp.zeros_like(vprev_ref)

    x = x_ref[...]
    qkv = jnp.dot(x.astype(BF16), wqkv_ref[...], preferred_element_type=F32) + bqkv_ref[...]
    q = (qkv[:, :ATT_OUT] * (HEAD_DIM ** -0.5)).astype(BF16)

    lane = lax.broadcasted_iota(I32, (CHUNK, KV_WIDTH), 1)
    low = lane < HEAD_DIM

    def halves(t):
        swapped = pltpu.roll(t, HEAD_DIM, 1)
        zero = jnp.zeros_like(t)
        return [jnp.where(low, t, zero).astype(BF16), jnp.where(low, zero, swapped).astype(BF16),
                jnp.where(low, swapped, zero).astype(BF16), jnp.where(low, zero, t).astype(BF16)]

    def stacked(prev, cur, j):
        return jnp.concatenate([prev[2 * j], cur[2 * j], prev[2 * j + 1], cur[2 * j + 1]], axis=0)

    a = lax.broadcasted_iota(I32, (CHUNK, 2 * CHUNK), 0)
    c = lax.broadcasted_iota(I32, (CHUNK, 2 * CHUNK), 1)
    window = (c > a) & (c <= a + WINDOW)
    neg = jnp.finfo(F32).min
    n_pairs = N_HEADS // 2
    kv_of = lambda pair: (2 * pair) // GQA_GROUP

    k_prev = [kprev_ref[idx] for idx in range(2 * N_KV)]
    v_prev = [vprev_ref[idx] for idx in range(2 * N_KV)]
    o_blocks = []
    for blk in range(tm // CHUNK):
        rows = slice(blk * CHUNK, (blk + 1) * CHUNK)
        k_cur = halves(qkv[rows, ATT_OUT:ATT_OUT + KV_WIDTH])
        v_cur = halves(qkv[rows, ATT_OUT + KV_WIDTH:])
        k_rhs = [stacked(k_prev, k_cur, j) for j in range(N_KV)]
        v_rhs = [stacked(v_prev, v_cur, j) for j in range(N_KV)]
        k_prev, v_prev = k_cur, v_cur
        mask = window & (c >= jnp.where(first, CHUNK, 0)) if blk == 0 else window

        scores = [lax.dot_general(q[rows, pair * 2 * HEAD_DIM:(pair + 1) * 2 * HEAD_DIM],
                                  k_rhs[kv_of(pair)], (((1,), (1,)), ((), ())),
                                  preferred_element_type=F32) for pair in range(n_pairs)]
        probs, inv_denoms = [], []
        for pair in range(n_pairs):
            sc2 = scores[pair] + bias_ref[pair]
            sides = []
            for side in range(2):
                sc = jnp.where(mask, sc2[:, side * 2 * CHUNK:(side + 1) * 2 * CHUNK], neg)
                sink = sink_ref[2 * pair + side]
                m = jnp.maximum(jnp.max(sc, axis=-1, keepdims=True), sink)
                p = jnp.exp(sc - m)
                denom = jnp.sum(p, axis=-1, keepdims=True) + jnp.exp(sink - m)
                sides.append(p.astype(BF16))
                inv_denoms.append(jnp.broadcast_to(1.0 / denom, (CHUNK, HEAD_DIM)))
            probs.append(jnp.concatenate(sides, axis=1))
        outs = [jnp.dot(probs[pair], v_rhs[kv_of(pair)], preferred_element_type=F32)
                for pair in range(n_pairs)]
        o = jnp.concatenate(outs, axis=1) * jnp.concatenate(inv_denoms, axis=1)
        o_blocks.append(o.astype(BF16))
    for idx in range(2 * N_KV):
        kprev_ref[idx] = k_prev[idx]
        vprev_ref[idx] = v_prev[idx]
    o = jnp.concatenate(o_blocks, axis=0)
    m_out = jnp.dot(o, wo_ref[...], preferred_element_type=F32) + bo_ref[...]
    x1 = _layer_norm(ALPHA * x + m_out, g1_ref[...], b1_ref[...])
    x1_ref[...] = x1
    _route_tile(x1, wr_ref, tri_ref, route_ref, route_t_ref, count_ref, base_ref)


def _t5_bucket(rel):
    n = jnp.maximum(rel, 0)
    max_exact = N_BUCKETS // 2
    nf = jnp.maximum(n, 1).astype(F32)
    large = max_exact + (jnp.log(nf / max_exact) / math.log(MAX_DISTANCE / max_exact)
                         * (N_BUCKETS - max_exact)).astype(I32)
    large = jnp.minimum(large, N_BUCKETS - 1)
    return jnp.where(n < max_exact, n, large)


def _rel_bias(rel_table):
    a = jnp.arange(CHUNK)[:, None]
    c = jnp.arange(2 * CHUNK)[None, :]
    onehot = jax.nn.one_hot(_t5_bucket(a + CHUNK - c), N_BUCKETS, dtype=F32)
    bias = jnp.einsum('acb,bh->hac', onehot, rel_table.astype(F32),
                      precision=lax.Precision.HIGHEST)
    bias = bias.reshape(N_HEADS // 2, 2, CHUNK, 2 * CHUNK).transpose(0, 2, 1, 3)
    return bias.reshape(N_HEADS // 2, CHUNK, 4 * CHUNK)


def _attn_layer(x, bias, w_qkv, b_qkv, sinks, w_o, b_o, g1, b1, w_router):
    t = x.shape[0]
    tm = ATTN_ROWS
    const2 = lambda i: (0, 0)
    r_in, r_out, r_shape, r_scratch = _route_specs(t, tm)
    return pl.pallas_call(
        _attn_kernel,
        grid=(t // tm,),
        in_specs=[
            pl.BlockSpec(memory_space=pltpu.SMEM),
            pl.BlockSpec((tm, D_MODEL), lambda i: (i, 0)),
            pl.BlockSpec((D_MODEL, QKV_DIM), const2),
            pl.BlockSpec((1, QKV_DIM), const2),
            pl.BlockSpec((N_HEADS // 2, CHUNK, 4 * CHUNK), lambda i: (0, 0, 0)),
            pl.BlockSpec((ATT_OUT, D_MODEL), const2),
            pl.BlockSpec((1, D_MODEL), const2),
            pl.BlockSpec((1, D_MODEL), const2),
            pl.BlockSpec((1, D_MODEL), const2),
        ] + r_in,
        out_specs=[pl.BlockSpec((tm, D_MODEL), lambda i: (i, 0))] + r_out,
        out_shape=[jax.ShapeDtypeStruct((t, D_MODEL), F32)] + r_shape,
        scratch_shapes=[pltpu.VMEM((2 * N_KV, CHUNK, KV_WIDTH), BF16),
                        pltpu.VMEM((2 * N_KV, CHUNK, KV_WIDTH), BF16), r_scratch],
        compiler_params=pltpu.CompilerParams(
            dimension_semantics=("arbitrary",), vmem_limit_bytes=VMEM_LIMIT),
        name="attn_layer",
    )(sinks, x, w_qkv.astype(BF16), b_qkv.reshape(1, -1), bias, w_o.astype(BF16),
      b_o.reshape(1, -1), g1.reshape(1, -1), b1.reshape(1, -1), *_route_operands(*w_router, tm))


def _route_tile(x1, wr_ref, tri_ref, route_ref, route_t_ref, count_ref, base_ref):
    @pl.when(pl.program_id(0) == 0)
    def _():
        base_ref[...] = jnp.zeros_like(base_ref)

    tm = x1.shape[0]
    lt = lax.dot_general(wr_ref[...], x1.astype(BF16), (((1,), (1,)), ((), ())),
                         preferred_element_type=F32)
    sub = lax.broadcasted_iota(I32, (SUBLANES, tm), 0).astype(F32)
    ninf = -jnp.inf

    def first_argmax(vals):
        m = jnp.max(vals, axis=0, keepdims=True)
        idx = jnp.min(jnp.where(vals == m, sub, float(SUBLANES)), axis=0, keepdims=True)
        return m, idx

    is_g = sub < N_GROUPS
    g_rows = lt[0:SUBLANES]
    gmax, g_idx = first_argmax(jnp.where(is_g, g_rows, ninf))
    g_p = 1.0 / jnp.sum(jnp.where(is_g, jnp.exp(g_rows - gmax), 0.0), axis=0, keepdims=True)

    group_rows = lambda a, g: a[SUBLANES * (g + 1):SUBLANES * (g + 2)]
    el = group_rows(lt, 0)
    for g in range(1, N_GROUPS):
        el = jnp.where(g_idx == g, group_rows(lt, g), el)
    m1, j1 = first_argmax(el)
    m2, j2 = first_argmax(jnp.where(sub == j1, ninf, el))
    a2 = jnp.exp(m2 - m1)
    gate0 = g_p / (1.0 + a2)
    gate1 = g_p * a2 / (1.0 + a2)

    hit0 = [(g_idx == g) & (sub == j1) for g in range(N_GROUPS)]
    hit1 = [(g_idx == g) & (sub == j2) for g in range(N_GROUPS)]
    onehot = jnp.concatenate([jnp.where(h0 | h1, 1.0, 0.0) for h0, h1 in zip(hit0, hit1)],
                             axis=0)
    before = jnp.dot(onehot.astype(BF16), tri_ref[...], preferred_element_type=F32)
    before = before + base_ref[:, 0:1]

    def picked(hits):
        total = jnp.zeros((SUBLANES, tm), F32)
        for g in range(N_GROUPS):
            total = total + jnp.where(hits[g], before[SUBLANES * g:SUBLANES * (g + 1)], 0.0)
        return jnp.sum(total, axis=0, keepdims=True)

    rank0 = picked(hit0)
    rank1 = picked(hit1)
    base_ref[...] += jnp.sum(onehot, axis=1, keepdims=True)

    e_base = g_idx * EXPERTS_PER_GROUP
    rec_t = jnp.concatenate([e_base + j1, e_base + j2, rank0, rank1, gate0, gate1,
                             jnp.zeros((SUBLANES - 6, tm), F32)], axis=0)
    route_t_ref[...] = rec_t
    padded = jnp.concatenate([rec_t, jnp.zeros((LANES - SUBLANES, tm), F32)], axis=0)
    route_ref[...] = jnp.transpose(padded)
    count_ref[...] = base_ref[...]


def _route_specs(t, tm):
    const2 = lambda i: (0, 0)
    in_specs = [pl.BlockSpec((ROUTER_ROWS, D_MODEL), const2), pl.BlockSpec((tm, tm), const2)]
    out_specs = [
        pl.BlockSpec((tm, LANES), lambda i: (i, 0)),
        pl.BlockSpec((SUBLANES, tm), lambda i: (0, i)),
        pl.BlockSpec((N_EXPERTS, LANES), const2),
    ]
    out_shape = [
        jax.ShapeDtypeStruct((t, LANES), F32),
        jax.ShapeDtypeStruct((SUBLANES, t), F32),
        jax.ShapeDtypeStruct((N_EXPERTS, LANES), F32),
    ]
    return in_specs, out_specs, out_shape, pltpu.VMEM((N_EXPERTS, LANES), F32)


def _route_operands(w_group, w_expert, tm):
    pad = jnp.zeros((D_MODEL, SUBLANES - N_GROUPS), F32)
    w_rows = jnp.concatenate([w_group, pad, w_expert], axis=1).T
    return w_rows.astype(BF16), jnp.triu(jnp.ones((tm, tm), BF16), 1)


def _plan(route_t, count_lanes):
    t = route_t.shape[1]
    rb = EXPERT_ROWS
    experts = jnp.arange(N_EXPERTS, dtype=I32)
    counts = count_lanes[:, 0].astype(I32)
    end = jnp.cumsum(counts)
    start = end - counts
    e = route_t[R_E0:R_E1 + 1].astype(I32)
    rank = route_t[R_RANK0:R_RANK1 + 1].astype(I32)
    onehot = e[:, None, :] == experts[None, :, None]
    dest = jnp.sum(jnp.where(onehot, start[None, :, None], 0), axis=1) + rank

    first_blk = start // rb
    last_blk = jnp.maximum(end - 1, 0) // rb
    n_steps_e = jnp.where(counts > 0, last_blk - first_blk + 1, 0)
    step_end = jnp.cumsum(n_steps_e)
    step_start = step_end - n_steps_e
    n_steps = step_end[-1]
    max_steps = t * TOP_K // rb + N_EXPERTS - 1
    s = jnp.minimum(jnp.arange(max_steps, dtype=I32), n_steps - 1)
    step_e = jnp.minimum(jnp.sum((step_end[None, :] <= s[:, None]).astype(I32), axis=1),
                         N_EXPERTS - 1)
    pick = step_e[:, None] == experts[None, :]
    take = lambda v: jnp.sum(jnp.where(pick, v[None, :], 0), axis=1)
    step_blk = take(first_blk) + s - take(step_start)
    step_lo = jnp.clip(take(start) - step_blk * rb, 0, rb)
    step_hi = jnp.clip(take(end) - step_blk * rb, 0, rb)
    return dest, (step_blk, step_e, step_lo, step_hi, n_steps.reshape(1))


def _dest_blocks(dest, tm):
    t = dest.shape[1]
    return dest.reshape(TOP_K, t // tm, tm).transpose(1, 0, 2).reshape(t // tm, 1, TOP_K * tm)


def _to_row_tiles(x):
    return x.reshape(x.shape[0], ROW_TILE[0], ROW_TILE[1])


def _from_row_tiles(x):
    return x.reshape(x.shape[0], D_MODEL)


PACKED_ROW_TILE = (SUBLANES // 2, LANES)
HIGH_HALF = 0xFFFF0000


def _pack_rows(x):
    half = D_MODEL // 2
    bits = lambda v: lax.bitcast_convert_type(v.astype(BF16).astype(F32), U32)
    words = (bits(x[:, :half]) >> 16) | (bits(x[:, half:]) & jnp.uint32(HIGH_HALF))
    return words.reshape((x.shape[0],) + PACKED_ROW_TILE)


def _unpack_rows(words):
    w = words.reshape(words.shape[0], D_MODEL // 2)
    low = lax.bitcast_convert_type(w << 16, F32).astype(BF16)
    high = lax.bitcast_convert_type(w & jnp.uint32(HIGH_HALF), F32).astype(BF16)
    return jnp.concatenate([low, high], axis=1)


def _dispatch_kernel(dest_ref, x1_ref, xs_hbm, xbuf, row_sems):
    tm = x1_ref.shape[0]
    i = pl.program_id(0)
    n = pl.num_programs(0)
    slot = i % DISPATCH_SLOTS

    def drain_rows(s):
        for k in range(TOP_K):
            pltpu.make_async_copy(xbuf.at[s], xs_hbm.at[pl.ds(0, tm)], row_sems.at[s]).wait()

    @pl.when(i >= DISPATCH_SLOTS)
    def _():
        drain_rows(slot)

    xbuf[slot] = _pack_rows(x1_ref[...])

    def issue(g, carry):
        for j in range(ROWS_PER_ISSUE):
            r = g * ROWS_PER_ISSUE + j
            for k in range(TOP_K):
                pltpu.make_async_copy(xbuf.at[slot, r], xs_hbm.at[dest_ref[0, 0, k * tm + r]],
                                      row_sems.at[slot]).start(priority=k)
        return carry

    lax.fori_loop(0, tm // ROWS_PER_ISSUE, issue, 0)

    @pl.when(i == n - 1)
    def _():
        for back in range(DISPATCH_SLOTS):
            drain_rows((i - back) % DISPATCH_SLOTS)


def _dispatch(x1, dest):
    t = x1.shape[0]
    tm = DISPATCH_ROWS
    return pl.pallas_call(
        _dispatch_kernel,
        grid=(t // tm,),
        in_specs=[
            pl.BlockSpec((1, 1, TOP_K * tm), lambda i: (i, 0, 0), memory_space=pltpu.SMEM),
            pl.BlockSpec((tm, D_MODEL), lambda i: (i, 0)),
        ],
        out_specs=pl.BlockSpec(memory_space=pl.ANY),
        out_shape=jax.ShapeDtypeStruct((t * TOP_K,) + PACKED_ROW_TILE, U32),
        scratch_shapes=[
            pltpu.VMEM((DISPATCH_SLOTS, tm) + PACKED_ROW_TILE, U32),
            pltpu.SemaphoreType.DMA((DISPATCH_SLOTS,)),
        ],
        compiler_params=pltpu.CompilerParams(
            dimension_semantics=("arbitrary",), vmem_limit_bytes=VMEM_LIMIT),
        name="dispatch",
    )(_dest_blocks(dest, tm), x1)


def _expert_kernel(blk_ref, e_ref, lo_ref, hi_ref, n_steps_ref, xs_ref, wg_ref, wu_ref, wd_ref,
                   ys_ref, wg_bf, wu_bf, wd_bf):
    s = pl.program_id(0)
    prev = jnp.maximum(s - 1, 0)
    new_expert = (s == 0) | (e_ref[s] != e_ref[prev])
    new_block = (s == 0) | (blk_ref[s] != blk_ref[prev])

    @pl.when(new_expert)
    def _():
        wg_bf[...] = wg_ref[0, 0].astype(BF16)
        wu_bf[...] = wu_ref[0, 0].astype(BF16)
        wd_bf[...] = wd_ref[0, 0].astype(BF16)

    rb = xs_ref.shape[0]
    half = rb // 2
    lo, hi = lo_ref[s], hi_ref[s]
    active = s < n_steps_ref[0]
    both_halves = (lo < half) & (hi > half)

    def masked_ffn(x_tiles, row0):
        xb = _unpack_rows(x_tiles)
        gate = jnp.dot(xb, wg_bf[...], preferred_element_type=F32)
        up = jnp.dot(xb, wu_bf[...], preferred_element_type=F32)
        hid = (gate * jax.nn.sigmoid(gate) * up).astype(BF16)
        y = jnp.dot(hid, wd_bf[...], preferred_element_type=F32)
        rows = lax.broadcasted_iota(I32, y.shape, 0) + row0
        return _to_row_tiles(jnp.where((rows >= lo) & (rows < hi), y, 0.0))

    @pl.when(active & both_halves)
    def _():
        y = masked_ffn(xs_ref[...], 0)

        @pl.when(new_block)
        def _():
            ys_ref[...] = y

        @pl.when(jnp.logical_not(new_block))
        def _():
            ys_ref[...] += y

    @pl.when(active & jnp.logical_not(both_halves))
    def _():
        off = pl.multiple_of(jnp.where(lo < half, 0, half), half)
        y = masked_ffn(xs_ref[pl.ds(off, half)], off)

        @pl.when(new_block)
        def _():
            ys_ref[pl.ds(off, half)] = y
            ys_ref[pl.ds(half - off, half)] = jnp.zeros_like(y)

        @pl.when(jnp.logical_not(new_block))
        def _():
            ys_ref[pl.ds(off, half)] += y


def _expert_ffn(xs, steps, layer, wg, wu, wd):
    step_blk, step_e, step_lo, step_hi, n_steps = steps
    rb = EXPERT_ROWS
    row_map = lambda s, blk, e, lo, hi, n: (blk[s], 0, 0)
    w_map = lambda s, blk, e, lo, hi, n: (layer, e[s], 0, 0)
    grid_spec = pltpu.PrefetchScalarGridSpec(
        num_scalar_prefetch=5,
        grid=(step_blk.shape[0],),
        in_specs=[
            pl.BlockSpec((rb,) + PACKED_ROW_TILE, row_map),
            pl.BlockSpec((1, 1, D_MODEL, D_EXPERT), w_map),
            pl.BlockSpec((1, 1, D_MODEL, D_EXPERT), w_map),
            pl.BlockSpec((1, 1, D_EXPERT, D_MODEL), w_map),
        ],
        out_specs=pl.BlockSpec((rb,) + ROW_TILE, row_map),
        scratch_shapes=[
            pltpu.VMEM((D_MODEL, D_EXPERT), BF16),
            pltpu.VMEM((D_MODEL, D_EXPERT), BF16),
            pltpu.VMEM((D_EXPERT, D_MODEL), BF16),
        ],
    )
    return pl.pallas_call(
        _expert_kernel,
        grid_spec=grid_spec,
        out_shape=jax.ShapeDtypeStruct((xs.shape[0],) + ROW_TILE, F32),
        compiler_params=pltpu.CompilerParams(
            dimension_semantics=("arbitrary",), vmem_limit_bytes=VMEM_LIMIT),
        name="expert_ffn",
    )(step_blk, step_e, step_lo, step_hi, n_steps, xs, wg, wu, wd)


def _combine_kernel(dest_ref, dest_next_ref, x1_ref, route_ref, g2_ref, b2_ref, ys_hbm,
                    x2_ref, ybuf, sems):
    tm = x1_ref.shape[0]
    i = pl.program_id(0)
    n = pl.num_programs(0)
    slot = i % 2

    def gather(d_ref, to_slot):
        def issue(g, carry):
            for j in range(ROWS_PER_ISSUE):
                r = g * ROWS_PER_ISSUE + j
                for k in range(TOP_K):
                    pltpu.make_async_copy(ys_hbm.at[d_ref[0, 0, k * tm + r]],
                                          ybuf.at[to_slot, k, r],
                                          sems.at[to_slot]).start(priority=k)
            return carry
        lax.fori_loop(0, tm // ROWS_PER_ISSUE, issue, 0)

    @pl.when(i == 0)
    def _():
        gather(dest_ref, slot)

    @pl.when(i + 1 < n)
    def _():
        gather(dest_next_ref, 1 - slot)

    for k in range(TOP_K):
        pltpu.make_async_copy(ys_hbm.at[pl.ds(0, tm)], ybuf.at[slot, k], sems.at[slot]).wait()

    route = route_ref[...]
    f = (route[:, R_GATE0:R_GATE0 + 1] * _from_row_tiles(ybuf[slot, 0])
         + route[:, R_GATE1:R_GATE1 + 1] * _from_row_tiles(ybuf[slot, 1]))
    x2_ref[...] = _layer_norm(ALPHA * x1_ref[...] + f, g2_ref[...], b2_ref[...])


def _combine_ln(x1, ys, dest, route, g2, b2):
    t = x1.shape[0]
    tm = COMBINE_ROWS
    n = t // tm
    row = lambda i: (i, 0)
    const2 = lambda i: (0, 0)
    dest_blocks = _dest_blocks(dest, tm)
    smem_block = lambda index_map: pl.BlockSpec((1, 1, TOP_K * tm), index_map,
                                                memory_space=pltpu.SMEM)
    return pl.pallas_call(
        _combine_kernel,
        grid=(n,),
        in_specs=[
            smem_block(lambda i: (i, 0, 0)),
            smem_block(lambda i: (jnp.minimum(i + 1, n - 1), 0, 0)),
            pl.BlockSpec((tm, D_MODEL), row),
            pl.BlockSpec((tm, LANES), row),
            pl.BlockSpec((1, D_MODEL), const2),
            pl.BlockSpec((1, D_MODEL), const2),
            pl.BlockSpec(memory_space=pl.ANY),
        ],
        out_specs=pl.BlockSpec((tm, D_MODEL), row),
        out_shape=jax.ShapeDtypeStruct((t, D_MODEL), F32),
        scratch_shapes=[
            pltpu.VMEM((2, TOP_K, tm) + ROW_TILE, F32),
            pltpu.SemaphoreType.DMA((2,)),
        ],
        compiler_params=pltpu.CompilerParams(
            dimension_semantics=("arbitrary",), vmem_limit_bytes=VMEM_LIMIT),
        name="combine_ln",
    )(dest_blocks, dest_blocks, x1, route, g2.reshape(1, -1), b2.reshape(1, -1), ys)


def _moe_layer(x1, route, route_t, counts, layer, wg, wu, wd, g2, b2):
    dest, steps = _plan(route_t, counts)
    xs = _dispatch(x1, dest)
    ys = _expert_ffn(xs, steps, layer, wg, wu, wd)
    return _combine_ln(x1, ys, dest, route, g2, b2)


def kernel(x, rel_bias_table, mix_w_in, gmlp_ln_g, gmlp_ln_b, gmlp_w_spatial, gmlp_b_spatial, conv_w, mix_w_out, attn_w_qkv, attn_b_qkv, attn_sinks, attn_w_o, attn_b_o, ln1_g, ln1_b, ln2_g, ln2_b, router_group, router_expert, expert_w_gate, expert_w_up, expert_w_down):
    bsz, s, d = x.shape
    assert (s, d) == (SEQ, D_MODEL)
    x = x.reshape(bsz * s, d)
    bias = _rel_bias(rel_bias_table)
    for l in range(DEPTH):
        i = l // 2
        w_router = (router_group[l], router_expert[l])
        if l % 2 == 0:
            x1, *routing = _mixer_layer(x, mix_w_in[i], gmlp_ln_g[i], gmlp_ln_b[i],
                                        gmlp_w_spatial[i], gmlp_b_spatial[i], conv_w[i],
                                        mix_w_out[i], ln1_g[l], ln1_b[l], w_router)
        else:
            x1, *routing = _attn_layer(x, bias, attn_w_qkv[i], attn_b_qkv[i], attn_sinks[i],
                                       attn_w_o[i], attn_b_o[i], ln1_g[l], ln1_b[l], w_router)
        x = _moe_layer(x1, *routing, l, expert_w_gate, expert_w_up, expert_w_down,
                       ln2_g[l], ln2_b[l])
    return x.reshape(bsz, s, d)
```

```python
import math

import jax
import jax.numpy as jnp
from jax import lax
from jax.experimental import pallas as pl
from jax.experimental.pallas import tpu as pltpu

D_MODEL = 1024
SEQ = 16384
DEPTH = 4
CHUNK = 128
A_GROUPS = 4
A_CH = 128
A_WIDTH = A_GROUPS * A_CH
B_WIDTH = 512
CONV_W = 3
MIX_IN = 2 * A_WIDTH + 3 * B_WIDTH
N_HEADS = 16
N_KV = 2
HEAD_DIM = 64
GQA_GROUP = N_HEADS // N_KV
WINDOW = 128
QKV_DIM = (N_HEADS + 2 * N_KV) * HEAD_DIM
ATT_OUT = N_HEADS * HEAD_DIM
KV_WIDTH = N_KV * HEAD_DIM
N_BUCKETS = 32
MAX_DISTANCE = 128
N_GROUPS = 4
EXPERTS_PER_GROUP = 8
N_EXPERTS = N_GROUPS * EXPERTS_PER_GROUP
TOP_K = 2
D_EXPERT = 512
ALPHA = (2 * DEPTH) ** 0.25
LN_EPS = 1e-5

LANES = 128
SUBLANES = 8
ROW_TILE = (SUBLANES, LANES)
assert D_MODEL == SUBLANES * LANES
ROUTER_ROWS = SUBLANES + N_EXPERTS
MIX_ROWS = 1024
ATTN_ROWS = 1024
DISPATCH_ROWS = 1024
COMBINE_ROWS = 512
EXPERT_ROWS = 1024
ROWS_PER_ISSUE = 16
DISPATCH_SLOTS = 3
VMEM_LIMIT = 56 * 1024 * 1024

R_E0, R_E1, R_RANK0, R_RANK1, R_GATE0, R_GATE1 = range(6)

F32 = jnp.float32
BF16 = jnp.bfloat16
I32 = jnp.int32
U32 = jnp.uint32


def _layer_norm(x, g, b):
    mu = jnp.mean(x, axis=-1, keepdims=True)
    xc = x - mu
    var = jnp.mean(xc * xc, axis=-1, keepdims=True)
    return xc * lax.rsqrt(var + LN_EPS) * g + b


def _gelu(x):
    return 0.5 * x * (1.0 + lax.erf(x * (2.0 ** -0.5)))


def _mixer_kernel(x_ref, win_ref, lng_ref, lnb_ref, wsp_ref, bsp_ref, cw_ref, wout_ref,
                  g1_ref, b1_ref, wr_ref, tri_ref, x1_ref, route_ref, route_t_ref, count_ref,
                  ztail_ref, base_ref):
    tm = x_ref.shape[0]
    i = pl.program_id(0)

    @pl.when(i % (SEQ // tm) == 0)
    def _():
        ztail_ref[...] = jnp.zeros_like(ztail_ref)

    x = x_ref[...]
    h = jnp.dot(x.astype(BF16), win_ref[...], preferred_element_type=F32)
    u = _gelu(h[:, :A_WIDTH])
    v = _gelu(h[:, A_WIDTH:2 * A_WIDTH])
    o = 2 * A_WIDTH
    g_b = h[:, o:o + B_WIDTH]
    g_c = h[:, o + B_WIDTH:o + 2 * B_WIDTH]
    hb = h[:, o + 2 * B_WIDTH:o + 3 * B_WIDTH]

    v = _layer_norm(v, lng_ref[...], lnb_ref[...]).astype(BF16)
    n_chunks = tm // CHUNK
    ri = lax.broadcasted_iota(I32, (CHUNK, CHUNK), 0)
    ci = lax.broadcasted_iota(I32, (CHUNK, CHUNK), 1)
    causal = ci <= ri
    sv_cols = [[None] * A_GROUPS for _ in range(n_chunks)]
    for g in range(A_GROUPS):
        ws = jnp.where(causal, wsp_ref[g], 0.0).astype(BF16)
        vg = jnp.concatenate(
            [v[c * CHUNK:(c + 1) * CHUNK, g * A_CH:(g + 1) * A_CH] for c in range(n_chunks)],
            axis=1)
        sg = jnp.dot(ws, vg, preferred_element_type=F32) + bsp_ref[:, g:g + 1]
        for c in range(n_chunks):
            sv_cols[c][g] = sg[:, c * A_CH:(c + 1) * A_CH]
    sv = jnp.concatenate([jnp.concatenate(row, axis=1) for row in sv_cols], axis=0)
    y_a = u * sv

    z = g_c * hb
    rows = lax.broadcasted_iota(I32, z.shape, 0)
    tail = ztail_ref[...]
    zm1 = jnp.where(rows == 0, tail[7:8, :], pltpu.roll(z, 1, 0))
    zm2 = jnp.where(rows == 0, tail[6:7, :],
                    jnp.where(rows == 1, tail[7:8, :], pltpu.roll(z, 2, 0)))
    conv = cw_ref[0:1, :] * zm2 + cw_ref[1:2, :] * zm1 + cw_ref[2:3, :] * z
    y_b = g_b * conv
    ztail_ref[...] = z[tm - 8:tm, :]

    y = jnp.concatenate([y_a, y_b], axis=1).astype(BF16)
    m = jnp.dot(y, wout_ref[...], preferred_element_type=F32)
    x1 = _layer_norm(ALPHA * x + m, g1_ref[...], b1_ref[...])
    x1_ref[...] = x1
    _route_tile(x1, wr_ref, tri_ref, route_ref, route_t_ref, count_ref, base_ref)


def _mixer_layer(x, w_in, ln_g, ln_b, w_sp, b_sp, conv_w, w_out, g1, b1, w_router):
    t = x.shape[0]
    tm = MIX_ROWS
    const2 = lambda i: (0, 0)
    r_in, r_out, r_shape, r_scratch = _route_specs(t, tm)
    return pl.pallas_call(
        _mixer_kernel,
        grid=(t // tm,),
        in_specs=[
            pl.BlockSpec((tm, D_MODEL), lambda i: (i, 0)),
            pl.BlockSpec((D_MODEL, MIX_IN), const2),
            pl.BlockSpec((1, A_WIDTH), const2),
            pl.BlockSpec((1, A_WIDTH), const2),
            pl.BlockSpec((A_GROUPS, CHUNK, CHUNK), lambda i: (0, 0, 0)),
            pl.BlockSpec((CHUNK, A_GROUPS), const2),
            pl.BlockSpec((CONV_W, B_WIDTH), const2),
            pl.BlockSpec((A_WIDTH + B_WIDTH, D_MODEL), const2),
            pl.BlockSpec((1, D_MODEL), const2),
            pl.BlockSpec((1, D_MODEL), const2),
        ] + r_in,
        out_specs=[pl.BlockSpec((tm, D_MODEL), lambda i: (i, 0))] + r_out,
        out_shape=[jax.ShapeDtypeStruct((t, D_MODEL), F32)] + r_shape,
        scratch_shapes=[pltpu.VMEM((8, B_WIDTH), F32), r_scratch],
        compiler_params=pltpu.CompilerParams(
            dimension_semantics=("arbitrary",), vmem_limit_bytes=VMEM_LIMIT),
        name="mixer_layer",
    )(x, w_in.astype(BF16), ln_g.reshape(1, -1), ln_b.reshape(1, -1), w_sp, b_sp.T,
      conv_w, w_out.astype(BF16), g1.reshape(1, -1), b1.reshape(1, -1),
      *_route_operands(*w_router, tm))


def _attn_kernel(sink_ref, x_ref, wqkv_ref, bqkv_ref, bias_ref, wo_ref, bo_ref,
                 g1_ref, b1_ref, wr_ref, tri_ref, x1_ref, route_ref, route_t_ref, count_ref,
                 kprev_ref, vprev_ref, base_ref):
    tm = x_ref.shape[0]
    i = pl.program_id(0)
    first = i % (SEQ // tm) == 0

    @pl.when(first)
    def _():
        kprev_ref[...] = jnp.zeros_like(kprev_ref)
        vprev_ref[...] = jnp.zeros_like(vprev_ref)

    x = x_ref[...]
    qkv = jnp.dot(x.astype(BF16), wqkv_ref[...], preferred_element_type=F32) + bqkv_ref[...]
    q = (qkv[:, :ATT_OUT] * (HEAD_DIM ** -0.5)).astype(BF16)

    lane = lax.broadcasted_iota(I32, (CHUNK, KV_WIDTH), 1)
    low = lane < HEAD_DIM

    def halves(t):
        swapped = pltpu.roll(t, HEAD_DIM, 1)
        zero = jnp.zeros_like(t)
        return [jnp.where(low, t, zero).astype(BF16), jnp.where(low, zero, swapped).astype(BF16),
                jnp.where(low, swapped, zero).astype(BF16), jnp.where(low, zero, t).astype(BF16)]

    def stacked(prev, cur, j):
        return jnp.concatenate([prev[2 * j], cur[2 * j], prev[2 * j + 1], cur[2 * j + 1]], axis=0)

    a = lax.broadcasted_iota(I32, (CHUNK, 2 * CHUNK), 0)
    c = lax.broadcasted_iota(I32, (CHUNK, 2 * CHUNK), 1)
    window = (c > a) & (c <= a + WINDOW)
    neg = jnp.finfo(F32).min
    n_pairs = N_HEADS // 2
    kv_of = lambda pair: (2 * pair) // GQA_GROUP

    k_prev = [kprev_ref[idx] for idx in range(2 * N_KV)]
    v_prev = [vprev_ref[idx] for idx in range(2 * N_KV)]
    o_blocks = []
    for blk in range(tm // CHUNK):
        rows = slice(blk * CHUNK, (blk + 1) * CHUNK)
        k_cur = halves(qkv[rows, ATT_OUT:ATT_OUT + KV_WIDTH])
        v_cur = halves(qkv[rows, ATT_OUT + KV_WIDTH:])
        k_rhs = [stacked(k_prev, k_cur, j) for j in range(N_KV)]
        v_rhs = [stacked(v_prev, v_cur, j) for j in range(N_KV)]
        k_prev, v_prev = k_cur, v_cur
        mask = window & (c >= jnp.where(first, CHUNK, 0)) if blk == 0 else window

        scores = [lax.dot_general(q[rows, pair * 2 * HEAD_DIM:(pair + 1) * 2 * HEAD_DIM],
                                  k_rhs[kv_of(pair)], (((1,), (1,)), ((), ())),
                                  preferred_element_type=F32) for pair in range(n_pairs)]
        probs, inv_denoms = [], []
        for pair in range(n_pairs):
            sc2 = scores[pair] + bias_ref[pair]
            sides = []
            for side in range(2):
                sc = jnp.where(mask, sc2[:, side * 2 * CHUNK:(side + 1) * 2 * CHUNK], neg)
                sink = sink_ref[2 * pair + side]
                m = jnp.maximum(jnp.max(sc, axis=-1, keepdims=True), sink)
                p = jnp.exp(sc - m)
                denom = jnp.sum(p, axis=-1, keepdims=True) + jnp.exp(sink - m)
                sides.append(p.astype(BF16))
                inv_denoms.append(jnp.broadcast_to(1.0 / denom, (CHUNK, HEAD_DIM)))
            probs.append(jnp.concatenate(sides, axis=1))
        outs = [jnp.dot(probs[pair], v_rhs[kv_of(pair)], preferred_element_type=F32)
                for pair in range(n_pairs)]
        o = jnp.concatenate(outs, axis=1) * jnp.concatenate(inv_denoms, axis=1)
        o_blocks.append(o.astype(BF16))
    for idx in range(2 * N_KV):
        kprev_ref[idx] = k_prev[idx]
        vprev_ref[idx] = v_prev[idx]
    o = jnp.concatenate(o_blocks, axis=0)
    m_out = jnp.dot(o, wo_ref[...], preferred_element_type=F32) + bo_ref[...]
    x1 = _layer_norm(ALPHA * x + m_out, g1_ref[...], b1_ref[...])
    x1_ref[...] = x1
    _route_tile(x1, wr_ref, tri_ref, route_ref, route_t_ref, count_ref, base_ref)


def _t5_bucket(rel):
    n = jnp.maximum(rel, 0)
    max_exact = N_BUCKETS // 2
    nf = jnp.maximum(n, 1).astype(F32)
    large = max_exact + (jnp.log(nf / max_exact) / math.log(MAX_DISTANCE / max_exact)
                         * (N_BUCKETS - max_exact)).astype(I32)
    large = jnp.minimum(large, N_BUCKETS - 1)
    return jnp.where(n < max_exact, n, large)


def _rel_bias(rel_table):
    a = jnp.arange(CHUNK)[:, None]
    c = jnp.arange(2 * CHUNK)[None, :]
    onehot = jax.nn.one_hot(_t5_bucket(a + CHUNK - c), N_BUCKETS, dtype=F32)
    bias = jnp.einsum('acb,bh->hac', onehot, rel_table.astype(F32),
                      precision=lax.Precision.HIGHEST)
    bias = bias.reshape(N_HEADS // 2, 2, CHUNK, 2 * CHUNK).transpose(0, 2, 1, 3)
    return bias.reshape(N_HEADS // 2, CHUNK, 4 * CHUNK)


def _attn_layer(x, bias, w_qkv, b_qkv, sinks, w_o, b_o, g1, b1, w_router):
    t = x.shape[0]
    tm = ATTN_ROWS
    const2 = lambda i: (0, 0)
    r_in, r_out, r_shape, r_scratch = _route_specs(t, tm)
    return pl.pallas_call(
        _attn_kernel,
        grid=(t // tm,),
        in_specs=[
            pl.BlockSpec(memory_space=pltpu.SMEM),
            pl.BlockSpec((tm, D_MODEL), lambda i: (i, 0)),
            pl.BlockSpec((D_MODEL, QKV_DIM), const2),
            pl.BlockSpec((1, QKV_DIM), const2),
            pl.BlockSpec((N_HEADS // 2, CHUNK, 4 * CHUNK), lambda i: (0, 0, 0)),
            pl.BlockSpec((ATT_OUT, D_MODEL), const2),
            pl.BlockSpec((1, D_MODEL), const2),
            pl.BlockSpec((1, D_MODEL), const2),
            pl.BlockSpec((1, D_MODEL), const2),
        ] + r_in,
        out_specs=[pl.BlockSpec((tm, D_MODEL), lambda i: (i, 0))] + r_out,
        out_shape=[jax.ShapeDtypeStruct((t, D_MODEL), F32)] + r_shape,
        scratch_shapes=[pltpu.VMEM((2 * N_KV, CHUNK, KV_WIDTH), BF16),
                        pltpu.VMEM((2 * N_KV, CHUNK, KV_WIDTH), BF16), r_scratch],
        compiler_params=pltpu.CompilerParams(
            dimension_semantics=("arbitrary",), vmem_limit_bytes=VMEM_LIMIT),
        name="attn_layer",
    )(sinks, x, w_qkv.astype(BF16), b_qkv.reshape(1, -1), bias, w_o.astype(BF16),
      b_o.reshape(1, -1), g1.reshape(1, -1), b1.reshape(1, -1), *_route_operands(*w_router, tm))


def _route_tile(x1, wr_ref, tri_ref, route_ref, route_t_ref, count_ref, base_ref):
    @pl.when(pl.program_id(0) == 0)
    def _():
        base_ref[...] = jnp.zeros_like(base_ref)

    tm = x1.shape[0]
    lt = lax.dot_general(wr_ref[...], x1.astype(BF16), (((1,), (1,)), ((), ())),
                         preferred_element_type=F32)
    sub = lax.broadcasted_iota(I32, (SUBLANES, tm), 0).astype(F32)
    ninf = -jnp.inf

    def first_argmax(vals):
        m = jnp.max(vals, axis=0, keepdims=True)
        idx = jnp.min(jnp.where(vals == m, sub, float(SUBLANES)), axis=0, keepdims=True)
        return m, idx

    is_g = sub < N_GROUPS
    g_rows = lt[0:SUBLANES]
    gmax, g_idx = first_argmax(jnp.where(is_g, g_rows, ninf))
    g_p = 1.0 / jnp.sum(jnp.where(is_g, jnp.exp(g_rows - gmax), 0.0), axis=0, keepdims=True)

    group_rows = lambda a, g: a[SUBLANES * (g + 1):SUBLANES * (g + 2)]
    el = group_rows(lt, 0)
    for g in range(1, N_GROUPS):
        el = jnp.where(g_idx == g, group_rows(lt, g), el)
    m1, j1 = first_argmax(el)
    m2, j2 = first_argmax(jnp.where(sub == j1, ninf, el))
    a2 = jnp.exp(m2 - m1)
    gate0 = g_p / (1.0 + a2)
    gate1 = g_p * a2 / (1.0 + a2)

    hit0 = [(g_idx == g) & (sub == j1) for g in range(N_GROUPS)]
    hit1 = [(g_idx == g) & (sub == j2) for g in range(N_GROUPS)]
    onehot = jnp.concatenate([jnp.where(h0 | h1, 1.0, 0.0) for h0, h1 in zip(hit0, hit1)],
                             axis=0)
    before = jnp.dot(onehot.astype(BF16), tri_ref[...], preferred_element_type=F32)
    before = before + base_ref[:, 0:1]

    def picked(hits):
        total = jnp.zeros((SUBLANES, tm), F32)
        for g in range(N_GROUPS):
            total = total + jnp.where(hits[g], before[SUBLANES * g:SUBLANES * (g + 1)], 0.0)
        return jnp.sum(total, axis=0, keepdims=True)

    rank0 = picked(hit0)
    rank1 = picked(hit1)
    base_ref[...] += jnp.sum(onehot, axis=1, keepdims=True)

    e_base = g_idx * EXPERTS_PER_GROUP
    rec_t = jnp.concatenate([e_base + j1, e_base + j2, rank0, rank1, gate0, gate1,
                             jnp.zeros((SUBLANES - 6, tm), F32)], axis=0)
    route_t_ref[...] = rec_t
    padded = jnp.concatenate([rec_t, jnp.zeros((LANES - SUBLANES, tm), F32)], axis=0)
    route_ref[...] = jnp.transpose(padded)
    count_ref[...] = base_ref[...]


def _route_specs(t, tm):
    const2 = lambda i: (0, 0)
    in_specs = [pl.BlockSpec((ROUTER_ROWS, D_MODEL), const2), pl.BlockSpec((tm, tm), const2)]
    out_specs = [
        pl.BlockSpec((tm, LANES), lambda i: (i, 0)),
        pl.BlockSpec((SUBLANES, tm), lambda i: (0, i)),
        pl.BlockSpec((N_EXPERTS, LANES), const2),
    ]
    out_shape = [
        jax.ShapeDtypeStruct((t, LANES), F32),
        jax.ShapeDtypeStruct((SUBLANES, t), F32),
        jax.ShapeDtypeStruct((N_EXPERTS, LANES), F32),
    ]
    return in_specs, out_specs, out_shape, pltpu.VMEM((N_EXPERTS, LANES), F32)


def _route_operands(w_group, w_expert, tm):
    pad = jnp.zeros((D_MODEL, SUBLANES - N_GROUPS), F32)
    w_rows = jnp.concatenate([w_group, pad, w_expert], axis=1).T
    return w_rows.astype(BF16), jnp.triu(jnp.ones((tm, tm), BF16), 1)


def _plan(route_t, count_lanes):
    t = route_t.shape[1]
    rb = EXPERT_ROWS
    experts = jnp.arange(N_EXPERTS, dtype=I32)
    counts = count_lanes[:, 0].astype(I32)
    end = jnp.cumsum(counts)
    start = end - counts
    e = route_t[R_E0:R_E1 + 1].astype(I32)
    rank = route_t[R_RANK0:R_RANK1 + 1].astype(I32)
    onehot = e[:, None, :] == experts[None, :, None]
    dest = jnp.sum(jnp.where(onehot, start[None, :, None], 0), axis=1) + rank

    first_blk = start // rb
    last_blk = jnp.maximum(end - 1, 0) // rb
    n_steps_e = jnp.where(counts > 0, last_blk - first_blk + 1, 0)
    step_end = jnp.cumsum(n_steps_e)
    step_start = step_end - n_steps_e
    n_steps = step_end[-1]
    max_steps = t * TOP_K // rb + N_EXPERTS - 1
    s = jnp.minimum(jnp.arange(max_steps, dtype=I32), n_steps - 1)
    step_e = jnp.minimum(jnp.sum((step_end[None, :] <= s[:, None]).astype(I32), axis=1),
                         N_EXPERTS - 1)
    pick = step_e[:, None] == experts[None, :]
    take = lambda v: jnp.sum(jnp.where(pick, v[None, :], 0), axis=1)
    step_blk = take(first_blk) + s - take(step_start)
    step_lo = jnp.clip(take(start) - step_blk * rb, 0, rb)
    step_hi = jnp.clip(take(end) - step_blk * rb, 0, rb)
    return dest, (step_blk, step_e, step_lo, step_hi, n_steps.reshape(1))


def _dest_blocks(dest, tm):
    t = dest.shape[1]
    return dest.reshape(TOP_K, t // tm, tm).transpose(1, 0, 2).reshape(t // tm, 1, TOP_K * tm)


def _to_row_tiles(x):
    return x.reshape(x.shape[0], ROW_TILE[0], ROW_TILE[1])


def _from_row_tiles(x):
    return x.reshape(x.shape[0], D_MODEL)


PACKED_ROW_TILE = (SUBLANES // 2, LANES)
HIGH_HALF = 0xFFFF0000


def _pack_rows(x):
    half = D_MODEL // 2
    bits = lambda v: lax.bitcast_convert_type(v.astype(BF16).astype(F32), U32)
    words = (bits(x[:, :half]) >> 16) | (bits(x[:, half:]) & jnp.uint32(HIGH_HALF))
    return words.reshape((x.shape[0],) + PACKED_ROW_TILE)


def _unpack_rows(words):
    w = words.reshape(words.shape[0], D_MODEL // 2)
    low = lax.bitcast_convert_type(w << 16, F32).astype(BF16)
    high = lax.bitcast_convert_type(w & jnp.uint32(HIGH_HALF), F32).astype(BF16)
    return jnp.concatenate([low, high], axis=1)


def _dispatch_kernel(dest_ref, x1_ref, xs_hbm, xbuf, row_sems):
    tm = x1_ref.shape[0]
    i = pl.program_id(0)
    n = pl.num_programs(0)
    slot = i % DISPATCH_SLOTS

    def drain_rows(s):
        for k in range(TOP_K):
            pltpu.make_async_copy(xbuf.at[s], xs_hbm.at[pl.ds(0, tm)], row_sems.at[s]).wait()

    @pl.when(i >= DISPATCH_SLOTS)
    def _():
        drain_rows(slot)

    xbuf[slot] = _pack_rows(x1_ref[...])

    def issue(g, carry):
        for j in range(ROWS_PER_ISSUE):
            r = g * ROWS_PER_ISSUE + j
            for k in range(TOP_K):
                pltpu.make_async_copy(xbuf.at[slot, r], xs_hbm.at[dest_ref[0, 0, k * tm + r]],
                                      row_sems.at[slot]).start(priority=k)
        return carry

    lax.fori_loop(0, tm // ROWS_PER_ISSUE, issue, 0)

    @pl.when(i == n - 1)
    def _():
        for back in range(DISPATCH_SLOTS):
            drain_rows((i - back) % DISPATCH_SLOTS)


def _dispatch(x1, dest):
    t = x1.shape[0]
    tm = DISPATCH_ROWS
    return pl.pallas_call(
        _dispatch_kernel,
        grid=(t // tm,),
        in_specs=[
            pl.BlockSpec((1, 1, TOP_K * tm), lambda i: (i, 0, 0), memory_space=pltpu.SMEM),
            pl.BlockSpec((tm, D_MODEL), lambda i: (i, 0)),
        ],
        out_specs=pl.BlockSpec(memory_space=pl.ANY),
        out_shape=jax.ShapeDtypeStruct((t * TOP_K,) + PACKED_ROW_TILE, U32),
        scratch_shapes=[
            pltpu.VMEM((DISPATCH_SLOTS, tm) + PACKED_ROW_TILE, U32),
            pltpu.SemaphoreType.DMA((DISPATCH_SLOTS,)),
        ],
        compiler_params=pltpu.CompilerParams(
            dimension_semantics=("arbitrary",), vmem_limit_bytes=VMEM_LIMIT),
        name="dispatch",
    )(_dest_blocks(dest, tm), x1)


def _expert_kernel(blk_ref, e_ref, lo_ref, hi_ref, n_steps_ref, xs_ref, wg_ref, wu_ref, wd_ref,
                   ys_ref, wg_bf, wu_bf, wd_bf):
    s = pl.program_id(0)
    prev = jnp.maximum(s - 1, 0)
    new_expert = (s == 0) | (e_ref[s] != e_ref[prev])
    new_block = (s == 0) | (blk_ref[s] != blk_ref[prev])

    @pl.when(new_expert)
    def _():
        wg_bf[...] = wg_ref[0, 0].astype(BF16)
        wu_bf[...] = wu_ref[0, 0].astype(BF16)
        wd_bf[...] = wd_ref[0, 0].astype(BF16)

    rb = xs_ref.shape[0]
    half = rb // 2
    lo, hi = lo_ref[s], hi_ref[s]
    active = s < n_steps_ref[0]
    both_halves = (lo < half) & (hi > half)

    def masked_ffn(x_tiles, row0):
        xb = _unpack_rows(x_tiles)
        gate = jnp.dot(xb, wg_bf[...], preferred_element_type=F32)
        up = jnp.dot(xb, wu_bf[...], preferred_element_type=F32)
        hid = (gate * jax.nn.sigmoid(gate) * up).astype(BF16)
        y = jnp.dot(hid, wd_bf[...], preferred_element_type=F32)
        rows = lax.broadcasted_iota(I32, y.shape, 0) + row0
        return _to_row_tiles(jnp.where((rows >= lo) & (rows < hi), y, 0.0))

    @pl.when(active & both_halves)
    def _():
        y = masked_ffn(xs_ref[...], 0)

        @pl.when(new_block)
        def _():
            ys_ref[...] = y

        @pl.when(jnp.logical_not(new_block))
        def _():
            ys_ref[...] += y

    @pl.when(active & jnp.logical_not(both_halves))
    def _():
        off = pl.multiple_of(jnp.where(lo < half, 0, half), half)
        y = masked_ffn(xs_ref[pl.ds(off, half)], off)

        @pl.when(new_block)
        def _():
            ys_ref[pl.ds(off, half)] = y
            ys_ref[pl.ds(half - off, half)] = jnp.zeros_like(y)

        @pl.when(jnp.logical_not(new_block))
        def _():
            ys_ref[pl.ds(off, half)] += y


def _expert_ffn(xs, steps, layer, wg, wu, wd):
    step_blk, step_e, step_lo, step_hi, n_steps = steps
    rb = EXPERT_ROWS
    row_map = lambda s, blk, e, lo, hi, n: (blk[s], 0, 0)
    w_map = lambda s, blk, e, lo, hi, n: (layer, e[s], 0, 0)
    grid_spec = pltpu.PrefetchScalarGridSpec(
        num_scalar_prefetch=5,
        grid=(step_blk.shape[0],),
        in_specs=[
            pl.BlockSpec((rb,) + PACKED_ROW_TILE, row_map),
            pl.BlockSpec((1, 1, D_MODEL, D_EXPERT), w_map),
            pl.BlockSpec((1, 1, D_MODEL, D_EXPERT), w_map),
            pl.BlockSpec((1, 1, D_EXPERT, D_MODEL), w_map),
        ],
        out_specs=pl.BlockSpec((rb,) + ROW_TILE, row_map),
        scratch_shapes=[
            pltpu.VMEM((D_MODEL, D_EXPERT), BF16),
            pltpu.VMEM((D_MODEL, D_EXPERT), BF16),
            pltpu.VMEM((D_EXPERT, D_MODEL), BF16),
        ],
    )
    return pl.pallas_call(
        _expert_kernel,
        grid_spec=grid_spec,
        out_shape=jax.ShapeDtypeStruct((xs.shape[0],) + ROW_TILE, F32),
        compiler_params=pltpu.CompilerParams(
            dimension_semantics=("arbitrary",), vmem_limit_bytes=VMEM_LIMIT),
        name="expert_ffn",
    )(step_blk, step_e, step_lo, step_hi, n_steps, xs, wg, wu, wd)


def _combine_kernel(dest_ref, dest_next_ref, x1_ref, route_ref, g2_ref, b2_ref, ys_hbm,
                    x2_ref, ybuf, sems):
    tm = x1_ref.shape[0]
    i = pl.program_id(0)
    n = pl.num_programs(0)
    slot = i % 2

    def gather(d_ref, to_slot):
        def issue(g, carry):
            for j in range(ROWS_PER_ISSUE):
                r = g * ROWS_PER_ISSUE + j
                for k in range(TOP_K):
                    pltpu.make_async_copy(ys_hbm.at[d_ref[0, 0, k * tm + r]],
                                          ybuf.at[to_slot, k, r],
                                          sems.at[to_slot]).start(priority=k)
            return carry
        lax.fori_loop(0, tm // ROWS_PER_ISSUE, issue, 0)

    @pl.when(i == 0)
    def _():
        gather(dest_ref, slot)

    @pl.when(i + 1 < n)
    def _():
        gather(dest_next_ref, 1 - slot)

    for k in range(TOP_K):
        pltpu.make_async_copy(ys_hbm.at[pl.ds(0, tm)], ybuf.at[slot, k], sems.at[slot]).wait()

    route = route_ref[...]
    f = (route[:, R_GATE0:R_GATE0 + 1] * _from_row_tiles(ybuf[slot, 0])
         + route[:, R_GATE1:R_GATE1 + 1] * _from_row_tiles(ybuf[slot, 1]))
    x2_ref[...] = _layer_norm(ALPHA * x1_ref[...] + f, g2_ref[...], b2_ref[...])


def _combine_ln(x1, ys, dest, route, g2, b2):
    t = x1.shape[0]
    tm = COMBINE_ROWS
    n = t // tm
    row = lambda i: (i, 0)
    const2 = lambda i: (0, 0)
    dest_blocks = _dest_blocks(dest, tm)
    smem_block = lambda index_map: pl.BlockSpec((1, 1, TOP_K * tm), index_map,
                                                memory_space=pltpu.SMEM)
    return pl.pallas_call(
        _combine_kernel,
        grid=(n,),
        in_specs=[
            smem_block(lambda i: (i, 0, 0)),
            smem_block(lambda i: (jnp.minimum(i + 1, n - 1), 0, 0)),
            pl.BlockSpec((tm, D_MODEL), row),
            pl.BlockSpec((tm, LANES), row),
            pl.BlockSpec((1, D_MODEL), const2),
            pl.BlockSpec((1, D_MODEL), const2),
            pl.BlockSpec(memory_space=pl.ANY),
        ],
        out_specs=pl.BlockSpec((tm, D_MODEL), row),
        out_shape=jax.ShapeDtypeStruct((t, D_MODEL), F32),
        scratch_shapes=[
            pltpu.VMEM((2, TOP_K, tm) + ROW_TILE, F32),
            pltpu.SemaphoreType.DMA((2,)),
        ],
        compiler_params=pltpu.CompilerParams(
            dimension_semantics=("arbitrary",), vmem_limit_bytes=VMEM_LIMIT),
        name="combine_ln",
    )(dest_blocks, dest_blocks, x1, route, g2.reshape(1, -1), b2.reshape(1, -1), ys)


def _moe_layer(x1, route, route_t, counts, layer, wg, wu, wd, g2, b2):
    dest, steps = _plan(route_t, counts)
    xs = _dispatch(x1, dest)
    ys = _expert_ffn(xs, steps, layer, wg, wu, wd)
    return _combine_ln(x1, ys, dest, route, g2, b2)


def kernel(x, rel_bias_table, mix_w_in, gmlp_ln_g, gmlp_ln_b, gmlp_w_spatial, gmlp_b_spatial, conv_w, mix_w_out, attn_w_qkv, attn_b_qkv, attn_sinks, attn_w_o, attn_b_o, ln1_g, ln1_b, ln2_g, ln2_b, router_group, router_expert, expert_w_gate, expert_w_up, expert_w_down):
    bsz, s, d = x.shape
    assert (s, d) == (SEQ, D_MODEL)
    x = x.reshape(bsz * s, d)
    bias = _rel_bias(rel_bias_table)
    for l in range(DEPTH):
        i = l // 2
        w_router = (router_group[l], router_expert[l])
        if l % 2 == 0:
            x1, *routing = _mixer_layer(x, mix_w_in[i], gmlp_ln_g[i], gmlp_ln_b[i],
                                        gmlp_w_spatial[i], gmlp_b_spatial[i], conv_w[i],
                                        mix_w_out[i], ln1_g[l], ln1_b[l], w_router)
        else:
            x1, *routing = _attn_layer(x, bias, attn_w_qkv[i], attn_b_qkv[i], attn_sinks[i],
                                       attn_w_o[i], attn_b_o[i], ln1_g[l], ln1_b[l], w_router)
        x = _moe_layer(x1, *routing, l, expert_w_gate, expert_w_up, expert_w_down,
                       ln2_g[l], ln2_b[l])
    return x.reshape(bsz, s, d)
```

```python
import math

import jax
import jax.numpy as jnp
from jax import lax
from jax.experimental import pallas as pl
from jax.experimental.pallas import tpu as pltpu

D_MODEL = 1024
SEQ = 16384
DEPTH = 4
CHUNK = 128
A_GROUPS = 4
A_CH = 128
A_WIDTH = A_GROUPS * A_CH
B_WIDTH = 512
CONV_W = 3
MIX_IN = 2 * A_WIDTH + 3 * B_WIDTH
N_HEADS = 16
N_KV = 2
HEAD_DIM = 64
GQA_GROUP = N_HEADS // N_KV
WINDOW = 128
QKV_DIM = (N_HEADS + 2 * N_KV) * HEAD_DIM
ATT_OUT = N_HEADS * HEAD_DIM
KV_WIDTH = N_KV * HEAD_DIM
N_BUCKETS = 32
MAX_DISTANCE = 128
N_GROUPS = 4
EXPERTS_PER_GROUP = 8
N_EXPERTS = N_GROUPS * EXPERTS_PER_GROUP
TOP_K = 2
D_EXPERT = 512
ALPHA = (2 * DEPTH) ** 0.25
LN_EPS = 1e-5

LANES = 128
SUBLANES = 8
ROW_TILE = (SUBLANES, LANES)
assert D_MODEL == SUBLANES * LANES
ROUTER_ROWS = SUBLANES + N_EXPERTS
MIX_ROWS = 1024
ATTN_ROWS = 1024
DISPATCH_ROWS = 2048
COMBINE_ROWS = 1024
EXPERT_ROWS = 1024
ROWS_PER_ISSUE = 16
DISPATCH_SLOTS = 3
VMEM_LIMIT = 56 * 1024 * 1024

R_E0, R_E1, R_RANK0, R_RANK1, R_GATE0, R_GATE1 = range(6)

F32 = jnp.float32
BF16 = jnp.bfloat16
I32 = jnp.int32
U32 = jnp.uint32


def _layer_norm(x, g, b):
    mu = jnp.mean(x, axis=-1, keepdims=True)
    xc = x - mu
    var = jnp.mean(xc * xc, axis=-1, keepdims=True)
    return xc * lax.rsqrt(var + LN_EPS) * g + b


def _gelu(x):
    return 0.5 * x * (1.0 + lax.erf(x * (2.0 ** -0.5)))


def _mixer_kernel(x_ref, win_ref, lng_ref, lnb_ref, wsp_ref, bsp_ref, cw_ref, wout_ref,
                  g1_ref, b1_ref, wr_ref, tri_ref, x1_ref, route_ref, route_t_ref, count_ref,
                  ztail_ref, base_ref):
    tm = x_ref.shape[0]
    i = pl.program_id(0)

    @pl.when(i % (SEQ // tm) == 0)
    def _():
        ztail_ref[...] = jnp.zeros_like(ztail_ref)

    x = x_ref[...]
    h = jnp.dot(x.astype(BF16), win_ref[...], preferred_element_type=F32)
    u = _gelu(h[:, :A_WIDTH])
    v = _gelu(h[:, A_WIDTH:2 * A_WIDTH])
    o = 2 * A_WIDTH
    g_b = h[:, o:o + B_WIDTH]
    g_c = h[:, o + B_WIDTH:o + 2 * B_WIDTH]
    hb = h[:, o + 2 * B_WIDTH:o + 3 * B_WIDTH]

    v = _layer_norm(v, lng_ref[...], lnb_ref[...]).astype(BF16)
    n_chunks = tm // CHUNK
    ri = lax.broadcasted_iota(I32, (CHUNK, CHUNK), 0)
    ci = lax.broadcasted_iota(I32, (CHUNK, CHUNK), 1)
    causal = ci <= ri
    sv_cols = [[None] * A_GROUPS for _ in range(n_chunks)]
    for g in range(A_GROUPS):
        ws = jnp.where(causal, wsp_ref[g], 0.0).astype(BF16)
        vg = jnp.concatenate(
            [v[c * CHUNK:(c + 1) * CHUNK, g * A_CH:(g + 1) * A_CH] for c in range(n_chunks)],
            axis=1)
        sg = jnp.dot(ws, vg, preferred_element_type=F32) + bsp_ref[:, g:g + 1]
        for c in range(n_chunks):
            sv_cols[c][g] = sg[:, c * A_CH:(c + 1) * A_CH]
    sv = jnp.concatenate([jnp.concatenate(row, axis=1) for row in sv_cols], axis=0)
    y_a = u * sv

    z = g_c * hb
    rows = lax.broadcasted_iota(I32, z.shape, 0)
    tail = ztail_ref[...]
    zm1 = jnp.where(rows == 0, tail[7:8, :], pltpu.roll(z, 1, 0))
    zm2 = jnp.where(rows == 0, tail[6:7, :],
                    jnp.where(rows == 1, tail[7:8, :], pltpu.roll(z, 2, 0)))
    conv = cw_ref[0:1, :] * zm2 + cw_ref[1:2, :] * zm1 + cw_ref[2:3, :] * z
    y_b = g_b * conv
    ztail_ref[...] = z[tm - 8:tm, :]

    y = jnp.concatenate([y_a, y_b], axis=1).astype(BF16)
    m = jnp.dot(y, wout_ref[...], preferred_element_type=F32)
    x1 = _layer_norm(ALPHA * x + m, g1_ref[...], b1_ref[...])
    x1_ref[...] = x1
    _route_tile(x1, wr_ref, tri_ref, route_ref, route_t_ref, count_ref, base_ref)


def _mixer_layer(x, w_in, ln_g, ln_b, w_sp, b_sp, conv_w, w_out, g1, b1, w_router):
    t = x.shape[0]
    tm = MIX_ROWS
    const2 = lambda i: (0, 0)
    r_in, r_out, r_shape, r_scratch = _route_specs(t, tm)
    return pl.pallas_call(
        _mixer_kernel,
        grid=(t // tm,),
        in_specs=[
            pl.BlockSpec((tm, D_MODEL), lambda i: (i, 0)),
            pl.BlockSpec((D_MODEL, MIX_IN), const2),
            pl.BlockSpec((1, A_WIDTH), const2),
            pl.BlockSpec((1, A_WIDTH), const2),
            pl.BlockSpec((A_GROUPS, CHUNK, CHUNK), lambda i: (0, 0, 0)),
            pl.BlockSpec((CHUNK, A_GROUPS), const2),
            pl.BlockSpec((CONV_W, B_WIDTH), const2),
            pl.BlockSpec((A_WIDTH + B_WIDTH, D_MODEL), const2),
            pl.BlockSpec((1, D_MODEL), const2),
            pl.BlockSpec((1, D_MODEL), const2),
        ] + r_in,
        out_specs=[pl.BlockSpec((tm, D_MODEL), lambda i: (i, 0))] + r_out,
        out_shape=[jax.ShapeDtypeStruct((t, D_MODEL), F32)] + r_shape,
        scratch_shapes=[pltpu.VMEM((8, B_WIDTH), F32), r_scratch],
        compiler_params=pltpu.CompilerParams(
            dimension_semantics=("arbitrary",), vmem_limit_bytes=VMEM_LIMIT),
        name="mixer_layer",
    )(x, w_in.astype(BF16), ln_g.reshape(1, -1), ln_b.reshape(1, -1), w_sp, b_sp.T,
      conv_w, w_out.astype(BF16), g1.reshape(1, -1), b1.reshape(1, -1),
      *_route_operands(*w_router, tm))


def _attn_kernel(sink_ref, x_ref, wqkv_ref, bqkv_ref, bias_ref, wo_ref, bo_ref,
                 g1_ref, b1_ref, wr_ref, tri_ref, x1_ref, route_ref, route_t_ref, count_ref,
                 kprev_ref, vprev_ref, base_ref):
    tm = x_ref.shape[0]
    i = pl.program_id(0)
    first = i % (SEQ // tm) == 0

    @pl.when(first)
    def _():
        kprev_ref[...] = jnp.zeros_like(kprev_ref)
        vprev_ref[...] = jnp.zeros_like(vprev_ref)

    x = x_ref[...]
    qkv = jnp.dot(x.astype(BF16), wqkv_ref[...], preferred_element_type=F32) + bqkv_ref[...]
    q = (qkv[:, :ATT_OUT] * (HEAD_DIM ** -0.5)).astype(BF16)

    lane = lax.broadcasted_iota(I32, (CHUNK, KV_WIDTH), 1)
    low = lane < HEAD_DIM

    def halves(t):
        swapped = pltpu.roll(t, HEAD_DIM, 1)
        zero = jnp.zeros_like(t)
        return [jnp.where(low, t, zero).astype(BF16), jnp.where(low, zero, swapped).astype(BF16),
                jnp.where(low, swapped, zero).astype(BF16), jnp.where(low, zero, t).astype(BF16)]

    def stacked(prev, cur, j):
        return jnp.concatenate([prev[2 * j], cur[2 * j], prev[2 * j + 1], cur[2 * j + 1]], axis=0)

    a = lax.broadcasted_iota(I32, (CHUNK, 2 * CHUNK), 0)
    c = lax.broadcasted_iota(I32, (CHUNK, 2 * CHUNK), 1)
    window = (c > a) & (c <= a + WINDOW)
    neg = jnp.finfo(F32).min
    n_pairs = N_HEADS // 2
    kv_of = lambda pair: (2 * pair) // GQA_GROUP

    k_prev = [kprev_ref[idx] for idx in range(2 * N_KV)]
    v_prev = [vprev_ref[idx] for idx in range(2 * N_KV)]
    o_blocks = []
    for blk in range(tm // CHUNK):
        rows = slice(blk * CHUNK, (blk + 1) * CHUNK)
        k_cur = halves(qkv[rows, ATT_OUT:ATT_OUT + KV_WIDTH])
        v_cur = halves(qkv[rows, ATT_OUT + KV_WIDTH:])
        k_rhs = [stacked(k_prev, k_cur, j) for j in range(N_KV)]
        v_rhs = [stacked(v_prev, v_cur, j) for j in range(N_KV)]
        k_prev, v_prev = k_cur, v_cur
        mask = window & (c >= jnp.where(first, CHUNK, 0)) if blk == 0 else window

        scores = [lax.dot_general(q[rows, pair * 2 * HEAD_DIM:(pair + 1) * 2 * HEAD_DIM],
                                  k_rhs[kv_of(pair)], (((1,), (1,)), ((), ())),
                                  preferred_element_type=F32) for pair in range(n_pairs)]
        probs, inv_denoms = [], []
        for pair in range(n_pairs):
            sc2 = scores[pair] + bias_ref[pair]
            sides = []
            for side in range(2):
                sc = jnp.where(mask, sc2[:, side * 2 * CHUNK:(side + 1) * 2 * CHUNK], neg)
                sink = sink_ref[2 * pair + side]
                m = jnp.maximum(jnp.max(sc, axis=-1, keepdims=True), sink)
                p = jnp.exp(sc - m)
                denom = jnp.sum(p, axis=-1, keepdims=True) + jnp.exp(sink - m)
                sides.append(p.astype(BF16))
                inv_denoms.append(jnp.broadcast_to(1.0 / denom, (CHUNK, HEAD_DIM)))
            probs.append(jnp.concatenate(sides, axis=1))
        outs = [jnp.dot(probs[pair], v_rhs[kv_of(pair)], preferred_element_type=F32)
                for pair in range(n_pairs)]
        o = jnp.concatenate(outs, axis=1) * jnp.concatenate(inv_denoms, axis=1)
        o_blocks.append(o.astype(BF16))
    for idx in range(2 * N_KV):
        kprev_ref[idx] = k_prev[idx]
        vprev_ref[idx] = v_prev[idx]
    o = jnp.concatenate(o_blocks, axis=0)
    m_out = jnp.dot(o, wo_ref[...], preferred_element_type=F32) + bo_ref[...]
    x1 = _layer_norm(ALPHA * x + m_out, g1_ref[...], b1_ref[...])
    x1_ref[...] = x1
    _route_tile(x1, wr_ref, tri_ref, route_ref, route_t_ref, count_ref, base_ref)


def _t5_bucket(rel):
    n = jnp.maximum(rel, 0)
    max_exact = N_BUCKETS // 2
    nf = jnp.maximum(n, 1).astype(F32)
    large = max_exact + (jnp.log(nf / max_exact) / math.log(MAX_DISTANCE / max_exact)
                         * (N_BUCKETS - max_exact)).astype(I32)
    large = jnp.minimum(large, N_BUCKETS - 1)
    return jnp.where(n < max_exact, n, large)


def _rel_bias(rel_table):
    a = jnp.arange(CHUNK)[:, None]
    c = jnp.arange(2 * CHUNK)[None, :]
    onehot = jax.nn.one_hot(_t5_bucket(a + CHUNK - c), N_BUCKETS, dtype=F32)
    bias = jnp.einsum('acb,bh->hac', onehot, rel_table.astype(F32),
                      precision=lax.Precision.HIGHEST)
    bias = bias.reshape(N_HEADS // 2, 2, CHUNK, 2 * CHUNK).transpose(0, 2, 1, 3)
    return bias.reshape(N_HEADS // 2, CHUNK, 4 * CHUNK)


def _attn_layer(x, bias, w_qkv, b_qkv, sinks, w_o, b_o, g1, b1, w_router):
    t = x.shape[0]
    tm = ATTN_ROWS
    const2 = lambda i: (0, 0)
    r_in, r_out, r_shape, r_scratch = _route_specs(t, tm)
    return pl.pallas_call(
        _attn_kernel,
        grid=(t // tm,),
        in_specs=[
            pl.BlockSpec(memory_space=pltpu.SMEM),
            pl.BlockSpec((tm, D_MODEL), lambda i: (i, 0)),
            pl.BlockSpec((D_MODEL, QKV_DIM), const2),
            pl.BlockSpec((1, QKV_DIM), const2),
            pl.BlockSpec((N_HEADS // 2, CHUNK, 4 * CHUNK), lambda i: (0, 0, 0)),
            pl.BlockSpec((ATT_OUT, D_MODEL), const2),
            pl.BlockSpec((1, D_MODEL), const2),
            pl.BlockSpec((1, D_MODEL), const2),
            pl.BlockSpec((1, D_MODEL), const2),
        ] + r_in,
        out_specs=[pl.BlockSpec((tm, D_MODEL), lambda i: (i, 0))] + r_out,
        out_shape=[jax.ShapeDtypeStruct((t, D_MODEL), F32)] + r_shape,
        scratch_shapes=[pltpu.VMEM((2 * N_KV, CHUNK, KV_WIDTH), BF16),
                        pltpu.VMEM((2 * N_KV, CHUNK, KV_WIDTH), BF16), r_scratch],
        compiler_params=pltpu.CompilerParams(
            dimension_semantics=("arbitrary",), vmem_limit_bytes=VMEM_LIMIT),
        name="attn_layer",
    )(sinks, x, w_qkv.astype(BF16), b_qkv.reshape(1, -1), bias, w_o.astype(BF16),
      b_o.reshape(1, -1), g1.reshape(1, -1), b1.reshape(1, -1), *_route_operands(*w_router, tm))


def _route_tile(x1, wr_ref, tri_ref, route_ref, route_t_ref, count_ref, base_ref):
    @pl.when(pl.program_id(0) == 0)
    def _():
        base_ref[...] = jnp.zeros_like(base_ref)

    tm = x1.shape[0]
    lt = lax.dot_general(wr_ref[...], x1.astype(BF16), (((1,), (1,)), ((), ())),
                         preferred_element_type=F32)
    sub = lax.broadcasted_iota(I32, (SUBLANES, tm), 0).astype(F32)
    ninf = -jnp.inf

    def first_argmax(vals):
        m = jnp.max(vals, axis=0, keepdims=True)
        idx = jnp.min(jnp.where(vals == m, sub, float(SUBLANES)), axis=0, keepdims=True)
        return m, idx

    is_g = sub < N_GROUPS
    g_rows = lt[0:SUBLANES]
    gmax, g_idx = first_argmax(jnp.where(is_g, g_rows, ninf))
    g_p = 1.0 / jnp.sum(jnp.where(is_g, jnp.exp(g_rows - gmax), 0.0), axis=0, keepdims=True)

    group_rows = lambda a, g: a[SUBLANES * (g + 1):SUBLANES * (g + 2)]
    el = group_rows(lt, 0)
    for g in range(1, N_GROUPS):
        el = jnp.where(g_idx == g, group_rows(lt, g), el)
    m1, j1 = first_argmax(el)
    m2, j2 = first_argmax(jnp.where(sub == j1, ninf, el))
    a2 = jnp.exp(m2 - m1)
    gate0 = g_p / (1.0 + a2)
    gate1 = g_p * a2 / (1.0 + a2)

    hit0 = [(g_idx == g) & (sub == j1) for g in range(N_GROUPS)]
    hit1 = [(g_idx == g) & (sub == j2) for g in range(N_GROUPS)]
    onehot = jnp.concatenate([jnp.where(h0 | h1, 1.0, 0.0) for h0, h1 in zip(hit0, hit1)],
                             axis=0)
    before = jnp.dot(onehot.astype(BF16), tri_ref[...], preferred_element_type=F32)
    before = before + base_ref[:, 0:1]

    def picked(hits):
        total = jnp.zeros((SUBLANES, tm), F32)
        for g in range(N_GROUPS):
            total = total + jnp.where(hits[g], before[SUBLANES * g:SUBLANES * (g + 1)], 0.0)
        return jnp.sum(total, axis=0, keepdims=True)

    rank0 = picked(hit0)
    rank1 = picked(hit1)
    base_ref[...] += jnp.sum(onehot, axis=1, keepdims=True)

    e_base = g_idx * EXPERTS_PER_GROUP
    rec_t = jnp.concatenate([e_base + j1, e_base + j2, rank0, rank1, gate0, gate1,
                             jnp.zeros((SUBLANES - 6, tm), F32)], axis=0)
    route_t_ref[...] = rec_t
    padded = jnp.concatenate([rec_t, jnp.zeros((LANES - SUBLANES, tm), F32)], axis=0)
    route_ref[...] = jnp.transpose(padded)
    count_ref[...] = base_ref[...]


def _route_specs(t, tm):
    const2 = lambda i: (0, 0)
    in_specs = [pl.BlockSpec((ROUTER_ROWS, D_MODEL), const2), pl.BlockSpec((tm, tm), const2)]
    out_specs = [
        pl.BlockSpec((tm, LANES), lambda i: (i, 0)),
        pl.BlockSpec((SUBLANES, tm), lambda i: (0, i)),
        pl.BlockSpec((N_EXPERTS, LANES), const2),
    ]
    out_shape = [
        jax.ShapeDtypeStruct((t, LANES), F32),
        jax.ShapeDtypeStruct((SUBLANES, t), F32),
        jax.ShapeDtypeStruct((N_EXPERTS, LANES), F32),
    ]
    return in_specs, out_specs, out_shape, pltpu.VMEM((N_EXPERTS, LANES), F32)


def _route_operands(w_group, w_expert, tm):
    pad = jnp.zeros((D_MODEL, SUBLANES - N_GROUPS), F32)
    w_rows = jnp.concatenate([w_group, pad, w_expert], axis=1).T
    return w_rows.astype(BF16), jnp.triu(jnp.ones((tm, tm), BF16), 1)


def _plan(route_t, count_lanes):
    t = route_t.shape[1]
    rb = EXPERT_ROWS
    experts = jnp.arange(N_EXPERTS, dtype=I32)
    counts = count_lanes[:, 0].astype(I32)
    end = jnp.cumsum(counts)
    start = end - counts
    e = route_t[R_E0:R_E1 + 1].astype(I32)
    rank = route_t[R_RANK0:R_RANK1 + 1].astype(I32)
    onehot = e[:, None, :] == experts[None, :, None]
    dest = jnp.sum(jnp.where(onehot, start[None, :, None], 0), axis=1) + rank

    first_blk = start // rb
    last_blk = jnp.maximum(end - 1, 0) // rb
    n_steps_e = jnp.where(counts > 0, last_blk - first_blk + 1, 0)
    step_end = jnp.cumsum(n_steps_e)
    step_start = step_end - n_steps_e
    n_steps = step_end[-1]
    max_steps = t * TOP_K // rb + N_EXPERTS - 1
    s = jnp.minimum(jnp.arange(max_steps, dtype=I32), n_steps - 1)
    step_e = jnp.minimum(jnp.sum((step_end[None, :] <= s[:, None]).astype(I32), axis=1),
                         N_EXPERTS - 1)
    pick = step_e[:, None] == experts[None, :]
    take = lambda v: jnp.sum(jnp.where(pick, v[None, :], 0), axis=1)
    step_blk = take(first_blk) + s - take(step_start)
    step_lo = jnp.clip(take(start) - step_blk * rb, 0, rb)
    step_hi = jnp.clip(take(end) - step_blk * rb, 0, rb)
    return dest, (step_blk, step_e, step_lo, step_hi, n_steps.reshape(1))


def _dest_blocks(dest, tm):
    t = dest.shape[1]
    return dest.reshape(TOP_K, t // tm, tm).transpose(1, 0, 2).reshape(t // tm, 1, TOP_K * tm)


def _to_row_tiles(x):
    return x.reshape(x.shape[0], ROW_TILE[0], ROW_TILE[1])


def _from_row_tiles(x):
    return x.reshape(x.shape[0], D_MODEL)


PACKED_ROW_TILE = (SUBLANES // 2, LANES)
HIGH_HALF = 0xFFFF0000


def _pack_rows(x):
    half = D_MODEL // 2
    bits = lambda v: lax.bitcast_convert_type(v.astype(BF16).astype(F32), U32)
    words = (bits(x[:, :half]) >> 16) | (bits(x[:, half:]) & jnp.uint32(HIGH_HALF))
    return words.reshape((x.shape[0],) + PACKED_ROW_TILE)


def _unpack_rows(words):
    w = words.reshape(words.shape[0], D_MODEL // 2)
    low = lax.bitcast_convert_type(w << 16, F32).astype(BF16)
    high = lax.bitcast_convert_type(w & jnp.uint32(HIGH_HALF), F32).astype(BF16)
    return jnp.concatenate([low, high], axis=1)


def _dispatch_kernel(dest_ref, x1_ref, xs_hbm, xbuf, row_sems):
    tm = x1_ref.shape[0]
    i = pl.program_id(0)
    n = pl.num_programs(0)
    slot = i % DISPATCH_SLOTS

    def drain_rows(s):
        for k in range(TOP_K):
            pltpu.make_async_copy(xbuf.at[s], xs_hbm.at[pl.ds(0, tm)], row_sems.at[s]).wait()

    @pl.when(i >= DISPATCH_SLOTS)
    def _():
        drain_rows(slot)

    xbuf[slot] = _pack_rows(x1_ref[...])

    def issue(g, carry):
        for j in range(ROWS_PER_ISSUE):
            r = g * ROWS_PER_ISSUE + j
            for k in range(TOP_K):
                pltpu.make_async_copy(xbuf.at[slot, r], xs_hbm.at[dest_ref[0, 0, k * tm + r]],
                                      row_sems.at[slot]).start(priority=k)
        return carry

    lax.fori_loop(0, tm // ROWS_PER_ISSUE, issue, 0)

    @pl.when(i == n - 1)
    def _():
        for back in range(DISPATCH_SLOTS):
            drain_rows((i - back) % DISPATCH_SLOTS)


def _dispatch(x1, dest):
    t = x1.shape[0]
    tm = DISPATCH_ROWS
    return pl.pallas_call(
        _dispatch_kernel,
        grid=(t // tm,),
        in_specs=[
            pl.BlockSpec((1, 1, TOP_K * tm), lambda i: (i, 0, 0), memory_space=pltpu.SMEM),
            pl.BlockSpec((tm, D_MODEL), lambda i: (i, 0)),
        ],
        out_specs=pl.BlockSpec(memory_space=pl.ANY),
        out_shape=jax.ShapeDtypeStruct((t * TOP_K,) + PACKED_ROW_TILE, U32),
        scratch_shapes=[
            pltpu.VMEM((DISPATCH_SLOTS, tm) + PACKED_ROW_TILE, U32),
            pltpu.SemaphoreType.DMA((DISPATCH_SLOTS,)),
        ],
        compiler_params=pltpu.CompilerParams(
            dimension_semantics=("arbitrary",), vmem_limit_bytes=VMEM_LIMIT),
        name="dispatch",
    )(_dest_blocks(dest, tm), x1)


def _expert_kernel(blk_ref, e_ref, lo_ref, hi_ref, n_steps_ref, xs_ref, wg_ref, wu_ref, wd_ref,
                   ys_ref, wg_bf, wu_bf, wd_bf):
    s = pl.program_id(0)
    prev = jnp.maximum(s - 1, 0)
    new_expert = (s == 0) | (e_ref[s] != e_ref[prev])
    new_block = (s == 0) | (blk_ref[s] != blk_ref[prev])

    @pl.when(new_expert)
    def _():
        wg_bf[...] = wg_ref[0, 0].astype(BF16)
        wu_bf[...] = wu_ref[0, 0].astype(BF16)
        wd_bf[...] = wd_ref[0, 0].astype(BF16)

    rb = xs_ref.shape[0]
    half = rb // 2
    lo, hi = lo_ref[s], hi_ref[s]
    active = s < n_steps_ref[0]
    both_halves = (lo < half) & (hi > half)

    def masked_ffn(x_tiles, row0):
        xb = _unpack_rows(x_tiles)
        gate = jnp.dot(xb, wg_bf[...], preferred_element_type=F32)
        up = jnp.dot(xb, wu_bf[...], preferred_element_type=F32)
        hid = (gate * jax.nn.sigmoid(gate) * up).astype(BF16)
        y = jnp.dot(hid, wd_bf[...], preferred_element_type=F32)
        rows = lax.broadcasted_iota(I32, y.shape, 0) + row0
        return _to_row_tiles(jnp.where((rows >= lo) & (rows < hi), y, 0.0))

    @pl.when(active & both_halves)
    def _():
        y = masked_ffn(xs_ref[...], 0)

        @pl.when(new_block)
        def _():
            ys_ref[...] = y

        @pl.when(jnp.logical_not(new_block))
        def _():
            ys_ref[...] += y

    @pl.when(active & jnp.logical_not(both_halves))
    def _():
        off = pl.multiple_of(jnp.where(lo < half, 0, half), half)
        y = masked_ffn(xs_ref[pl.ds(off, half)], off)

        @pl.when(new_block)
        def _():
            ys_ref[pl.ds(off, half)] = y
            ys_ref[pl.ds(half - off, half)] = jnp.zeros_like(y)

        @pl.when(jnp.logical_not(new_block))
        def _():
            ys_ref[pl.ds(off, half)] += y


def _expert_ffn(xs, steps, layer, wg, wu, wd):
    step_blk, step_e, step_lo, step_hi, n_steps = steps
    rb = EXPERT_ROWS
    row_map = lambda s, blk, e, lo, hi, n: (blk[s], 0, 0)
    w_map = lambda s, blk, e, lo, hi, n: (layer, e[s], 0, 0)
    grid_spec = pltpu.PrefetchScalarGridSpec(
        num_scalar_prefetch=5,
        grid=(step_blk.shape[0],),
        in_specs=[
            pl.BlockSpec((rb,) + PACKED_ROW_TILE, row_map),
            pl.BlockSpec((1, 1, D_MODEL, D_EXPERT), w_map),
            pl.BlockSpec((1, 1, D_MODEL, D_EXPERT), w_map),
            pl.BlockSpec((1, 1, D_EXPERT, D_MODEL), w_map),
        ],
        out_specs=pl.BlockSpec((rb,) + ROW_TILE, row_map),
        scratch_shapes=[
            pltpu.VMEM((D_MODEL, D_EXPERT), BF16),
            pltpu.VMEM((D_MODEL, D_EXPERT), BF16),
            pltpu.VMEM((D_EXPERT, D_MODEL), BF16),
        ],
    )
    return pl.pallas_call(
        _expert_kernel,
        grid_spec=grid_spec,
        out_shape=jax.ShapeDtypeStruct((xs.shape[0],) + ROW_TILE, F32),
        compiler_params=pltpu.CompilerParams(
            dimension_semantics=("arbitrary",), vmem_limit_bytes=VMEM_LIMIT),
        name="expert_ffn",
    )(step_blk, step_e, step_lo, step_hi, n_steps, xs, wg, wu, wd)


def _combine_kernel(dest_ref, dest_next_ref, x1_ref, route_ref, g2_ref, b2_ref, ys_hbm,
                    x2_ref, ybuf, sems):
    tm = x1_ref.shape[0]
    i = pl.program_id(0)
    n = pl.num_programs(0)
    slot = i % 2

    def gather(d_ref, to_slot):
        def issue(g, carry):
            for j in range(ROWS_PER_ISSUE):
                r = g * ROWS_PER_ISSUE + j
                for k in range(TOP_K):
                    pltpu.make_async_copy(ys_hbm.at[d_ref[0, 0, k * tm + r]],
                                          ybuf.at[to_slot, k, r],
                                          sems.at[to_slot]).start(priority=k)
            return carry
        lax.fori_loop(0, tm // ROWS_PER_ISSUE, issue, 0)

    @pl.when(i == 0)
    def _():
        gather(dest_ref, slot)

    @pl.when(i + 1 < n)
    def _():
        gather(dest_next_ref, 1 - slot)

    for k in range(TOP_K):
        pltpu.make_async_copy(ys_hbm.at[pl.ds(0, tm)], ybuf.at[slot, k], sems.at[slot]).wait()

    route = route_ref[...]
    f = (route[:, R_GATE0:R_GATE0 + 1] * _from_row_tiles(ybuf[slot, 0])
         + route[:, R_GATE1:R_GATE1 + 1] * _from_row_tiles(ybuf[slot, 1]))
    x2_ref[...] = _layer_norm(ALPHA * x1_ref[...] + f, g2_ref[...], b2_ref[...])


def _combine_ln(x1, ys, dest, route, g2, b2):
    t = x1.shape[0]
    tm = COMBINE_ROWS
    n = t // tm
    row = lambda i: (i, 0)
    const2 = lambda i: (0, 0)
    dest_blocks = _dest_blocks(dest, tm)
    smem_block = lambda index_map: pl.BlockSpec((1, 1, TOP_K * tm), index_map,
                                                memory_space=pltpu.SMEM)
    return pl.pallas_call(
        _combine_kernel,
        grid=(n,),
        in_specs=[
            smem_block(lambda i: (i, 0, 0)),
            smem_block(lambda i: (jnp.minimum(i + 1, n - 1), 0, 0)),
            pl.BlockSpec((tm, D_MODEL), row),
            pl.BlockSpec((tm, LANES), row),
            pl.BlockSpec((1, D_MODEL), const2),
            pl.BlockSpec((1, D_MODEL), const2),
            pl.BlockSpec(memory_space=pl.ANY),
        ],
        out_specs=pl.BlockSpec((tm, D_MODEL), row),
        out_shape=jax.ShapeDtypeStruct((t, D_MODEL), F32),
        scratch_shapes=[
            pltpu.VMEM((2, TOP_K, tm) + ROW_TILE, F32),
            pltpu.SemaphoreType.DMA((2,)),
        ],
        compiler_params=pltpu.CompilerParams(
            dimension_semantics=("arbitrary",), vmem_limit_bytes=VMEM_LIMIT),
        name="combine_ln",
    )(dest_blocks, dest_blocks, x1, route, g2.reshape(1, -1), b2.reshape(1, -1), ys)


def _moe_layer(x1, route, route_t, counts, layer, wg, wu, wd, g2, b2):
    dest, steps = _plan(route_t, counts)
    xs = _dispatch(x1, dest)
    ys = _expert_ffn(xs, steps, layer, wg, wu, wd)
    return _combine_ln(x1, ys, dest, route, g2, b2)


def kernel(x, rel_bias_table, mix_w_in, gmlp_ln_g, gmlp_ln_b, gmlp_w_spatial, gmlp_b_spatial, conv_w, mix_w_out, attn_w_qkv, attn_b_qkv, attn_sinks, attn_w_o, attn_b_o, ln1_g, ln1_b, ln2_g, ln2_b, router_group, router_expert, expert_w_gate, expert_w_up, expert_w_down):
    bsz, s, d = x.shape
    assert (s, d) == (SEQ, D_MODEL)
    x = x.reshape(bsz * s, d)
    bias = _rel_bias(rel_bias_table)
    for l in range(DEPTH):
        i = l // 2
        w_router = (router_group[l], router_expert[l])
        if l % 2 == 0:
            x1, *routing = _mixer_layer(x, mix_w_in[i], gmlp_ln_g[i], gmlp_ln_b[i],
                                        gmlp_w_spatial[i], gmlp_b_spatial[i], conv_w[i],
                                        mix_w_out[i], ln1_g[l], ln1_b[l], w_router)
        else:
            x1, *routing = _attn_layer(x, bias, attn_w_qkv[i], attn_b_qkv[i], attn_sinks[i],
                                       attn_w_o[i], attn_b_o[i], ln1_g[l], ln1_b[l], w_router)
        x = _moe_layer(x1, *routing, l, expert_w_gate, expert_w_up, expert_w_down,
                       ln2_g[l], ln2_b[l])
    return x.reshape(bsz, s, d)
```

```python
import math

import jax
import jax.numpy as jnp
from jax import lax
from jax.experimental import pallas as pl
from jax.experimental.pallas import tpu as pltpu

D_MODEL = 1024
SEQ = 16384
DEPTH = 4
CHUNK = 128
A_GROUPS = 4
A_CH = 128
A_WIDTH = A_GROUPS * A_CH
B_WIDTH = 512
CONV_W = 3
MIX_IN = 2 * A_WIDTH + 3 * B_WIDTH
N_HEADS = 16
N_KV = 2
HEAD_DIM = 64
GQA_GROUP = N_HEADS // N_KV
WINDOW = 128
QKV_DIM = (N_HEADS + 2 * N_KV) * HEAD_DIM
ATT_OUT = N_HEADS * HEAD_DIM
KV_WIDTH = N_KV * HEAD_DIM
N_BUCKETS = 32
MAX_DISTANCE = 128
N_GROUPS = 4
EXPERTS_PER_GROUP = 8
N_EXPERTS = N_GROUPS * EXPERTS_PER_GROUP
TOP_K = 2
D_EXPERT = 512
ALPHA = (2 * DEPTH) ** 0.25
LN_EPS = 1e-5

LANES = 128
SUBLANES = 8
ROW_TILE = (SUBLANES, LANES)
assert D_MODEL == SUBLANES * LANES
ROUTER_ROWS = SUBLANES + N_EXPERTS
MIX_ROWS = 1024
ATTN_ROWS = 1024
DISPATCH_ROWS = 1024
COMBINE_ROWS = 256
EXPERT_ROWS = 1024
ROWS_PER_ISSUE = 16
DISPATCH_SLOTS = 3
VMEM_LIMIT = 56 * 1024 * 1024

R_E0, R_E1, R_RANK0, R_RANK1, R_GATE0, R_GATE1 = range(6)

F32 = jnp.float32
BF16 = jnp.bfloat16
I32 = jnp.int32
U32 = jnp.uint32


def _layer_norm(x, g, b):
    mu = jnp.mean(x, axis=-1, keepdims=True)
    xc = x - mu
    var = jnp.mean(xc * xc, axis=-1, keepdims=True)
    return xc * lax.rsqrt(var + LN_EPS) * g + b


def _gelu(x):
    return 0.5 * x * (1.0 + lax.erf(x * (2.0 ** -0.5)))


def _mixer_kernel(x_ref, win_ref, lng_ref, lnb_ref, wsp_ref, bsp_ref, cw_ref, wout_ref,
                  g1_ref, b1_ref, wr_ref, tri_ref, x1_ref, route_ref, route_t_ref, count_ref,
                  ztail_ref, base_ref):
    tm = x_ref.shape[0]
    i = pl.program_id(0)

    @pl.when(i % (SEQ // tm) == 0)
    def _():
        ztail_ref[...] = jnp.zeros_like(ztail_ref)

    x = x_ref[...]
    h = jnp.dot(x.astype(BF16), win_ref[...], preferred_element_type=F32)
    u = _gelu(h[:, :A_WIDTH])
    v = _gelu(h[:, A_WIDTH:2 * A_WIDTH])
    o = 2 * A_WIDTH
    g_b = h[:, o:o + B_WIDTH]
    g_c = h[:, o + B_WIDTH:o + 2 * B_WIDTH]
    hb = h[:, o + 2 * B_WIDTH:o + 3 * B_WIDTH]

    v = _layer_norm(v, lng_ref[...], lnb_ref[...]).astype(BF16)
    n_chunks = tm // CHUNK
    ri = lax.broadcasted_iota(I32, (CHUNK, CHUNK), 0)
    ci = lax.broadcasted_iota(I32, (CHUNK, CHUNK), 1)
    causal = ci <= ri
    sv_cols = [[None] * A_GROUPS for _ in range(n_chunks)]
    for g in range(A_GROUPS):
        ws = jnp.where(causal, wsp_ref[g], 0.0).astype(BF16)
        vg = jnp.concatenate(
            [v[c * CHUNK:(c + 1) * CHUNK, g * A_CH:(g + 1) * A_CH] for c in range(n_chunks)],
            axis=1)
        sg = jnp.dot(ws, vg, preferred_element_type=F32) + bsp_ref[:, g:g + 1]
        for c in range(n_chunks):
            sv_cols[c][g] = sg[:, c * A_CH:(c + 1) * A_CH]
    sv = jnp.concatenate([jnp.concatenate(row, axis=1) for row in sv_cols], axis=0)
    y_a = u * sv

    z = g_c * hb
    rows = lax.broadcasted_iota(I32, z.shape, 0)
    tail = ztail_ref[...]
    zm1 = jnp.where(rows == 0, tail[7:8, :], pltpu.roll(z, 1, 0))
    zm2 = jnp.where(rows == 0, tail[6:7, :],
                    jnp.where(rows == 1, tail[7:8, :], pltpu.roll(z, 2, 0)))
    conv = cw_ref[0:1, :] * zm2 + cw_ref[1:2, :] * zm1 + cw_ref[2:3, :] * z
    y_b = g_b * conv
    ztail_ref[...] = z[tm - 8:tm, :]

    y = jnp.concatenate([y_a, y_b], axis=1).astype(BF16)
    m = jnp.dot(y, wout_ref[...], preferred_element_type=F32)
    x1 = _layer_norm(ALPHA * x + m, g1_ref[...], b1_ref[...])
    x1_ref[...] = x1
    _route_tile(x1, wr_ref, tri_ref, route_ref, route_t_ref, count_ref, base_ref)


def _mixer_layer(x, w_in, ln_g, ln_b, w_sp, b_sp, conv_w, w_out, g1, b1, w_router):
    t = x.shape[0]
    tm = MIX_ROWS
    const2 = lambda i: (0, 0)
    r_in, r_out, r_shape, r_scratch = _route_specs(t, tm)
    return pl.pallas_call(
        _mixer_kernel,
        grid=(t // tm,),
        in_specs=[
            pl.BlockSpec((tm, D_MODEL), lambda i: (i, 0)),
            pl.BlockSpec((D_MODEL, MIX_IN), const2),
            pl.BlockSpec((1, A_WIDTH), const2),
            pl.BlockSpec((1, A_WIDTH), const2),
            pl.BlockSpec((A_GROUPS, CHUNK, CHUNK), lambda i: (0, 0, 0)),
            pl.BlockSpec((CHUNK, A_GROUPS), const2),
            pl.BlockSpec((CONV_W, B_WIDTH), const2),
            pl.BlockSpec((A_WIDTH + B_WIDTH, D_MODEL), const2),
            pl.BlockSpec((1, D_MODEL), const2),
            pl.BlockSpec((1, D_MODEL), const2),
        ] + r_in,
        out_specs=[pl.BlockSpec((tm, D_MODEL), lambda i: (i, 0))] + r_out,
        out_shape=[jax.ShapeDtypeStruct((t, D_MODEL), F32)] + r_shape,
        scratch_shapes=[pltpu.VMEM((8, B_WIDTH), F32), r_scratch],
        compiler_params=pltpu.CompilerParams(
            dimension_semantics=("arbitrary",), vmem_limit_bytes=VMEM_LIMIT),
        name="mixer_layer",
    )(x, w_in.astype(BF16), ln_g.reshape(1, -1), ln_b.reshape(1, -1), w_sp, b_sp.T,
      conv_w, w_out.astype(BF16), g1.reshape(1, -1), b1.reshape(1, -1),
      *_route_operands(*w_router, tm))


def _attn_kernel(sink_ref, x_ref, wqkv_ref, bqkv_ref, bias_ref, wo_ref, bo_ref,
                 g1_ref, b1_ref, wr_ref, tri_ref, x1_ref, route_ref, route_t_ref, count_ref,
                 kprev_ref, vprev_ref, base_ref):
    tm = x_ref.shape[0]
    i = pl.program_id(0)
    first = i % (SEQ // tm) == 0

    @pl.when(first)
    def _():
        kprev_ref[...] = jnp.zeros_like(kprev_ref)
        vprev_ref[...] = jnp.zeros_like(vprev_ref)

    x = x_ref[...]
    qkv = jnp.dot(x.astype(BF16), wqkv_ref[...], preferred_element_type=F32) + bqkv_ref[...]
    q = (qkv[:, :ATT_OUT] * (HEAD_DIM ** -0.5)).astype(BF16)

    lane = lax.broadcasted_iota(I32, (CHUNK, KV_WIDTH), 1)
    low = lane < HEAD_DIM

    def halves(t):
        swapped = pltpu.roll(t, HEAD_DIM, 1)
        zero = jnp.zeros_like(t)
        return [jnp.where(low, t, zero).astype(BF16), jnp.where(low, zero, swapped).astype(BF16),
                jnp.where(low, swapped, zero).astype(BF16), jnp.where(low, zero, t).astype(BF16)]

    def stacked(prev, cur, j):
        return jnp.concatenate([prev[2 * j], cur[2 * j], prev[2 * j + 1], cur[2 * j + 1]], axis=0)

    a = lax.broadcasted_iota(I32, (CHUNK, 2 * CHUNK), 0)
    c = lax.broadcasted_iota(I32, (CHUNK, 2 * CHUNK), 1)
    window = (c > a) & (c <= a + WINDOW)
    neg = jnp.finfo(F32).min
    n_pairs = N_HEADS // 2
    kv_of = lambda pair: (2 * pair) // GQA_GROUP

    k_prev = [kprev_ref[idx] for idx in range(2 * N_KV)]
    v_prev = [vprev_ref[idx] for idx in range(2 * N_KV)]
    o_blocks = []
    for blk in range(tm // CHUNK):
        rows = slice(blk * CHUNK, (blk + 1) * CHUNK)
        k_cur = halves(qkv[rows, ATT_OUT:ATT_OUT + KV_WIDTH])
        v_cur = halves(qkv[rows, ATT_OUT + KV_WIDTH:])
        k_rhs = [stacked(k_prev, k_cur, j) for j in range(N_KV)]
        v_rhs = [stacked(v_prev, v_cur, j) for j in range(N_KV)]
        k_prev, v_prev = k_cur, v_cur
        mask = window & (c >= jnp.where(first, CHUNK, 0)) if blk == 0 else window

        scores = [lax.dot_general(q[rows, pair * 2 * HEAD_DIM:(pair + 1) * 2 * HEAD_DIM],
                                  k_rhs[kv_of(pair)], (((1,), (1,)), ((), ())),
                                  preferred_element_type=F32) for pair in range(n_pairs)]
        probs, inv_denoms = [], []
        for pair in range(n_pairs):
            sc2 = scores[pair] + bias_ref[pair]
            sides = []
            for side in range(2):
                sc = jnp.where(mask, sc2[:, side * 2 * CHUNK:(side + 1) * 2 * CHUNK], neg)
                sink = sink_ref[2 * pair + side]
                m = jnp.maximum(jnp.max(sc, axis=-1, keepdims=True), sink)
                p = jnp.exp(sc - m)
                denom = jnp.sum(p, axis=-1, keepdims=True) + jnp.exp(sink - m)
                sides.append(p.astype(BF16))
                inv_denoms.append(jnp.broadcast_to(1.0 / denom, (CHUNK, HEAD_DIM)))
            probs.append(jnp.concatenate(sides, axis=1))
        outs = [jnp.dot(probs[pair], v_rhs[kv_of(pair)], preferred_element_type=F32)
                for pair in range(n_pairs)]
        o = jnp.concatenate(outs, axis=1) * jnp.concatenate(inv_denoms, axis=1)
        o_blocks.append(o.astype(BF16))
    for idx in range(2 * N_KV):
        kprev_ref[idx] = k_prev[idx]
        vprev_ref[idx] = v_prev[idx]
    o = jnp.concatenate(o_blocks, axis=0)
    m_out = jnp.dot(o, wo_ref[...], preferred_element_type=F32) + bo_ref[...]
    x1 = _layer_norm(ALPHA * x + m_out, g1_ref[...], b1_ref[...])
    x1_ref[...] = x1
    _route_tile(x1, wr_ref, tri_ref, route_ref, route_t_ref, count_ref, base_ref)


def _t5_bucket(rel):
    n = jnp.maximum(rel, 0)
    max_exact = N_BUCKETS // 2
    nf = jnp.maximum(n, 1).astype(F32)
    large = max_exact + (jnp.log(nf / max_exact) / math.log(MAX_DISTANCE / max_exact)
                         * (N_BUCKETS - max_exact)).astype(I32)
    large = jnp.minimum(large, N_BUCKETS - 1)
    return jnp.where(n < max_exact, n, large)


def _rel_bias(rel_table):
    a = jnp.arange(CHUNK)[:, None]
    c = jnp.arange(2 * CHUNK)[None, :]
    onehot = jax.nn.one_hot(_t5_bucket(a + CHUNK - c), N_BUCKETS, dtype=F32)
    bias = jnp.einsum('acb,bh->hac', onehot, rel_table.astype(F32),
                      precision=lax.Precision.HIGHEST)
    bias = bias.reshape(N_HEADS // 2, 2, CHUNK, 2 * CHUNK).transpose(0, 2, 1, 3)
    return bias.reshape(N_HEADS // 2, CHUNK, 4 * CHUNK)


def _attn_layer(x, bias, w_qkv, b_qkv, sinks, w_o, b_o, g1, b1, w_router):
    t = x.shape[0]
    tm = ATTN_ROWS
    const2 = lambda i: (0, 0)
    r_in, r_out, r_shape, r_scratch = _route_specs(t, tm)
    return pl.pallas_call(
        _attn_kernel,
        grid=(t // tm,),
        in_specs=[
            pl.BlockSpec(memory_space=pltpu.SMEM),
            pl.BlockSpec((tm, D_MODEL), lambda i: (i, 0)),
            pl.BlockSpec((D_MODEL, QKV_DIM), const2),
            pl.BlockSpec((1, QKV_DIM), const2),
            pl.BlockSpec((N_HEADS // 2, CHUNK, 4 * CHUNK), lambda i: (0, 0, 0)),
            pl.BlockSpec((ATT_OUT, D_MODEL), const2),
            pl.BlockSpec((1, D_MODEL), const2),
            pl.BlockSpec((1, D_MODEL), const2),
            pl.BlockSpec((1, D_MODEL), const2),
        ] + r_in,
        out_specs=[pl.BlockSpec((tm, D_MODEL), lambda i: (i, 0))] + r_out,
        out_shape=[jax.ShapeDtypeStruct((t, D_MODEL), F32)] + r_shape,
        scratch_shapes=[pltpu.VMEM((2 * N_KV, CHUNK, KV_WIDTH), BF16),
                        pltpu.VMEM((2 * N_KV, CHUNK, KV_WIDTH), BF16), r_scratch],
        compiler_params=pltpu.CompilerParams(
            dimension_semantics=("arbitrary",), vmem_limit_bytes=VMEM_LIMIT),
        name="attn_layer",
    )(sinks, x, w_qkv.astype(BF16), b_qkv.reshape(1, -1), bias, w_o.astype(BF16),
      b_o.reshape(1, -1), g1.reshape(1, -1), b1.reshape(1, -1), *_route_operands(*w_router, tm))


def _route_tile(x1, wr_ref, tri_ref, route_ref, route_t_ref, count_ref, base_ref):
    @pl.when(pl.program_id(0) == 0)
    def _():
        base_ref[...] = jnp.zeros_like(base_ref)

    tm = x1.shape[0]
    lt = lax.dot_general(wr_ref[...], x1.astype(BF16), (((1,), (1,)), ((), ())),
                         preferred_element_type=F32)
    sub = lax.broadcasted_iota(I32, (SUBLANES, tm), 0).astype(F32)
    ninf = -jnp.inf

    def first_argmax(vals):
        m = jnp.max(vals, axis=0, keepdims=True)
        idx = jnp.min(jnp.where(vals == m, sub, float(SUBLANES)), axis=0, keepdims=True)
        return m, idx

    is_g = sub < N_GROUPS
    g_rows = lt[0:SUBLANES]
    gmax, g_idx = first_argmax(jnp.where(is_g, g_rows, ninf))
    g_p = 1.0 / jnp.sum(jnp.where(is_g, jnp.exp(g_rows - gmax), 0.0), axis=0, keepdims=True)

    group_rows = lambda a, g: a[SUBLANES * (g + 1):SUBLANES * (g + 2)]
    el = group_rows(lt, 0)
    for g in range(1, N_GROUPS):
        el = jnp.where(g_idx == g, group_rows(lt, g), el)
    m1, j1 = first_argmax(el)
    m2, j2 = first_argmax(jnp.where(sub == j1, ninf, el))
    a2 = jnp.exp(m2 - m1)
    gate0 = g_p / (1.0 + a2)
    gate1 = g_p * a2 / (1.0 + a2)

    hit0 = [(g_idx == g) & (sub == j1) for g in range(N_GROUPS)]
    hit1 = [(g_idx == g) & (sub == j2) for g in range(N_GROUPS)]
    onehot = jnp.concatenate([jnp.where(h0 | h1, 1.0, 0.0) for h0, h1 in zip(hit0, hit1)],
                             axis=0)
    before = jnp.dot(onehot.astype(BF16), tri_ref[...], preferred_element_type=F32)
    before = before + base_ref[:, 0:1]

    def picked(hits):
        total = jnp.zeros((SUBLANES, tm), F32)
        for g in range(N_GROUPS):
            total = total + jnp.where(hits[g], before[SUBLANES * g:SUBLANES * (g + 1)], 0.0)
        return jnp.sum(total, axis=0, keepdims=True)

    rank0 = picked(hit0)
    rank1 = picked(hit1)
    base_ref[...] += jnp.sum(onehot, axis=1, keepdims=True)

    e_base = g_idx * EXPERTS_PER_GROUP
    rec_t = jnp.concatenate([e_base + j1, e_base + j2, rank0, rank1, gate0, gate1,
                             jnp.zeros((SUBLANES - 6, tm), F32)], axis=0)
    route_t_ref[...] = rec_t
    padded = jnp.concatenate([rec_t, jnp.zeros((LANES - SUBLANES, tm), F32)], axis=0)
    route_ref[...] = jnp.transpose(padded)
    count_ref[...] = base_ref[...]


def _route_specs(t, tm):
    const2 = lambda i: (0, 0)
    in_specs = [pl.BlockSpec((ROUTER_ROWS, D_MODEL), const2), pl.BlockSpec((tm, tm), const2)]
    out_specs = [
        pl.BlockSpec((tm, LANES), lambda i: (i, 0)),
        pl.BlockSpec((SUBLANES, tm), lambda i: (0, i)),
        pl.BlockSpec((N_EXPERTS, LANES), const2),
    ]
    out_shape = [
        jax.ShapeDtypeStruct((t, LANES), F32),
        jax.ShapeDtypeStruct((SUBLANES, t), F32),
        jax.ShapeDtypeStruct((N_EXPERTS, LANES), F32),
    ]
    return in_specs, out_specs, out_shape, pltpu.VMEM((N_EXPERTS, LANES), F32)


def _route_operands(w_group, w_expert, tm):
    pad = jnp.zeros((D_MODEL, SUBLANES - N_GROUPS), F32)
    w_rows = jnp.concatenate([w_group, pad, w_expert], axis=1).T
    return w_rows.astype(BF16), jnp.triu(jnp.ones((tm, tm), BF16), 1)


def _plan(route_t, count_lanes):
    t = route_t.shape[1]
    rb = EXPERT_ROWS
    experts = jnp.arange(N_EXPERTS, dtype=I32)
    counts = count_lanes[:, 0].astype(I32)
    end = jnp.cumsum(counts)
    start = end - counts
    e = route_t[R_E0:R_E1 + 1].astype(I32)
    rank = route_t[R_RANK0:R_RANK1 + 1].astype(I32)
    onehot = e[:, None, :] == experts[None, :, None]
    dest = jnp.sum(jnp.where(onehot, start[None, :, None], 0), axis=1) + rank

    first_blk = start // rb
    last_blk = jnp.maximum(end - 1, 0) // rb
    n_steps_e = jnp.where(counts > 0, last_blk - first_blk + 1, 0)
    step_end = jnp.cumsum(n_steps_e)
    step_start = step_end - n_steps_e
    n_steps = step_end[-1]
    max_steps = t * TOP_K // rb + N_EXPERTS - 1
    s = jnp.minimum(jnp.arange(max_steps, dtype=I32), n_steps - 1)
    step_e = jnp.minimum(jnp.sum((step_end[None, :] <= s[:, None]).astype(I32), axis=1),
                         N_EXPERTS - 1)
    pick = step_e[:, None] == experts[None, :]
    take = lambda v: jnp.sum(jnp.where(pick, v[None, :], 0), axis=1)
    step_blk = take(first_blk) + s - take(step_start)
    step_lo = jnp.clip(take(start) - step_blk * rb, 0, rb)
    step_hi = jnp.clip(take(end) - step_blk * rb, 0, rb)
    return dest, (step_blk, step_e, step_lo, step_hi, n_steps.reshape(1))


def _dest_blocks(dest, tm):
    t = dest.shape[1]
    return dest.reshape(TOP_K, t // tm, tm).transpose(1, 0, 2).reshape(t // tm, 1, TOP_K * tm)


def _to_row_tiles(x):
    return x.reshape(x.shape[0], ROW_TILE[0], ROW_TILE[1])


def _from_row_tiles(x):
    return x.reshape(x.shape[0], D_MODEL)


PACKED_ROW_TILE = (SUBLANES // 2, LANES)
HIGH_HALF = 0xFFFF0000


def _pack_rows(x):
    half = D_MODEL // 2
    bits = lambda v: lax.bitcast_convert_type(v.astype(BF16).astype(F32), U32)
    words = (bits(x[:, :half]) >> 16) | (bits(x[:, half:]) & jnp.uint32(HIGH_HALF))
    return words.reshape((x.shape[0],) + PACKED_ROW_TILE)


def _unpack_rows(words):
    w = words.reshape(words.shape[0], D_MODEL // 2)
    low = lax.bitcast_convert_type(w << 16, F32).astype(BF16)
    high = lax.bitcast_convert_type(w & jnp.uint32(HIGH_HALF), F32).astype(BF16)
    return jnp.concatenate([low, high], axis=1)


def _dispatch_kernel(dest_ref, x1_ref, xs_hbm, xbuf, row_sems):
    tm = x1_ref.shape[0]
    i = pl.program_id(0)
    n = pl.num_programs(0)
    slot = i % DISPATCH_SLOTS

    def drain_rows(s):
        for k in range(TOP_K):
            pltpu.make_async_copy(xbuf.at[s], xs_hbm.at[pl.ds(0, tm)], row_sems.at[s]).wait()

    @pl.when(i >= DISPATCH_SLOTS)
    def _():
        drain_rows(slot)

    xbuf[slot] = _pack_rows(x1_ref[...])

    def issue(g, carry):
        for j in range(ROWS_PER_ISSUE):
            r = g * ROWS_PER_ISSUE + j
            for k in range(TOP_K):
                pltpu.make_async_copy(xbuf.at[slot, r], xs_hbm.at[dest_ref[0, 0, k * tm + r]],
                                      row_sems.at[slot]).start(priority=k)
        return carry

    lax.fori_loop(0, tm // ROWS_PER_ISSUE, issue, 0)

    @pl.when(i == n - 1)
    def _():
        for back in range(DISPATCH_SLOTS):
            drain_rows((i - back) % DISPATCH_SLOTS)


def _dispatch(x1, dest):
    t = x1.shape[0]
    tm = DISPATCH_ROWS
    return pl.pallas_call(
        _dispatch_kernel,
        grid=(t // tm,),
        in_specs=[
            pl.BlockSpec((1, 1, TOP_K * tm), lambda i: (i, 0, 0), memory_space=pltpu.SMEM),
            pl.BlockSpec((tm, D_MODEL), lambda i: (i, 0)),
        ],
        out_specs=pl.BlockSpec(memory_space=pl.ANY),
        out_shape=jax.ShapeDtypeStruct((t * TOP_K,) + PACKED_ROW_TILE, U32),
        scratch_shapes=[
            pltpu.VMEM((DISPATCH_SLOTS, tm) + PACKED_ROW_TILE, U32),
            pltpu.SemaphoreType.DMA((DISPATCH_SLOTS,)),
        ],
        compiler_params=pltpu.CompilerParams(
            dimension_semantics=("arbitrary",), vmem_limit_bytes=VMEM_LIMIT),
        name="dispatch",
    )(_dest_blocks(dest, tm), x1)


def _expert_kernel(blk_ref, e_ref, lo_ref, hi_ref, n_steps_ref, xs_ref, wg_ref, wu_ref, wd_ref,
                   ys_ref, wg_bf, wu_bf, wd_bf):
    s = pl.program_id(0)
    prev = jnp.maximum(s - 1, 0)
    new_expert = (s == 0) | (e_ref[s] != e_ref[prev])
    new_block = (s == 0) | (blk_ref[s] != blk_ref[prev])

    @pl.when(new_expert)
    def _():
        wg_bf[...] = wg_ref[0, 0].astype(BF16)
        wu_bf[...] = wu_ref[0, 0].astype(BF16)
        wd_bf[...] = wd_ref[0, 0].astype(BF16)

    rb = xs_ref.shape[0]
    half = rb // 2
    lo, hi = lo_ref[s], hi_ref[s]
    active = s < n_steps_ref[0]
    both_halves = (lo < half) & (hi > half)

    def masked_ffn(x_tiles, row0):
        xb = _unpack_rows(x_tiles)
        gate = jnp.dot(xb, wg_bf[...], preferred_element_type=F32)
        up = jnp.dot(xb, wu_bf[...], preferred_element_type=F32)
        hid = (gate * jax.nn.sigmoid(gate) * up).astype(BF16)
        y = jnp.dot(hid, wd_bf[...], preferred_element_type=F32)
        rows = lax.broadcasted_iota(I32, y.shape, 0) + row0
        return _to_row_tiles(jnp.where((rows >= lo) & (rows < hi), y, 0.0))

    @pl.when(active & both_halves)
    def _():
        y = masked_ffn(xs_ref[...], 0)

        @pl.when(new_block)
        def _():
            ys_ref[...] = y

        @pl.when(jnp.logical_not(new_block))
        def _():
            ys_ref[...] += y

    @pl.when(active & jnp.logical_not(both_halves))
    def _():
        off = pl.multiple_of(jnp.where(lo < half, 0, half), half)
        y = masked_ffn(xs_ref[pl.ds(off, half)], off)

        @pl.when(new_block)
        def _():
            ys_ref[pl.ds(off, half)] = y
            ys_ref[pl.ds(half - off, half)] = jnp.zeros_like(y)

        @pl.when(jnp.logical_not(new_block))
        def _():
            ys_ref[pl.ds(off, half)] += y


def _expert_ffn(xs, steps, layer, wg, wu, wd):
    step_blk, step_e, step_lo, step_hi, n_steps = steps
    rb = EXPERT_ROWS
    row_map = lambda s, blk, e, lo, hi, n: (blk[s], 0, 0)
    w_map = lambda s, blk, e, lo, hi, n: (layer, e[s], 0, 0)
    grid_spec = pltpu.PrefetchScalarGridSpec(
        num_scalar_prefetch=5,
        grid=(step_blk.shape[0],),
        in_specs=[
            pl.BlockSpec((rb,) + PACKED_ROW_TILE, row_map),
            pl.BlockSpec((1, 1, D_MODEL, D_EXPERT), w_map),
            pl.BlockSpec((1, 1, D_MODEL, D_EXPERT), w_map),
            pl.BlockSpec((1, 1, D_EXPERT, D_MODEL), w_map),
        ],
        out_specs=pl.BlockSpec((rb,) + ROW_TILE, row_map),
        scratch_shapes=[
            pltpu.VMEM((D_MODEL, D_EXPERT), BF16),
            pltpu.VMEM((D_MODEL, D_EXPERT), BF16),
            pltpu.VMEM((D_EXPERT, D_MODEL), BF16),
        ],
    )
    return pl.pallas_call(
        _expert_kernel,
        grid_spec=grid_spec,
        out_shape=jax.ShapeDtypeStruct((xs.shape[0],) + ROW_TILE, F32),
        compiler_params=pltpu.CompilerParams(
            dimension_semantics=("arbitrary",), vmem_limit_bytes=VMEM_LIMIT),
        name="expert_ffn",
    )(step_blk, step_e, step_lo, step_hi, n_steps, xs, wg, wu, wd)


def _combine_kernel(dest_ref, dest_next_ref, x1_ref, route_ref, g2_ref, b2_ref, ys_hbm,
                    x2_ref, ybuf, sems):
    tm = x1_ref.shape[0]
    i = pl.program_id(0)
    n = pl.num_programs(0)
    slot = i % 2

    def gather(d_ref, to_slot):
        def issue(g, carry):
            for j in range(ROWS_PER_ISSUE):
                r = g * ROWS_PER_ISSUE + j
                for k in range(TOP_K):
                    pltpu.make_async_copy(ys_hbm.at[d_ref[0, 0, k * tm + r]],
                                          ybuf.at[to_slot, k, r],
                                          sems.at[to_slot]).start(priority=k)
            return carry
        lax.fori_loop(0, tm // ROWS_PER_ISSUE, issue, 0)

    @pl.when(i == 0)
    def _():
        gather(dest_ref, slot)

    @pl.when(i + 1 < n)
    def _():
        gather(dest_next_ref, 1 - slot)

    for k in range(TOP_K):
        pltpu.make_async_copy(ys_hbm.at[pl.ds(0, tm)], ybuf.at[slot, k], sems.at[slot]).wait()

    route = route_ref[...]
    f = (route[:, R_GATE0:R_GATE0 + 1] * _from_row_tiles(ybuf[slot, 0])
         + route[:, R_GATE1:R_GATE1 + 1] * _from_row_tiles(ybuf[slot, 1]))
    x2_ref[...] = _layer_norm(ALPHA * x1_ref[...] + f, g2_ref[...], b2_ref[...])


def _combine_ln(x1, ys, dest, route, g2, b2):
    t = x1.shape[0]
    tm = COMBINE_ROWS
    n = t // tm
    row = lambda i: (i, 0)
    const2 = lambda i: (0, 0)
    dest_blocks = _dest_blocks(dest, tm)
    smem_block = lambda index_map: pl.BlockSpec((1, 1, TOP_K * tm), index_map,
                                                memory_space=pltpu.SMEM)
    return pl.pallas_call(
        _combine_kernel,
        grid=(n,),
        in_specs=[
            smem_block(lambda i: (i, 0, 0)),
            smem_block(lambda i: (jnp.minimum(i + 1, n - 1), 0, 0)),
            pl.BlockSpec((tm, D_MODEL), row),
            pl.BlockSpec((tm, LANES), row),
            pl.BlockSpec((1, D_MODEL), const2),
            pl.BlockSpec((1, D_MODEL), const2),
            pl.BlockSpec(memory_space=pl.ANY),
        ],
        out_specs=pl.BlockSpec((tm, D_MODEL), row),
        out_shape=jax.ShapeDtypeStruct((t, D_MODEL), F32),
        scratch_shapes=[
            pltpu.VMEM((2, TOP_K, tm) + ROW_TILE, F32),
            pltpu.SemaphoreType.DMA((2,)),
        ],
        compiler_params=pltpu.CompilerParams(
            dimension_semantics=("arbitrary",), vmem_limit_bytes=VMEM_LIMIT),
        name="combine_ln",
    )(dest_blocks, dest_blocks, x1, route, g2.reshape(1, -1), b2.reshape(1, -1), ys)


def _moe_layer(x1, route, route_t, counts, layer, wg, wu, wd, g2, b2):
    dest, steps = _plan(route_t, counts)
    xs = _dispatch(x1, dest)
    ys = _expert_ffn(xs, steps, layer, wg, wu, wd)
    return _combine_ln(x1, ys, dest, route, g2, b2)


def kernel(x, rel_bias_table, mix_w_in, gmlp_ln_g, gmlp_ln_b, gmlp_w_spatial, gmlp_b_spatial, conv_w, mix_w_out, attn_w_qkv, attn_b_qkv, attn_sinks, attn_w_o, attn_b_o, ln1_g, ln1_b, ln2_g, ln2_b, router_group, router_expert, expert_w_gate, expert_w_up, expert_w_down):
    bsz, s, d = x.shape
    assert (s, d) == (SEQ, D_MODEL)
    x = x.reshape(bsz * s, d)
    bias = _rel_bias(rel_bias_table)
    for l in range(DEPTH):
        i = l // 2
        w_router = (router_group[l], router_expert[l])
        if l % 2 == 0:
            x1, *routing = _mixer_layer(x, mix_w_in[i], gmlp_ln_g[i], gmlp_ln_b[i],
                                        gmlp_w_spatial[i], gmlp_b_spatial[i], conv_w[i],
                                        mix_w_out[i], ln1_g[l], ln1_b[l], w_router)
        else:
            x1, *routing = _attn_layer(x, bias, attn_w_qkv[i], attn_b_qkv[i], attn_sinks[i],
                                       attn_w_o[i], attn_b_o[i], ln1_g[l], ln1_b[l], w_router)
        x = _moe_layer(x1, *routing, l, expert_w_gate, expert_w_up, expert_w_down,
                       ln2_g[l], ln2_b[l])
    return x.reshape(bsz, s, d)
```

```python
import math

import jax
import jax.numpy as jnp
from jax import lax
from jax.experimental import pallas as pl
from jax.experimental.pallas import tpu as pltpu

D_MODEL = 1024
SEQ = 16384
DEPTH = 4
CHUNK = 128
A_GROUPS = 4
A_CH = 128
A_WIDTH = A_GROUPS * A_CH
B_WIDTH = 512
CONV_W = 3
MIX_IN = 2 * A_WIDTH + 3 * B_WIDTH
N_HEADS = 16
N_KV = 2
HEAD_DIM = 64
GQA_GROUP = N_HEADS // N_KV
WINDOW = 128
QKV_DIM = (N_HEADS + 2 * N_KV) * HEAD_DIM
ATT_OUT = N_HEADS * HEAD_DIM
KV_WIDTH = N_KV * HEAD_DIM
N_BUCKETS = 32
MAX_DISTANCE = 128
N_GROUPS = 4
EXPERTS_PER_GROUP = 8
N_EXPERTS = N_GROUPS * EXPERTS_PER_GROUP
TOP_K = 2
D_EXPERT = 512
ALPHA = (2 * DEPTH) ** 0.25
LN_EPS = 1e-5

LANES = 128
SUBLANES = 8
ROW_TILE = (SUBLANES, LANES)
assert D_MODEL == SUBLANES * LANES
ROUTER_ROWS = SUBLANES + N_EXPERTS
MIX_ROWS = 1024
ATTN_ROWS = 1024
DISPATCH_ROWS = 1024
COMBINE_ROWS = 128
EXPERT_ROWS = 1024
ROWS_PER_ISSUE = 16
DISPATCH_SLOTS = 3
VMEM_LIMIT = 56 * 1024 * 1024

R_E0, R_E1, R_RANK0, R_RANK1, R_GATE0, R_GATE1 = range(6)

F32 = jnp.float32
BF16 = jnp.bfloat16
I32 = jnp.int32
U32 = jnp.uint32


def _layer_norm(x, g, b):
    mu = jnp.mean(x, axis=-1, keepdims=True)
    xc = x - mu
    var = jnp.mean(xc * xc, axis=-1, keepdims=True)
    return xc * lax.rsqrt(var + LN_EPS) * g + b


def _gelu(x):
    return 0.5 * x * (1.0 + lax.erf(x * (2.0 ** -0.5)))


def _mixer_kernel(x_ref, win_ref, lng_ref, lnb_ref, wsp_ref, bsp_ref, cw_ref, wout_ref,
                  g1_ref, b1_ref, wr_ref, tri_ref, x1_ref, route_ref, route_t_ref, count_ref,
                  ztail_ref, base_ref):
    tm = x_ref.shape[0]
    i = pl.program_id(0)

    @pl.when(i % (SEQ // tm) == 0)
    def _():
        ztail_ref[...] = jnp.zeros_like(ztail_ref)

    x = x_ref[...]
    h = jnp.dot(x.astype(BF16), win_ref[...], preferred_element_type=F32)
    u = _gelu(h[:, :A_WIDTH])
    v = _gelu(h[:, A_WIDTH:2 * A_WIDTH])
    o = 2 * A_WIDTH
    g_b = h[:, o:o + B_WIDTH]
    g_c = h[:, o + B_WIDTH:o + 2 * B_WIDTH]
    hb = h[:, o + 2 * B_WIDTH:o + 3 * B_WIDTH]

    v = _layer_norm(v, lng_ref[...], lnb_ref[...]).astype(BF16)
    n_chunks = tm // CHUNK
    ri = lax.broadcasted_iota(I32, (CHUNK, CHUNK), 0)
    ci = lax.broadcasted_iota(I32, (CHUNK, CHUNK), 1)
    causal = ci <= ri
    sv_cols = [[None] * A_GROUPS for _ in range(n_chunks)]
    for g in range(A_GROUPS):
        ws = jnp.where(causal, wsp_ref[g], 0.0).astype(BF16)
        vg = jnp.concatenate(
            [v[c * CHUNK:(c + 1) * CHUNK, g * A_CH:(g + 1) * A_CH] for c in range(n_chunks)],
            axis=1)
        sg = jnp.dot(ws, vg, preferred_element_type=F32) + bsp_ref[:, g:g + 1]
        for c in range(n_chunks):
            sv_cols[c][g] = sg[:, c * A_CH:(c + 1) * A_CH]
    sv = jnp.concatenate([jnp.concatenate(row, axis=1) for row in sv_cols], axis=0)
    y_a = u * sv

    z = g_c * hb
    rows = lax.broadcasted_iota(I32, z.shape, 0)
    tail = ztail_ref[...]
    zm1 = jnp.where(rows == 0, tail[7:8, :], pltpu.roll(z, 1, 0))
    zm2 = jnp.where(rows == 0, tail[6:7, :],
                    jnp.where(rows == 1, tail[7:8, :], pltpu.roll(z, 2, 0)))
    conv = cw_ref[0:1, :] * zm2 + cw_ref[1:2, :] * zm1 + cw_ref[2:3, :] * z
    y_b = g_b * conv
    ztail_ref[...] = z[tm - 8:tm, :]

    y = jnp.concatenate([y_a, y_b], axis=1).astype(BF16)
    m = jnp.dot(y, wout_ref[...], preferred_element_type=F32)
    x1 = _layer_norm(ALPHA * x + m, g1_ref[...], b1_ref[...])
    x1_ref[...] = x1
    _route_tile(x1, wr_ref, tri_ref, route_ref, route_t_ref, count_ref, base_ref)


def _mixer_layer(x, w_in, ln_g, ln_b, w_sp, b_sp, conv_w, w_out, g1, b1, w_router):
    t = x.shape[0]
    tm = MIX_ROWS
    const2 = lambda i: (0, 0)
    r_in, r_out, r_shape, r_scratch = _route_specs(t, tm)
    return pl.pallas_call(
        _mixer_kernel,
        grid=(t // tm,),
        in_specs=[
            pl.BlockSpec((tm, D_MODEL), lambda i: (i, 0)),
            pl.BlockSpec((D_MODEL, MIX_IN), const2),
            pl.BlockSpec((1, A_WIDTH), const2),
            pl.BlockSpec((1, A_WIDTH), const2),
            pl.BlockSpec((A_GROUPS, CHUNK, CHUNK), lambda i: (0, 0, 0)),
            pl.BlockSpec((CHUNK, A_GROUPS), const2),
            pl.BlockSpec((CONV_W, B_WIDTH), const2),
            pl.BlockSpec((A_WIDTH + B_WIDTH, D_MODEL), const2),
            pl.BlockSpec((1, D_MODEL), const2),
            pl.BlockSpec((1, D_MODEL), const2),
        ] + r_in,
        out_specs=[pl.BlockSpec((tm, D_MODEL), lambda i: (i, 0))] + r_out,
        out_shape=[jax.ShapeDtypeStruct((t, D_MODEL), F32)] + r_shape,
        scratch_shapes=[pltpu.VMEM((8, B_WIDTH), F32), r_scratch],
        compiler_params=pltpu.CompilerParams(
            dimension_semantics=("arbitrary",), vmem_limit_bytes=VMEM_LIMIT),
        name="mixer_layer",
    )(x, w_in.astype(BF16), ln_g.reshape(1, -1), ln_b.reshape(1, -1), w_sp, b_sp.T,
      conv_w, w_out.astype(BF16), g1.reshape(1, -1), b1.reshape(1, -1),
      *_route_operands(*w_router, tm))


def _attn_kernel(sink_ref, x_ref, wqkv_ref, bqkv_ref, bias_ref, wo_ref, bo_ref,
                 g1_ref, b1_ref, wr_ref, tri_ref, x1_ref, route_ref, route_t_ref, count_ref,
                 kprev_ref, vprev_ref, base_ref):
    tm = x_ref.shape[0]
    i = pl.program_id(0)
    first = i % (SEQ // tm) == 0

    @pl.when(first)
    def _():
        kprev_ref[...] = jnp.zeros_like(kprev_ref)
        vprev_ref[...] = jnp.zeros_like(vprev_ref)

    x = x_ref[...]
    qkv = jnp.dot(x.astype(BF16), wqkv_ref[...], preferred_element_type=F32) + bqkv_ref[...]
    q = (qkv[:, :ATT_OUT] * (HEAD_DIM ** -0.5)).astype(BF16)

    lane = lax.broadcasted_iota(I32, (CHUNK, KV_WIDTH), 1)
    low = lane < HEAD_DIM

    def halves(t):
        swapped = pltpu.roll(t, HEAD_DIM, 1)
        zero = jnp.zeros_like(t)
        return [jnp.where(low, t, zero).astype(BF16), jnp.where(low, zero, swapped).astype(BF16),
                jnp.where(low, swapped, zero).astype(BF16), jnp.where(low, zero, t).astype(BF16)]

    def stacked(prev, cur, j):
        return jnp.concatenate([prev[2 * j], cur[2 * j], prev[2 * j + 1], cur[2 * j + 1]], axis=0)

    a = lax.broadcasted_iota(I32, (CHUNK, 2 * CHUNK), 0)
    c = lax.broadcasted_iota(I32, (CHUNK, 2 * CHUNK), 1)
    window = (c > a) & (c <= a + WINDOW)
    neg = jnp.finfo(F32).min
    n_pairs = N_HEADS // 2
    kv_of = lambda pair: (2 * pair) // GQA_GROUP

    k_prev = [kprev_ref[idx] for idx in range(2 * N_KV)]
    v_prev = [vprev_ref[idx] for idx in range(2 * N_KV)]
    o_blocks = []
    for blk in range(tm // CHUNK):
        rows = slice(blk * CHUNK, (blk + 1) * CHUNK)
        k_cur = halves(qkv[rows, ATT_OUT:ATT_OUT + KV_WIDTH])
        v_cur = halves(qkv[rows, ATT_OUT + KV_WIDTH:])
        k_rhs = [stacked(k_prev, k_cur, j) for j in range(N_KV)]
        v_rhs = [stacked(v_prev, v_cur, j) for j in range(N_KV)]
        k_prev, v_prev = k_cur, v_cur
        mask = window & (c >= jnp.where(first, CHUNK, 0)) if blk == 0 else window

        scores = [lax.dot_general(q[rows, pair * 2 * HEAD_DIM:(pair + 1) * 2 * HEAD_DIM],
                                  k_rhs[kv_of(pair)], (((1,), (1,)), ((), ())),
                                  preferred_element_type=F32) for pair in range(n_pairs)]
        probs, inv_denoms = [], []
        for pair in range(n_pairs):
            sc2 = scores[pair] + bias_ref[pair]
            sides = []
            for side in range(2):
                sc = jnp.where(mask, sc2[:, side * 2 * CHUNK:(side + 1) * 2 * CHUNK], neg)
                sink = sink_ref[2 * pair + side]
                m = jnp.maximum(jnp.max(sc, axis=-1, keepdims=True), sink)
                p = jnp.exp(sc - m)
                denom = jnp.sum(p, axis=-1, keepdims=True) + jnp.exp(sink - m)
                sides.append(p.astype(BF16))
                inv_denoms.append(jnp.broadcast_to(1.0 / denom, (CHUNK, HEAD_DIM)))
            probs.append(jnp.concatenate(sides, axis=1))
        outs = [jnp.dot(probs[pair], v_rhs[kv_of(pair)], preferred_element_type=F32)
                for pair in range(n_pairs)]
        o = jnp.concatenate(outs, axis=1) * jnp.concatenate(inv_denoms, axis=1)
        o_blocks.append(o.astype(BF16))
    for idx in range(2 * N_KV):
        kprev_ref[idx] = k_prev[idx]
        vprev_ref[idx] = v_prev[idx]
    o = jnp.concatenate(o_blocks, axis=0)
    m_out = jnp.dot(o, wo_ref[...], preferred_element_type=F32) + bo_ref[...]
    x1 = _layer_norm(ALPHA * x + m_out, g1_ref[...], b1_ref[...])
    x1_ref[...] = x1
    _route_tile(x1, wr_ref, tri_ref, route_ref, route_t_ref, count_ref, base_ref)


def _t5_bucket(rel):
    n = jnp.maximum(rel, 0)
    max_exact = N_BUCKETS // 2
    nf = jnp.maximum(n, 1).astype(F32)
    large = max_exact + (jnp.log(nf / max_exact) / math.log(MAX_DISTANCE / max_exact)
                         * (N_BUCKETS - max_exact)).astype(I32)
    large = jnp.minimum(large, N_BUCKETS - 1)
    return jnp.where(n < max_exact, n, large)


def _rel_bias(rel_table):
    a = jnp.arange(CHUNK)[:, None]
    c = jnp.arange(2 * CHUNK)[None, :]
    onehot = jax.nn.one_hot(_t5_bucket(a + CHUNK - c), N_BUCKETS, dtype=F32)
    bias = jnp.einsum('acb,bh->hac', onehot, rel_table.astype(F32),
                      precision=lax.Precision.HIGHEST)
    bias = bias.reshape(N_HEADS // 2, 2, CHUNK, 2 * CHUNK).transpose(0, 2, 1, 3)
    return bias.reshape(N_HEADS // 2, CHUNK, 4 * CHUNK)


def _attn_layer(x, bias, w_qkv, b_qkv, sinks, w_o, b_o, g1, b1, w_router):
    t = x.shape[0]
    tm = ATTN_ROWS
    const2 = lambda i: (0, 0)
    r_in, r_out, r_shape, r_scratch = _route_specs(t, tm)
    return pl.pallas_call(
        _attn_kernel,
        grid=(t // tm,),
        in_specs=[
            pl.BlockSpec(memory_space=pltpu.SMEM),
            pl.BlockSpec((tm, D_MODEL), lambda i: (i, 0)),
            pl.BlockSpec((D_MODEL, QKV_DIM), const2),
            pl.BlockSpec((1, QKV_DIM), const2),
            pl.BlockSpec((N_HEADS // 2, CHUNK, 4 * CHUNK), lambda i: (0, 0, 0)),
            pl.BlockSpec((ATT_OUT, D_MODEL), const2),
            pl.BlockSpec((1, D_MODEL), const2),
            pl.BlockSpec((1, D_MODEL), const2),
            pl.BlockSpec((1, D_MODEL), const2),
        ] + r_in,
        out_specs=[pl.BlockSpec((tm, D_MODEL), lambda i: (i, 0))] + r_out,
        out_shape=[jax.ShapeDtypeStruct((t, D_MODEL), F32)] + r_shape,
        scratch_shapes=[pltpu.VMEM((2 * N_KV, CHUNK, KV_WIDTH), BF16),
                        pltpu.VMEM((2 * N_KV, CHUNK, KV_WIDTH), BF16), r_scratch],
        compiler_params=pltpu.CompilerParams(
            dimension_semantics=("arbitrary",), vmem_limit_bytes=VMEM_LIMIT),
        name="attn_layer",
    )(sinks, x, w_qkv.astype(BF16), b_qkv.reshape(1, -1), bias, w_o.astype(BF16),
      b_o.reshape(1, -1), g1.reshape(1, -1), b1.reshape(1, -1), *_route_operands(*w_router, tm))


def _route_tile(x1, wr_ref, tri_ref, route_ref, route_t_ref, count_ref, base_ref):
    @pl.when(pl.program_id(0) == 0)
    def _():
        base_ref[...] = jnp.zeros_like(base_ref)

    tm = x1.shape[0]
    lt = lax.dot_general(wr_ref[...], x1.astype(BF16), (((1,), (1,)), ((), ())),
                         preferred_element_type=F32)
    sub = lax.broadcasted_iota(I32, (SUBLANES, tm), 0).astype(F32)
    ninf = -jnp.inf

    def first_argmax(vals):
        m = jnp.max(vals, axis=0, keepdims=True)
        idx = jnp.min(jnp.where(vals == m, sub, float(SUBLANES)), axis=0, keepdims=True)
        return m, idx

    is_g = sub < N_GROUPS
    g_rows = lt[0:SUBLANES]
    gmax, g_idx = first_argmax(jnp.where(is_g, g_rows, ninf))
    g_p = 1.0 / jnp.sum(jnp.where(is_g, jnp.exp(g_rows - gmax), 0.0), axis=0, keepdims=True)

    group_rows = lambda a, g: a[SUBLANES * (g + 1):SUBLANES * (g + 2)]
    el = group_rows(lt, 0)
    for g in range(1, N_GROUPS):
        el = jnp.where(g_idx == g, group_rows(lt, g), el)
    m1, j1 = first_argmax(el)
    m2, j2 = first_argmax(jnp.where(sub == j1, ninf, el))
    a2 = jnp.exp(m2 - m1)
    gate0 = g_p / (1.0 + a2)
    gate1 = g_p * a2 / (1.0 + a2)

    hit0 = [(g_idx == g) & (sub == j1) for g in range(N_GROUPS)]
    hit1 = [(g_idx == g) & (sub == j2) for g in range(N_GROUPS)]
    onehot = jnp.concatenate([jnp.where(h0 | h1, 1.0, 0.0) for h0, h1 in zip(hit0, hit1)],
                             axis=0)
    before = jnp.dot(onehot.astype(BF16), tri_ref[...], preferred_element_type=F32)
    before = before + base_ref[:, 0:1]

    def picked(hits):
        total = jnp.zeros((SUBLANES, tm), F32)
        for g in range(N_GROUPS):
            total = total + jnp.where(hits[g], before[SUBLANES * g:SUBLANES * (g + 1)], 0.0)
        return jnp.sum(total, axis=0, keepdims=True)

    rank0 = picked(hit0)
    rank1 = picked(hit1)
    base_ref[...] += jnp.sum(onehot, axis=1, keepdims=True)

    e_base = g_idx * EXPERTS_PER_GROUP
    rec_t = jnp.concatenate([e_base + j1, e_base + j2, rank0, rank1, gate0, gate1,
                             jnp.zeros((SUBLANES - 6, tm), F32)], axis=0)
    route_t_ref[...] = rec_t
    padded = jnp.concatenate([rec_t, jnp.zeros((LANES - SUBLANES, tm), F32)], axis=0)
    route_ref[...] = jnp.transpose(padded)
    count_ref[...] = base_ref[...]


def _route_specs(t, tm):
    const2 = lambda i: (0, 0)
    in_specs = [pl.BlockSpec((ROUTER_ROWS, D_MODEL), const2), pl.BlockSpec((tm, tm), const2)]
    out_specs = [
        pl.BlockSpec((tm, LANES), lambda i: (i, 0)),
        pl.BlockSpec((SUBLANES, tm), lambda i: (0, i)),
        pl.BlockSpec((N_EXPERTS, LANES), const2),
    ]
    out_shape = [
        jax.ShapeDtypeStruct((t, LANES), F32),
        jax.ShapeDtypeStruct((SUBLANES, t), F32),
        jax.ShapeDtypeStruct((N_EXPERTS, LANES), F32),
    ]
    return in_specs, out_specs, out_shape, pltpu.VMEM((N_EXPERTS, LANES), F32)


def _route_operands(w_group, w_expert, tm):
    pad = jnp.zeros((D_MODEL, SUBLANES - N_GROUPS), F32)
    w_rows = jnp.concatenate([w_group, pad, w_expert], axis=1).T
    return w_rows.astype(BF16), jnp.triu(jnp.ones((tm, tm), BF16), 1)


def _plan(route_t, count_lanes):
    t = route_t.shape[1]
    rb = EXPERT_ROWS
    experts = jnp.arange(N_EXPERTS, dtype=I32)
    counts = count_lanes[:, 0].astype(I32)
    end = jnp.cumsum(counts)
    start = end - counts
    e = route_t[R_E0:R_E1 + 1].astype(I32)
    rank = route_t[R_RANK0:R_RANK1 + 1].astype(I32)
    onehot = e[:, None, :] == experts[None, :, None]
    dest = jnp.sum(jnp.where(onehot, start[None, :, None], 0), axis=1) + rank

    first_blk = start // rb
    last_blk = jnp.maximum(end - 1, 0) // rb
    n_steps_e = jnp.where(counts > 0, last_blk - first_blk + 1, 0)
    step_end = jnp.cumsum(n_steps_e)
    step_start = step_end - n_steps_e
    n_steps = step_end[-1]
    max_steps = t * TOP_K // rb + N_EXPERTS - 1
    s = jnp.minimum(jnp.arange(max_steps, dtype=I32), n_steps - 1)
    step_e = jnp.minimum(jnp.sum((step_end[None, :] <= s[:, None]).astype(I32), axis=1),
                         N_EXPERTS - 1)
    pick = step_e[:, None] == experts[None, :]
    take = lambda v: jnp.sum(jnp.where(pick, v[None, :], 0), axis=1)
    step_blk = take(first_blk) + s - take(step_start)
    step_lo = jnp.clip(take(start) - step_blk * rb, 0, rb)
    step_hi = jnp.clip(take(end) - step_blk * rb, 0, rb)
    return dest, (step_blk, step_e, step_lo, step_hi, n_steps.reshape(1))


def _dest_blocks(dest, tm):
    t = dest.shape[1]
    return dest.reshape(TOP_K, t // tm, tm).transpose(1, 0, 2).reshape(t // tm, 1, TOP_K * tm)


def _to_row_tiles(x):
    return x.reshape(x.shape[0], ROW_TILE[0], ROW_TILE[1])


def _from_row_tiles(x):
    return x.reshape(x.shape[0], D_MODEL)


PACKED_ROW_TILE = (SUBLANES // 2, LANES)
HIGH_HALF = 0xFFFF0000


def _pack_rows(x):
    half = D_MODEL // 2
    bits = lambda v: lax.bitcast_convert_type(v.astype(BF16).astype(F32), U32)
    words = (bits(x[:, :half]) >> 16) | (bits(x[:, half:]) & jnp.uint32(HIGH_HALF))
    return words.reshape((x.shape[0],) + PACKED_ROW_TILE)


def _unpack_rows(words):
    w = words.reshape(words.shape[0], D_MODEL // 2)
    low = lax.bitcast_convert_type(w << 16, F32).astype(BF16)
    high = lax.bitcast_convert_type(w & jnp.uint32(HIGH_HALF), F32).astype(BF16)
    return jnp.concatenate([low, high], axis=1)


def _dispatch_kernel(dest_ref, x1_ref, xs_hbm, xbuf, row_sems):
    tm = x1_ref.shape[0]
    i = pl.program_id(0)
    n = pl.num_programs(0)
    slot = i % DISPATCH_SLOTS

    def drain_rows(s):
        for k in range(TOP_K):
            pltpu.make_async_copy(xbuf.at[s], xs_hbm.at[pl.ds(0, tm)], row_sems.at[s]).wait()

    @pl.when(i >= DISPATCH_SLOTS)
    def _():
        drain_rows(slot)

    xbuf[slot] = _pack_rows(x1_ref[...])

    def issue(g, carry):
        for j in range(ROWS_PER_ISSUE):
            r = g * ROWS_PER_ISSUE + j
            for k in range(TOP_K):
                pltpu.make_async_copy(xbuf.at[slot, r], xs_hbm.at[dest_ref[0, 0, k * tm + r]],
                                      row_sems.at[slot]).start(priority=k)
        return carry

    lax.fori_loop(0, tm // ROWS_PER_ISSUE, issue, 0)

    @pl.when(i == n - 1)
    def _():
        for back in range(DISPATCH_SLOTS):
            drain_rows((i - back) % DISPATCH_SLOTS)


def _dispatch(x1, dest):
    t = x1.shape[0]
    tm = DISPATCH_ROWS
    return pl.pallas_call(
        _dispatch_kernel,
        grid=(t // tm,),
        in_specs=[
            pl.BlockSpec((1, 1, TOP_K * tm), lambda i: (i, 0, 0), memory_space=pltpu.SMEM),
            pl.BlockSpec((tm, D_MODEL), lambda i: (i, 0)),
        ],
        out_specs=pl.BlockSpec(memory_space=pl.ANY),
        out_shape=jax.ShapeDtypeStruct((t * TOP_K,) + PACKED_ROW_TILE, U32),
        scratch_shapes=[
            pltpu.VMEM((DISPATCH_SLOTS, tm) + PACKED_ROW_TILE, U32),
            pltpu.SemaphoreType.DMA((DISPATCH_SLOTS,)),
        ],
        compiler_params=pltpu.CompilerParams(
            dimension_semantics=("arbitrary",), vmem_limit_bytes=VMEM_LIMIT),
        name="dispatch",
    )(_dest_blocks(dest, tm), x1)


def _expert_kernel(blk_ref, e_ref, lo_ref, hi_ref, n_steps_ref, xs_ref, wg_ref, wu_ref, wd_ref,
                   ys_ref, wg_bf, wu_bf, wd_bf):
    s = pl.program_id(0)
    prev = jnp.maximum(s - 1, 0)
    new_expert = (s == 0) | (e_ref[s] != e_ref[prev])
    new_block = (s == 0) | (blk_ref[s] != blk_ref[prev])

    @pl.when(new_expert)
    def _():
        wg_bf[...] = wg_ref[0, 0].astype(BF16)
        wu_bf[...] = wu_ref[0, 0].astype(BF16)
        wd_bf[...] = wd_ref[0, 0].astype(BF16)

    rb = xs_ref.shape[0]
    half = rb // 2
    lo, hi = lo_ref[s], hi_ref[s]
    active = s < n_steps_ref[0]
    both_halves = (lo < half) & (hi > half)

    def masked_ffn(x_tiles, row0):
        xb = _unpack_rows(x_tiles)
        gate = jnp.dot(xb, wg_bf[...], preferred_element_type=F32)
        up = jnp.dot(xb, wu_bf[...], preferred_element_type=F32)
        hid = (gate * jax.nn.sigmoid(gate) * up).astype(BF16)
        y = jnp.dot(hid, wd_bf[...], preferred_element_type=F32)
        rows = lax.broadcasted_iota(I32, y.shape, 0) + row0
        return _to_row_tiles(jnp.where((rows >= lo) & (rows < hi), y, 0.0))

    @pl.when(active & both_halves)
    def _():
        y = masked_ffn(xs_ref[...], 0)

        @pl.when(new_block)
        def _():
            ys_ref[...] = y

        @pl.when(jnp.logical_not(new_block))
        def _():
            ys_ref[...] += y

    @pl.when(active & jnp.logical_not(both_halves))
    def _():
        off = pl.multiple_of(jnp.where(lo < half, 0, half), half)
        y = masked_ffn(xs_ref[pl.ds(off, half)], off)

        @pl.when(new_block)
        def _():
            ys_ref[pl.ds(off, half)] = y
            ys_ref[pl.ds(half - off, half)] = jnp.zeros_like(y)

        @pl.when(jnp.logical_not(new_block))
        def _():
            ys_ref[pl.ds(off, half)] += y


def _expert_ffn(xs, steps, layer, wg, wu, wd):
    step_blk, step_e, step_lo, step_hi, n_steps = steps
    rb = EXPERT_ROWS
    row_map = lambda s, blk, e, lo, hi, n: (blk[s], 0, 0)
    w_map = lambda s, blk, e, lo, hi, n: (layer, e[s], 0, 0)
    grid_spec = pltpu.PrefetchScalarGridSpec(
        num_scalar_prefetch=5,
        grid=(step_blk.shape[0],),
        in_specs=[
            pl.BlockSpec((rb,) + PACKED_ROW_TILE, row_map),
            pl.BlockSpec((1, 1, D_MODEL, D_EXPERT), w_map),
            pl.BlockSpec((1, 1, D_MODEL, D_EXPERT), w_map),
            pl.BlockSpec((1, 1, D_EXPERT, D_MODEL), w_map),
        ],
        out_specs=pl.BlockSpec((rb,) + ROW_TILE, row_map),
        scratch_shapes=[
            pltpu.VMEM((D_MODEL, D_EXPERT), BF16),
            pltpu.VMEM((D_MODEL, D_EXPERT), BF16),
            pltpu.VMEM((D_EXPERT, D_MODEL), BF16),
        ],
    )
    return pl.pallas_call(
        _expert_kernel,
        grid_spec=grid_spec,
        out_shape=jax.ShapeDtypeStruct((xs.shape[0],) + ROW_TILE, F32),
        compiler_params=pltpu.CompilerParams(
            dimension_semantics=("arbitrary",), vmem_limit_bytes=VMEM_LIMIT),
        name="expert_ffn",
    )(step_blk, step_e, step_lo, step_hi, n_steps, xs, wg, wu, wd)


def _combine_kernel(dest_ref, dest_next_ref, x1_ref, route_ref, g2_ref, b2_ref, ys_hbm,
                    x2_ref, ybuf, sems):
    tm = x1_ref.shape[0]
    i = pl.program_id(0)
    n = pl.num_programs(0)
    slot = i % 2

    def gather(d_ref, to_slot):
        def issue(g, carry):
            for j in range(ROWS_PER_ISSUE):
                r = g * ROWS_PER_ISSUE + j
                for k in range(TOP_K):
                    pltpu.make_async_copy(ys_hbm.at[d_ref[0, 0, k * tm + r]],
                                          ybuf.at[to_slot, k, r],
                                          sems.at[to_slot]).start(priority=k)
            return carry
        lax.fori_loop(0, tm // ROWS_PER_ISSUE, issue, 0)

    @pl.when(i == 0)
    def _():
        gather(dest_ref, slot)

    @pl.when(i + 1 < n)
    def _():
        gather(dest_next_ref, 1 - slot)

    for k in range(TOP_K):
        pltpu.make_async_copy(ys_hbm.at[pl.ds(0, tm)], ybuf.at[slot, k], sems.at[slot]).wait()

    route = route_ref[...]
    f = (route[:, R_GATE0:R_GATE0 + 1] * _from_row_tiles(ybuf[slot, 0])
         + route[:, R_GATE1:R_GATE1 + 1] * _from_row_tiles(ybuf[slot, 1]))
    x2_ref[...] = _layer_norm(ALPHA * x1_ref[...] + f, g2_ref[...], b2_ref[...])


def _combine_ln(x1, ys, dest, route, g2, b2):
    t = x1.shape[0]
    tm = COMBINE_ROWS
    n = t // tm
    row = lambda i: (i, 0)
    const2 = lambda i: (0, 0)
    dest_blocks = _dest_blocks(dest, tm)
    smem_block = lambda index_map: pl.BlockSpec((1, 1, TOP_K * tm), index_map,
                                                memory_space=pltpu.SMEM)
    return pl.pallas_call(
        _combine_kernel,
        grid=(n,),
        in_specs=[
            smem_block(lambda i: (i, 0, 0)),
            smem_block(lambda i: (jnp.minimum(i + 1, n - 1), 0, 0)),
            pl.BlockSpec((tm, D_MODEL), row),
            pl.BlockSpec((tm, LANES), row),
            pl.BlockSpec((1, D_MODEL), const2),
            pl.BlockSpec((1, D_MODEL), const2),
            pl.BlockSpec(memory_space=pl.ANY),
        ],
        out_specs=pl.BlockSpec((tm, D_MODEL), row),
        out_shape=jax.ShapeDtypeStruct((t, D_MODEL), F32),
        scratch_shapes=[
            pltpu.VMEM((2, TOP_K, tm) + ROW_TILE, F32),
            pltpu.SemaphoreType.DMA((2,)),
        ],
        compiler_params=pltpu.CompilerParams(
            dimension_semantics=("arbitrary",), vmem_limit_bytes=VMEM_LIMIT),
        name="combine_ln",
    )(dest_blocks, dest_blocks, x1, route, g2.reshape(1, -1), b2.reshape(1, -1), ys)


def _moe_layer(x1, route, route_t, counts, layer, wg, wu, wd, g2, b2):
    dest, steps = _plan(route_t, counts)
    xs = _dispatch(x1, dest)
    ys = _expert_ffn(xs, steps, layer, wg, wu, wd)
    return _combine_ln(x1, ys, dest, route, g2, b2)


def kernel(x, rel_bias_table, mix_w_in, gmlp_ln_g, gmlp_ln_b, gmlp_w_spatial, gmlp_b_spatial, conv_w, mix_w_out, attn_w_qkv, attn_b_qkv, attn_sinks, attn_w_o, attn_b_o, ln1_g, ln1_b, ln2_g, ln2_b, router_group, router_expert, expert_w_gate, expert_w_up, expert_w_down):
    bsz, s, d = x.shape
    assert (s, d) == (SEQ, D_MODEL)
    x = x.reshape(bsz * s, d)
    bias = _rel_bias(rel_bias_table)
    for l in range(DEPTH):
        i = l // 2
        w_router = (router_group[l], router_expert[l])
        if l % 2 == 0:
            x1, *routing = _mixer_layer(x, mix_w_in[i], gmlp_ln_g[i], gmlp_ln_b[i],
                                        gmlp_w_spatial[i], gmlp_b_spatial[i], conv_w[i],
                                        mix_w_out[i], ln1_g[l], ln1_b[l], w_router)
        else:
            x1, *routing = _attn_layer(x, bias, attn_w_qkv[i], attn_b_qkv[i], attn_sinks[i],
                                       attn_w_o[i], attn_b_o[i], ln1_g[l], ln1_b[l], w_router)
        x = _moe_layer(x1, *routing, l, expert_w_gate, expert_w_up, expert_w_down,
                       ln2_g[l], ln2_b[l])
    return x.reshape(bsz, s, d)
```

```python
import math

import jax
import jax.numpy as jnp
from jax import lax
from jax.experimental import pallas as pl
from jax.experimental.pallas import tpu as pltpu

D_MODEL = 1024
SEQ = 16384
DEPTH = 4
CHUNK = 128
A_GROUPS = 4
A_CH = 128
A_WIDTH = A_GROUPS * A_CH
B_WIDTH = 512
CONV_W = 3
MIX_IN = 2 * A_WIDTH + 3 * B_WIDTH
N_HEADS = 16
N_KV = 2
HEAD_DIM = 64
GQA_GROUP = N_HEADS // N_KV
WINDOW = 128
QKV_DIM = (N_HEADS + 2 * N_KV) * HEAD_DIM
ATT_OUT = N_HEADS * HEAD_DIM
KV_WIDTH = N_KV * HEAD_DIM
N_BUCKETS = 32
MAX_DISTANCE = 128
N_GROUPS = 4
EXPERTS_PER_GROUP = 8
N_EXPERTS = N_GROUPS * EXPERTS_PER_GROUP
TOP_K = 2
D_EXPERT = 512
ALPHA = (2 * DEPTH) ** 0.25
LN_EPS = 1e-5

LANES = 128
SUBLANES = 8
ROW_TILE = (SUBLANES, LANES)
assert D_MODEL == SUBLANES * LANES
ROUTER_ROWS = SUBLANES + N_EXPERTS
MIX_ROWS = 1024
ATTN_ROWS = 1024
DISPATCH_ROWS = 1024
COMBINE_ROWS = 256
EXPERT_ROWS = 1024
ROWS_PER_ISSUE = 16
DISPATCH_SLOTS = 3
VMEM_LIMIT = 56 * 1024 * 1024

N_ROUTE_FIELDS = 6
R_E0, R_E1, R_RANK0, R_RANK1, R_GATE0, R_GATE1 = range(N_ROUTE_FIELDS)

F32 = jnp.float32
BF16 = jnp.bfloat16
I32 = jnp.int32
U32 = jnp.uint32


def _layer_norm(x, g, b):
    mu = jnp.mean(x, axis=-1, keepdims=True)
    xc = x - mu
    var = jnp.mean(xc * xc, axis=-1, keepdims=True)
    return xc * lax.rsqrt(var + LN_EPS) * g + b


def _gelu(x):
    return 0.5 * x * (1.0 + lax.erf(x * (2.0 ** -0.5)))


def _mixer_kernel(x_ref, win_ref, lng_ref, lnb_ref, wsp_ref, bsp_ref, cw_ref, wout_ref,
                  g1_ref, b1_ref, wr_ref, tri_ref, x1_ref, route_ref, route_t_ref, count_ref,
                  ztail_ref, base_ref):
    tm = x_ref.shape[0]
    i = pl.program_id(0)

    @pl.when(i % (SEQ // tm) == 0)
    def _():
        ztail_ref[...] = jnp.zeros_like(ztail_ref)

    x = x_ref[...]
    h = jnp.dot(x.astype(BF16), win_ref[...], preferred_element_type=F32)
    u = _gelu(h[:, :A_WIDTH])
    v = _gelu(h[:, A_WIDTH:2 * A_WIDTH])
    o = 2 * A_WIDTH
    g_b = h[:, o:o + B_WIDTH]
    g_c = h[:, o + B_WIDTH:o + 2 * B_WIDTH]
    hb = h[:, o + 2 * B_WIDTH:o + 3 * B_WIDTH]

    v = _layer_norm(v, lng_ref[...], lnb_ref[...]).astype(BF16)
    n_chunks = tm // CHUNK
    ri = lax.broadcasted_iota(I32, (CHUNK, CHUNK), 0)
    ci = lax.broadcasted_iota(I32, (CHUNK, CHUNK), 1)
    causal = ci <= ri
    sv_cols = [[None] * A_GROUPS for _ in range(n_chunks)]
    for g in range(A_GROUPS):
        ws = jnp.where(causal, wsp_ref[g], 0.0).astype(BF16)
        vg = jnp.concatenate(
            [v[c * CHUNK:(c + 1) * CHUNK, g * A_CH:(g + 1) * A_CH] for c in range(n_chunks)],
            axis=1)
        sg = jnp.dot(ws, vg, preferred_element_type=F32) + bsp_ref[:, g:g + 1]
        for c in range(n_chunks):
            sv_cols[c][g] = sg[:, c * A_CH:(c + 1) * A_CH]
    sv = jnp.concatenate([jnp.concatenate(row, axis=1) for row in sv_cols], axis=0)
    y_a = u * sv

    z = g_c * hb
    rows = lax.broadcasted_iota(I32, z.shape, 0)
    tail = ztail_ref[...]
    prev1 = tail[SUBLANES - 1:SUBLANES, :]
    prev2 = tail[SUBLANES - 2:SUBLANES - 1, :]
    zm1 = jnp.where(rows == 0, prev1, pltpu.roll(z, 1, 0))
    zm2 = jnp.where(rows == 0, prev2, jnp.where(rows == 1, prev1, pltpu.roll(z, 2, 0)))
    conv = cw_ref[0:1, :] * zm2 + cw_ref[1:2, :] * zm1 + cw_ref[2:3, :] * z
    y_b = g_b * conv
    ztail_ref[...] = z[tm - SUBLANES:tm, :]

    y = jnp.concatenate([y_a, y_b], axis=1).astype(BF16)
    m = jnp.dot(y, wout_ref[...], preferred_element_type=F32)
    x1 = _layer_norm(ALPHA * x + m, g1_ref[...], b1_ref[...])
    x1_ref[...] = x1
    _route_tile(x1, wr_ref, tri_ref, route_ref, route_t_ref, count_ref, base_ref)


def _mixer_layer(x, w_in, ln_g, ln_b, w_sp, b_sp, conv_w, w_out, g1, b1, w_router):
    t = x.shape[0]
    tm = MIX_ROWS
    const2 = lambda i: (0, 0)
    r_in, r_out, r_shape, r_scratch = _route_specs(t, tm)
    return pl.pallas_call(
        _mixer_kernel,
        grid=(t // tm,),
        in_specs=[
            pl.BlockSpec((tm, D_MODEL), lambda i: (i, 0)),
            pl.BlockSpec((D_MODEL, MIX_IN), const2),
            pl.BlockSpec((1, A_WIDTH), const2),
            pl.BlockSpec((1, A_WIDTH), const2),
            pl.BlockSpec((A_GROUPS, CHUNK, CHUNK), lambda i: (0, 0, 0)),
            pl.BlockSpec((CHUNK, A_GROUPS), const2),
            pl.BlockSpec((CONV_W, B_WIDTH), const2),
            pl.BlockSpec((A_WIDTH + B_WIDTH, D_MODEL), const2),
            pl.BlockSpec((1, D_MODEL), const2),
            pl.BlockSpec((1, D_MODEL), const2),
        ] + r_in,
        out_specs=[pl.BlockSpec((tm, D_MODEL), lambda i: (i, 0))] + r_out,
        out_shape=[jax.ShapeDtypeStruct((t, D_MODEL), F32)] + r_shape,
        scratch_shapes=[pltpu.VMEM((SUBLANES, B_WIDTH), F32), r_scratch],
        compiler_params=pltpu.CompilerParams(
            dimension_semantics=("arbitrary",), vmem_limit_bytes=VMEM_LIMIT),
        name="mixer_layer",
    )(x, w_in.astype(BF16), ln_g.reshape(1, -1), ln_b.reshape(1, -1), w_sp, b_sp.T,
      conv_w, w_out.astype(BF16), g1.reshape(1, -1), b1.reshape(1, -1),
      *_route_operands(*w_router, tm))


def _attn_kernel(sink_ref, x_ref, wqkv_ref, bqkv_ref, bias_ref, wo_ref, bo_ref,
                 g1_ref, b1_ref, wr_ref, tri_ref, x1_ref, route_ref, route_t_ref, count_ref,
                 kprev_ref, vprev_ref, base_ref):
    tm = x_ref.shape[0]
    i = pl.program_id(0)
    first = i % (SEQ // tm) == 0

    @pl.when(first)
    def _():
        kprev_ref[...] = jnp.zeros_like(kprev_ref)
        vprev_ref[...] = jnp.zeros_like(vprev_ref)

    x = x_ref[...]
    qkv = jnp.dot(x.astype(BF16), wqkv_ref[...], preferred_element_type=F32) + bqkv_ref[...]
    q = (qkv[:, :ATT_OUT] * (HEAD_DIM ** -0.5)).astype(BF16)

    lane = lax.broadcasted_iota(I32, (CHUNK, KV_WIDTH), 1)
    low = lane < HEAD_DIM

    def halves(t):
        swapped = pltpu.roll(t, HEAD_DIM, 1)
        zero = jnp.zeros_like(t)
        return [jnp.where(low, t, zero).astype(BF16), jnp.where(low, zero, swapped).astype(BF16),
                jnp.where(low, swapped, zero).astype(BF16), jnp.where(low, zero, t).astype(BF16)]

    def stacked(prev, cur, j):
        return jnp.concatenate([prev[2 * j], cur[2 * j], prev[2 * j + 1], cur[2 * j + 1]], axis=0)

    a = lax.broadcasted_iota(I32, (CHUNK, 2 * CHUNK), 0)
    c = lax.broadcasted_iota(I32, (CHUNK, 2 * CHUNK), 1)
    window = (c > a) & (c <= a + WINDOW)
    neg = jnp.finfo(F32).min
    n_pairs = N_HEADS // 2
    kv_of = lambda pair: (2 * pair) // GQA_GROUP

    k_prev = [kprev_ref[idx] for idx in range(2 * N_KV)]
    v_prev = [vprev_ref[idx] for idx in range(2 * N_KV)]
    o_blocks = []
    for blk in range(tm // CHUNK):
        rows = slice(blk * CHUNK, (blk + 1) * CHUNK)
        k_cur = halves(qkv[rows, ATT_OUT:ATT_OUT + KV_WIDTH])
        v_cur = halves(qkv[rows, ATT_OUT + KV_WIDTH:])
        k_rhs = [stacked(k_prev, k_cur, j) for j in range(N_KV)]
        v_rhs = [stacked(v_prev, v_cur, j) for j in range(N_KV)]
        k_prev, v_prev = k_cur, v_cur
        mask = window & (c >= jnp.where(first, CHUNK, 0)) if blk == 0 else window

        scores = [lax.dot_general(q[rows, pair * 2 * HEAD_DIM:(pair + 1) * 2 * HEAD_DIM],
                                  k_rhs[kv_of(pair)], (((1,), (1,)), ((), ())),
                                  preferred_element_type=F32) for pair in range(n_pairs)]
        probs, inv_denoms = [], []
        for pair in range(n_pairs):
            sc2 = scores[pair] + bias_ref[pair]
            sides = []
            for side in range(2):
                sc = jnp.where(mask, sc2[:, side * 2 * CHUNK:(side + 1) * 2 * CHUNK], neg)
                sink = sink_ref[2 * pair + side]
                m = jnp.maximum(jnp.max(sc, axis=-1, keepdims=True), sink)
                p = jnp.exp(sc - m)
                denom = jnp.sum(p, axis=-1, keepdims=True) + jnp.exp(sink - m)
                sides.append(p.astype(BF16))
                inv_denoms.append(jnp.broadcast_to(1.0 / denom, (CHUNK, HEAD_DIM)))
            probs.append(jnp.concatenate(sides, axis=1))
        outs = [jnp.dot(probs[pair], v_rhs[kv_of(pair)], preferred_element_type=F32)
                for pair in range(n_pairs)]
        o = jnp.concatenate(outs, axis=1) * jnp.concatenate(inv_denoms, axis=1)
        o_blocks.append(o.astype(BF16))
    for idx in range(2 * N_KV):
        kprev_ref[idx] = k_prev[idx]
        vprev_ref[idx] = v_prev[idx]
    o = jnp.concatenate(o_blocks, axis=0)
    m_out = jnp.dot(o, wo_ref[...], preferred_element_type=F32) + bo_ref[...]
    x1 = _layer_norm(ALPHA * x + m_out, g1_ref[...], b1_ref[...])
    x1_ref[...] = x1
    _route_tile(x1, wr_ref, tri_ref, route_ref, route_t_ref, count_ref, base_ref)


def _t5_bucket(rel):
    n = jnp.maximum(rel, 0)
    max_exact = N_BUCKETS // 2
    nf = jnp.maximum(n, 1).astype(F32)
    large = max_exact + (jnp.log(nf / max_exact) / math.log(MAX_DISTANCE / max_exact)
                         * (N_BUCKETS - max_exact)).astype(I32)
    large = jnp.minimum(large, N_BUCKETS - 1)
    return jnp.where(n < max_exact, n, large)


def _rel_bias(rel_table):
    a = jnp.arange(CHUNK)[:, None]
    c = jnp.arange(2 * CHUNK)[None, :]
    onehot = jax.nn.one_hot(_t5_bucket(a + CHUNK - c), N_BUCKETS, dtype=F32)
    bias = jnp.einsum('acb,bh->hac', onehot, rel_table.astype(F32),
                      precision=lax.Precision.HIGHEST)
    bias = bias.reshape(N_HEADS // 2, 2, CHUNK, 2 * CHUNK).transpose(0, 2, 1, 3)
    return bias.reshape(N_HEADS // 2, CHUNK, 4 * CHUNK)


def _attn_layer(x, bias, w_qkv, b_qkv, sinks, w_o, b_o, g1, b1, w_router):
    t = x.shape[0]
    tm = ATTN_ROWS
    const2 = lambda i: (0, 0)
    r_in, r_out, r_shape, r_scratch = _route_specs(t, tm)
    return pl.pallas_call(
        _attn_kernel,
        grid=(t // tm,),
        in_specs=[
            pl.BlockSpec(memory_space=pltpu.SMEM),
            pl.BlockSpec((tm, D_MODEL), lambda i: (i, 0)),
            pl.BlockSpec((D_MODEL, QKV_DIM), const2),
            pl.BlockSpec((1, QKV_DIM), const2),
            pl.BlockSpec((N_HEADS // 2, CHUNK, 4 * CHUNK), lambda i: (0, 0, 0)),
            pl.BlockSpec((ATT_OUT, D_MODEL), const2),
            pl.BlockSpec((1, D_MODEL), const2),
            pl.BlockSpec((1, D_MODEL), const2),
            pl.BlockSpec((1, D_MODEL), const2),
        ] + r_in,
        out_specs=[pl.BlockSpec((tm, D_MODEL), lambda i: (i, 0))] + r_out,
        out_shape=[jax.ShapeDtypeStruct((t, D_MODEL), F32)] + r_shape,
        scratch_shapes=[pltpu.VMEM((2 * N_KV, CHUNK, KV_WIDTH), BF16),
                        pltpu.VMEM((2 * N_KV, CHUNK, KV_WIDTH), BF16), r_scratch],
        compiler_params=pltpu.CompilerParams(
            dimension_semantics=("arbitrary",), vmem_limit_bytes=VMEM_LIMIT),
        name="attn_layer",
    )(sinks, x, w_qkv.astype(BF16), b_qkv.reshape(1, -1), bias, w_o.astype(BF16),
      b_o.reshape(1, -1), g1.reshape(1, -1), b1.reshape(1, -1), *_route_operands(*w_router, tm))


def _route_tile(x1, wr_ref, tri_ref, route_ref, route_t_ref, count_ref, base_ref):
    @pl.when(pl.program_id(0) == 0)
    def _():
        base_ref[...] = jnp.zeros_like(base_ref)

    tm = x1.shape[0]
    lt = lax.dot_general(wr_ref[...], x1.astype(BF16), (((1,), (1,)), ((), ())),
                         preferred_element_type=F32)
    sub = lax.broadcasted_iota(I32, (SUBLANES, tm), 0).astype(F32)
    ninf = -jnp.inf

    def first_argmax(vals):
        m = jnp.max(vals, axis=0, keepdims=True)
        idx = jnp.min(jnp.where(vals == m, sub, float(SUBLANES)), axis=0, keepdims=True)
        return m, idx

    is_g = sub < N_GROUPS
    g_rows = lt[0:SUBLANES]
    gmax, g_idx = first_argmax(jnp.where(is_g, g_rows, ninf))
    g_p = 1.0 / jnp.sum(jnp.where(is_g, jnp.exp(g_rows - gmax), 0.0), axis=0, keepdims=True)

    group_rows = lambda a, g: a[SUBLANES * (g + 1):SUBLANES * (g + 2)]
    el = group_rows(lt, 0)
    for g in range(1, N_GROUPS):
        el = jnp.where(g_idx == g, group_rows(lt, g), el)
    m1, j1 = first_argmax(el)
    m2, j2 = first_argmax(jnp.where(sub == j1, ninf, el))
    a2 = jnp.exp(m2 - m1)
    gate0 = g_p / (1.0 + a2)
    gate1 = g_p * a2 / (1.0 + a2)

    hit0 = [(g_idx == g) & (sub == j1) for g in range(N_GROUPS)]
    hit1 = [(g_idx == g) & (sub == j2) for g in range(N_GROUPS)]
    onehot = jnp.concatenate([jnp.where(h0 | h1, 1.0, 0.0) for h0, h1 in zip(hit0, hit1)],
                             axis=0)
    before = jnp.dot(onehot.astype(BF16), tri_ref[...], preferred_element_type=F32)
    before = before + base_ref[:, 0:1]

    def picked(hits):
        total = jnp.zeros((SUBLANES, tm), F32)
        for g in range(N_GROUPS):
            total = total + jnp.where(hits[g], before[SUBLANES * g:SUBLANES * (g + 1)], 0.0)
        return jnp.sum(total, axis=0, keepdims=True)

    rank0 = picked(hit0)
    rank1 = picked(hit1)
    base_ref[...] += jnp.sum(onehot, axis=1, keepdims=True)

    e_base = g_idx * EXPERTS_PER_GROUP
    rec_t = jnp.concatenate([e_base + j1, e_base + j2, rank0, rank1, gate0, gate1,
                             jnp.zeros((SUBLANES - N_ROUTE_FIELDS, tm), F32)], axis=0)
    route_t_ref[...] = rec_t
    padded = jnp.concatenate([rec_t, jnp.zeros((LANES - SUBLANES, tm), F32)], axis=0)
    route_ref[...] = jnp.transpose(padded)
    count_ref[...] = base_ref[...]


def _route_specs(t, tm):
    const2 = lambda i: (0, 0)
    in_specs = [pl.BlockSpec((ROUTER_ROWS, D_MODEL), const2), pl.BlockSpec((tm, tm), const2)]
    out_specs = [
        pl.BlockSpec((tm, LANES), lambda i: (i, 0)),
        pl.BlockSpec((SUBLANES, tm), lambda i: (0, i)),
        pl.BlockSpec((N_EXPERTS, LANES), const2),
    ]
    out_shape = [
        jax.ShapeDtypeStruct((t, LANES), F32),
        jax.ShapeDtypeStruct((SUBLANES, t), F32),
        jax.ShapeDtypeStruct((N_EXPERTS, LANES), F32),
    ]
    return in_specs, out_specs, out_shape, pltpu.VMEM((N_EXPERTS, LANES), F32)


def _route_operands(w_group, w_expert, tm):
    pad = jnp.zeros((D_MODEL, SUBLANES - N_GROUPS), F32)
    w_rows = jnp.concatenate([w_group, pad, w_expert], axis=1).T
    return w_rows.astype(BF16), jnp.triu(jnp.ones((tm, tm), BF16), 1)


def _plan(route_t, count_lanes):
    t = route_t.shape[1]
    rb = EXPERT_ROWS
    experts = jnp.arange(N_EXPERTS, dtype=I32)
    counts = count_lanes[:, 0].astype(I32)
    end = jnp.cumsum(counts)
    start = end - counts
    e = route_t[R_E0:R_E1 + 1].astype(I32)
    rank = route_t[R_RANK0:R_RANK1 + 1].astype(I32)
    onehot = e[:, None, :] == experts[None, :, None]
    dest = jnp.sum(jnp.where(onehot, start[None, :, None], 0), axis=1) + rank

    first_blk = start // rb
    last_blk = jnp.maximum(end - 1, 0) // rb
    n_steps_e = jnp.where(counts > 0, last_blk - first_blk + 1, 0)
    step_end = jnp.cumsum(n_steps_e)
    step_start = step_end - n_steps_e
    n_steps = step_end[-1]
    max_steps = t * TOP_K // rb + N_EXPERTS - 1
    s = jnp.minimum(jnp.arange(max_steps, dtype=I32), n_steps - 1)
    step_e = jnp.minimum(jnp.sum((step_end[None, :] <= s[:, None]).astype(I32), axis=1),
                         N_EXPERTS - 1)
    pick = step_e[:, None] == experts[None, :]
    take = lambda v: jnp.sum(jnp.where(pick, v[None, :], 0), axis=1)
    step_blk = take(first_blk) + s - take(step_start)
    step_lo = jnp.clip(take(start) - step_blk * rb, 0, rb)
    step_hi = jnp.clip(take(end) - step_blk * rb, 0, rb)
    return dest, (step_blk, step_e, step_lo, step_hi, n_steps.reshape(1))


def _dest_blocks(dest, tm):
    t = dest.shape[1]
    return dest.reshape(TOP_K, t // tm, tm).transpose(1, 0, 2).reshape(t // tm, 1, TOP_K * tm)


def _to_row_tiles(x):
    return x.reshape(x.shape[0], ROW_TILE[0], ROW_TILE[1])


def _from_row_tiles(x):
    return x.reshape(x.shape[0], D_MODEL)


PACKED_ROW_TILE = (SUBLANES // 2, LANES)
BF16_BITS = 16
HIGH_HALF = 0xFFFF0000


def _pack_rows(x):
    half = D_MODEL // 2
    bits = lambda v: lax.bitcast_convert_type(v.astype(BF16).astype(F32), U32)
    words = (bits(x[:, :half]) >> BF16_BITS) | (bits(x[:, half:]) & jnp.uint32(HIGH_HALF))
    return words.reshape((x.shape[0],) + PACKED_ROW_TILE)


def _unpack_rows(words):
    w = words.reshape(words.shape[0], D_MODEL // 2)
    low = lax.bitcast_convert_type(w << BF16_BITS, F32).astype(BF16)
    high = lax.bitcast_convert_type(w & jnp.uint32(HIGH_HALF), F32).astype(BF16)
    return jnp.concatenate([low, high], axis=1)


def _dispatch_kernel(dest_ref, x1_ref, xs_hbm, xbuf, row_sems):
    tm = x1_ref.shape[0]
    i = pl.program_id(0)
    n = pl.num_programs(0)
    slot = i % DISPATCH_SLOTS

    def drain_rows(s):
        for k in range(TOP_K):
            pltpu.make_async_copy(xbuf.at[s], xs_hbm.at[pl.ds(0, tm)], row_sems.at[s]).wait()

    @pl.when(i >= DISPATCH_SLOTS)
    def _():
        drain_rows(slot)

    xbuf[slot] = _pack_rows(x1_ref[...])

    def issue(g, carry):
        for j in range(ROWS_PER_ISSUE):
            r = g * ROWS_PER_ISSUE + j
            for k in range(TOP_K):
                pltpu.make_async_copy(xbuf.at[slot, r], xs_hbm.at[dest_ref[0, 0, k * tm + r]],
                                      row_sems.at[slot]).start(priority=k)
        return carry

    lax.fori_loop(0, tm // ROWS_PER_ISSUE, issue, 0)

    @pl.when(i == n - 1)
    def _():
        for back in range(DISPATCH_SLOTS):
            drain_rows((i - back) % DISPATCH_SLOTS)


def _dispatch(x1, dest):
    t = x1.shape[0]
    tm = DISPATCH_ROWS
    return pl.pallas_call(
        _dispatch_kernel,
        grid=(t // tm,),
        in_specs=[
            pl.BlockSpec((1, 1, TOP_K * tm), lambda i: (i, 0, 0), memory_space=pltpu.SMEM),
            pl.BlockSpec((tm, D_MODEL), lambda i: (i, 0)),
        ],
        out_specs=pl.BlockSpec(memory_space=pl.ANY),
        out_shape=jax.ShapeDtypeStruct((t * TOP_K,) + PACKED_ROW_TILE, U32),
        scratch_shapes=[
            pltpu.VMEM((DISPATCH_SLOTS, tm) + PACKED_ROW_TILE, U32),
            pltpu.SemaphoreType.DMA((DISPATCH_SLOTS,)),
        ],
        compiler_params=pltpu.CompilerParams(
            dimension_semantics=("arbitrary",), vmem_limit_bytes=VMEM_LIMIT),
        name="dispatch",
    )(_dest_blocks(dest, tm), x1)


def _expert_kernel(blk_ref, e_ref, lo_ref, hi_ref, n_steps_ref, xs_ref, wg_ref, wu_ref, wd_ref,
                   ys_ref, wg_bf, wu_bf, wd_bf):
    s = pl.program_id(0)
    prev = jnp.maximum(s - 1, 0)
    new_expert = (s == 0) | (e_ref[s] != e_ref[prev])
    new_block = (s == 0) | (blk_ref[s] != blk_ref[prev])

    @pl.when(new_expert)
    def _():
        wg_bf[...] = wg_ref[0, 0].astype(BF16)
        wu_bf[...] = wu_ref[0, 0].astype(BF16)
        wd_bf[...] = wd_ref[0, 0].astype(BF16)

    rb = xs_ref.shape[0]
    half = rb // 2
    lo, hi = lo_ref[s], hi_ref[s]
    active = s < n_steps_ref[0]
    both_halves = (lo < half) & (hi > half)

    def masked_ffn(x_tiles, row0):
        xb = _unpack_rows(x_tiles)
        gate = jnp.dot(xb, wg_bf[...], preferred_element_type=F32)
        up = jnp.dot(xb, wu_bf[...], preferred_element_type=F32)
        hid = (gate * jax.nn.sigmoid(gate) * up).astype(BF16)
        y = jnp.dot(hid, wd_bf[...], preferred_element_type=F32)
        rows = lax.broadcasted_iota(I32, y.shape, 0) + row0
        return _to_row_tiles(jnp.where((rows >= lo) & (rows < hi), y, 0.0))

    @pl.when(active & both_halves)
    def _():
        y = masked_ffn(xs_ref[...], 0)

        @pl.when(new_block)
        def _():
            ys_ref[...] = y

        @pl.when(jnp.logical_not(new_block))
        def _():
            ys_ref[...] += y

    @pl.when(active & jnp.logical_not(both_halves))
    def _():
        off = pl.multiple_of(jnp.where(lo < half, 0, half), half)
        y = masked_ffn(xs_ref[pl.ds(off, half)], off)

        @pl.when(new_block)
        def _():
            ys_ref[pl.ds(off, half)] = y
            ys_ref[pl.ds(half - off, half)] = jnp.zeros_like(y)

        @pl.when(jnp.logical_not(new_block))
        def _():
            ys_ref[pl.ds(off, half)] += y


def _expert_ffn(xs, steps, layer, wg, wu, wd):
    step_blk, step_e, step_lo, step_hi, n_steps = steps
    rb = EXPERT_ROWS
    row_map = lambda s, blk, e, lo, hi, n: (blk[s], 0, 0)
    w_map = lambda s, blk, e, lo, hi, n: (layer, e[s], 0, 0)
    grid_spec = pltpu.PrefetchScalarGridSpec(
        num_scalar_prefetch=5,
        grid=(step_blk.shape[0],),
        in_specs=[
            pl.BlockSpec((rb,) + PACKED_ROW_TILE, row_map),
            pl.BlockSpec((1, 1, D_MODEL, D_EXPERT), w_map),
            pl.BlockSpec((1, 1, D_MODEL, D_EXPERT), w_map),
            pl.BlockSpec((1, 1, D_EXPERT, D_MODEL), w_map),
        ],
        out_specs=pl.BlockSpec((rb,) + ROW_TILE, row_map),
        scratch_shapes=[
            pltpu.VMEM((D_MODEL, D_EXPERT), BF16),
            pltpu.VMEM((D_MODEL, D_EXPERT), BF16),
            pltpu.VMEM((D_EXPERT, D_MODEL), BF16),
        ],
    )
    return pl.pallas_call(
        _expert_kernel,
        grid_spec=grid_spec,
        out_shape=jax.ShapeDtypeStruct((xs.shape[0],) + ROW_TILE, F32),
        compiler_params=pltpu.CompilerParams(
            dimension_semantics=("arbitrary",), vmem_limit_bytes=VMEM_LIMIT),
        name="expert_ffn",
    )(step_blk, step_e, step_lo, step_hi, n_steps, xs, wg, wu, wd)


def _combine_kernel(dest_ref, dest_next_ref, x1_ref, route_ref, g2_ref, b2_ref, ys_hbm,
                    x2_ref, ybuf, sems):
    tm = x1_ref.shape[0]
    i = pl.program_id(0)
    n = pl.num_programs(0)
    slot = i % 2

    def gather(d_ref, to_slot):
        def issue(g, carry):
            for j in range(ROWS_PER_ISSUE):
                r = g * ROWS_PER_ISSUE + j
                for k in range(TOP_K):
                    pltpu.make_async_copy(ys_hbm.at[d_ref[0, 0, k * tm + r]],
                                          ybuf.at[to_slot, k, r],
                                          sems.at[to_slot]).start(priority=k)
            return carry
        lax.fori_loop(0, tm // ROWS_PER_ISSUE, issue, 0)

    @pl.when(i == 0)
    def _():
        gather(dest_ref, slot)

    @pl.when(i + 1 < n)
    def _():
        gather(dest_next_ref, 1 - slot)

    for k in range(TOP_K):
        pltpu.make_async_copy(ys_hbm.at[pl.ds(0, tm)], ybuf.at[slot, k], sems.at[slot]).wait()

    route = route_ref[...]
    f = (route[:, R_GATE0:R_GATE0 + 1] * _from_row_tiles(ybuf[slot, 0])
         + route[:, R_GATE1:R_GATE1 + 1] * _from_row_tiles(ybuf[slot, 1]))
    x2_ref[...] = _layer_norm(ALPHA * x1_ref[...] + f, g2_ref[...], b2_ref[...])


def _combine_ln(x1, ys, dest, route, g2, b2):
    t = x1.shape[0]
    tm = COMBINE_ROWS
    n = t // tm
    row = lambda i: (i, 0)
    const2 = lambda i: (0, 0)
    dest_blocks = _dest_blocks(dest, tm)
    smem_block = lambda index_map: pl.BlockSpec((1, 1, TOP_K * tm), index_map,
                                                memory_space=pltpu.SMEM)
    return pl.pallas_call(
        _combine_kernel,
        grid=(n,),
        in_specs=[
            smem_block(lambda i: (i, 0, 0)),
            smem_block(lambda i: (jnp.minimum(i + 1, n - 1), 0, 0)),
            pl.BlockSpec((tm, D_MODEL), row),
            pl.BlockSpec((tm, LANES), row),
            pl.BlockSpec((1, D_MODEL), const2),
            pl.BlockSpec((1, D_MODEL), const2),
            pl.BlockSpec(memory_space=pl.ANY),
        ],
        out_specs=pl.BlockSpec((tm, D_MODEL), row),
        out_shape=jax.ShapeDtypeStruct((t, D_MODEL), F32),
        scratch_shapes=[
            pltpu.VMEM((2, TOP_K, tm) + ROW_TILE, F32),
            pltpu.SemaphoreType.DMA((2,)),
        ],
        compiler_params=pltpu.CompilerParams(
            dimension_semantics=("arbitrary",), vmem_limit_bytes=VMEM_LIMIT),
        name="combine_ln",
    )(dest_blocks, dest_blocks, x1, route, g2.reshape(1, -1), b2.reshape(1, -1), ys)


def _moe_layer(x1, route, route_t, counts, layer, wg, wu, wd, g2, b2):
    dest, steps = _plan(route_t, counts)
    xs = _dispatch(x1, dest)
    ys = _expert_ffn(xs, steps, layer, wg, wu, wd)
    return _combine_ln(x1, ys, dest, route, g2, b2)


def kernel(x, rel_bias_table, mix_w_in, gmlp_ln_g, gmlp_ln_b, gmlp_w_spatial, gmlp_b_spatial, conv_w, mix_w_out, attn_w_qkv, attn_b_qkv, attn_sinks, attn_w_o, attn_b_o, ln1_g, ln1_b, ln2_g, ln2_b, router_group, router_expert, expert_w_gate, expert_w_up, expert_w_down):
    bsz, s, d = x.shape
    assert (s, d) == (SEQ, D_MODEL)
    x = x.reshape(bsz * s, d)
    bias = _rel_bias(rel_bias_table)
    for l in range(DEPTH):
        i = l // 2
        w_router = (router_group[l], router_expert[l])
        if l % 2 == 0:
            x1, *routing = _mixer_layer(x, mix_w_in[i], gmlp_ln_g[i], gmlp_ln_b[i],
                                        gmlp_w_spatial[i], gmlp_b_spatial[i], conv_w[i],
                                        mix_w_out[i], ln1_g[l], ln1_b[l], w_router)
        else:
            x1, *routing = _attn_layer(x, bias, attn_w_qkv[i], attn_b_qkv[i], attn_sinks[i],
                                       attn_w_o[i], attn_b_o[i], ln1_g[l], ln1_b[l], w_router)
        x = _moe_layer(x1, *routing, l, expert_w_gate, expert_w_up, expert_w_down,
                       ln2_g[l], ln2_b[l])
    return x.reshape(bsz, s, d)
```

```python
import math

import jax
import jax.numpy as jnp
from jax import lax
from jax.experimental import pallas as pl
from jax.experimental.pallas import tpu as pltpu

D_MODEL = 1024
SEQ = 16384
DEPTH = 4
CHUNK = 128
A_GROUPS = 4
A_CH = 128
A_WIDTH = A_GROUPS * A_CH
B_WIDTH = 512
CONV_W = 3
MIX_IN = 2 * A_WIDTH + 3 * B_WIDTH
N_HEADS = 16
N_KV = 2
HEAD_DIM = 64
GQA_GROUP = N_HEADS // N_KV
WINDOW = 128
QKV_DIM = (N_HEADS + 2 * N_KV) * HEAD_DIM
ATT_OUT = N_HEADS * HEAD_DIM
KV_WIDTH = N_KV * HEAD_DIM
N_BUCKETS = 32
MAX_DISTANCE = 128
N_GROUPS = 4
EXPERTS_PER_GROUP = 8
N_EXPERTS = N_GROUPS * EXPERTS_PER_GROUP
TOP_K = 2
D_EXPERT = 512
ALPHA = (2 * DEPTH) ** 0.25
LN_EPS = 1e-5

LANES = 128
SUBLANES = 8
ROW_TILE = (SUBLANES, LANES)
assert D_MODEL == SUBLANES * LANES
ROUTER_ROWS = SUBLANES + N_EXPERTS
MIX_ROWS = 1024
ATTN_ROWS = 1024
DISPATCH_ROWS = 1024
COMBINE_ROWS = 256
EXPERT_ROWS = 1024
ROWS_PER_ISSUE = 32
DISPATCH_SLOTS = 3
VMEM_LIMIT = 56 * 1024 * 1024

N_ROUTE_FIELDS = 6
R_E0, R_E1, R_RANK0, R_RANK1, R_GATE0, R_GATE1 = range(N_ROUTE_FIELDS)

F32 = jnp.float32
BF16 = jnp.bfloat16
I32 = jnp.int32
U32 = jnp.uint32


def _layer_norm(x, g, b):
    mu = jnp.mean(x, axis=-1, keepdims=True)
    xc = x - mu
    var = jnp.mean(xc * xc, axis=-1, keepdims=True)
    return xc * lax.rsqrt(var + LN_EPS) * g + b


def _gelu(x):
    return 0.5 * x * (1.0 + lax.erf(x * (2.0 ** -0.5)))


def _mixer_kernel(x_ref, win_ref, lng_ref, lnb_ref, wsp_ref, bsp_ref, cw_ref, wout_ref,
                  g1_ref, b1_ref, wr_ref, tri_ref, x1_ref, route_ref, route_t_ref, count_ref,
                  ztail_ref, base_ref):
    tm = x_ref.shape[0]
    i = pl.program_id(0)

    @pl.when(i % (SEQ // tm) == 0)
    def _():
        ztail_ref[...] = jnp.zeros_like(ztail_ref)

    x = x_ref[...]
    h = jnp.dot(x.astype(BF16), win_ref[...], preferred_element_type=F32)
    u = _gelu(h[:, :A_WIDTH])
    v = _gelu(h[:, A_WIDTH:2 * A_WIDTH])
    o = 2 * A_WIDTH
    g_b = h[:, o:o + B_WIDTH]
    g_c = h[:, o + B_WIDTH:o + 2 * B_WIDTH]
    hb = h[:, o + 2 * B_WIDTH:o + 3 * B_WIDTH]

    v = _layer_norm(v, lng_ref[...], lnb_ref[...]).astype(BF16)
    n_chunks = tm // CHUNK
    ri = lax.broadcasted_iota(I32, (CHUNK, CHUNK), 0)
    ci = lax.broadcasted_iota(I32, (CHUNK, CHUNK), 1)
    causal = ci <= ri
    sv_cols = [[None] * A_GROUPS for _ in range(n_chunks)]
    for g in range(A_GROUPS):
        ws = jnp.where(causal, wsp_ref[g], 0.0).astype(BF16)
        vg = jnp.concatenate(
            [v[c * CHUNK:(c + 1) * CHUNK, g * A_CH:(g + 1) * A_CH] for c in range(n_chunks)],
            axis=1)
        sg = jnp.dot(ws, vg, preferred_element_type=F32) + bsp_ref[:, g:g + 1]
        for c in range(n_chunks):
            sv_cols[c][g] = sg[:, c * A_CH:(c + 1) * A_CH]
    sv = jnp.concatenate([jnp.concatenate(row, axis=1) for row in sv_cols], axis=0)
    y_a = u * sv

    z = g_c * hb
    rows = lax.broadcasted_iota(I32, z.shape, 0)
    tail = ztail_ref[...]
    prev1 = tail[SUBLANES - 1:SUBLANES, :]
    prev2 = tail[SUBLANES - 2:SUBLANES - 1, :]
    zm1 = jnp.where(rows == 0, prev1, pltpu.roll(z, 1, 0))
    zm2 = jnp.where(rows == 0, prev2, jnp.where(rows == 1, prev1, pltpu.roll(z, 2, 0)))
    conv = cw_ref[0:1, :] * zm2 + cw_ref[1:2, :] * zm1 + cw_ref[2:3, :] * z
    y_b = g_b * conv
    ztail_ref[...] = z[tm - SUBLANES:tm, :]

    y = jnp.concatenate([y_a, y_b], axis=1).astype(BF16)
    m = jnp.dot(y, wout_ref[...], preferred_element_type=F32)
    x1 = _layer_norm(ALPHA * x + m, g1_ref[...], b1_ref[...])
    x1_ref[...] = x1
    _route_tile(x1, wr_ref, tri_ref, route_ref, route_t_ref, count_ref, base_ref)


def _mixer_layer(x, w_in, ln_g, ln_b, w_sp, b_sp, conv_w, w_out, g1, b1, w_router):
    t = x.shape[0]
    tm = MIX_ROWS
    const2 = lambda i: (0, 0)
    r_in, r_out, r_shape, r_scratch = _route_specs(t, tm)
    return pl.pallas_call(
        _mixer_kernel,
        grid=(t // tm,),
        in_specs=[
            pl.BlockSpec((tm, D_MODEL), lambda i: (i, 0)),
            pl.BlockSpec((D_MODEL, MIX_IN), const2),
            pl.BlockSpec((1, A_WIDTH), const2),
            pl.BlockSpec((1, A_WIDTH), const2),
            pl.BlockSpec((A_GROUPS, CHUNK, CHUNK), lambda i: (0, 0, 0)),
            pl.BlockSpec((CHUNK, A_GROUPS), const2),
            pl.BlockSpec((CONV_W, B_WIDTH), const2),
            pl.BlockSpec((A_WIDTH + B_WIDTH, D_MODEL), const2),
            pl.BlockSpec((1, D_MODEL), const2),
            pl.BlockSpec((1, D_MODEL), const2),
        ] + r_in,
        out_specs=[pl.BlockSpec((tm, D_MODEL), lambda i: (i, 0))] + r_out,
        out_shape=[jax.ShapeDtypeStruct((t, D_MODEL), F32)] + r_shape,
        scratch_shapes=[pltpu.VMEM((SUBLANES, B_WIDTH), F32), r_scratch],
        compiler_params=pltpu.CompilerParams(
            dimension_semantics=("arbitrary",), vmem_limit_bytes=VMEM_LIMIT),
        name="mixer_layer",
    )(x, w_in.astype(BF16), ln_g.reshape(1, -1), ln_b.reshape(1, -1), w_sp, b_sp.T,
      conv_w, w_out.astype(BF16), g1.reshape(1, -1), b1.reshape(1, -1),
      *_route_operands(*w_router, tm))


def _attn_kernel(sink_ref, x_ref, wqkv_ref, bqkv_ref, bias_ref, wo_ref, bo_ref,
                 g1_ref, b1_ref, wr_ref, tri_ref, x1_ref, route_ref, route_t_ref, count_ref,
                 kprev_ref, vprev_ref, base_ref):
    tm = x_ref.shape[0]
    i = pl.program_id(0)
    first = i % (SEQ // tm) == 0

    @pl.when(first)
    def _():
        kprev_ref[...] = jnp.zeros_like(kprev_ref)
        vprev_ref[...] = jnp.zeros_like(vprev_ref)

    x = x_ref[...]
    qkv = jnp.dot(x.astype(BF16), wqkv_ref[...], preferred_element_type=F32) + bqkv_ref[...]
    q = (qkv[:, :ATT_OUT] * (HEAD_DIM ** -0.5)).astype(BF16)

    lane = lax.broadcasted_iota(I32, (CHUNK, KV_WIDTH), 1)
    low = lane < HEAD_DIM

    def halves(t):
        swapped = pltpu.roll(t, HEAD_DIM, 1)
        zero = jnp.zeros_like(t)
        return [jnp.where(low, t, zero).astype(BF16), jnp.where(low, zero, swapped).astype(BF16),
                jnp.where(low, swapped, zero).astype(BF16), jnp.where(low, zero, t).astype(BF16)]

    def stacked(prev, cur, j):
        return jnp.concatenate([prev[2 * j], cur[2 * j], prev[2 * j + 1], cur[2 * j + 1]], axis=0)

    a = lax.broadcasted_iota(I32, (CHUNK, 2 * CHUNK), 0)
    c = lax.broadcasted_iota(I32, (CHUNK, 2 * CHUNK), 1)
    window = (c > a) & (c <= a + WINDOW)
    neg = jnp.finfo(F32).min
    n_pairs = N_HEADS // 2
    kv_of = lambda pair: (2 * pair) // GQA_GROUP

    k_prev = [kprev_ref[idx] for idx in range(2 * N_KV)]
    v_prev = [vprev_ref[idx] for idx in range(2 * N_KV)]
    o_blocks = []
    for blk in range(tm // CHUNK):
        rows = slice(blk * CHUNK, (blk + 1) * CHUNK)
        k_cur = halves(qkv[rows, ATT_OUT:ATT_OUT + KV_WIDTH])
        v_cur = halves(qkv[rows, ATT_OUT + KV_WIDTH:])
        k_rhs = [stacked(k_prev, k_cur, j) for j in range(N_KV)]
        v_rhs = [stacked(v_prev, v_cur, j) for j in range(N_KV)]
        k_prev, v_prev = k_cur, v_cur
        mask = window & (c >= jnp.where(first, CHUNK, 0)) if blk == 0 else window

        scores = [lax.dot_general(q[rows, pair * 2 * HEAD_DIM:(pair + 1) * 2 * HEAD_DIM],
                                  k_rhs[kv_of(pair)], (((1,), (1,)), ((), ())),
                                  preferred_element_type=F32) for pair in range(n_pairs)]
        probs, inv_denoms = [], []
        for pair in range(n_pairs):
            sc2 = scores[pair] + bias_ref[pair]
            sides = []
            for side in range(2):
                sc = jnp.where(mask, sc2[:, side * 2 * CHUNK:(side + 1) * 2 * CHUNK], neg)
                sink = sink_ref[2 * pair + side]
                m = jnp.maximum(jnp.max(sc, axis=-1, keepdims=True), sink)
                p = jnp.exp(sc - m)
                denom = jnp.sum(p, axis=-1, keepdims=True) + jnp.exp(sink - m)
                sides.append(p.astype(BF16))
                inv_denoms.append(jnp.broadcast_to(1.0 / denom, (CHUNK, HEAD_DIM)))
            probs.append(jnp.concatenate(sides, axis=1))
        outs = [jnp.dot(probs[pair], v_rhs[kv_of(pair)], preferred_element_type=F32)
                for pair in range(n_pairs)]
        o = jnp.concatenate(outs, axis=1) * jnp.concatenate(inv_denoms, axis=1)
        o_blocks.append(o.astype(BF16))
    for idx in range(2 * N_KV):
        kprev_ref[idx] = k_prev[idx]
        vprev_ref[idx] = v_prev[idx]
    o = jnp.concatenate(o_blocks, axis=0)
    m_out = jnp.dot(o, wo_ref[...], preferred_element_type=F32) + bo_ref[...]
    x1 = _layer_norm(ALPHA * x + m_out, g1_ref[...], b1_ref[...])
    x1_ref[...] = x1
    _route_tile(x1, wr_ref, tri_ref, route_ref, route_t_ref, count_ref, base_ref)


def _t5_bucket(rel):
    n = jnp.maximum(rel, 0)
    max_exact = N_BUCKETS // 2
    nf = jnp.maximum(n, 1).astype(F32)
    large = max_exact + (jnp.log(nf / max_exact) / math.log(MAX_DISTANCE / max_exact)
                         * (N_BUCKETS - max_exact)).astype(I32)
    large = jnp.minimum(large, N_BUCKETS - 1)
    return jnp.where(n < max_exact, n, large)


def _rel_bias(rel_table):
    a = jnp.arange(CHUNK)[:, None]
    c = jnp.arange(2 * CHUNK)[None, :]
    onehot = jax.nn.one_hot(_t5_bucket(a + CHUNK - c), N_BUCKETS, dtype=F32)
    bias = jnp.einsum('acb,bh->hac', onehot, rel_table.astype(F32),
                      precision=lax.Precision.HIGHEST)
    bias = bias.reshape(N_HEADS // 2, 2, CHUNK, 2 * CHUNK).transpose(0, 2, 1, 3)
    return bias.reshape(N_HEADS // 2, CHUNK, 4 * CHUNK)


def _attn_layer(x, bias, w_qkv, b_qkv, sinks, w_o, b_o, g1, b1, w_router):
    t = x.shape[0]
    tm = ATTN_ROWS
    const2 = lambda i: (0, 0)
    r_in, r_out, r_shape, r_scratch = _route_specs(t, tm)
    return pl.pallas_call(
        _attn_kernel,
        grid=(t // tm,),
        in_specs=[
            pl.BlockSpec(memory_space=pltpu.SMEM),
            pl.BlockSpec((tm, D_MODEL), lambda i: (i, 0)),
            pl.BlockSpec((D_MODEL, QKV_DIM), const2),
            pl.BlockSpec((1, QKV_DIM), const2),
            pl.BlockSpec((N_HEADS // 2, CHUNK, 4 * CHUNK), lambda i: (0, 0, 0)),
            pl.BlockSpec((ATT_OUT, D_MODEL), const2),
            pl.BlockSpec((1, D_MODEL), const2),
            pl.BlockSpec((1, D_MODEL), const2),
            pl.BlockSpec((1, D_MODEL), const2),
        ] + r_in,
        out_specs=[pl.BlockSpec((tm, D_MODEL), lambda i: (i, 0))] + r_out,
        out_shape=[jax.ShapeDtypeStruct((t, D_MODEL), F32)] + r_shape,
        scratch_shapes=[pltpu.VMEM((2 * N_KV, CHUNK, KV_WIDTH), BF16),
                        pltpu.VMEM((2 * N_KV, CHUNK, KV_WIDTH), BF16), r_scratch],
        compiler_params=pltpu.CompilerParams(
            dimension_semantics=("arbitrary",), vmem_limit_bytes=VMEM_LIMIT),
        name="attn_layer",
    )(sinks, x, w_qkv.astype(BF16), b_qkv.reshape(1, -1), bias, w_o.astype(BF16),
      b_o.reshape(1, -1), g1.reshape(1, -1), b1.reshape(1, -1), *_route_operands(*w_router, tm))


def _route_tile(x1, wr_ref, tri_ref, route_ref, route_t_ref, count_ref, base_ref):
    @pl.when(pl.program_id(0) == 0)
    def _():
        base_ref[...] = jnp.zeros_like(base_ref)

    tm = x1.shape[0]
    lt = lax.dot_general(wr_ref[...], x1.astype(BF16), (((1,), (1,)), ((), ())),
                         preferred_element_type=F32)
    sub = lax.broadcasted_iota(I32, (SUBLANES, tm), 0).astype(F32)
    ninf = -jnp.inf

    def first_argmax(vals):
        m = jnp.max(vals, axis=0, keepdims=True)
        idx = jnp.min(jnp.where(vals == m, sub, float(SUBLANES)), axis=0, keepdims=True)
        return m, idx

    is_g = sub < N_GROUPS
    g_rows = lt[0:SUBLANES]
    gmax, g_idx = first_argmax(jnp.where(is_g, g_rows, ninf))
    g_p = 1.0 / jnp.sum(jnp.where(is_g, jnp.exp(g_rows - gmax), 0.0), axis=0, keepdims=True)

    group_rows = lambda a, g: a[SUBLANES * (g + 1):SUBLANES * (g + 2)]
    el = group_rows(lt, 0)
    for g in range(1, N_GROUPS):
        el = jnp.where(g_idx == g, group_rows(lt, g), el)
    m1, j1 = first_argmax(el)
    m2, j2 = first_argmax(jnp.where(sub == j1, ninf, el))
    a2 = jnp.exp(m2 - m1)
    gate0 = g_p / (1.0 + a2)
    gate1 = g_p * a2 / (1.0 + a2)

    hit0 = [(g_idx == g) & (sub == j1) for g in range(N_GROUPS)]
    hit1 = [(g_idx == g) & (sub == j2) for g in range(N_GROUPS)]
    onehot = jnp.concatenate([jnp.where(h0 | h1, 1.0, 0.0) for h0, h1 in zip(hit0, hit1)],
                             axis=0)
    before = jnp.dot(onehot.astype(BF16), tri_ref[...], preferred_element_type=F32)
    before = before + base_ref[:, 0:1]

    def picked(hits):
        total = jnp.zeros((SUBLANES, tm), F32)
        for g in range(N_GROUPS):
            total = total + jnp.where(hits[g], before[SUBLANES * g:SUBLANES * (g + 1)], 0.0)
        return jnp.sum(total, axis=0, keepdims=True)

    rank0 = picked(hit0)
    rank1 = picked(hit1)
    base_ref[...] += jnp.sum(onehot, axis=1, keepdims=True)

    e_base = g_idx * EXPERTS_PER_GROUP
    rec_t = jnp.concatenate([e_base + j1, e_base + j2, rank0, rank1, gate0, gate1,
                             jnp.zeros((SUBLANES - N_ROUTE_FIELDS, tm), F32)], axis=0)
    route_t_ref[...] = rec_t
    padded = jnp.concatenate([rec_t, jnp.zeros((LANES - SUBLANES, tm), F32)], axis=0)
    route_ref[...] = jnp.transpose(padded)
    count_ref[...] = base_ref[...]


def _route_specs(t, tm):
    const2 = lambda i: (0, 0)
    in_specs = [pl.BlockSpec((ROUTER_ROWS, D_MODEL), const2), pl.BlockSpec((tm, tm), const2)]
    out_specs = [
        pl.BlockSpec((tm, LANES), lambda i: (i, 0)),
        pl.BlockSpec((SUBLANES, tm), lambda i: (0, i)),
        pl.BlockSpec((N_EXPERTS, LANES), const2),
    ]
    out_shape = [
        jax.ShapeDtypeStruct((t, LANES), F32),
        jax.ShapeDtypeStruct((SUBLANES, t), F32),
        jax.ShapeDtypeStruct((N_EXPERTS, LANES), F32),
    ]
    return in_specs, out_specs, out_shape, pltpu.VMEM((N_EXPERTS, LANES), F32)


def _route_operands(w_group, w_expert, tm):
    pad = jnp.zeros((D_MODEL, SUBLANES - N_GROUPS), F32)
    w_rows = jnp.concatenate([w_group, pad, w_expert], axis=1).T
    return w_rows.astype(BF16), jnp.triu(jnp.ones((tm, tm), BF16), 1)


def _plan(route_t, count_lanes):
    t = route_t.shape[1]
    rb = EXPERT_ROWS
    experts = jnp.arange(N_EXPERTS, dtype=I32)
    counts = count_lanes[:, 0].astype(I32)
    end = jnp.cumsum(counts)
    start = end - counts
    e = route_t[R_E0:R_E1 + 1].astype(I32)
    rank = route_t[R_RANK0:R_RANK1 + 1].astype(I32)
    onehot = e[:, None, :] == experts[None, :, None]
    dest = jnp.sum(jnp.where(onehot, start[None, :, None], 0), axis=1) + rank

    first_blk = start // rb
    last_blk = jnp.maximum(end - 1, 0) // rb
    n_steps_e = jnp.where(counts > 0, last_blk - first_blk + 1, 0)
    step_end = jnp.cumsum(n_steps_e)
    step_start = step_end - n_steps_e
    n_steps = step_end[-1]
    max_steps = t * TOP_K // rb + N_EXPERTS - 1
    s = jnp.minimum(jnp.arange(max_steps, dtype=I32), n_steps - 1)
    step_e = jnp.minimum(jnp.sum((step_end[None, :] <= s[:, None]).astype(I32), axis=1),
                         N_EXPERTS - 1)
    pick = step_e[:, None] == experts[None, :]
    take = lambda v: jnp.sum(jnp.where(pick, v[None, :], 0), axis=1)
    step_blk = take(first_blk) + s - take(step_start)
    step_lo = jnp.clip(take(start) - step_blk * rb, 0, rb)
    step_hi = jnp.clip(take(end) - step_blk * rb, 0, rb)
    return dest, (step_blk, step_e, step_lo, step_hi, n_steps.reshape(1))


def _dest_blocks(dest, tm):
    t = dest.shape[1]
    return dest.reshape(TOP_K, t // tm, tm).transpose(1, 0, 2).reshape(t // tm, 1, TOP_K * tm)


def _to_row_tiles(x):
    return x.reshape(x.shape[0], ROW_TILE[0], ROW_TILE[1])


def _from_row_tiles(x):
    return x.reshape(x.shape[0], D_MODEL)


PACKED_ROW_TILE = (SUBLANES // 2, LANES)
BF16_BITS = 16
HIGH_HALF = 0xFFFF0000


def _pack_rows(x):
    half = D_MODEL // 2
    bits = lambda v: lax.bitcast_convert_type(v.astype(BF16).astype(F32), U32)
    words = (bits(x[:, :half]) >> BF16_BITS) | (bits(x[:, half:]) & jnp.uint32(HIGH_HALF))
    return words.reshape((x.shape[0],) + PACKED_ROW_TILE)


def _unpack_rows(words):
    w = words.reshape(words.shape[0], D_MODEL // 2)
    low = lax.bitcast_convert_type(w << BF16_BITS, F32).astype(BF16)
    high = lax.bitcast_convert_type(w & jnp.uint32(HIGH_HALF), F32).astype(BF16)
    return jnp.concatenate([low, high], axis=1)


def _dispatch_kernel(dest_ref, x1_ref, xs_hbm, xbuf, row_sems):
    tm = x1_ref.shape[0]
    i = pl.program_id(0)
    n = pl.num_programs(0)
    slot = i % DISPATCH_SLOTS

    def drain_rows(s):
        for k in range(TOP_K):
            pltpu.make_async_copy(xbuf.at[s], xs_hbm.at[pl.ds(0, tm)], row_sems.at[s]).wait()

    @pl.when(i >= DISPATCH_SLOTS)
    def _():
        drain_rows(slot)

    xbuf[slot] = _pack_rows(x1_ref[...])

    def issue(g, carry):
        for j in range(ROWS_PER_ISSUE):
            r = g * ROWS_PER_ISSUE + j
            for k in range(TOP_K):
                pltpu.make_async_copy(xbuf.at[slot, r], xs_hbm.at[dest_ref[0, 0, k * tm + r]],
                                      row_sems.at[slot]).start(priority=k)
        return carry

    lax.fori_loop(0, tm // ROWS_PER_ISSUE, issue, 0)

    @pl.when(i == n - 1)
    def _():
        for back in range(DISPATCH_SLOTS):
            drain_rows((i - back) % DISPATCH_SLOTS)


def _dispatch(x1, dest):
    t = x1.shape[0]
    tm = DISPATCH_ROWS
    return pl.pallas_call(
        _dispatch_kernel,
        grid=(t // tm,),
        in_specs=[
            pl.BlockSpec((1, 1, TOP_K * tm), lambda i: (i, 0, 0), memory_space=pltpu.SMEM),
            pl.BlockSpec((tm, D_MODEL), lambda i: (i, 0)),
        ],
        out_specs=pl.BlockSpec(memory_space=pl.ANY),
        out_shape=jax.ShapeDtypeStruct((t * TOP_K,) + PACKED_ROW_TILE, U32),
        scratch_shapes=[
            pltpu.VMEM((DISPATCH_SLOTS, tm) + PACKED_ROW_TILE, U32),
            pltpu.SemaphoreType.DMA((DISPATCH_SLOTS,)),
        ],
        compiler_params=pltpu.CompilerParams(
            dimension_semantics=("arbitrary",), vmem_limit_bytes=VMEM_LIMIT),
        name="dispatch",
    )(_dest_blocks(dest, tm), x1)


def _expert_kernel(blk_ref, e_ref, lo_ref, hi_ref, n_steps_ref, xs_ref, wg_ref, wu_ref, wd_ref,
                   ys_ref, wg_bf, wu_bf, wd_bf):
    s = pl.program_id(0)
    prev = jnp.maximum(s - 1, 0)
    new_expert = (s == 0) | (e_ref[s] != e_ref[prev])
    new_block = (s == 0) | (blk_ref[s] != blk_ref[prev])

    @pl.when(new_expert)
    def _():
        wg_bf[...] = wg_ref[0, 0].astype(BF16)
        wu_bf[...] = wu_ref[0, 0].astype(BF16)
        wd_bf[...] = wd_ref[0, 0].astype(BF16)

    rb = xs_ref.shape[0]
    half = rb // 2
    lo, hi = lo_ref[s], hi_ref[s]
    active = s < n_steps_ref[0]
    both_halves = (lo < half) & (hi > half)

    def masked_ffn(x_tiles, row0):
        xb = _unpack_rows(x_tiles)
        gate = jnp.dot(xb, wg_bf[...], preferred_element_type=F32)
        up = jnp.dot(xb, wu_bf[...], preferred_element_type=F32)
        hid = (gate * jax.nn.sigmoid(gate) * up).astype(BF16)
        y = jnp.dot(hid, wd_bf[...], preferred_element_type=F32)
        rows = lax.broadcasted_iota(I32, y.shape, 0) + row0
        return _to_row_tiles(jnp.where((rows >= lo) & (rows < hi), y, 0.0))

    @pl.when(active & both_halves)
    def _():
        y = masked_ffn(xs_ref[...], 0)

        @pl.when(new_block)
        def _():
            ys_ref[...] = y

        @pl.when(jnp.logical_not(new_block))
        def _():
            ys_ref[...] += y

    @pl.when(active & jnp.logical_not(both_halves))
    def _():
        off = pl.multiple_of(jnp.where(lo < half, 0, half), half)
        y = masked_ffn(xs_ref[pl.ds(off, half)], off)

        @pl.when(new_block)
        def _():
            ys_ref[pl.ds(off, half)] = y
            ys_ref[pl.ds(half - off, half)] = jnp.zeros_like(y)

        @pl.when(jnp.logical_not(new_block))
        def _():
            ys_ref[pl.ds(off, half)] += y


def _expert_ffn(xs, steps, layer, wg, wu, wd):
    step_blk, step_e, step_lo, step_hi, n_steps = steps
    rb = EXPERT_ROWS
    row_map = lambda s, blk, e, lo, hi, n: (blk[s], 0, 0)
    w_map = lambda s, blk, e, lo, hi, n: (layer, e[s], 0, 0)
    grid_spec = pltpu.PrefetchScalarGridSpec(
        num_scalar_prefetch=5,
        grid=(step_blk.shape[0],),
        in_specs=[
            pl.BlockSpec((rb,) + PACKED_ROW_TILE, row_map),
            pl.BlockSpec((1, 1, D_MODEL, D_EXPERT), w_map),
            pl.BlockSpec((1, 1, D_MODEL, D_EXPERT), w_map),
            pl.BlockSpec((1, 1, D_EXPERT, D_MODEL), w_map),
        ],
        out_specs=pl.BlockSpec((rb,) + ROW_TILE, row_map),
        scratch_shapes=[
            pltpu.VMEM((D_MODEL, D_EXPERT), BF16),
            pltpu.VMEM((D_MODEL, D_EXPERT), BF16),
            pltpu.VMEM((D_EXPERT, D_MODEL), BF16),
        ],
    )
    return pl.pallas_call(
        _expert_kernel,
        grid_spec=grid_spec,
        out_shape=jax.ShapeDtypeStruct((xs.shape[0],) + ROW_TILE, F32),
        compiler_params=pltpu.CompilerParams(
            dimension_semantics=("arbitrary",), vmem_limit_bytes=VMEM_LIMIT),
        name="expert_ffn",
    )(step_blk, step_e, step_lo, step_hi, n_steps, xs, wg, wu, wd)


def _combine_kernel(dest_ref, dest_next_ref, x1_ref, route_ref, g2_ref, b2_ref, ys_hbm,
                    x2_ref, ybuf, sems):
    tm = x1_ref.shape[0]
    i = pl.program_id(0)
    n = pl.num_programs(0)
    slot = i % 2

    def gather(d_ref, to_slot):
        def issue(g, carry):
            for j in range(ROWS_PER_ISSUE):
                r = g * ROWS_PER_ISSUE + j
                for k in range(TOP_K):
                    pltpu.make_async_copy(ys_hbm.at[d_ref[0, 0, k * tm + r]],
                                          ybuf.at[to_slot, k, r],
                                          sems.at[to_slot]).start(priority=k)
            return carry
        lax.fori_loop(0, tm // ROWS_PER_ISSUE, issue, 0)

    @pl.when(i == 0)
    def _():
        gather(dest_ref, slot)

    @pl.when(i + 1 < n)
    def _():
        gather(dest_next_ref, 1 - slot)

    for k in range(TOP_K):
        pltpu.make_async_copy(ys_hbm.at[pl.ds(0, tm)], ybuf.at[slot, k], sems.at[slot]).wait()

    route = route_ref[...]
    f = (route[:, R_GATE0:R_GATE0 + 1] * _from_row_tiles(ybuf[slot, 0])
         + route[:, R_GATE1:R_GATE1 + 1] * _from_row_tiles(ybuf[slot, 1]))
    x2_ref[...] = _layer_norm(ALPHA * x1_ref[...] + f, g2_ref[...], b2_ref[...])


def _combine_ln(x1, ys, dest, route, g2, b2):
    t = x1.shape[0]
    tm = COMBINE_ROWS
    n = t // tm
    row = lambda i: (i, 0)
    const2 = lambda i: (0, 0)
    dest_blocks = _dest_blocks(dest, tm)
    smem_block = lambda index_map: pl.BlockSpec((1, 1, TOP_K * tm), index_map,
                                                memory_space=pltpu.SMEM)
    return pl.pallas_call(
        _combine_kernel,
        grid=(n,),
        in_specs=[
            smem_block(lambda i: (i, 0, 0)),
            smem_block(lambda i: (jnp.minimum(i + 1, n - 1), 0, 0)),
            pl.BlockSpec((tm, D_MODEL), row),
            pl.BlockSpec((tm, LANES), row),
            pl.BlockSpec((1, D_MODEL), const2),
            pl.BlockSpec((1, D_MODEL), const2),
            pl.BlockSpec(memory_space=pl.ANY),
        ],
        out_specs=pl.BlockSpec((tm, D_MODEL), row),
        out_shape=jax.ShapeDtypeStruct((t, D_MODEL), F32),
        scratch_shapes=[
            pltpu.VMEM((2, TOP_K, tm) + ROW_TILE, F32),
            pltpu.SemaphoreType.DMA((2,)),
        ],
        compiler_params=pltpu.CompilerParams(
            dimension_semantics=("arbitrary",), vmem_limit_bytes=VMEM_LIMIT),
        name="combine_ln",
    )(dest_blocks, dest_blocks, x1, route, g2.reshape(1, -1), b2.reshape(1, -1), ys)


def _moe_layer(x1, route, route_t, counts, layer, wg, wu, wd, g2, b2):
    dest, steps = _plan(route_t, counts)
    xs = _dispatch(x1, dest)
    ys = _expert_ffn(xs, steps, layer, wg, wu, wd)
    return _combine_ln(x1, ys, dest, route, g2, b2)


def kernel(x, rel_bias_table, mix_w_in, gmlp_ln_g, gmlp_ln_b, gmlp_w_spatial, gmlp_b_spatial, conv_w, mix_w_out, attn_w_qkv, attn_b_qkv, attn_sinks, attn_w_o, attn_b_o, ln1_g, ln1_b, ln2_g, ln2_b, router_group, router_expert, expert_w_gate, expert_w_up, expert_w_down):
    bsz, s, d = x.shape
    assert (s, d) == (SEQ, D_MODEL)
    x = x.reshape(bsz * s, d)
    bias = _rel_bias(rel_bias_table)
    for l in range(DEPTH):
        i = l // 2
        w_router = (router_group[l], router_expert[l])
        if l % 2 == 0:
            x1, *routing = _mixer_layer(x, mix_w_in[i], gmlp_ln_g[i], gmlp_ln_b[i],
                                        gmlp_w_spatial[i], gmlp_b_spatial[i], conv_w[i],
                                        mix_w_out[i], ln1_g[l], ln1_b[l], w_router)
        else:
            x1, *routing = _attn_layer(x, bias, attn_w_qkv[i], attn_b_qkv[i], attn_sinks[i],
                                       attn_w_o[i], attn_b_o[i], ln1_g[l], ln1_b[l], w_router)
        x = _moe_layer(x1, *routing, l, expert_w_gate, expert_w_up, expert_w_down,
                       ln2_g[l], ln2_b[l])
    return x.reshape(bsz, s, d)
```

```python
import math

import jax
import jax.numpy as jnp
from jax import lax
from jax.experimental import pallas as pl
from jax.experimental.pallas import tpu as pltpu

D_MODEL = 1024
SEQ = 16384
DEPTH = 4
CHUNK = 128
A_GROUPS = 4
A_CH = 128
A_WIDTH = A_GROUPS * A_CH
B_WIDTH = 512
CONV_W = 3
MIX_IN = 2 * A_WIDTH + 3 * B_WIDTH
N_HEADS = 16
N_KV = 2
HEAD_DIM = 64
GQA_GROUP = N_HEADS // N_KV
WINDOW = 128
QKV_DIM = (N_HEADS + 2 * N_KV) * HEAD_DIM
ATT_OUT = N_HEADS * HEAD_DIM
KV_WIDTH = N_KV * HEAD_DIM
N_BUCKETS = 32
MAX_DISTANCE = 128
N_GROUPS = 4
EXPERTS_PER_GROUP = 8
N_EXPERTS = N_GROUPS * EXPERTS_PER_GROUP
TOP_K = 2
D_EXPERT = 512
ALPHA = (2 * DEPTH) ** 0.25
LN_EPS = 1e-5

LANES = 128
SUBLANES = 8
ROW_TILE = (SUBLANES, LANES)
assert D_MODEL == SUBLANES * LANES
ROUTER_ROWS = SUBLANES + N_EXPERTS
MIX_ROWS = 1024
ATTN_ROWS = 1024
DISPATCH_ROWS = 1024
COMBINE_ROWS = 256
EXPERT_ROWS = 1024
ROWS_PER_ISSUE = 32
DISPATCH_SLOTS = 3
EXPERT_SLOTS = 3
VMEM_LIMIT = 56 * 1024 * 1024

N_ROUTE_FIELDS = 6
R_E0, R_E1, R_RANK0, R_RANK1, R_GATE0, R_GATE1 = range(N_ROUTE_FIELDS)

F32 = jnp.float32
BF16 = jnp.bfloat16
I32 = jnp.int32
U32 = jnp.uint32


def _layer_norm(x, g, b):
    mu = jnp.mean(x, axis=-1, keepdims=True)
    xc = x - mu
    var = jnp.mean(xc * xc, axis=-1, keepdims=True)
    return xc * lax.rsqrt(var + LN_EPS) * g + b


def _gelu(x):
    return 0.5 * x * (1.0 + lax.erf(x * (2.0 ** -0.5)))


def _mixer_kernel(x_ref, win_ref, lng_ref, lnb_ref, wsp_ref, bsp_ref, cw_ref, wout_ref,
                  g1_ref, b1_ref, wr_ref, tri_ref, x1_ref, route_ref, route_t_ref, count_ref,
                  ztail_ref, base_ref):
    tm = x_ref.shape[0]
    i = pl.program_id(0)

    @pl.when(i % (SEQ // tm) == 0)
    def _():
        ztail_ref[...] = jnp.zeros_like(ztail_ref)

    x = x_ref[...]
    h = jnp.dot(x.astype(BF16), win_ref[...], preferred_element_type=F32)
    u = _gelu(h[:, :A_WIDTH])
    v = _gelu(h[:, A_WIDTH:2 * A_WIDTH])
    o = 2 * A_WIDTH
    g_b = h[:, o:o + B_WIDTH]
    g_c = h[:, o + B_WIDTH:o + 2 * B_WIDTH]
    hb = h[:, o + 2 * B_WIDTH:o + 3 * B_WIDTH]

    v = _layer_norm(v, lng_ref[...], lnb_ref[...]).astype(BF16)
    n_chunks = tm // CHUNK
    ri = lax.broadcasted_iota(I32, (CHUNK, CHUNK), 0)
    ci = lax.broadcasted_iota(I32, (CHUNK, CHUNK), 1)
    causal = ci <= ri
    sv_cols = [[None] * A_GROUPS for _ in range(n_chunks)]
    for g in range(A_GROUPS):
        ws = jnp.where(causal, wsp_ref[g], 0.0).astype(BF16)
        vg = jnp.concatenate(
            [v[c * CHUNK:(c + 1) * CHUNK, g * A_CH:(g + 1) * A_CH] for c in range(n_chunks)],
            axis=1)
        sg = jnp.dot(ws, vg, preferred_element_type=F32) + bsp_ref[:, g:g + 1]
        for c in range(n_chunks):
            sv_cols[c][g] = sg[:, c * A_CH:(c + 1) * A_CH]
    sv = jnp.concatenate([jnp.concatenate(row, axis=1) for row in sv_cols], axis=0)
    y_a = u * sv

    z = g_c * hb
    rows = lax.broadcasted_iota(I32, z.shape, 0)
    tail = ztail_ref[...]
    prev1 = tail[SUBLANES - 1:SUBLANES, :]
    prev2 = tail[SUBLANES - 2:SUBLANES - 1, :]
    zm1 = jnp.where(rows == 0, prev1, pltpu.roll(z, 1, 0))
    zm2 = jnp.where(rows == 0, prev2, jnp.where(rows == 1, prev1, pltpu.roll(z, 2, 0)))
    conv = cw_ref[0:1, :] * zm2 + cw_ref[1:2, :] * zm1 + cw_ref[2:3, :] * z
    y_b = g_b * conv
    ztail_ref[...] = z[tm - SUBLANES:tm, :]

    y = jnp.concatenate([y_a, y_b], axis=1).astype(BF16)
    m = jnp.dot(y, wout_ref[...], preferred_element_type=F32)
    x1 = _layer_norm(ALPHA * x + m, g1_ref[...], b1_ref[...])
    x1_ref[...] = x1
    _route_tile(x1, wr_ref, tri_ref, route_ref, route_t_ref, count_ref, base_ref)


def _mixer_layer(x, w_in, ln_g, ln_b, w_sp, b_sp, conv_w, w_out, g1, b1, w_router):
    t = x.shape[0]
    tm = MIX_ROWS
    const2 = lambda i: (0, 0)
    r_in, r_out, r_shape, r_scratch = _route_specs(t, tm)
    return pl.pallas_call(
        _mixer_kernel,
        grid=(t // tm,),
        in_specs=[
            pl.BlockSpec((tm, D_MODEL), lambda i: (i, 0)),
            pl.BlockSpec((D_MODEL, MIX_IN), const2),
            pl.BlockSpec((1, A_WIDTH), const2),
            pl.BlockSpec((1, A_WIDTH), const2),
            pl.BlockSpec((A_GROUPS, CHUNK, CHUNK), lambda i: (0, 0, 0)),
            pl.BlockSpec((CHUNK, A_GROUPS), const2),
            pl.BlockSpec((CONV_W, B_WIDTH), const2),
            pl.BlockSpec((A_WIDTH + B_WIDTH, D_MODEL), const2),
            pl.BlockSpec((1, D_MODEL), const2),
            pl.BlockSpec((1, D_MODEL), const2),
        ] + r_in,
        out_specs=[pl.BlockSpec((tm, D_MODEL), lambda i: (i, 0))] + r_out,
        out_shape=[jax.ShapeDtypeStruct((t, D_MODEL), F32)] + r_shape,
        scratch_shapes=[pltpu.VMEM((SUBLANES, B_WIDTH), F32), r_scratch],
        compiler_params=pltpu.CompilerParams(
            dimension_semantics=("arbitrary",), vmem_limit_bytes=VMEM_LIMIT),
        name="mixer_layer",
    )(x, w_in.astype(BF16), ln_g.reshape(1, -1), ln_b.reshape(1, -1), w_sp, b_sp.T,
      conv_w, w_out.astype(BF16), g1.reshape(1, -1), b1.reshape(1, -1),
      *_route_operands(*w_router, tm))


def _attn_kernel(sink_ref, x_ref, wqkv_ref, bqkv_ref, bias_ref, wo_ref, bo_ref,
                 g1_ref, b1_ref, wr_ref, tri_ref, x1_ref, route_ref, route_t_ref, count_ref,
                 kprev_ref, vprev_ref, base_ref):
    tm = x_ref.shape[0]
    i = pl.program_id(0)
    first = i % (SEQ // tm) == 0

    @pl.when(first)
    def _():
        kprev_ref[...] = jnp.zeros_like(kprev_ref)
        vprev_ref[...] = jnp.zeros_like(vprev_ref)

    x = x_ref[...]
    qkv = jnp.dot(x.astype(BF16), wqkv_ref[...], preferred_element_type=F32) + bqkv_ref[...]
    q = (qkv[:, :ATT_OUT] * (HEAD_DIM ** -0.5)).astype(BF16)

    lane = lax.broadcasted_iota(I32, (CHUNK, KV_WIDTH), 1)
    low = lane < HEAD_DIM

    def halves(t):
        swapped = pltpu.roll(t, HEAD_DIM, 1)
        zero = jnp.zeros_like(t)
        return [jnp.where(low, t, zero).astype(BF16), jnp.where(low, zero, swapped).astype(BF16),
                jnp.where(low, swapped, zero).astype(BF16), jnp.where(low, zero, t).astype(BF16)]

    def stacked(prev, cur, j):
        return jnp.concatenate([prev[2 * j], cur[2 * j], prev[2 * j + 1], cur[2 * j + 1]], axis=0)

    a = lax.broadcasted_iota(I32, (CHUNK, 2 * CHUNK), 0)
    c = lax.broadcasted_iota(I32, (CHUNK, 2 * CHUNK), 1)
    window = (c > a) & (c <= a + WINDOW)
    neg = jnp.finfo(F32).min
    n_pairs = N_HEADS // 2
    kv_of = lambda pair: (2 * pair) // GQA_GROUP

    k_prev = [kprev_ref[idx] for idx in range(2 * N_KV)]
    v_prev = [vprev_ref[idx] for idx in range(2 * N_KV)]
    o_blocks = []
    for blk in range(tm // CHUNK):
        rows = slice(blk * CHUNK, (blk + 1) * CHUNK)
        k_cur = halves(qkv[rows, ATT_OUT:ATT_OUT + KV_WIDTH])
        v_cur = halves(qkv[rows, ATT_OUT + KV_WIDTH:])
        k_rhs = [stacked(k_prev, k_cur, j) for j in range(N_KV)]
        v_rhs = [stacked(v_prev, v_cur, j) for j in range(N_KV)]
        k_prev, v_prev = k_cur, v_cur
        mask = window & (c >= jnp.where(first, CHUNK, 0)) if blk == 0 else window

        scores = [lax.dot_general(q[rows, pair * 2 * HEAD_DIM:(pair + 1) * 2 * HEAD_DIM],
                                  k_rhs[kv_of(pair)], (((1,), (1,)), ((), ())),
                                  preferred_element_type=F32) for pair in range(n_pairs)]
        probs, inv_denoms = [], []
        for pair in range(n_pairs):
            sc2 = scores[pair] + bias_ref[pair]
            sides = []
            for side in range(2):
                sc = jnp.where(mask, sc2[:, side * 2 * CHUNK:(side + 1) * 2 * CHUNK], neg)
                sink = sink_ref[2 * pair + side]
                m = jnp.maximum(jnp.max(sc, axis=-1, keepdims=True), sink)
                p = jnp.exp(sc - m)
                denom = jnp.sum(p, axis=-1, keepdims=True) + jnp.exp(sink - m)
                sides.append(p.astype(BF16))
                inv_denoms.append(jnp.broadcast_to(1.0 / denom, (CHUNK, HEAD_DIM)))
            probs.append(jnp.concatenate(sides, axis=1))
        outs = [jnp.dot(probs[pair], v_rhs[kv_of(pair)], preferred_element_type=F32)
                for pair in range(n_pairs)]
        o = jnp.concatenate(outs, axis=1) * jnp.concatenate(inv_denoms, axis=1)
        o_blocks.append(o.astype(BF16))
    for idx in range(2 * N_KV):
        kprev_ref[idx] = k_prev[idx]
        vprev_ref[idx] = v_prev[idx]
    o = jnp.concatenate(o_blocks, axis=0)
    m_out = jnp.dot(o, wo_ref[...], preferred_element_type=F32) + bo_ref[...]
    x1 = _layer_norm(ALPHA * x + m_out, g1_ref[...], b1_ref[...])
    x1_ref[...] = x1
    _route_tile(x1, wr_ref, tri_ref, route_ref, route_t_ref, count_ref, base_ref)


def _t5_bucket(rel):
    n = jnp.maximum(rel, 0)
    max_exact = N_BUCKETS // 2
    nf = jnp.maximum(n, 1).astype(F32)
    large = max_exact + (jnp.log(nf / max_exact) / math.log(MAX_DISTANCE / max_exact)
                         * (N_BUCKETS - max_exact)).astype(I32)
    large = jnp.minimum(large, N_BUCKETS - 1)
    return jnp.where(n < max_exact, n, large)


def _rel_bias(rel_table):
    a = jnp.arange(CHUNK)[:, None]
    c = jnp.arange(2 * CHUNK)[None, :]
    onehot = jax.nn.one_hot(_t5_bucket(a + CHUNK - c), N_BUCKETS, dtype=F32)
    bias = jnp.einsum('acb,bh->hac', onehot, rel_table.astype(F32),
                      precision=lax.Precision.HIGHEST)
    bias = bias.reshape(N_HEADS // 2, 2, CHUNK, 2 * CHUNK).transpose(0, 2, 1, 3)
    return bias.reshape(N_HEADS // 2, CHUNK, 4 * CHUNK)


def _attn_layer(x, bias, w_qkv, b_qkv, sinks, w_o, b_o, g1, b1, w_router):
    t = x.shape[0]
    tm = ATTN_ROWS
    const2 = lambda i: (0, 0)
    r_in, r_out, r_shape, r_scratch = _route_specs(t, tm)
    return pl.pallas_call(
        _attn_kernel,
        grid=(t // tm,),
        in_specs=[
            pl.BlockSpec(memory_space=pltpu.SMEM),
            pl.BlockSpec((tm, D_MODEL), lambda i: (i, 0)),
            pl.BlockSpec((D_MODEL, QKV_DIM), const2),
            pl.BlockSpec((1, QKV_DIM), const2),
            pl.BlockSpec((N_HEADS // 2, CHUNK, 4 * CHUNK), lambda i: (0, 0, 0)),
            pl.BlockSpec((ATT_OUT, D_MODEL), const2),
            pl.BlockSpec((1, D_MODEL), const2),
            pl.BlockSpec((1, D_MODEL), const2),
            pl.BlockSpec((1, D_MODEL), const2),
        ] + r_in,
        out_specs=[pl.BlockSpec((tm, D_MODEL), lambda i: (i, 0))] + r_out,
        out_shape=[jax.ShapeDtypeStruct((t, D_MODEL), F32)] + r_shape,
        scratch_shapes=[pltpu.VMEM((2 * N_KV, CHUNK, KV_WIDTH), BF16),
                        pltpu.VMEM((2 * N_KV, CHUNK, KV_WIDTH), BF16), r_scratch],
        compiler_params=pltpu.CompilerParams(
            dimension_semantics=("arbitrary",), vmem_limit_bytes=VMEM_LIMIT),
        name="attn_layer",
    )(sinks, x, w_qkv.astype(BF16), b_qkv.reshape(1, -1), bias, w_o.astype(BF16),
      b_o.reshape(1, -1), g1.reshape(1, -1), b1.reshape(1, -1), *_route_operands(*w_router, tm))


def _route_tile(x1, wr_ref, tri_ref, route_ref, route_t_ref, count_ref, base_ref):
    @pl.when(pl.program_id(0) == 0)
    def _():
        base_ref[...] = jnp.zeros_like(base_ref)

    tm = x1.shape[0]
    lt = lax.dot_general(wr_ref[...], x1.astype(BF16), (((1,), (1,)), ((), ())),
                         preferred_element_type=F32)
    sub = lax.broadcasted_iota(I32, (SUBLANES, tm), 0).astype(F32)
    ninf = -jnp.inf

    def first_argmax(vals):
        m = jnp.max(vals, axis=0, keepdims=True)
        idx = jnp.min(jnp.where(vals == m, sub, float(SUBLANES)), axis=0, keepdims=True)
        return m, idx

    is_g = sub < N_GROUPS
    g_rows = lt[0:SUBLANES]
    gmax, g_idx = first_argmax(jnp.where(is_g, g_rows, ninf))
    g_p = 1.0 / jnp.sum(jnp.where(is_g, jnp.exp(g_rows - gmax), 0.0), axis=0, keepdims=True)

    group_rows = lambda a, g: a[SUBLANES * (g + 1):SUBLANES * (g + 2)]
    el = group_rows(lt, 0)
    for g in range(1, N_GROUPS):
        el = jnp.where(g_idx == g, group_rows(lt, g), el)
    m1, j1 = first_argmax(el)
    m2, j2 = first_argmax(jnp.where(sub == j1, ninf, el))
    a2 = jnp.exp(m2 - m1)
    gate0 = g_p / (1.0 + a2)
    gate1 = g_p * a2 / (1.0 + a2)

    hit0 = [(g_idx == g) & (sub == j1) for g in range(N_GROUPS)]
    hit1 = [(g_idx == g) & (sub == j2) for g in range(N_GROUPS)]
    onehot = jnp.concatenate([jnp.where(h0 | h1, 1.0, 0.0) for h0, h1 in zip(hit0, hit1)],
                             axis=0)
    before = jnp.dot(onehot.astype(BF16), tri_ref[...], preferred_element_type=F32)
    before = before + base_ref[:, 0:1]

    def picked(hits):
        total = jnp.zeros((SUBLANES, tm), F32)
        for g in range(N_GROUPS):
            total = total + jnp.where(hits[g], before[SUBLANES * g:SUBLANES * (g + 1)], 0.0)
        return jnp.sum(total, axis=0, keepdims=True)

    rank0 = picked(hit0)
    rank1 = picked(hit1)
    base_ref[...] += jnp.sum(onehot, axis=1, keepdims=True)

    e_base = g_idx * EXPERTS_PER_GROUP
    rec_t = jnp.concatenate([e_base + j1, e_base + j2, rank0, rank1, gate0, gate1,
                             jnp.zeros((SUBLANES - N_ROUTE_FIELDS, tm), F32)], axis=0)
    route_t_ref[...] = rec_t
    padded = jnp.concatenate([rec_t, jnp.zeros((LANES - SUBLANES, tm), F32)], axis=0)
    route_ref[...] = jnp.transpose(padded)
    count_ref[...] = base_ref[...]


def _route_specs(t, tm):
    const2 = lambda i: (0, 0)
    in_specs = [pl.BlockSpec((ROUTER_ROWS, D_MODEL), const2), pl.BlockSpec((tm, tm), const2)]
    out_specs = [
        pl.BlockSpec((tm, LANES), lambda i: (i, 0)),
        pl.BlockSpec((SUBLANES, tm), lambda i: (0, i)),
        pl.BlockSpec((N_EXPERTS, LANES), const2),
    ]
    out_shape = [
        jax.ShapeDtypeStruct((t, LANES), F32),
        jax.ShapeDtypeStruct((SUBLANES, t), F32),
        jax.ShapeDtypeStruct((N_EXPERTS, LANES), F32),
    ]
    return in_specs, out_specs, out_shape, pltpu.VMEM((N_EXPERTS, LANES), F32)


def _route_operands(w_group, w_expert, tm):
    pad = jnp.zeros((D_MODEL, SUBLANES - N_GROUPS), F32)
    w_rows = jnp.concatenate([w_group, pad, w_expert], axis=1).T
    return w_rows.astype(BF16), jnp.triu(jnp.ones((tm, tm), BF16), 1)


def _plan(route_t, count_lanes):
    t = route_t.shape[1]
    rb = EXPERT_ROWS
    experts = jnp.arange(N_EXPERTS, dtype=I32)
    counts = count_lanes[:, 0].astype(I32)
    end = jnp.cumsum(counts)
    start = end - counts
    e = route_t[R_E0:R_E1 + 1].astype(I32)
    rank = route_t[R_RANK0:R_RANK1 + 1].astype(I32)
    onehot = e[:, None, :] == experts[None, :, None]
    dest = jnp.sum(jnp.where(onehot, start[None, :, None], 0), axis=1) + rank

    first_blk = start // rb
    last_blk = jnp.maximum(end - 1, 0) // rb
    n_steps_e = jnp.where(counts > 0, last_blk - first_blk + 1, 0)
    step_end = jnp.cumsum(n_steps_e)
    step_start = step_end - n_steps_e
    n_steps = step_end[-1]
    max_steps = t * TOP_K // rb + N_EXPERTS - 1
    s = jnp.minimum(jnp.arange(max_steps, dtype=I32), n_steps - 1)
    step_e = jnp.minimum(jnp.sum((step_end[None, :] <= s[:, None]).astype(I32), axis=1),
                         N_EXPERTS - 1)
    pick = step_e[:, None] == experts[None, :]
    take = lambda v: jnp.sum(jnp.where(pick, v[None, :], 0), axis=1)
    step_blk = take(first_blk) + s - take(step_start)
    step_lo = jnp.clip(take(start) - step_blk * rb, 0, rb)
    step_hi = jnp.clip(take(end) - step_blk * rb, 0, rb)
    return dest, (step_blk, step_e, step_lo, step_hi, n_steps.reshape(1))


def _dest_blocks(dest, tm):
    t = dest.shape[1]
    return dest.reshape(TOP_K, t // tm, tm).transpose(1, 0, 2).reshape(t // tm, 1, TOP_K * tm)


def _to_row_tiles(x):
    return x.reshape(x.shape[0], ROW_TILE[0], ROW_TILE[1])


def _from_row_tiles(x):
    return x.reshape(x.shape[0], D_MODEL)


PACKED_ROW_TILE = (SUBLANES // 2, LANES)
BF16_BITS = 16
HIGH_HALF = 0xFFFF0000


def _pack_rows(x):
    half = D_MODEL // 2
    bits = lambda v: lax.bitcast_convert_type(v.astype(BF16).astype(F32), U32)
    words = (bits(x[:, :half]) >> BF16_BITS) | (bits(x[:, half:]) & jnp.uint32(HIGH_HALF))
    return words.reshape((x.shape[0],) + PACKED_ROW_TILE)


def _unpack_rows(words):
    w = words.reshape(words.shape[0], D_MODEL // 2)
    low = lax.bitcast_convert_type(w << BF16_BITS, F32).astype(BF16)
    high = lax.bitcast_convert_type(w & jnp.uint32(HIGH_HALF), F32).astype(BF16)
    return jnp.concatenate([low, high], axis=1)


def _dispatch_kernel(dest_ref, x1_ref, xs_hbm, xbuf, row_sems):
    tm = x1_ref.shape[0]
    i = pl.program_id(0)
    n = pl.num_programs(0)
    slot = i % DISPATCH_SLOTS

    def drain_rows(s):
        for k in range(TOP_K):
            pltpu.make_async_copy(xbuf.at[s], xs_hbm.at[pl.ds(0, tm)], row_sems.at[s]).wait()

    @pl.when(i >= DISPATCH_SLOTS)
    def _():
        drain_rows(slot)

    xbuf[slot] = _pack_rows(x1_ref[...])

    def issue(g, carry):
        for j in range(ROWS_PER_ISSUE):
            r = g * ROWS_PER_ISSUE + j
            for k in range(TOP_K):
                pltpu.make_async_copy(xbuf.at[slot, r], xs_hbm.at[dest_ref[0, 0, k * tm + r]],
                                      row_sems.at[slot]).start(priority=k)
        return carry

    lax.fori_loop(0, tm // ROWS_PER_ISSUE, issue, 0)

    @pl.when(i == n - 1)
    def _():
        for back in range(DISPATCH_SLOTS):
            drain_rows((i - back) % DISPATCH_SLOTS)


def _dispatch(x1, dest):
    t = x1.shape[0]
    tm = DISPATCH_ROWS
    return pl.pallas_call(
        _dispatch_kernel,
        grid=(t // tm,),
        in_specs=[
            pl.BlockSpec((1, 1, TOP_K * tm), lambda i: (i, 0, 0), memory_space=pltpu.SMEM),
            pl.BlockSpec((tm, D_MODEL), lambda i: (i, 0)),
        ],
        out_specs=pl.BlockSpec(memory_space=pl.ANY),
        out_shape=jax.ShapeDtypeStruct((t * TOP_K,) + PACKED_ROW_TILE, U32),
        scratch_shapes=[
            pltpu.VMEM((DISPATCH_SLOTS, tm) + PACKED_ROW_TILE, U32),
            pltpu.SemaphoreType.DMA((DISPATCH_SLOTS,)),
        ],
        compiler_params=pltpu.CompilerParams(
            dimension_semantics=("arbitrary",), vmem_limit_bytes=VMEM_LIMIT),
        name="dispatch",
    )(_dest_blocks(dest, tm), x1)


def _expert_kernel(blk_ref, e_ref, lo_ref, hi_ref, n_steps_ref, xs_hbm, wg_ref, wu_ref, wd_ref,
                   ys_ref, wg_bf, wu_bf, wd_bf, xring, xsems):
    s = pl.program_id(0)
    n = pl.num_programs(0)
    rb = xring.shape[1]

    def load(j):
        slot_j = j % EXPERT_SLOTS
        rows_j = pl.ds(pl.multiple_of(blk_ref[j] * rb, rb), rb)
        return pltpu.make_async_copy(xs_hbm.at[rows_j], xring.at[slot_j], xsems.at[slot_j])

    @pl.when(s == 0)
    def _():
        load(0).start()

        @pl.when(n > 1)
        def _():
            load(1).start()

    @pl.when(s + 2 < n)
    def _():
        load(s + 2).start()

    load(s).wait()
    xs_ref = xring.at[s % EXPERT_SLOTS]
    prev = jnp.maximum(s - 1, 0)
    new_expert = (s == 0) | (e_ref[s] != e_ref[prev])
    new_block = (s == 0) | (blk_ref[s] != blk_ref[prev])

    @pl.when(new_expert)
    def _():
        wg_bf[...] = wg_ref[0, 0].astype(BF16)
        wu_bf[...] = wu_ref[0, 0].astype(BF16)
        wd_bf[...] = wd_ref[0, 0].astype(BF16)

    half = rb // 2
    lo, hi = lo_ref[s], hi_ref[s]
    active = s < n_steps_ref[0]
    both_halves = (lo < half) & (hi > half)

    def masked_ffn(x_tiles, row0):
        xb = _unpack_rows(x_tiles)
        gate = jnp.dot(xb, wg_bf[...], preferred_element_type=F32)
        up = jnp.dot(xb, wu_bf[...], preferred_element_type=F32)
        hid = (gate * jax.nn.sigmoid(gate) * up).astype(BF16)
        y = jnp.dot(hid, wd_bf[...], preferred_element_type=F32)
        rows = lax.broadcasted_iota(I32, y.shape, 0) + row0
        return _to_row_tiles(jnp.where((rows >= lo) & (rows < hi), y, 0.0))

    @pl.when(active & both_halves)
    def _():
        y = masked_ffn(xs_ref[...], 0)

        @pl.when(new_block)
        def _():
            ys_ref[...] = y

        @pl.when(jnp.logical_not(new_block))
        def _():
            ys_ref[...] += y

    @pl.when(active & jnp.logical_not(both_halves))
    def _():
        off = pl.multiple_of(jnp.where(lo < half, 0, half), half)
        y = masked_ffn(xs_ref[pl.ds(off, half)], off)

        @pl.when(new_block)
        def _():
            ys_ref[pl.ds(off, half)] = y
            ys_ref[pl.ds(half - off, half)] = jnp.zeros_like(y)

        @pl.when(jnp.logical_not(new_block))
        def _():
            ys_ref[pl.ds(off, half)] += y


def _expert_ffn(xs, steps, layer, wg, wu, wd):
    step_blk, step_e, step_lo, step_hi, n_steps = steps
    rb = EXPERT_ROWS
    row_map = lambda s, blk, e, lo, hi, n: (blk[s], 0, 0)
    w_map = lambda s, blk, e, lo, hi, n: (layer, e[s], 0, 0)
    grid_spec = pltpu.PrefetchScalarGridSpec(
        num_scalar_prefetch=5,
        grid=(step_blk.shape[0],),
        in_specs=[
            pl.BlockSpec(memory_space=pl.ANY),
            pl.BlockSpec((1, 1, D_MODEL, D_EXPERT), w_map),
            pl.BlockSpec((1, 1, D_MODEL, D_EXPERT), w_map),
            pl.BlockSpec((1, 1, D_EXPERT, D_MODEL), w_map),
        ],
        out_specs=pl.BlockSpec((rb,) + ROW_TILE, row_map),
        scratch_shapes=[
            pltpu.VMEM((D_MODEL, D_EXPERT), BF16),
            pltpu.VMEM((D_MODEL, D_EXPERT), BF16),
            pltpu.VMEM((D_EXPERT, D_MODEL), BF16),
            pltpu.VMEM((EXPERT_SLOTS, rb) + PACKED_ROW_TILE, U32),
            pltpu.SemaphoreType.DMA((EXPERT_SLOTS,)),
        ],
    )
    return pl.pallas_call(
        _expert_kernel,
        grid_spec=grid_spec,
        out_shape=jax.ShapeDtypeStruct((xs.shape[0],) + ROW_TILE, F32),
        compiler_params=pltpu.CompilerParams(
            dimension_semantics=("arbitrary",), vmem_limit_bytes=VMEM_LIMIT),
        name="expert_ffn",
    )(step_blk, step_e, step_lo, step_hi, n_steps, xs, wg, wu, wd)


def _combine_kernel(dest_ref, dest_next_ref, x1_ref, route_ref, g2_ref, b2_ref, ys_hbm,
                    x2_ref, ybuf, sems):
    tm = x1_ref.shape[0]
    i = pl.program_id(0)
    n = pl.num_programs(0)
    slot = i % 2

    def gather(d_ref, to_slot):
        def issue(g, carry):
            for j in range(ROWS_PER_ISSUE):
                r = g * ROWS_PER_ISSUE + j
                for k in range(TOP_K):
                    pltpu.make_async_copy(ys_hbm.at[d_ref[0, 0, k * tm + r]],
                                          ybuf.at[to_slot, k, r],
                                          sems.at[to_slot]).start(priority=k)
            return carry
        lax.fori_loop(0, tm // ROWS_PER_ISSUE, issue, 0)

    @pl.when(i == 0)
    def _():
        gather(dest_ref, slot)

    @pl.when(i + 1 < n)
    def _():
        gather(dest_next_ref, 1 - slot)

    for k in range(TOP_K):
        pltpu.make_async_copy(ys_hbm.at[pl.ds(0, tm)], ybuf.at[slot, k], sems.at[slot]).wait()

    route = route_ref[...]
    f = (route[:, R_GATE0:R_GATE0 + 1] * _from_row_tiles(ybuf[slot, 0])
         + route[:, R_GATE1:R_GATE1 + 1] * _from_row_tiles(ybuf[slot, 1]))
    x2_ref[...] = _layer_norm(ALPHA * x1_ref[...] + f, g2_ref[...], b2_ref[...])


def _combine_ln(x1, ys, dest, route, g2, b2):
    t = x1.shape[0]
    tm = COMBINE_ROWS
    n = t // tm
    row = lambda i: (i, 0)
    const2 = lambda i: (0, 0)
    dest_blocks = _dest_blocks(dest, tm)
    smem_block = lambda index_map: pl.BlockSpec((1, 1, TOP_K * tm), index_map,
                                                memory_space=pltpu.SMEM)
    return pl.pallas_call(
        _combine_kernel,
        grid=(n,),
        in_specs=[
            smem_block(lambda i: (i, 0, 0)),
            smem_block(lambda i: (jnp.minimum(i + 1, n - 1), 0, 0)),
            pl.BlockSpec((tm, D_MODEL), row),
            pl.BlockSpec((tm, LANES), row),
            pl.BlockSpec((1, D_MODEL), const2),
            pl.BlockSpec((1, D_MODEL), const2),
            pl.BlockSpec(memory_space=pl.ANY),
        ],
        out_specs=pl.BlockSpec((tm, D_MODEL), row),
        out_shape=jax.ShapeDtypeStruct((t, D_MODEL), F32),
        scratch_shapes=[
            pltpu.VMEM((2, TOP_K, tm) + ROW_TILE, F32),
            pltpu.SemaphoreType.DMA((2,)),
        ],
        compiler_params=pltpu.CompilerParams(
            dimension_semantics=("arbitrary",), vmem_limit_bytes=VMEM_LIMIT),
        name="combine_ln",
    )(dest_blocks, dest_blocks, x1, route, g2.reshape(1, -1), b2.reshape(1, -1), ys)


def _moe_layer(x1, route, route_t, counts, layer, wg, wu, wd, g2, b2):
    dest, steps = _plan(route_t, counts)
    xs = _dispatch(x1, dest)
    ys = _expert_ffn(xs, steps, layer, wg, wu, wd)
    return _combine_ln(x1, ys, dest, route, g2, b2)


def kernel(x, rel_bias_table, mix_w_in, gmlp_ln_g, gmlp_ln_b, gmlp_w_spatial, gmlp_b_spatial, conv_w, mix_w_out, attn_w_qkv, attn_b_qkv, attn_sinks, attn_w_o, attn_b_o, ln1_g, ln1_b, ln2_g, ln2_b, router_group, router_expert, expert_w_gate, expert_w_up, expert_w_down):
    bsz, s, d = x.shape
    assert (s, d) == (SEQ, D_MODEL)
    x = x.reshape(bsz * s, d)
    bias = _rel_bias(rel_bias_table)
    for l in range(DEPTH):
        i = l // 2
        w_router = (router_group[l], router_expert[l])
        if l % 2 == 0:
            x1, *routing = _mixer_layer(x, mix_w_in[i], gmlp_ln_g[i], gmlp_ln_b[i],
                                        gmlp_w_spatial[i], gmlp_b_spatial[i], conv_w[i],
                                        mix_w_out[i], ln1_g[l], ln1_b[l], w_router)
        else:
            x1, *routing = _attn_layer(x, bias, attn_w_qkv[i], attn_b_qkv[i], attn_sinks[i],
                                       attn_w_o[i], attn_b_o[i], ln1_g[l], ln1_b[l], w_router)
        x = _moe_layer(x1, *routing, l, expert_w_gate, expert_w_up, expert_w_down,
                       ln2_g[l], ln2_b[l])
    return x.reshape(bsz, s, d)
```
